```python
import math
import jax
import jax.numpy as jnp
from jax import lax
import numpy as np

D_MODEL = 2048
BATCH = 1
SEQ = 8192
DEPTH = 2
DEC_BATCH = 32
DEC_SEQ = 4
PAST_LEN = 8192
PAGE_SIZE = 128

N_A_LAYERS = DEPTH // 2
N_B_LAYERS = DEPTH - N_A_LAYERS
SSM_GROUP = 16
N_GROUPS = D_MODEL // SSM_GROUP
STATE_DIM = 64
DT_MIN = 0.001
DT_MAX = 0.1
HEAD_DIM = 128
N_HEADS = D_MODEL // HEAD_DIM
N_KV = 4
HPG = N_HEADS // N_KV
CMP_STRIDE = 16
CMP_BLOCK = 2 * CMP_STRIDE
CMP_HID = HEAD_DIM
SEL_BLOCK = 64
SUBS_PER_SEL = SEL_BLOCK // CMP_STRIDE
TOPK = 16
WINDOW = 512
QBLK = 128
D_FF = 4 * D_MODEL
EPS = 1e-6
NEG = -1e30
FORCE_BONUS = 1e4
F32 = jnp.float32

kernel_name = 'yoco_s5_nsa_decode_step'


def rmsnorm(x, g):
    xf = x.astype(F32)
    y = xf * lax.rsqrt(jnp.mean(xf * xf, axis=-1, keepdims=True) + EPS)
    return (y * g.astype(F32)).astype(x.dtype)


def modulation(c, w, b):
    return (jax.nn.silu(c.astype(F32)) @ w.astype(F32) + b.astype(F32))[:, None, :]


def alibi_slopes():
    exps = np.arange(1, N_HEADS + 1, dtype=np.float32) * np.float32(-8.0 / N_HEADS)
    return jnp.asarray(np.exp2(exps), dtype=F32).reshape(N_KV, HPG)


def masked_softmax(s, mask):
    s = jnp.where(mask, s, NEG)
    m = jnp.max(s, axis=-1, keepdims=True)
    p = jnp.where(mask, jnp.exp(s - m), 0.0)
    return p / jnp.maximum(jnp.sum(p, axis=-1, keepdims=True), 1e-30)


def sqrelu_mlp(h, w1, w2):
    return jnp.square(jax.nn.relu(h @ w1)) @ w2


def s5_mixer(u, h0, lam_re, lam_im, log_dt, b_re, b_im, c_re, c_im, d_skip, w_glu):
    Bn, L, _ = u.shape
    uf = u.astype(F32).reshape(Bn, L, N_GROUPS, SSM_GROUP)
    lam = lax.complex(lam_re.astype(F32), lam_im.astype(F32))
    dt = jnp.exp(log_dt.astype(F32))[:, None]
    a_bar = jnp.exp(lam * dt)
    b_c = lax.complex(b_re.astype(F32), b_im.astype(F32))
    b_bar = ((a_bar - 1.0) / lam)[..., None] * b_c
    bu = lax.complex(jnp.einsum('blgc,gpc->blgp', uf, b_bar.real),
                     jnp.einsum('blgc,gpc->blgp', uf, b_bar.imag))
    h_init = lax.complex(h0[..., 0].astype(F32), h0[..., 1].astype(F32))
    bu = bu.at[:, 0].add(a_bar * h_init)
    a = jnp.broadcast_to(a_bar, bu.shape)

    def combine(e1, e2):
        a1, b1 = e1
        a2, b2 = e2
        return a1 * a2, a2 * b1 + b2

    _, h = lax.associative_scan(combine, (a, bu), axis=1)
    y = (jnp.einsum('blgp,gcp->blgc', h.real, c_re.astype(F32))
         - jnp.einsum('blgp,gcp->blgc', h.imag, c_im.astype(F32)))
    y = y.reshape(Bn, L, D_MODEL) + d_skip.astype(F32) * u.astype(F32)
    z = jax.nn.gelu(y) @ w_glu.astype(F32)
    out = z[..., :D_MODEL] * jax.nn.sigmoid(z[..., D_MODEL:])
    h_last = jnp.stack([h[:, -1].real, h[:, -1].imag], axis=-1)
    return out.astype(u.dtype), h_last


def shared_kv_rows(x, c, kv_norm, kv_mod_w, kv_mod_b, w_kv):
    shift, scale = jnp.split(modulation(c, kv_mod_w, kv_mod_b), 2, axis=-1)
    h = rmsnorm(x, kv_norm) * (1.0 + scale) + shift
    Bn, L, _ = x.shape
    kv = (h @ w_kv).reshape(Bn, L, 3, 2, N_KV, HEAD_DIM)
    return kv[:, :, 0], kv[:, :, 1], kv[:, :, 2]


def cmp_partials(rows, pe, w1):
    Bn, L = rows.shape[:2]
    sub = rows.astype(F32).reshape(Bn, L // CMP_STRIDE, CMP_STRIDE, 2, N_KV, HEAD_DIM)
    w_lo = w1[:, :CMP_STRIDE].astype(F32)
    w_hi = w1[:, CMP_STRIDE:].astype(F32)
    pe_lo = jnp.einsum('crd,crde->ce', pe[:, :CMP_STRIDE].astype(F32), w_lo)
    pe_hi = jnp.einsum('crd,crde->ce', pe[:, CMP_STRIDE:].astype(F32), w_hi)
    lo = jnp.einsum('bnrckd,crde->bncke', sub, w_lo) + pe_lo[:, None, :]
    hi = jnp.einsum('bnrckd,crde->bncke', sub, w_hi) + pe_hi[:, None, :]
    return lo, hi


def compress(lo, hi, b1, w2):
    h = jax.nn.gelu(lo[:, :-1] + hi[:, 1:] + b1.astype(F32)[:, None, :])
    kv = jnp.einsum('bncke,ced->bnckd', h, w2.astype(F32))
    return kv[:, :, 0], kv[:, :, 1]


def nsa_queries(h, w_qg):
    Bn, L, _ = h.shape
    z = (h @ w_qg).astype(F32)
    q = z[..., :N_HEADS * HEAD_DIM].reshape(Bn, L, N_KV, HPG, HEAD_DIM)
    gates = jax.nn.sigmoid(z[..., N_HEADS * HEAD_DIM:]).reshape(Bn, L, N_KV, HPG, 3)
    return q, gates


def nsa_attend(q, gates, pos, kc, vc, c_end, gather_sel, n_sel, kw, vw, w_pos):
    Bn, T = q.shape[:2]
    slopes = alibi_slopes()
    scale = HEAD_DIM ** -0.5
    dist_c = (pos[:, None] - c_end[None, :]).astype(F32)
    s_c = (jnp.einsum('btghd,bngd->btghn', q, kc) * scale
           - slopes[None, None, :, :, None] * dist_c[None, :, None, None, :])
    p_c = masked_softmax(s_c, (dist_c >= 0)[None, :, None, None, :])
    o_c = jnp.einsum('btghn,bngd->btghd', p_c, vc)
    imp = p_c.sum(axis=3)
    n_cmp = imp.shape[-1]
    imp = jnp.pad(imp, ((0, 0), (0, 0), (0, 0), (1, SUBS_PER_SEL * (n_sel + 1) - 1 - n_cmp)))
    r = imp.reshape(Bn, T, N_KV, n_sel + 1, SUBS_PER_SEL)
    imp_sel = r[..., :n_sel, :].sum(-1) + r[..., 1:, 0]
    blk = pos // SEL_BLOCK
    j = jnp.arange(n_sel)
    forced = (j[None, :] == 0) | (j[None, :] == blk[:, None]) | (j[None, :] == blk[:, None] - 1)
    visible = j[None, :] <= blk[:, None]
    score = jnp.where(visible[None, :, None, :], imp_sel + FORCE_BONUS * forced[None, :, None, :], NEG)
    top_s, idx = lax.top_k(score, min(TOPK, n_sel))
    valid = top_s > 0.5 * NEG
    ks, vs = gather_sel(idx)
    n_k = idx.shape[-1]
    s_pos = idx[..., None] * SEL_BLOCK + jnp.arange(SEL_BLOCK)
    dist_s = (pos[None, :, None, None, None] - s_pos).astype(F32)
    mask_s = valid[..., None] & (dist_s >= 0)
    s_s = (jnp.einsum('btghd,btgksd->btghks', q, ks.astype(F32)) * scale
           - slopes[None, None, :, :, None, None] * dist_s[:, :, :, None])
    p_s = masked_softmax(s_s.reshape(Bn, T, N_KV, HPG, n_k * SEL_BLOCK),
                         mask_s[:, :, :, None].reshape(Bn, T, N_KV, 1, n_k * SEL_BLOCK))
    o_s = jnp.einsum('btghm,btgmd->btghd', p_s,
                     vs.astype(F32).reshape(Bn, T, N_KV, n_k * SEL_BLOCK, HEAD_DIM))
    dist_w = pos[:, None] - w_pos[None, :]
    mask_w = (dist_w >= 0) & (dist_w < WINDOW) & (w_pos[None, :] >= 0)
    s_w = (jnp.einsum('btghd,bmgd->btghm', q, kw.astype(F32)) * scale
           - slopes[None, None, :, :, None] * dist_w.astype(F32)[None, :, None, None, :])
    p_w = masked_softmax(s_w, mask_w[None, :, None, None, :])
    o_w = jnp.einsum('btghm,bmgd->btghd', p_w, vw.astype(F32))
    return gates[..., 0:1] * o_c + gates[..., 1:2] * o_s + gates[..., 2:3] * o_w


def prompt_attention(kv_cmp, kv_slc, kv_win, cmp_pe, cmp_w1, cmp_b1, cmp_w2):
    Bn, L = kv_cmp.shape[:2]
    lo, hi = cmp_partials(kv_cmp, cmp_pe, cmp_w1)
    kc, vc = compress(lo, hi, cmp_b1, cmp_w2)
    c_end = jnp.arange(kc.shape[1]) * CMP_STRIDE + (CMP_BLOCK - 1)
    n_sel = L // SEL_BLOCK
    sel_blocks = kv_slc.reshape(Bn, n_sel, SEL_BLOCK, 2, N_KV, HEAD_DIM)
    b_i = jnp.arange(Bn)[:, None, None, None]
    g_i = jnp.arange(N_KV)[None, None, :, None]

    def gather_sel(idx):
        return (sel_blocks[b_i, idx, :, 0, g_i, :], sel_blocks[b_i, idx, :, 1, g_i, :])

    win_pad = jnp.pad(kv_win, ((0, 0), (WINDOW, 0), (0, 0), (0, 0), (0, 0)))
    n_blk = L // QBLK
    starts = jnp.arange(n_blk) * QBLK

    def attend(q, gates):
        q_b = q.reshape(Bn, n_blk, QBLK, N_KV, HPG, HEAD_DIM).swapaxes(0, 1)
        g_b = gates.reshape(Bn, n_blk, QBLK, N_KV, HPG, 3).swapaxes(0, 1)

        def block(args):
            q_blk, g_blk, s0 = args
            w_rows = lax.dynamic_slice_in_dim(win_pad, s0, WINDOW + QBLK, axis=1)
            w_pos = s0 - WINDOW + jnp.arange(WINDOW + QBLK)
            return nsa_attend(q_blk, g_blk, s0 + jnp.arange(QBLK), kc, vc, c_end, gather_sel, n_sel,
                              w_rows[:, :, 0], w_rows[:, :, 1], w_pos)

        o = lax.map(block, (q_b, g_b, starts))
        return o.swapaxes(0, 1).reshape(Bn, L, N_KV, HPG, HEAD_DIM)

    return attend


def sample_attention(kv_cmp, kv_slc, kv_win, cache_cmp, cache_slc, cache_win, page_table,
                     cmp_pe, cmp_w1, cmp_b1, cmp_w2):
    Bn, T = kv_cmp.shape[:2]
    n_pages = PAST_LEN // PAGE_SIZE
    past_cmp = cache_cmp[page_table].reshape(Bn, n_pages * PAGE_SIZE, 2, N_KV, HEAD_DIM)
    lo, hi = cmp_partials(past_cmp, cmp_pe, cmp_w1)
    n_new_sub = T // CMP_STRIDE
    if n_new_sub > 0:
        lo_new, hi_new = cmp_partials(kv_cmp[:, :n_new_sub * CMP_STRIDE], cmp_pe, cmp_w1)
        lo = jnp.concatenate([lo, lo_new], axis=1)
        hi = jnp.concatenate([hi, hi_new], axis=1)
    kc, vc = compress(lo, hi, cmp_b1, cmp_w2)
    c_end = jnp.arange(kc.shape[1]) * CMP_STRIDE + (CMP_BLOCK - 1)
    n_past_blk = PAST_LEN // SEL_BLOCK
    n_new_blk = -(-T // SEL_BLOCK)
    n_sel = n_past_blk + n_new_blk
    blk_per_page = PAGE_SIZE // SEL_BLOCK
    pool = cache_slc.reshape(cache_slc.shape[0], blk_per_page, SEL_BLOCK, 2, N_KV, HEAD_DIM)
    new_blocks = jnp.pad(kv_slc, ((0, 0), (0, n_new_blk * SEL_BLOCK - T), (0, 0), (0, 0), (0, 0)))
    new_blocks = new_blocks.reshape(Bn, n_new_blk, SEL_BLOCK, 2, N_KV, HEAD_DIM)
    b_i = jnp.arange(Bn)[:, None, None, None]
    g_i = jnp.arange(N_KV)[None, None, :, None]

    def gather_sel(idx):
        jp = jnp.minimum(idx, n_past_blk - 1)
        page = page_table[b_i, jp // blk_per_page]
        sub = jp % blk_per_page
        jn = jnp.clip(idx - n_past_blk, 0, n_new_blk - 1)
        is_new = (idx >= n_past_blk)[..., None, None]
        k = jnp.where(is_new, new_blocks[b_i, jn, :, 0, g_i, :], pool[page, sub, :, 0, g_i, :])
        v = jnp.where(is_new, new_blocks[b_i, jn, :, 1, g_i, :], pool[page, sub, :, 1, g_i, :])
        return k, v

    w_rows = jnp.concatenate([cache_win.astype(kv_win.dtype), kv_win], axis=1)
    w_pos = PAST_LEN - cache_win.shape[1] + jnp.arange(w_rows.shape[1])
    pos = PAST_LEN + jnp.arange(T)

    def attend(q, gates):
        return nsa_attend(q, gates, pos, kc, vc, c_end, gather_sel, n_sel,
                          w_rows[:, :, 0], w_rows[:, :, 1], w_pos)

    return attend


def nsa_mixer(h, w_qg, w_o, attend):
    Bn, L, _ = h.shape
    q, gates = nsa_queries(h, w_qg)
    o = attend(q, gates).reshape(Bn, L, N_HEADS * HEAD_DIM)
    return (o @ w_o.astype(F32)).astype(h.dtype)


def trunk(x, c, h0, make_attend, layer_w, ssm_w, kv_w, w_qg, w_o):
    mod_w, mod_b, norm_pre, norm_post, mlp_w1, mlp_w2 = layer_w
    new_h = []
    for l in range(DEPTH):
        sh1, sc1, ga1, sh2, sc2, ga2 = jnp.split(modulation(c, mod_w[l], mod_b[l]), 6, axis=-1)
        if l == N_A_LAYERS:
            kv_rows = shared_kv_rows(x, c, *kv_w)
            attend = make_attend(*kv_rows)
        h = rmsnorm(x, norm_pre[l, 0]) * (1.0 + sc1) + sh1
        if l < N_A_LAYERS:
            m, h_last = s5_mixer(h, h0[l], *[w[l] for w in ssm_w])
            new_h.append(h_last)
        else:
            m = nsa_mixer(h, w_qg[l - N_A_LAYERS], w_o[l - N_A_LAYERS], attend)
        x = x + ga1 * rmsnorm(m, norm_post[l, 0])
        h = rmsnorm(x, norm_pre[l, 1]) * (1.0 + sc2) + sh2
        x = x + ga2 * rmsnorm(sqrelu_mlp(h, mlp_w1[l], mlp_w2[l]), norm_post[l, 1])
    return x, jnp.stack(new_h), kv_rows


def setup_inputs(seed: int = 0) -> dict:
    key = jax.random.key(seed)
    ks = iter(jax.random.split(key, 40))

    def nrm(shape, s=1.0):
        return s * jax.random.normal(next(ks), shape, F32)

    n_pages = PAST_LEN // PAGE_SIZE
    n_used = DEC_BATCH * n_pages
    n_phys = n_used + max(1, n_used // 4)
    win_buf = min(WINDOW, PAST_LEN)
    qg_cols = N_HEADS * HEAD_DIM + 3 * N_HEADS
    kv_cols = 3 * 2 * N_KV * HEAD_DIM
    dm = D_MODEL ** -0.5
    x_prompt = nrm((BATCH, SEQ, D_MODEL))
    x_sample = nrm((DEC_BATCH, DEC_SEQ, D_MODEL))
    c_prompt = nrm((BATCH, D_MODEL))
    c_sample = nrm((DEC_BATCH, D_MODEL))
    state_ssm = nrm((N_A_LAYERS, DEC_BATCH, N_GROUPS, STATE_DIM, 2), 0.5)
    cache_cmp = nrm((n_phys, PAGE_SIZE, 2, N_KV, HEAD_DIM))
    cache_slc = nrm((n_phys, PAGE_SIZE, 2, N_KV, HEAD_DIM))
    cache_win = nrm((DEC_BATCH, win_buf, 2, N_KV, HEAD_DIM))
    page_table = jax.random.permutation(next(ks), n_phys)[:n_used].reshape(DEC_BATCH, n_pages).astype(jnp.int32)
    return {
        'x_prompt': x_prompt,
        'x_sample': x_sample,
        'c_prompt': c_prompt,
        'c_sample': c_sample,
        'state_ssm': state_ssm,
        'cache_cmp': cache_cmp,
        'cache_slc': cache_slc,
        'cache_win': cache_win,
        'page_table': page_table,
        'mod_w': nrm((DEPTH, D_MODEL, 6 * D_MODEL), 0.5 * dm),
        'mod_b': nrm((DEPTH, 6 * D_MODEL), 0.02),
        'norm_pre': 1.0 + nrm((DEPTH, 2, D_MODEL), 0.05),
        'norm_post': 1.0 + nrm((DEPTH, 2, D_MODEL), 0.05),
        'mlp_w1': nrm((DEPTH, D_MODEL, D_FF), dm),
        'mlp_w2': nrm((DEPTH, D_FF, D_MODEL), D_FF ** -0.5),
        'ssm_lam_re': -0.5 + nrm((N_A_LAYERS, N_GROUPS, STATE_DIM), 0.01),
        'ssm_lam_im': jnp.pi * jnp.arange(STATE_DIM, dtype=F32) + nrm((N_A_LAYERS, N_GROUPS, STATE_DIM), 0.01),
        'ssm_log_dt': jax.random.uniform(next(ks), (N_A_LAYERS, N_GROUPS), F32,
                                         minval=math.log(DT_MIN), maxval=math.log(DT_MAX)),
        'ssm_b_re': nrm((N_A_LAYERS, N_GROUPS, STATE_DIM, SSM_GROUP), (2 * SSM_GROUP) ** -0.5),
        'ssm_b_im': nrm((N_A_LAYERS, N_GROUPS, STATE_DIM, SSM_GROUP), (2 * SSM_GROUP) ** -0.5),
        'ssm_c_re': nrm((N_A_LAYERS, N_GROUPS, SSM_GROUP, STATE_DIM), (2 * STATE_DIM) ** -0.5),
        'ssm_c_im': nrm((N_A_LAYERS, N_GROUPS, SSM_GROUP, STATE_DIM), (2 * STATE_DIM) ** -0.5),
        'ssm_d': nrm((N_A_LAYERS, D_MODEL)),
        'ssm_w_glu': nrm((N_A_LAYERS, D_MODEL, 2 * D_MODEL), dm),
        'kv_norm': 1.0 + nrm((D_MODEL,), 0.05),
        'kv_mod_w': nrm((D_MODEL, 2 * D_MODEL), 0.5 * dm),
        'kv_mod_b': nrm((2 * D_MODEL,), 0.02),
        'w_kv': nrm((D_MODEL, kv_cols), dm),
        'cmp_pe': nrm((2, CMP_BLOCK, HEAD_DIM), 0.1),
        'cmp_w1': nrm((2, CMP_BLOCK, HEAD_DIM, CMP_HID), (CMP_BLOCK * HEAD_DIM) ** -0.5),
        'cmp_b1': nrm((2, CMP_HID), 0.02),
        'cmp_w2': nrm((2, CMP_HID, HEAD_DIM), CMP_HID ** -0.5),
        'nsa_w_qg': nrm((N_B_LAYERS, D_MODEL, qg_cols), dm),
        'nsa_w_o': nrm((N_B_LAYERS, N_HEADS * HEAD_DIM, D_MODEL), (N_HEADS * HEAD_DIM) ** -0.5),
    }


def reference(x_prompt, x_sample, c_prompt, c_sample, state_ssm, cache_cmp, cache_slc, cache_win, page_table,
              mod_w, mod_b, norm_pre, norm_post, mlp_w1, mlp_w2,
              ssm_lam_re, ssm_lam_im, ssm_log_dt, ssm_b_re, ssm_b_im, ssm_c_re, ssm_c_im, ssm_d, ssm_w_glu,
              kv_norm, kv_mod_w, kv_mod_b, w_kv, cmp_pe, cmp_w1, cmp_b1, cmp_w2,
              nsa_w_qg, nsa_w_o):
    layer_w = (mod_w, mod_b, norm_pre, norm_post, mlp_w1, mlp_w2)
    ssm_w = (ssm_lam_re, ssm_lam_im, ssm_log_dt, ssm_b_re, ssm_b_im, ssm_c_re, ssm_c_im, ssm_d, ssm_w_glu)
    kv_w = (kv_norm, kv_mod_w, kv_mod_b, w_kv)

    def make_prompt(kv_c, kv_s, kv_w_rows):
        return prompt_attention(kv_c, kv_s, kv_w_rows, cmp_pe, cmp_w1, cmp_b1, cmp_w2)

    def make_sample(kv_c, kv_s, kv_w_rows):
        return sample_attention(kv_c, kv_s, kv_w_rows, cache_cmp, cache_slc, cache_win, page_table,
                                cmp_pe, cmp_w1, cmp_b1, cmp_w2)

    h0_prompt = jnp.zeros((N_A_LAYERS, x_prompt.shape[0], N_GROUPS, STATE_DIM, 2), F32)
    y_prompt, ssm_prompt, rows_prompt = trunk(x_prompt, c_prompt, h0_prompt, make_prompt,
                                              layer_w, ssm_w, kv_w, nsa_w_qg, nsa_w_o)
    y_sample, ssm_sample, rows_sample = trunk(x_sample, c_sample, state_ssm, make_sample,
                                              layer_w, ssm_w, kv_w, nsa_w_qg, nsa_w_o)
    cmp_prompt, slc_prompt, win_rows_prompt = rows_prompt
    cmp_sample, slc_sample, win_sample = rows_sample
    win_prompt = win_rows_prompt[:, -min(WINDOW, x_prompt.shape[1]):]
    return (y_prompt, y_sample, ssm_prompt, ssm_sample, cmp_prompt, cmp_sample,
            slc_prompt, slc_sample, win_prompt, win_sample)
```

```python
import functools
import math

import jax
import jax.numpy as jnp
import numpy as np
from jax import lax
from jax.experimental import pallas as pl
from jax.experimental.pallas import tpu as pltpu

F32 = jnp.float32
BF16 = jnp.bfloat16

SSM_GROUP = 16
STATE_DIM = 64
HEAD_DIM = 128
N_KV = 4
CMP_STRIDE = 16
CMP_BLOCK = 2 * CMP_STRIDE
SEL_BLOCK = 64
SUBS_PER_SEL = SEL_BLOCK // CMP_STRIDE
TOPK = 16
WINDOW = 512
QBLK = 128
PAGE_SIZE = 128
EPS = 1e-6
NEG = -1e30
FORCE_BONUS = 1e4

V7X_VMEM_LIMIT_BYTES = 56 * 1024 * 1024
HIGHEST = lax.Precision.HIGHEST


def _cparams(n_axes):
    return pltpu.CompilerParams(dimension_semantics=("arbitrary",) * n_axes,
                                vmem_limit_bytes=V7X_VMEM_LIMIT_BYTES)


def _pick(n, pref):
    if n <= pref:
        return n
    t = pref
    while n % t:
        t //= 2
    return t


def _norm_mod_kernel(x_ref, g_ref, sc_ref, sh_ref, o_ref):
    x = x_ref[...]
    r = lax.rsqrt(jnp.mean(x * x, axis=-1, keepdims=True) + EPS)
    y = (x * r) * g_ref[...]
    o_ref[...] = (y * (1.0 + sc_ref[...]) + sh_ref[...]).astype(o_ref.dtype)


def norm_mod(x, g, scale, shift, out_dtype):
    M, D = x.shape
    tm = _pick(M, 512)
    per_row = scale.shape[0] != 1
    mod_spec = pl.BlockSpec((tm, D), lambda i: (i, 0)) if per_row else pl.BlockSpec((1, D), lambda i: (0, 0))
    return pl.pallas_call(
        _norm_mod_kernel,
        grid=(M // tm,),
        in_specs=[pl.BlockSpec((tm, D), lambda i: (i, 0)),
                  pl.BlockSpec((1, D), lambda i: (0, 0)),
                  mod_spec, mod_spec],
        out_specs=pl.BlockSpec((tm, D), lambda i: (i, 0)),
        out_shape=jax.ShapeDtypeStruct((M, D), out_dtype),
        compiler_params=_cparams(1),
        name="norm_mod",
    )(x, g.reshape(1, D), scale, shift)


def _mm_wide_kernel(*refs, prologue, epilogue, has_bias, exact, n_w):
    a_ref = refs[0]
    w_refs = refs[1:1 + n_w]
    pos = 1 + n_w
    b_ref = refs[pos] if has_bias else None
    pos += int(has_bias)
    o_ref = refs[pos]
    wbf_refs = refs[pos + 1:]

    a = a_ref[...]
    if prologue == "silu":
        a = a * jax.nn.sigmoid(a)
    if exact:
        zs = [jnp.dot(a, w[...], preferred_element_type=F32, precision=HIGHEST) for w in w_refs]
    else:
        @pl.when(pl.program_id(1) == 0)
        def _():
            for w, wbf in zip(w_refs, wbf_refs):
                wbf[...] = w[...].astype(BF16)

        a = a.astype(BF16)
        zs = [jnp.dot(a, wbf[...], preferred_element_type=F32) for wbf in wbf_refs]
    z = zs[0]
    if has_bias:
        z = z + b_ref[...]
    if epilogue == "sqrelu":
        z = jnp.square(jnp.maximum(z, 0.0))
    elif epilogue == "sigmoid":
        z = jax.nn.sigmoid(z)
    elif epilogue == "glu":
        z = z * jax.nn.sigmoid(zs[1])
    o_ref[...] = z.astype(o_ref.dtype)


def mm_wide(a, w, *, layer=None, col0=0, n_out=None, bias=None, prologue=None, epilogue=None,
            exact=False, out_dtype=F32, tm=1024, tn=512):
    M, K = a.shape
    n_out = n_out if n_out is not None else w.shape[-1] - col0
    tm = _pick(M, tm)
    tn = _pick(n_out, tn)
    assert col0 % tn == 0 and n_out % tn == 0
    n_w = 2 if epilogue == "glu" else 1
    jb = col0 // tn

    def w_spec(extra):
        if layer is None:
            return pl.BlockSpec((K, tn), lambda j, i: (0, jb + extra + j))
        return pl.BlockSpec((None, K, tn), lambda j, i: (layer, 0, jb + extra + j))

    in_specs = [pl.BlockSpec((tm, K), lambda j, i: (i, 0))] + [w_spec(e * (n_out // tn)) for e in range(n_w)]
    args = [a] + [w] * n_w
    if bias is not None:
        if layer is None:
            in_specs.append(pl.BlockSpec((1, tn), lambda j, i: (0, jb + j)))
            args.append(bias.reshape(1, -1))
        else:
            in_specs.append(pl.BlockSpec((None, 1, tn), lambda j, i: (layer, 0, jb + j)))
            args.append(bias.reshape(bias.shape[0], 1, -1))
    scratch = [] if exact else [pltpu.VMEM((K, tn), BF16) for _ in range(n_w)]
    return pl.pallas_call(
        functools.partial(_mm_wide_kernel, prologue=prologue, epilogue=epilogue,
                          has_bias=bias is not None, exact=exact, n_w=n_w),
        grid=(n_out // tn, M // tm),
        in_specs=in_specs,
        out_specs=pl.BlockSpec((tm, tn), lambda j, i: (i, j)),
        out_shape=jax.ShapeDtypeStruct((M, n_out), out_dtype),
        scratch_shapes=scratch,
        compiler_params=_cparams(2),
        name="mm_wide",
    )(*args)


def _mm_tall_kernel(a_ref, *refs, n_w, nk):
    w_refs = refs[:n_w]
    res_ref, g_ref, ga_ref, o_ref = refs[n_w:n_w + 4]
    acc_refs = refs[n_w + 4:]
    k = pl.program_id(1)

    @pl.when(k == 0)
    def _():
        for acc in acc_refs:
            acc[...] = jnp.zeros_like(acc)

    a = a_ref[...].astype(BF16)
    for w, acc in zip(w_refs, acc_refs):
        acc[...] += jnp.dot(a, w[...].astype(BF16), preferred_element_type=F32)

    @pl.when(k == nk - 1)
    def _():
        m = acc_refs[0][...]
        if n_w == 2:
            m = m * jax.nn.sigmoid(acc_refs[1][...])
        r = lax.rsqrt(jnp.mean(m * m, axis=-1, keepdims=True) + EPS)
        o_ref[...] = res_ref[...] + ga_ref[...] * ((m * r) * g_ref[...])


def mm_tall(a, w, res, g, gate, *, layer=None, glu=False, tm=512, tk=256):
    M, K = a.shape
    N = res.shape[1]
    tm = _pick(M, tm)
    tk = _pick(K, tk)
    nk = K // tk
    n_w = 2 if glu else 1

    def w_spec(e):
        if layer is None:
            return pl.BlockSpec((tk, N), lambda i, k: (k, e))
        return pl.BlockSpec((None, tk, N), lambda i, k: (layer, k, e))

    per_row = gate.shape[0] != 1
    ga_spec = pl.BlockSpec((tm, N), lambda i, k: (i, 0)) if per_row else pl.BlockSpec((1, N), lambda i, k: (0, 0))
    return pl.pallas_call(
        functools.partial(_mm_tall_kernel, n_w=n_w, nk=nk),
        grid=(M // tm, nk),
        in_specs=[pl.BlockSpec((tm, tk), lambda i, k: (i, k))] + [w_spec(e) for e in range(n_w)]
        + [pl.BlockSpec((tm, N), lambda i, k: (i, 0)),
           pl.BlockSpec((1, N), lambda i, k: (0, 0)),
           ga_spec],
        out_specs=pl.BlockSpec((tm, N), lambda i, k: (i, 0)),
        out_shape=jax.ShapeDtypeStruct((M, N), F32),
        scratch_shapes=[pltpu.VMEM((tm, N), F32) for _ in range(n_w)],
        compiler_params=_cparams(2),
        name="mm_tall",
    )(a, *([w] * n_w), res, g.reshape(1, N), gate)


def alibi_slopes(n_heads):
    exps = np.arange(1, n_heads + 1, dtype=np.float32) * np.float32(-8.0 / n_heads)
    return jnp.asarray(np.exp2(exps), dtype=F32).reshape(N_KV, n_heads // N_KV)


def masked_softmax(s, mask):
    s = jnp.where(mask, s, NEG)
    m = jnp.max(s, axis=-1, keepdims=True)
    p = jnp.where(mask, jnp.exp(s - m), 0.0)
    return p / jnp.maximum(jnp.sum(p, axis=-1, keepdims=True), 1e-30)


def s5_core(u, h0, lam_re, lam_im, log_dt, b_re, b_im, c_re, c_im, d_skip):
    Bn, L, D = u.shape
    G = D // SSM_GROUP
    uf = u.reshape(Bn, L, G, SSM_GROUP)
    lam = lax.complex(lam_re, lam_im)
    dt = jnp.exp(log_dt)[:, None]
    a_bar = jnp.exp(lam * dt)
    b_c = lax.complex(b_re, b_im)
    b_bar = ((a_bar - 1.0) / lam)[..., None] * b_c
    bu = lax.complex(jnp.einsum('blgc,gpc->blgp', uf, b_bar.real),
                     jnp.einsum('blgc,gpc->blgp', uf, b_bar.imag))
    h_init = lax.complex(h0[..., 0], h0[..., 1])
    bu = bu.at[:, 0].add(a_bar * h_init)
    a = jnp.broadcast_to(a_bar, bu.shape)

    def combine(e1, e2):
        a1, b1 = e1
        a2, b2 = e2
        return a1 * a2, a2 * b1 + b2

    _, h = lax.associative_scan(combine, (a, bu), axis=1)
    y = (jnp.einsum('blgp,gcp->blgc', h.real, c_re) - jnp.einsum('blgp,gcp->blgc', h.imag, c_im))
    y = y.reshape(Bn, L, D) + d_skip * u
    h_last = jnp.stack([h[:, -1].real, h[:, -1].imag], axis=-1)
    return jax.nn.gelu(y), h_last


def cmp_partials(rows, pe, w1):
    Bn, L = rows.shape[:2]
    sub = rows.reshape(Bn, L // CMP_STRIDE, CMP_STRIDE, 2, N_KV, HEAD_DIM)
    w_lo = w1[:, :CMP_STRIDE]
    w_hi = w1[:, CMP_STRIDE:]
    pe_lo = jnp.einsum('crd,crde->ce', pe[:, :CMP_STRIDE], w_lo)
    pe_hi = jnp.einsum('crd,crde->ce', pe[:, CMP_STRIDE:], w_hi)
    lo = jnp.einsum('bnrckd,crde->bncke', sub, w_lo) + pe_lo[:, None, :]
    hi = jnp.einsum('bnrckd,crde->bncke', sub, w_hi) + pe_hi[:, None, :]
    return lo, hi


def compress(lo, hi, b1, w2):
    h = jax.nn.gelu(lo[:, :-1] + hi[:, 1:] + b1[:, None, :])
    kv = jnp.einsum('bncke,ced->bnckd', h, w2)
    return kv[:, :, 0], kv[:, :, 1]


def nsa_attend(q, gates, pos, kc, vc, c_end, gather_sel, n_sel, kw, vw, w_pos):
    Bn, T = q.shape[:2]
    hpg = q.shape[3]
    slopes = alibi_slopes(N_KV * hpg)
    scale = HEAD_DIM ** -0.5
    dist_c = (pos[:, None] - c_end[None, :]).astype(F32)
    s_c = (jnp.einsum('btghd,bngd->btghn', q, kc) * scale
           - slopes[None, None, :, :, None] * dist_c[None, :, None, None, :])
    p_c = masked_softmax(s_c, (dist_c >= 0)[None, :, None, None, :])
    o_c = jnp.einsum('btghn,bngd->btghd', p_c, vc)
    imp = p_c.sum(axis=3)
    n_cmp = imp.shape[-1]
    imp = jnp.pad(imp, ((0, 0), (0, 0), (0, 0), (1, SUBS_PER_SEL * (n_sel + 1) - 1 - n_cmp)))
    r = imp.reshape(Bn, T, N_KV, n_sel + 1, SUBS_PER_SEL)
    imp_sel = r[..., :n_sel, :].sum(-1) + r[..., 1:, 0]
    blk = pos // SEL_BLOCK
    j = jnp.arange(n_sel)
    forced = (j[None, :] == 0) | (j[None, :] == blk[:, None]) | (j[None, :] == blk[:, None] - 1)
    visible = j[None, :] <= blk[:, None]
    score = jnp.where(visible[None, :, None, :], imp_sel + FORCE_BONUS * forced[None, :, None, :], NEG)
    top_s, idx = lax.top_k(score, min(TOPK, n_sel))
    valid = top_s > 0.5 * NEG
    ks, vs = gather_sel(idx)
    n_k = idx.shape[-1]
    s_pos = idx[..., None] * SEL_BLOCK + jnp.arange(SEL_BLOCK)
    dist_s = (pos[None, :, None, None, None] - s_pos).astype(F32)
    mask_s = valid[..., None] & (dist_s >= 0)
    s_s = (jnp.einsum('btghd,btgksd->btghks', q, ks) * scale
           - slopes[None, None, :, :, None, None] * dist_s[:, :, :, None])
    p_s = masked_softmax(s_s.reshape(Bn, T, N_KV, hpg, n_k * SEL_BLOCK),
                         mask_s[:, :, :, None].reshape(Bn, T, N_KV, 1, n_k * SEL_BLOCK))
    o_s = jnp.einsum('btghm,btgmd->btghd', p_s, vs.reshape(Bn, T, N_KV, n_k * SEL_BLOCK, HEAD_DIM))
    dist_w = pos[:, None] - w_pos[None, :]
    mask_w = (dist_w >= 0) & (dist_w < WINDOW) & (w_pos[None, :] >= 0)
    s_w = (jnp.einsum('btghd,bmgd->btghm', q, kw) * scale
           - slopes[None, None, :, :, None] * dist_w.astype(F32)[None, :, None, None, :])
    p_w = masked_softmax(s_w, mask_w[None, :, None, None, :])
    o_w = jnp.einsum('btghm,bmgd->btghd', p_w, vw)
    return gates[..., 0:1] * o_c + gates[..., 1:2] * o_s + gates[..., 2:3] * o_w


def prompt_attention(kv_cmp, kv_slc, kv_win, cmp_pe, cmp_w1, cmp_b1, cmp_w2):
    Bn, L = kv_cmp.shape[:2]
    lo, hi = cmp_partials(kv_cmp, cmp_pe, cmp_w1)
    kc, vc = compress(lo, hi, cmp_b1, cmp_w2)
    c_end = jnp.arange(kc.shape[1]) * CMP_STRIDE + (CMP_BLOCK - 1)
    n_sel = L // SEL_BLOCK
    sel_blocks = kv_slc.reshape(Bn, n_sel, SEL_BLOCK, 2, N_KV, HEAD_DIM)
    b_i = jnp.arange(Bn)[:, None, None, None]
    g_i = jnp.arange(N_KV)[None, None, :, None]

    def gather_sel(idx):
        return (sel_blocks[b_i, idx, :, 0, g_i, :], sel_blocks[b_i, idx, :, 1, g_i, :])

    win_pad = jnp.pad(kv_win, ((0, 0), (WINDOW, 0), (0, 0), (0, 0), (0, 0)))
    n_blk = L // QBLK
    starts = jnp.arange(n_blk) * QBLK

    def attend(q, gates):
        hpg = q.shape[3]
        q_b = q.reshape(Bn, n_blk, QBLK, N_KV, hpg, HEAD_DIM).swapaxes(0, 1)
        g_b = gates.reshape(Bn, n_blk, QBLK, N_KV, hpg, 3).swapaxes(0, 1)

        def block(args):
            q_blk, g_blk, s0 = args
            w_rows = lax.dynamic_slice_in_dim(win_pad, s0, WINDOW + QBLK, axis=1)
            w_pos = s0 - WINDOW + jnp.arange(WINDOW + QBLK)
            return nsa_attend(q_blk, g_blk, s0 + jnp.arange(QBLK), kc, vc, c_end, gather_sel, n_sel,
                              w_rows[:, :, 0], w_rows[:, :, 1], w_pos)

        o = lax.map(block, (q_b, g_b, starts))
        return o.swapaxes(0, 1).reshape(Bn, L, N_KV, hpg, HEAD_DIM)

    return attend


def sample_attention(kv_cmp, kv_slc, kv_win, cache_cmp, cache_slc, cache_win, page_table,
                     cmp_pe, cmp_w1, cmp_b1, cmp_w2):
    Bn, T = kv_cmp.shape[:2]
    n_pages = page_table.shape[1]
    past_len = n_pages * PAGE_SIZE
    past_cmp = cache_cmp[page_table].reshape(Bn, past_len, 2, N_KV, HEAD_DIM)
    lo, hi = cmp_partials(past_cmp, cmp_pe, cmp_w1)
    n_new_sub = T // CMP_STRIDE
    if n_new_sub > 0:
        lo_new, hi_new = cmp_partials(kv_cmp[:, :n_new_sub * CMP_STRIDE], cmp_pe, cmp_w1)
        lo = jnp.concatenate([lo, lo_new], axis=1)
        hi = jnp.concatenate([hi, hi_new], axis=1)
    kc, vc = compress(lo, hi, cmp_b1, cmp_w2)
    c_end = jnp.arange(kc.shape[1]) * CMP_STRIDE + (CMP_BLOCK - 1)
    n_past_blk = past_len // SEL_BLOCK
    n_new_blk = -(-T // SEL_BLOCK)
    n_sel = n_past_blk + n_new_blk
    blk_per_page = PAGE_SIZE // SEL_BLOCK
    pool = cache_slc.reshape(cache_slc.shape[0], blk_per_page, SEL_BLOCK, 2, N_KV, HEAD_DIM)
    new_blocks = jnp.pad(kv_slc, ((0, 0), (0, n_new_blk * SEL_BLOCK - T), (0, 0), (0, 0), (0, 0)))
    new_blocks = new_blocks.reshape(Bn, n_new_blk, SEL_BLOCK, 2, N_KV, HEAD_DIM)
    b_i = jnp.arange(Bn)[:, None, None, None]
    g_i = jnp.arange(N_KV)[None, None, :, None]

    def gather_sel(idx):
        jp = jnp.minimum(idx, n_past_blk - 1)
        page = page_table[b_i, jp // blk_per_page]
        sub = jp % blk_per_page
        jn = jnp.clip(idx - n_past_blk, 0, n_new_blk - 1)
        is_new = (idx >= n_past_blk)[..., None, None]
        k = jnp.where(is_new, new_blocks[b_i, jn, :, 0, g_i, :], pool[page, sub, :, 0, g_i, :])
        v = jnp.where(is_new, new_blocks[b_i, jn, :, 1, g_i, :], pool[page, sub, :, 1, g_i, :])
        return k, v

    w_rows = jnp.concatenate([cache_win, kv_win], axis=1)
    w_pos = past_len - cache_win.shape[1] + jnp.arange(w_rows.shape[1])
    pos = past_len + jnp.arange(T)

    def attend(q, gates):
        return nsa_attend(q, gates, pos, kc, vc, c_end, gather_sel, n_sel,
                          w_rows[:, :, 0], w_rows[:, :, 1], w_pos)

    return attend


def _rows(v, per_tok):
    return v if v.shape[0] == 1 else jnp.repeat(v, per_tok, axis=0)


def trunk(x, mods, kv_mod, h0, make_attend, p):
    Bn, L, D = x.shape
    M = Bn * L
    n_heads = D // HEAD_DIM
    hpg = n_heads // N_KV
    depth = p["mod_w"].shape[0]
    n_a = depth // 2
    xr = x.reshape(M, D)
    new_h = []
    kv_rows = None
    attend = None
    for l in range(depth):
        sh1, sc1, ga1, sh2, sc2, ga2 = [_rows(m, L) for m in jnp.split(mods[l], 6, axis=-1)]
        if l == n_a:
            shift, scale = [_rows(m, L) for m in jnp.split(kv_mod, 2, axis=-1)]
            hk = norm_mod(xr, p["kv_norm"], scale, shift, BF16)
            kv = mm_wide(hk, p["w_kv"], tn=512).reshape(Bn, L, 3, 2, N_KV, HEAD_DIM)
            kv_rows = (kv[:, :, 0], kv[:, :, 1], kv[:, :, 2])
            attend = make_attend(*kv_rows)
        if l < n_a:
            u = norm_mod(xr, p["norm_pre"][l, 0], sc1, sh1, F32)
            gy, h_last = s5_core(u.reshape(Bn, L, D), h0[l], p["ssm_lam_re"][l], p["ssm_lam_im"][l],
                                 p["ssm_log_dt"][l], p["ssm_b_re"][l], p["ssm_b_im"][l],
                                 p["ssm_c_re"][l], p["ssm_c_im"][l], p["ssm_d"][l])
            new_h.append(h_last)
            xr = mm_tall(gy.reshape(M, D), p["ssm_w_glu"], xr, p["norm_post"][l, 0], ga1, layer=l, glu=True)
        else:
            lb = l - n_a
            h = norm_mod(xr, p["norm_pre"][l, 0], sc1, sh1, BF16)
            q = mm_wide(h, p["nsa_w_qg"], layer=lb, n_out=n_heads * HEAD_DIM, tn=512)
            gates = mm_wide(h, p["w_gate_pad"], layer=lb, epilogue="sigmoid", tn=128)[:, :3 * n_heads]
            o = attend(q.reshape(Bn, L, N_KV, hpg, HEAD_DIM), gates.reshape(Bn, L, N_KV, hpg, 3))
            xr = mm_tall(o.reshape(M, D), p["nsa_w_o"], xr, p["norm_post"][l, 0], ga1, layer=lb)
        h = norm_mod(xr, p["norm_pre"][l, 1], sc2, sh2, BF16)
        f = mm_wide(h, p["mlp_w1"], layer=l, epilogue="sqrelu", out_dtype=BF16, tn=1024)
        xr = mm_tall(f, p["mlp_w2"], xr, p["norm_post"][l, 1], ga2, layer=l, tk=512)
    return xr.reshape(Bn, L, D), jnp.stack(new_h), kv_rows


def kernel(x_prompt, x_sample, c_prompt, c_sample, state_ssm, cache_cmp, cache_slc, cache_win, page_table, mod_w, mod_b, norm_pre, norm_post, mlp_w1, mlp_w2, ssm_lam_re, ssm_lam_im, ssm_log_dt, ssm_b_re, ssm_b_im, ssm_c_re, ssm_c_im, ssm_d, ssm_w_glu, kv_norm, kv_mod_w, kv_mod_b, w_kv, cmp_pe, cmp_w1, cmp_b1, cmp_w2, nsa_w_qg, nsa_w_o):
    D = x_prompt.shape[-1]
    depth = mod_w.shape[0]
    n_heads = D // HEAD_DIM
    bp, bs = c_prompt.shape[0], c_sample.shape[0]
    c_all = jnp.concatenate([c_prompt, c_sample], axis=0)
    n_c = c_all.shape[0]
    c_all = jnp.pad(c_all, ((0, -n_c % 8), (0, 0)))
    mods = [mm_wide(c_all, mod_w, layer=l, bias=mod_b, prologue="silu", exact=True, tn=512) for l in range(depth)]
    kv_mod = mm_wide(c_all, kv_mod_w, bias=kv_mod_b, prologue="silu", exact=True, tn=512)
    w_gate_pad = jnp.pad(nsa_w_qg[:, :, n_heads * HEAD_DIM:], ((0, 0), (0, 0), (0, 128 - 3 * n_heads)))
    p = dict(mod_w=mod_w, norm_pre=norm_pre, norm_post=norm_post, mlp_w1=mlp_w1, mlp_w2=mlp_w2,
             ssm_lam_re=ssm_lam_re, ssm_lam_im=ssm_lam_im, ssm_log_dt=ssm_log_dt, ssm_b_re=ssm_b_re,
             ssm_b_im=ssm_b_im, ssm_c_re=ssm_c_re, ssm_c_im=ssm_c_im, ssm_d=ssm_d, ssm_w_glu=ssm_w_glu,
             kv_norm=kv_norm, w_kv=w_kv, nsa_w_qg=nsa_w_qg, nsa_w_o=nsa_w_o, w_gate_pad=w_gate_pad)

    def make_prompt(kv_c, kv_s, kv_w_rows):
        return prompt_attention(kv_c, kv_s, kv_w_rows, cmp_pe, cmp_w1, cmp_b1, cmp_w2)

    def make_sample(kv_c, kv_s, kv_w_rows):
        return sample_attention(kv_c, kv_s, kv_w_rows, cache_cmp, cache_slc, cache_win, page_table,
                                cmp_pe, cmp_w1, cmp_b1, cmp_w2)

    n_a = depth // 2
    G = D // SSM_GROUP
    h0_prompt = jnp.zeros((n_a, bp, G, STATE_DIM, 2), F32)
    y_prompt, ssm_prompt, rows_prompt = trunk(
        x_prompt, [m[:bp] for m in mods], kv_mod[:bp], h0_prompt, make_prompt, p)
    y_sample, ssm_sample, rows_sample = trunk(
        x_sample, [m[bp:bp + bs] for m in mods], kv_mod[bp:bp + bs], state_ssm, make_sample, p)
    cmp_prompt, slc_prompt, win_rows_prompt = rows_prompt
    cmp_sample, slc_sample, win_sample = rows_sample
    win_prompt = win_rows_prompt[:, -min(WINDOW, x_prompt.shape[1]):]
    return (y_prompt, y_sample, ssm_prompt, ssm_sample, cmp_prompt, cmp_sample,
            slc_prompt, slc_sample, win_prompt, win_sample)
```

```python
import functools
import math

import jax
import jax.numpy as jnp
import numpy as np
from jax import lax
from jax.experimental import pallas as pl
from jax.experimental.pallas import tpu as pltpu

F32 = jnp.float32
BF16 = jnp.bfloat16

SSM_GROUP = 16
STATE_DIM = 64
HEAD_DIM = 128
N_KV = 4
CMP_STRIDE = 16
CMP_BLOCK = 2 * CMP_STRIDE
SEL_BLOCK = 64
SUBS_PER_SEL = SEL_BLOCK // CMP_STRIDE
TOPK = 16
WINDOW = 512
QBLK = 128
PAGE_SIZE = 128
EPS = 1e-6
NEG = -1e30
FORCE_BONUS = 1e4

V7X_VMEM_LIMIT_BYTES = 56 * 1024 * 1024
HIGHEST = lax.Precision.HIGHEST


def _cparams(n_axes):
    return pltpu.CompilerParams(dimension_semantics=("arbitrary",) * n_axes,
                                vmem_limit_bytes=V7X_VMEM_LIMIT_BYTES)


def _pick(n, pref):
    if n <= pref:
        return n
    t = pref
    while n % t:
        t //= 2
    return t


def _norm_mod_kernel(x_ref, g_ref, sc_ref, sh_ref, o_ref):
    x = x_ref[...]
    r = lax.rsqrt(jnp.mean(x * x, axis=-1, keepdims=True) + EPS)
    y = (x * r) * g_ref[...]
    o_ref[...] = (y * (1.0 + sc_ref[...]) + sh_ref[...]).astype(o_ref.dtype)


def norm_mod(x, g, scale, shift, out_dtype):
    M, D = x.shape
    tm = _pick(M, 512)
    per_row = scale.shape[0] != 1
    mod_spec = pl.BlockSpec((tm, D), lambda i: (i, 0)) if per_row else pl.BlockSpec((1, D), lambda i: (0, 0))
    return pl.pallas_call(
        _norm_mod_kernel,
        grid=(M // tm,),
        in_specs=[pl.BlockSpec((tm, D), lambda i: (i, 0)),
                  pl.BlockSpec((1, D), lambda i: (0, 0)),
                  mod_spec, mod_spec],
        out_specs=pl.BlockSpec((tm, D), lambda i: (i, 0)),
        out_shape=jax.ShapeDtypeStruct((M, D), out_dtype),
        compiler_params=_cparams(1),
        name="norm_mod",
    )(x, g.reshape(1, D), scale, shift)


def _mm_wide_kernel(*refs, prologue, epilogue, has_bias, exact, n_w):
    a_ref = refs[0]
    w_refs = refs[1:1 + n_w]
    pos = 1 + n_w
    b_ref = refs[pos] if has_bias else None
    pos += int(has_bias)
    o_ref = refs[pos]
    wbf_refs = refs[pos + 1:]

    a = a_ref[...]
    if prologue == "silu":
        a = a * jax.nn.sigmoid(a)
    if exact:
        zs = [jnp.dot(a, w[...], preferred_element_type=F32, precision=HIGHEST) for w in w_refs]
    else:
        @pl.when(pl.program_id(1) == 0)
        def _():
            for w, wbf in zip(w_refs, wbf_refs):
                wbf[...] = w[...].astype(BF16)

        a = a.astype(BF16)
        zs = [jnp.dot(a, wbf[...], preferred_element_type=F32) for wbf in wbf_refs]
    z = zs[0]
    if has_bias:
        z = z + b_ref[...]
    if epilogue == "sqrelu":
        z = jnp.square(jnp.maximum(z, 0.0))
    elif epilogue == "sigmoid":
        z = jax.nn.sigmoid(z)
    elif epilogue == "glu":
        z = z * jax.nn.sigmoid(zs[1])
    o_ref[...] = z.astype(o_ref.dtype)


def mm_wide(a, w, *, layer=None, col0=0, n_out=None, bias=None, prologue=None, epilogue=None,
            exact=False, out_dtype=F32, tm=1024, tn=512):
    M, K = a.shape
    n_out = n_out if n_out is not None else w.shape[-1] - col0
    tm = _pick(M, tm)
    tn = _pick(n_out, tn)
    assert col0 % tn == 0 and n_out % tn == 0
    n_w = 2 if epilogue == "glu" else 1
    jb = col0 // tn

    def w_spec(extra):
        if layer is None:
            return pl.BlockSpec((K, tn), lambda j, i: (0, jb + extra + j))
        return pl.BlockSpec((None, K, tn), lambda j, i: (layer, 0, jb + extra + j))

    in_specs = [pl.BlockSpec((tm, K), lambda j, i: (i, 0))] + [w_spec(e * (n_out // tn)) for e in range(n_w)]
    args = [a] + [w] * n_w
    if bias is not None:
        if layer is None:
            in_specs.append(pl.BlockSpec((1, tn), lambda j, i: (0, jb + j)))
            args.append(bias.reshape(1, -1))
        else:
            in_specs.append(pl.BlockSpec((None, 1, tn), lambda j, i: (layer, 0, jb + j)))
            args.append(bias.reshape(bias.shape[0], 1, -1))
    scratch = [] if exact else [pltpu.VMEM((K, tn), BF16) for _ in range(n_w)]
    return pl.pallas_call(
        functools.partial(_mm_wide_kernel, prologue=prologue, epilogue=epilogue,
                          has_bias=bias is not None, exact=exact, n_w=n_w),
        grid=(n_out // tn, M // tm),
        in_specs=in_specs,
        out_specs=pl.BlockSpec((tm, tn), lambda j, i: (i, j)),
        out_shape=jax.ShapeDtypeStruct((M, n_out), out_dtype),
        scratch_shapes=scratch,
        compiler_params=_cparams(2),
        name="mm_wide",
    )(*args)


def _mm_tall_kernel(a_ref, *refs, n_w, nk):
    w_refs = refs[:n_w]
    res_ref, g_ref, ga_ref, o_ref = refs[n_w:n_w + 4]
    acc_refs = refs[n_w + 4:]
    k = pl.program_id(1)

    @pl.when(k == 0)
    def _():
        for acc in acc_refs:
            acc[...] = jnp.zeros_like(acc)

    a = a_ref[...].astype(BF16)
    for w, acc in zip(w_refs, acc_refs):
        acc[...] += jnp.dot(a, w[...].astype(BF16), preferred_element_type=F32)

    @pl.when(k == nk - 1)
    def _():
        m = acc_refs[0][...]
        if n_w == 2:
            m = m * jax.nn.sigmoid(acc_refs[1][...])
        r = lax.rsqrt(jnp.mean(m * m, axis=-1, keepdims=True) + EPS)
        o_ref[...] = res_ref[...] + ga_ref[...] * ((m * r) * g_ref[...])


def mm_tall(a, w, res, g, gate, *, layer=None, glu=False, tm=512, tk=256):
    M, K = a.shape
    N = res.shape[1]
    tm = _pick(M, tm)
    tk = _pick(K, tk)
    nk = K // tk
    n_w = 2 if glu else 1

    def w_spec(e):
        if layer is None:
            return pl.BlockSpec((tk, N), lambda i, k: (k, e))
        return pl.BlockSpec((None, tk, N), lambda i, k: (layer, k, e))

    per_row = gate.shape[0] != 1
    ga_spec = pl.BlockSpec((tm, N), lambda i, k: (i, 0)) if per_row else pl.BlockSpec((1, N), lambda i, k: (0, 0))
    return pl.pallas_call(
        functools.partial(_mm_tall_kernel, n_w=n_w, nk=nk),
        grid=(M // tm, nk),
        in_specs=[pl.BlockSpec((tm, tk), lambda i, k: (i, k))] + [w_spec(e) for e in range(n_w)]
        + [pl.BlockSpec((tm, N), lambda i, k: (i, 0)),
           pl.BlockSpec((1, N), lambda i, k: (0, 0)),
           ga_spec],
        out_specs=pl.BlockSpec((tm, N), lambda i, k: (i, 0)),
        out_shape=jax.ShapeDtypeStruct((M, N), F32),
        scratch_shapes=[pltpu.VMEM((tm, N), F32) for _ in range(n_w)],
        compiler_params=_cparams(2),
        name="mm_tall",
    )(a, *([w] * n_w), res, g.reshape(1, N), gate)


S5_GROUPS_PER_STEP = 8
P2 = 2 * STATE_DIM


def _s5_prep_kernel(lam_re_ref, lam_im_ref, ldt_ref, btr_ref, bti_ref, cr_ref, ci_ref,
                    mt_ref, bt_ref, cs_ref, at_ref, *, tc, gb):
    C = SSM_GROUP
    tcc = tc * C
    wk = max(tcc, 128)
    lane = lax.broadcasted_iota(jnp.int32, (C, P2), 1)
    is_re = lane < STATE_DIM
    kk = lax.broadcasted_iota(jnp.int32, (tc + 1, P2), 0).astype(F32)
    lane_k = lax.broadcasted_iota(jnp.int32, (C, wk), 1)
    for gg in range(gb):
        lr = lam_re_ref[gg]
        li = lam_im_ref[gg]
        dt = jnp.exp(ldt_ref[gg])
        mag = jnp.exp(kk * (lr * dt))
        ang = kk * (li * dt)
        pr = mag * jnp.cos(ang)
        pi = mag * jnp.sin(ang)
        x = pr[1:2] - 1.0
        y = pi[1:2]
        den = lr * lr + li * li
        cfr = (x * lr + y * li) / den
        cfi = (y * lr - x * li) / den
        btr = btr_ref[gg]
        bti = bti_ref[gg]
        bbr = cfr * btr - cfi * bti
        bbi = cfr * bti + cfi * btr
        cre = cr_ref[gg]
        cim = ci_ref[gg]

        def bm(k):
            return jnp.where(is_re, pr[k:k + 1] * bbr - pi[k:k + 1] * bbi, pr[k:k + 1] * bbi + pi[k:k + 1] * bbr)

        def cm(k):
            return jnp.where(is_re, cre * pr[k:k + 1] - cim * pi[k:k + 1], -(cre * pi[k:k + 1] + cim * pr[k:k + 1]))

        cms = [cm(k) for k in range(tc + 1)]
        cs0 = jnp.concatenate(cms[:tc] + [jnp.zeros((wk - tcc, P2), F32)] * (wk > tcc), axis=0)
        kst = lax.dot_general(bm(0), cs0, (((1,), (1,)), ((), ())), preferred_element_type=F32,
                              precision=HIGHEST)
        for s in range(tc):
            shifted = kst if s == 0 else jnp.where(lane_k >= s * C, pltpu.roll(kst, s * C, 1), 0.0)
            mt_ref[gg, s * C:(s + 1) * C, :] = shifted[:, :tcc]
            bt_ref[gg, s * C:(s + 1) * C, :] = bm(tc - 1 - s)
            cs_ref[gg, s * C:(s + 1) * C, :] = cms[s + 1]
        at_ref[gg, 0:1, :] = pr[tc:tc + 1]
        at_ref[gg, 1:2, :] = jnp.where(is_re[0:1], -pi[tc:tc + 1], pi[tc:tc + 1])


def s5_prep(lam_re, lam_im, log_dt, b_re, b_im, c_re, c_im, tc):
    G = lam_re.shape[0]
    gb = _pick(G, S5_GROUPS_PER_STEP)
    C = SSM_GROUP
    tcc = tc * C
    dup = lambda v: jnp.concatenate([v, v], axis=-1)
    lam_re2 = dup(lam_re)[:, None, :]
    lam_im2 = dup(lam_im)[:, None, :]
    ldt2 = jnp.broadcast_to(log_dt[:, None, None], (G, 1, P2))
    btr = dup(jnp.swapaxes(b_re, 1, 2))
    bti = dup(jnp.swapaxes(b_im, 1, 2))
    cr2 = dup(c_re)
    ci2 = dup(c_im)
    vec = pl.BlockSpec((gb, 1, P2), lambda i: (i, 0, 0))
    mat = pl.BlockSpec((gb, C, P2), lambda i: (i, 0, 0))
    return pl.pallas_call(
        functools.partial(_s5_prep_kernel, tc=tc, gb=gb),
        grid=(G // gb,),
        in_specs=[vec, vec, vec, mat, mat, mat, mat],
        out_specs=[pl.BlockSpec((gb, tcc, tcc), lambda i: (i, 0, 0)),
                   pl.BlockSpec((gb, tcc, P2), lambda i: (i, 0, 0)),
                   pl.BlockSpec((gb, tcc, P2), lambda i: (i, 0, 0)),
                   pl.BlockSpec((gb, 2, P2), lambda i: (i, 0, 0))],
        out_shape=[jax.ShapeDtypeStruct((G, tcc, tcc), F32),
                   jax.ShapeDtypeStruct((G, tcc, P2), F32),
                   jax.ShapeDtypeStruct((G, tcc, P2), F32),
                   jax.ShapeDtypeStruct((G, 2, P2), F32)],
        compiler_params=_cparams(1),
        name="s5_prep",
    )(lam_re2, lam_im2, ldt2, btr, bti, cr2, ci2)


def _gmm_kernel(*refs, gb, two):
    if two:
        a_ref, w_ref, a2_ref, w2_ref, o_ref = refs
    else:
        a_ref, w_ref, o_ref = refs
    for gg in range(gb):
        z = jnp.dot(a_ref[gg], w_ref[gg], preferred_element_type=F32, precision=HIGHEST)
        if two:
            z = z + lax.dot_general(a2_ref[gg], w2_ref[gg], (((1,), (1,)), ((), ())),
                                    preferred_element_type=F32, precision=HIGHEST)
        o_ref[gg] = z


def gmm(a, w, a2=None, w2=None):
    G, R, K = a.shape
    N = w.shape[2]
    gb = _pick(G, S5_GROUPS_PER_STEP)
    two = a2 is not None
    spec = lambda arr: pl.BlockSpec((gb,) + arr.shape[1:], lambda i: (i, 0, 0))
    args = [a, w] + ([a2, w2] if two else [])
    return pl.pallas_call(
        functools.partial(_gmm_kernel, gb=gb, two=two),
        grid=(G // gb,),
        in_specs=[spec(x) for x in args],
        out_specs=pl.BlockSpec((gb, R, N), lambda i: (i, 0, 0)),
        out_shape=jax.ShapeDtypeStruct((G, R, N), F32),
        compiler_params=_cparams(1),
        name="s5_gmm",
    )(*args)


S5_CHUNKS_PER_STEP = 64


def _s5_scan_kernel(s_ref, h0_ref, at_ref, hin_ref, hfin_ref, h_scr, *, cb, n_steps):
    j = pl.program_id(1)

    @pl.when(j == 0)
    def _():
        h_scr[...] = h0_ref[...]

    ar = at_ref[0]
    ai = at_ref[1]

    def step(c, h):
        hin_ref[c] = h
        return ar * h + ai * pltpu.roll(h, STATE_DIM, 1) + s_ref[c]

    h = lax.fori_loop(0, cb, step, h_scr[...])
    h_scr[...] = h

    @pl.when(j == n_steps - 1)
    def _():
        hfin_ref[...] = h


def s5_scan(s, h0, at):
    Bn, n_chunk, G, _ = s.shape
    cb = _pick(n_chunk, S5_CHUNKS_PER_STEP)
    n_steps = n_chunk // cb
    return pl.pallas_call(
        functools.partial(_s5_scan_kernel, cb=cb, n_steps=n_steps),
        grid=(Bn, n_steps),
        in_specs=[pl.BlockSpec((None, cb, G, P2), lambda b, j: (b, j, 0, 0)),
                  pl.BlockSpec((None, G, P2), lambda b, j: (b, 0, 0)),
                  pl.BlockSpec((2, G, P2), lambda b, j: (0, 0, 0))],
        out_specs=[pl.BlockSpec((None, cb, G, P2), lambda b, j: (b, j, 0, 0)),
                   pl.BlockSpec((None, G, P2), lambda b, j: (b, 0, 0))],
        out_shape=[jax.ShapeDtypeStruct((Bn, n_chunk, G, P2), F32),
                   jax.ShapeDtypeStruct((Bn, G, P2), F32)],
        scratch_shapes=[pltpu.VMEM((G, P2), F32)],
        compiler_params=_cparams(2),
        name="s5_scan",
    )(s, h0, at)


def _s5_out_kernel(y_ref, u_ref, d_ref, o_ref):
    o_ref[...] = jax.nn.gelu(y_ref[...] + d_ref[...] * u_ref[...]).astype(o_ref.dtype)


def s5_out(y, u, d_skip):
    M, D = y.shape
    tm = _pick(M, 512)
    row = pl.BlockSpec((tm, D), lambda i: (i, 0))
    return pl.pallas_call(
        _s5_out_kernel,
        grid=(M // tm,),
        in_specs=[row, row, pl.BlockSpec((1, D), lambda i: (0, 0))],
        out_specs=row,
        out_shape=jax.ShapeDtypeStruct((M, D), BF16),
        compiler_params=_cparams(1),
        name="s5_out",
    )(y, u, d_skip.reshape(1, D))


def s5_mixer_core(u, h0, lam_re, lam_im, log_dt, b_re, b_im, c_re, c_im, d_skip):
    Bn, L, D = u.shape
    G = D // SSM_GROUP
    C = SSM_GROUP
    tc = _pick(L, 16)
    n_chunk = L // tc
    mt, bt, cs, at = s5_prep(lam_re, lam_im, log_dt, b_re, b_im, c_re, c_im, tc)
    x = u.reshape(Bn, n_chunk, tc, G, C).transpose(3, 0, 1, 2, 4).reshape(G, Bn * n_chunk, tc * C)
    s = gmm(x, bt)
    s = s.reshape(G, Bn, n_chunk, P2).transpose(1, 2, 0, 3)
    h0v = jnp.concatenate([h0[..., 0], h0[..., 1]], axis=-1)
    hin, hfin = s5_scan(s, h0v, at.transpose(1, 0, 2))
    hin = hin.transpose(2, 0, 1, 3).reshape(G, Bn * n_chunk, P2)
    y = gmm(x, mt, hin, cs)
    y = y.reshape(G, Bn, n_chunk, tc, C).transpose(1, 2, 3, 0, 4).reshape(Bn * L, D)
    h_last = jnp.stack([hfin[..., :STATE_DIM], hfin[..., STATE_DIM:]], axis=-1)
    return s5_out(y, u.reshape(Bn * L, D), d_skip), h_last


SUBS_PER_PAGE = PAGE_SIZE // CMP_STRIDE
KV_COLS = 2 * N_KV * HEAD_DIM
CMP_PAGES_PER_STEP = 8


def _cmp_lohi_kernel(pt_ref, *refs, n_pg):
    x_refs = refs[:n_pg]
    w_ref = refs[n_pg]
    o_ref = refs[n_pg + 1]
    rows = n_pg * SUBS_PER_PAGE
    for c in range(2):
        acc = jnp.zeros((N_KV * rows, 2 * HEAD_DIM), F32)
        for rp in range(CMP_STRIDE // 2):
            parts = []
            for k in range(N_KV):
                col = (c * N_KV + k) * HEAD_DIM
                for p in range(n_pg):
                    a0 = x_refs[p][:, 2 * rp, col:col + HEAD_DIM]
                    a1 = x_refs[p][:, 2 * rp + 1, col:col + HEAD_DIM]
                    parts.append(jnp.concatenate([a0, a1], axis=1))
            xs = jnp.concatenate(parts, axis=0).astype(BF16)
            acc = acc + jnp.dot(xs, w_ref[c, rp], preferred_element_type=F32)
        for k in range(N_KV):
            o_ref[c, k] = acc[k * rows:(k + 1) * rows]


def cmp_lohi(pages, page_table, cmp_w1):
    Bn, n_pages = page_table.shape
    n_pg = _pick(n_pages, CMP_PAGES_PER_STEP)
    x = pages.reshape(pages.shape[0], SUBS_PER_PAGE, CMP_STRIDE, KV_COLS)
    half = CMP_STRIDE // 2
    w_lo = cmp_w1[:, :CMP_STRIDE].reshape(2, half, 2 * HEAD_DIM, HEAD_DIM)
    w_hi = cmp_w1[:, CMP_STRIDE:].reshape(2, half, 2 * HEAD_DIM, HEAD_DIM)
    w = jnp.concatenate([w_lo, w_hi], axis=-1).astype(BF16)
    n_sub = n_pages * SUBS_PER_PAGE

    def x_spec(p):
        return pl.BlockSpec((None, SUBS_PER_PAGE, CMP_STRIDE, KV_COLS),
                            lambda b, i, pt: (pt[b, i * n_pg + p], 0, 0, 0))

    grid_spec = pltpu.PrefetchScalarGridSpec(
        num_scalar_prefetch=1,
        grid=(Bn, n_pages // n_pg),
        in_specs=[x_spec(p) for p in range(n_pg)]
        + [pl.BlockSpec(w.shape, lambda b, i, pt: (0, 0, 0, 0))],
        out_specs=pl.BlockSpec((None, 2, N_KV, n_pg * SUBS_PER_PAGE, 2 * HEAD_DIM),
                               lambda b, i, pt: (b, 0, 0, i, 0)),
    )
    return pl.pallas_call(
        functools.partial(_cmp_lohi_kernel, n_pg=n_pg),
        grid_spec=grid_spec,
        out_shape=jax.ShapeDtypeStruct((Bn, 2, N_KV, n_sub, 2 * HEAD_DIM), F32),
        compiler_params=_cparams(2),
        name="cmp_lohi",
    )(page_table, *([x] * n_pg), w)


def _compress_kernel(x_ref, pe_ref, w1_ref, b1_ref, w2_ref, o_ref):
    n_sub = x_ref.shape[0]
    half = w1_ref.shape[0] // 2
    pe = jnp.broadcast_to(pe_ref[...], (8, 2 * half))
    pe_lo = jnp.dot(pe[:, :half], w1_ref[:half], preferred_element_type=F32, precision=HIGHEST)[0:1]
    pe_hi = jnp.dot(pe[:, half:], w1_ref[half:], preferred_element_type=F32, precision=HIGHEST)[0:1]
    x = x_ref[...]
    lo = x[:, :HEAD_DIM] + pe_lo
    hi = x[:, HEAD_DIM:] + pe_hi
    hi_next = pltpu.roll(hi, n_sub - 1, 0)
    h = jax.nn.gelu(lo + hi_next + b1_ref[...])
    o_ref[...] = jnp.dot(h.astype(BF16), w2_ref[...].astype(BF16), preferred_element_type=F32).astype(o_ref.dtype)


def compress_blocks(lohi, cmp_pe, cmp_w1, cmp_b1, cmp_w2):
    Bn, _, _, n_sub, _ = lohi.shape
    kdim = CMP_BLOCK * HEAD_DIM
    return pl.pallas_call(
        _compress_kernel,
        grid=(Bn, 2, N_KV),
        in_specs=[pl.BlockSpec((None, None, None, n_sub, 2 * HEAD_DIM), lambda b, c, k: (b, c, k, 0, 0)),
                  pl.BlockSpec((None, 1, kdim), lambda b, c, k: (c, 0, 0)),
                  pl.BlockSpec((None, kdim, HEAD_DIM), lambda b, c, k: (c, 0, 0)),
                  pl.BlockSpec((None, 1, HEAD_DIM), lambda b, c, k: (c, 0, 0)),
                  pl.BlockSpec((None, HEAD_DIM, HEAD_DIM), lambda b, c, k: (c, 0, 0))],
        out_specs=pl.BlockSpec((None, None, None, n_sub, HEAD_DIM), lambda b, c, k: (b, c, k, 0, 0)),
        out_shape=jax.ShapeDtypeStruct((Bn, 2, N_KV, n_sub, HEAD_DIM), BF16),
        compiler_params=_cparams(3),
        name="compress",
    )(lohi, cmp_pe.reshape(2, 1, kdim), cmp_w1.reshape(2, kdim, HEAD_DIM),
      cmp_b1.reshape(2, 1, HEAD_DIM), cmp_w2)


N_CAND = 128
NEG_TAKEN = -3e38


def _topk_mask(score, axis, k_sel):
    idx = lax.broadcasted_iota(jnp.int32, score.shape, axis)
    sel = jnp.zeros(score.shape, F32)
    for _ in range(k_sel):
        m = jnp.max(score, axis=axis, keepdims=True)
        first = jnp.min(jnp.where(score == m, idx, N_CAND), axis=axis, keepdims=True)
        hit = idx == first
        sel = jnp.where(hit & (m > 0.5 * NEG), 1.0, sel)
        score = jnp.where(hit, NEG_TAKEN, score)
    return sel


def _cmp_attn_kernel(slope_ref, q_ref, kc_ref, vc_ref, gate_ref, wsel_ref, oc_ref, sel_ref, *,
                     tq, hpg, pos0, pos_step, t_real, transposed, k_sel, n_cand):
    i = pl.program_id(1)
    g = pl.program_id(2)
    n_cmp = kc_ref.shape[0]
    scale = HEAD_DIM ** -0.5
    q = q_ref[...]
    qs = jnp.concatenate([q[:, h * HEAD_DIM:(h + 1) * HEAD_DIM] for h in range(hpg)], axis=0)
    qs = (qs * scale).astype(BF16)
    s = lax.dot_general(qs, kc_ref[...], (((1,), (1,)), ((), ())), preferred_element_type=F32)
    base = pos0 + i * pos_step
    t_idx = lax.broadcasted_iota(jnp.int32, (tq, n_cmp), 0)
    n_idx = lax.broadcasted_iota(jnp.int32, (tq, n_cmp), 1)
    dist_i = base + t_idx - (n_idx * CMP_STRIDE + (CMP_BLOCK - 1))
    valid = dist_i >= 0
    dist = dist_i.astype(F32)
    gates = gate_ref[...]
    vc = vc_ref[...]
    psum = jnp.zeros((tq, n_cmp), F32)
    for h in range(hpg):
        sh = s[h * tq:(h + 1) * tq] - slope_ref[g * hpg + h] * dist
        sh = jnp.where(valid, sh, NEG)
        m = jnp.max(sh, axis=-1, keepdims=True)
        e = jnp.where(valid, jnp.exp(sh - m), 0.0)
        p = e / jnp.maximum(jnp.sum(e, axis=-1, keepdims=True), 1e-30)
        psum = psum + p
        o_h = jnp.dot(p.astype(BF16), vc, preferred_element_type=F32)
        oc_ref[:, h * HEAD_DIM:(h + 1) * HEAD_DIM] = o_h * gates[:, 3 * h:3 * h + 1]
    if transposed:
        imp = lax.dot_general(wsel_ref[...], psum, (((1,), (1,)), ((), ())),
                              preferred_element_type=F32, precision=HIGHEST)
        shape, j_ax, t_ax = (N_CAND, tq), 0, 1
    else:
        imp = lax.dot_general(psum, wsel_ref[...], (((1,), (1,)), ((), ())),
                              preferred_element_type=F32, precision=HIGHEST)
        shape, j_ax, t_ax = (tq, N_CAND), 1, 0
    j = lax.broadcasted_iota(jnp.int32, shape, j_ax)
    blk = (base + lax.broadcasted_iota(jnp.int32, shape, t_ax)) // SEL_BLOCK
    forced = (j == 0) | (j == blk) | (j == blk - 1)
    visible = (j <= blk) & (j < n_cand)
    score = jnp.where(visible, imp + jnp.where(forced, FORCE_BONUS, 0.0), NEG)
    sel = _topk_mask(score, j_ax, k_sel)
    if transposed:
        sel = sel.T
    sel_ref[...] = sel.astype(sel_ref.dtype)


def _sel_weights(n_cmp_pad, n_cmp):
    j = np.arange(N_CAND)[:, None]
    n = np.arange(n_cmp_pad)[None, :]
    w = (n >= SUBS_PER_SEL * j - 1) & (n <= SUBS_PER_SEL * j + SUBS_PER_SEL - 1) & (n < n_cmp)
    return jnp.asarray(w.astype(np.float32))


def cmp_attention(q, kcvc, gates, *, tq, pos0, pos_step, transposed, k_sel, n_cand):
    Bn, T, D = q.shape
    n_heads = D // HEAD_DIM
    hpg = n_heads // N_KV
    n_sub = kcvc.shape[3]
    gw = hpg * HEAD_DIM
    slopes = alibi_slopes(n_heads).reshape(-1)
    wsel = _sel_weights(n_sub, n_sub - 1)
    kern = functools.partial(_cmp_attn_kernel, tq=tq, hpg=hpg, pos0=pos0, pos_step=pos_step, t_real=T,
                             transposed=transposed, k_sel=k_sel, n_cand=n_cand)
    return pl.pallas_call(
        kern,
        grid=(Bn, T // tq, N_KV),
        in_specs=[pl.BlockSpec(memory_space=pltpu.SMEM),
                  pl.BlockSpec((None, tq, gw), lambda b, i, g: (b, i, g)),
                  pl.BlockSpec((None, None, None, n_sub, HEAD_DIM), lambda b, i, g: (b, 0, g, 0, 0)),
                  pl.BlockSpec((None, None, None, n_sub, HEAD_DIM), lambda b, i, g: (b, 1, g, 0, 0)),
                  pl.BlockSpec((None, tq, 128), lambda b, i, g: (b, i, g)),
                  pl.BlockSpec((N_CAND, n_sub), lambda b, i, g: (0, 0))],
        out_specs=[pl.BlockSpec((None, tq, gw), lambda b, i, g: (b, i, g)),
                   pl.BlockSpec((None, tq, N_CAND), lambda b, i, g: (b, i, g))],
        out_shape=[jax.ShapeDtypeStruct((Bn, T, D), F32),
                   jax.ShapeDtypeStruct((Bn, T, N_KV * N_CAND), BF16)],
        compiler_params=_cparams(3),
        name="cmp_attn",
    )(slopes, q, kcvc, kcvc, gates, wsel)


SEL_TK = 512
MASK_BIG = 1e30


def _sel_win_kernel(slope_ref, q_ref, ks_ref, vs_ref, kw_ref, vw_ref, sel_ref, oc_ref, gate_ref, e_ref,
                    o_ref, ksb, vsb, kwb, vwb, *, tq, hpg):
    g = pl.program_id(0)
    i = pl.program_id(1)
    L = ks_ref.shape[0]
    s0 = i * tq
    scale = HEAD_DIM ** -0.5

    @pl.when(i == 0)
    def _():
        ksb[...] = ks_ref[...].astype(BF16)
        vsb[...] = vs_ref[...].astype(BF16)
        kwb[0:WINDOW] = jnp.zeros((WINDOW, HEAD_DIM), BF16)
        vwb[0:WINDOW] = jnp.zeros((WINDOW, HEAD_DIM), BF16)
        kwb[WINDOW:] = kw_ref[...].astype(BF16)
        vwb[WINDOW:] = vw_ref[...].astype(BF16)

    q = q_ref[...]
    qs = jnp.concatenate([q[:, h * HEAD_DIM:(h + 1) * HEAD_DIM] for h in range(hpg)], axis=0)
    qs = (qs * scale).astype(BF16)
    slopes = [slope_ref[g * hpg + h] for h in range(hpg)]
    neg_sel = sel_ref[...] - 1.0
    c_iota = lax.broadcasted_iota(jnp.int32, (1, SEL_TK), 1)
    kd = s0 // SEL_TK

    def sweep(kt, carry, diagonal):
        ms, ls, accs = carry
        k0 = pl.multiple_of(kt * SEL_TK, SEL_TK)
        k_t = ksb[pl.ds(k0, SEL_TK), :]
        v_t = vsb[pl.ds(k0, SEL_TK), :]
        s = lax.dot_general(qs, k_t, (((1,), (1,)), ((), ())), preferred_element_type=F32)
        addm = jnp.dot(neg_sel, e_ref[kt], preferred_element_type=F32)
        if diagonal:
            t_idx = lax.broadcasted_iota(jnp.int32, (tq, SEL_TK), 0)
            c_idx = lax.broadcasted_iota(jnp.int32, (tq, SEL_TK), 1)
            addm = jnp.where(s0 + t_idx >= k0 + c_idx, addm, -MASK_BIG)
        rel = (k0 - s0 + c_iota).astype(F32)
        new_m, new_l, new_acc = [], [], []
        for h in range(hpg):
            sh = s[h * tq:(h + 1) * tq] + slopes[h] * rel + addm
            m_new = jnp.maximum(ms[h], jnp.max(sh, axis=-1, keepdims=True))
            p = jnp.exp(sh - m_new)
            alpha = jnp.exp(ms[h] - m_new)
            new_l.append(alpha * ls[h] + jnp.sum(p, axis=-1, keepdims=True))
            new_acc.append(alpha * accs[h] + jnp.dot(p.astype(BF16), v_t, preferred_element_type=F32))
            new_m.append(m_new)
        return new_m, new_l, new_acc

    init = ([jnp.full((tq, 1), NEG, F32)] * hpg, [jnp.zeros((tq, 1), F32)] * hpg,
            [jnp.zeros((tq, HEAD_DIM), F32)] * hpg)
    carry = lax.fori_loop(0, kd, lambda kt, c: tuple(sweep(kt, c, False)), tuple(init))
    ms, ls, accs = sweep(kd, carry, True)

    wn = WINDOW + tq
    w0 = pl.multiple_of(s0, tq)
    kw_t = kwb[pl.ds(w0, wn), :]
    vw_t = vwb[pl.ds(w0, wn), :]
    sw = lax.dot_general(qs, kw_t, (((1,), (1,)), ((), ())), preferred_element_type=F32)
    t_idx = lax.broadcasted_iota(jnp.int32, (tq, wn), 0)
    c_idx = lax.broadcasted_iota(jnp.int32, (tq, wn), 1)
    dist_i = t_idx + WINDOW - c_idx
    valid = (dist_i >= 0) & (dist_i < WINDOW) & (c_idx + s0 >= WINDOW)
    dist = dist_i.astype(F32)
    gates = gate_ref[...]
    oc = oc_ref[...]
    for h in range(hpg):
        sh = jnp.where(valid, sw[h * tq:(h + 1) * tq] - slopes[h] * dist, NEG)
        m = jnp.max(sh, axis=-1, keepdims=True)
        e = jnp.where(valid, jnp.exp(sh - m), 0.0)
        p = e / jnp.maximum(jnp.sum(e, axis=-1, keepdims=True), 1e-30)
        o_w = jnp.dot(p.astype(BF16), vw_t, preferred_element_type=F32)
        o_s = accs[h] / jnp.maximum(ls[h], 1e-30)
        out = (oc[:, h * HEAD_DIM:(h + 1) * HEAD_DIM] + gates[:, 3 * h + 1:3 * h + 2] * o_s
               + gates[:, 3 * h + 2:3 * h + 3] * o_w)
        o_ref[:, h * HEAD_DIM:(h + 1) * HEAD_DIM] = out.astype(o_ref.dtype)


def _block_expand(n_tiles):
    t = np.arange(n_tiles)[:, None, None]
    j = np.arange(N_CAND)[None, :, None]
    c = np.arange(SEL_TK)[None, None, :]
    return jnp.asarray(((t * SEL_TK + c) // SEL_BLOCK == j).astype(np.float32) * MASK_BIG, dtype=BF16)


def sel_win_attention(q, kv, sel, oc, gates, *, tq):
    _, L, D = q.shape
    n_heads = D // HEAD_DIM
    hpg = n_heads // N_KV
    gw = hpg * HEAD_DIM
    assert L % SEL_TK == 0 and SEL_TK % tq == 0 and L // SEL_BLOCK <= N_CAND
    slopes = alibi_slopes(n_heads).reshape(-1)
    e = _block_expand(L // SEL_TK)

    def kv_spec(branch, which):
        cb = (branch * 2 + which) * N_KV
        return pl.BlockSpec((L, HEAD_DIM), lambda g, i: (0, cb + g))

    return pl.pallas_call(
        functools.partial(_sel_win_kernel, tq=tq, hpg=hpg),
        grid=(N_KV, L // tq),
        in_specs=[pl.BlockSpec(memory_space=pltpu.SMEM),
                  pl.BlockSpec((None, tq, gw), lambda g, i: (0, i, g)),
                  kv_spec(1, 0), kv_spec(1, 1), kv_spec(2, 0), kv_spec(2, 1),
                  pl.BlockSpec((None, tq, N_CAND), lambda g, i: (0, i, g)),
                  pl.BlockSpec((None, tq, gw), lambda g, i: (0, i, g)),
                  pl.BlockSpec((None, tq, 128), lambda g, i: (0, i, g)),
                  pl.BlockSpec(e.shape, lambda g, i: (0, 0, 0))],
        out_specs=pl.BlockSpec((tq, gw), lambda g, i: (i, g)),
        out_shape=jax.ShapeDtypeStruct((L, D), BF16),
        scratch_shapes=[pltpu.VMEM((L, HEAD_DIM), BF16), pltpu.VMEM((L, HEAD_DIM), BF16),
                        pltpu.VMEM((WINDOW + L, HEAD_DIM), BF16), pltpu.VMEM((WINDOW + L, HEAD_DIM), BF16)],
        compiler_params=_cparams(2),
        name="sel_win_attn",
    )(slopes, q, kv, kv, kv, kv, sel, oc, gates, e)


def prompt_nsa(q, gates, kv, kcvc):
    L = q.shape[1]
    n_sel = L // SEL_BLOCK
    oc, sel = cmp_attention(q, kcvc, gates, tq=QBLK, pos0=0, pos_step=QBLK, transposed=True,
                            k_sel=min(TOPK, n_sel), n_cand=n_sel)
    return sel_win_attention(q, kv, sel, oc, gates, tq=QBLK)


def alibi_slopes(n_heads):
    exps = np.arange(1, n_heads + 1, dtype=np.float32) * np.float32(-8.0 / n_heads)
    return jnp.asarray(np.exp2(exps), dtype=F32).reshape(N_KV, n_heads // N_KV)


def masked_softmax(s, mask):
    s = jnp.where(mask, s, NEG)
    m = jnp.max(s, axis=-1, keepdims=True)
    p = jnp.where(mask, jnp.exp(s - m), 0.0)
    return p / jnp.maximum(jnp.sum(p, axis=-1, keepdims=True), 1e-30)


def s5_core(u, h0, lam_re, lam_im, log_dt, b_re, b_im, c_re, c_im, d_skip):
    Bn, L, D = u.shape
    G = D // SSM_GROUP
    uf = u.reshape(Bn, L, G, SSM_GROUP)
    lam = lax.complex(lam_re, lam_im)
    dt = jnp.exp(log_dt)[:, None]
    a_bar = jnp.exp(lam * dt)
    b_c = lax.complex(b_re, b_im)
    b_bar = ((a_bar - 1.0) / lam)[..., None] * b_c
    bu = lax.complex(jnp.einsum('blgc,gpc->blgp', uf, b_bar.real),
                     jnp.einsum('blgc,gpc->blgp', uf, b_bar.imag))
    h_init = lax.complex(h0[..., 0], h0[..., 1])
    bu = bu.at[:, 0].add(a_bar * h_init)
    a = jnp.broadcast_to(a_bar, bu.shape)

    def combine(e1, e2):
        a1, b1 = e1
        a2, b2 = e2
        return a1 * a2, a2 * b1 + b2

    _, h = lax.associative_scan(combine, (a, bu), axis=1)
    y = (jnp.einsum('blgp,gcp->blgc', h.real, c_re) - jnp.einsum('blgp,gcp->blgc', h.imag, c_im))
    y = y.reshape(Bn, L, D) + d_skip * u
    h_last = jnp.stack([h[:, -1].real, h[:, -1].imag], axis=-1)
    return jax.nn.gelu(y), h_last


def cmp_partials(rows, pe, w1):
    Bn, L = rows.shape[:2]
    sub = rows.reshape(Bn, L // CMP_STRIDE, CMP_STRIDE, 2, N_KV, HEAD_DIM)
    w_lo = w1[:, :CMP_STRIDE]
    w_hi = w1[:, CMP_STRIDE:]
    pe_lo = jnp.einsum('crd,crde->ce', pe[:, :CMP_STRIDE], w_lo)
    pe_hi = jnp.einsum('crd,crde->ce', pe[:, CMP_STRIDE:], w_hi)
    lo = jnp.einsum('bnrckd,crde->bncke', sub, w_lo) + pe_lo[:, None, :]
    hi = jnp.einsum('bnrckd,crde->bncke', sub, w_hi) + pe_hi[:, None, :]
    return lo, hi


def compress(lo, hi, b1, w2):
    h = jax.nn.gelu(lo[:, :-1] + hi[:, 1:] + b1[:, None, :])
    kv = jnp.einsum('bncke,ced->bnckd', h, w2)
    return kv[:, :, 0], kv[:, :, 1]


def nsa_attend(q, gates, pos, kc, vc, c_end, gather_sel, n_sel, kw, vw, w_pos):
    Bn, T = q.shape[:2]
    hpg = q.shape[3]
    slopes = alibi_slopes(N_KV * hpg)
    scale = HEAD_DIM ** -0.5
    dist_c = (pos[:, None] - c_end[None, :]).astype(F32)
    s_c = (jnp.einsum('btghd,bngd->btghn', q, kc) * scale
           - slopes[None, None, :, :, None] * dist_c[None, :, None, None, :])
    p_c = masked_softmax(s_c, (dist_c >= 0)[None, :, None, None, :])
    o_c = jnp.einsum('btghn,bngd->btghd', p_c, vc)
    imp = p_c.sum(axis=3)
    n_cmp = imp.shape[-1]
    imp = jnp.pad(imp, ((0, 0), (0, 0), (0, 0), (1, SUBS_PER_SEL * (n_sel + 1) - 1 - n_cmp)))
    r = imp.reshape(Bn, T, N_KV, n_sel + 1, SUBS_PER_SEL)
    imp_sel = r[..., :n_sel, :].sum(-1) + r[..., 1:, 0]
    blk = pos // SEL_BLOCK
    j = jnp.arange(n_sel)
    forced = (j[None, :] == 0) | (j[None, :] == blk[:, None]) | (j[None, :] == blk[:, None] - 1)
    visible = j[None, :] <= blk[:, None]
    score = jnp.where(visible[None, :, None, :], imp_sel + FORCE_BONUS * forced[None, :, None, :], NEG)
    top_s, idx = lax.top_k(score, min(TOPK, n_sel))
    valid = top_s > 0.5 * NEG
    ks, vs = gather_sel(idx)
    n_k = idx.shape[-1]
    s_pos = idx[..., None] * SEL_BLOCK + jnp.arange(SEL_BLOCK)
    dist_s = (pos[None, :, None, None, None] - s_pos).astype(F32)
    mask_s = valid[..., None] & (dist_s >= 0)
    s_s = (jnp.einsum('btghd,btgksd->btghks', q, ks) * scale
           - slopes[None, None, :, :, None, None] * dist_s[:, :, :, None])
    p_s = masked_softmax(s_s.reshape(Bn, T, N_KV, hpg, n_k * SEL_BLOCK),
                         mask_s[:, :, :, None].reshape(Bn, T, N_KV, 1, n_k * SEL_BLOCK))
    o_s = jnp.einsum('btghm,btgmd->btghd', p_s, vs.reshape(Bn, T, N_KV, n_k * SEL_BLOCK, HEAD_DIM))
    dist_w = pos[:, None] - w_pos[None, :]
    mask_w = (dist_w >= 0) & (dist_w < WINDOW) & (w_pos[None, :] >= 0)
    s_w = (jnp.einsum('btghd,bmgd->btghm', q, kw) * scale
           - slopes[None, None, :, :, None] * dist_w.astype(F32)[None, :, None, None, :])
    p_w = masked_softmax(s_w, mask_w[None, :, None, None, :])
    o_w = jnp.einsum('btghm,bmgd->btghd', p_w, vw)
    return gates[..., 0:1] * o_c + gates[..., 1:2] * o_s + gates[..., 2:3] * o_w


def prompt_attention(kv2d, kv_rows, cmp_pe, cmp_w1, cmp_b1, cmp_w2):
    kv_cmp = kv_rows[0]
    L = kv_cmp.shape[1]
    pages = kv_cmp.reshape(L // PAGE_SIZE, PAGE_SIZE, KV_COLS)
    table = jnp.arange(L // PAGE_SIZE, dtype=jnp.int32)[None]
    kcvc = compress_blocks(cmp_lohi(pages, table, cmp_w1), cmp_pe, cmp_w1, cmp_b1, cmp_w2)

    def attend(q2d, gates_pad):
        return prompt_nsa(q2d[None], gates_pad[None], kv2d, kcvc)

    return attend


def sample_attention(kv2d, kv_rows, cache_cmp, cache_slc, cache_win, page_table,
                     cmp_pe, cmp_w1, cmp_b1, cmp_w2):
    kv_cmp, kv_slc, kv_win = kv_rows
    Bn, T = kv_cmp.shape[:2]
    n_pages = page_table.shape[1]
    past_len = n_pages * PAGE_SIZE
    past_cmp = cache_cmp[page_table].reshape(Bn, past_len, 2, N_KV, HEAD_DIM)
    lo, hi = cmp_partials(past_cmp, cmp_pe, cmp_w1)
    n_new_sub = T // CMP_STRIDE
    if n_new_sub > 0:
        lo_new, hi_new = cmp_partials(kv_cmp[:, :n_new_sub * CMP_STRIDE], cmp_pe, cmp_w1)
        lo = jnp.concatenate([lo, lo_new], axis=1)
        hi = jnp.concatenate([hi, hi_new], axis=1)
    kc, vc = compress(lo, hi, cmp_b1, cmp_w2)
    c_end = jnp.arange(kc.shape[1]) * CMP_STRIDE + (CMP_BLOCK - 1)
    n_past_blk = past_len // SEL_BLOCK
    n_new_blk = -(-T // SEL_BLOCK)
    n_sel = n_past_blk + n_new_blk
    blk_per_page = PAGE_SIZE // SEL_BLOCK
    pool = cache_slc.reshape(cache_slc.shape[0], blk_per_page, SEL_BLOCK, 2, N_KV, HEAD_DIM)
    new_blocks = jnp.pad(kv_slc, ((0, 0), (0, n_new_blk * SEL_BLOCK - T), (0, 0), (0, 0), (0, 0)))
    new_blocks = new_blocks.reshape(Bn, n_new_blk, SEL_BLOCK, 2, N_KV, HEAD_DIM)
    b_i = jnp.arange(Bn)[:, None, None, None]
    g_i = jnp.arange(N_KV)[None, None, :, None]

    def gather_sel(idx):
        jp = jnp.minimum(idx, n_past_blk - 1)
        page = page_table[b_i, jp // blk_per_page]
        sub = jp % blk_per_page
        jn = jnp.clip(idx - n_past_blk, 0, n_new_blk - 1)
        is_new = (idx >= n_past_blk)[..., None, None]
        k = jnp.where(is_new, new_blocks[b_i, jn, :, 0, g_i, :], pool[page, sub, :, 0, g_i, :])
        v = jnp.where(is_new, new_blocks[b_i, jn, :, 1, g_i, :], pool[page, sub, :, 1, g_i, :])
        return k, v

    w_rows = jnp.concatenate([cache_win, kv_win], axis=1)
    w_pos = past_len - cache_win.shape[1] + jnp.arange(w_rows.shape[1])
    pos = past_len + jnp.arange(T)

    def attend(q2d, gates_pad):
        hpg = q2d.shape[1] // (N_KV * HEAD_DIM)
        q = q2d.reshape(Bn, T, N_KV, hpg, HEAD_DIM)
        gates = gates_pad.reshape(Bn, T, N_KV, 128)[..., :3 * hpg].reshape(Bn, T, N_KV, hpg, 3)
        o = nsa_attend(q, gates, pos, kc, vc, c_end, gather_sel, n_sel,
                       w_rows[:, :, 0], w_rows[:, :, 1], w_pos)
        return o.reshape(Bn * T, -1)

    return attend


def _rows(v, per_tok):
    return v if v.shape[0] == 1 else jnp.repeat(v, per_tok, axis=0)


def trunk(x, mods, kv_mod, h0, make_attend, p):
    Bn, L, D = x.shape
    M = Bn * L
    n_heads = D // HEAD_DIM
    hpg = n_heads // N_KV
    depth = p["mod_w"].shape[0]
    n_a = depth // 2
    xr = x.reshape(M, D)
    new_h = []
    kv_rows = None
    attend = None
    for l in range(depth):
        sh1, sc1, ga1, sh2, sc2, ga2 = [_rows(m, L) for m in jnp.split(mods[l], 6, axis=-1)]
        if l == n_a:
            shift, scale = [_rows(m, L) for m in jnp.split(kv_mod, 2, axis=-1)]
            hk = norm_mod(xr, p["kv_norm"], scale, shift, BF16)
            kv2d = mm_wide(hk, p["w_kv"], tn=512)
            kv = kv2d.reshape(Bn, L, 3, 2, N_KV, HEAD_DIM)
            kv_rows = (kv[:, :, 0], kv[:, :, 1], kv[:, :, 2])
            attend = make_attend(kv2d, kv_rows)
        if l < n_a:
            u = norm_mod(xr, p["norm_pre"][l, 0], sc1, sh1, F32)
            gy, h_last = s5_mixer_core(u.reshape(Bn, L, D), h0[l], p["ssm_lam_re"][l], p["ssm_lam_im"][l],
                                       p["ssm_log_dt"][l], p["ssm_b_re"][l], p["ssm_b_im"][l],
                                       p["ssm_c_re"][l], p["ssm_c_im"][l], p["ssm_d"][l])
            new_h.append(h_last)
            xr = mm_tall(gy, p["ssm_w_glu"], xr, p["norm_post"][l, 0], ga1, layer=l, glu=True)
        else:
            lb = l - n_a
            h = norm_mod(xr, p["norm_pre"][l, 0], sc1, sh1, BF16)
            q = mm_wide(h, p["nsa_w_qg"], layer=lb, n_out=n_heads * HEAD_DIM, tn=512)
            gates_pad = mm_wide(h, p["w_gate_pad"], layer=lb, epilogue="sigmoid", tn=512)
            o = attend(q, gates_pad)
            xr = mm_tall(o, p["nsa_w_o"], xr, p["norm_post"][l, 0], ga1, layer=lb)
        h = norm_mod(xr, p["norm_pre"][l, 1], sc2, sh2, BF16)
        f = mm_wide(h, p["mlp_w1"], layer=l, epilogue="sqrelu", out_dtype=BF16, tn=1024)
        xr = mm_tall(f, p["mlp_w2"], xr, p["norm_post"][l, 1], ga2, layer=l, tk=512)
    return xr.reshape(Bn, L, D), jnp.stack(new_h), kv_rows


def kernel(x_prompt, x_sample, c_prompt, c_sample, state_ssm, cache_cmp, cache_slc, cache_win, page_table, mod_w, mod_b, norm_pre, norm_post, mlp_w1, mlp_w2, ssm_lam_re, ssm_lam_im, ssm_log_dt, ssm_b_re, ssm_b_im, ssm_c_re, ssm_c_im, ssm_d, ssm_w_glu, kv_norm, kv_mod_w, kv_mod_b, w_kv, cmp_pe, cmp_w1, cmp_b1, cmp_w2, nsa_w_qg, nsa_w_o):
    D = x_prompt.shape[-1]
    depth = mod_w.shape[0]
    n_heads = D // HEAD_DIM
    bp, bs = c_prompt.shape[0], c_sample.shape[0]
    c_all = jnp.concatenate([c_prompt, c_sample], axis=0)
    n_c = c_all.shape[0]
    c_all = jnp.pad(c_all, ((0, -n_c % 8), (0, 0)))
    mods = [mm_wide(c_all, mod_w, layer=l, bias=mod_b, prologue="silu", exact=True, tn=512) for l in range(depth)]
    kv_mod = mm_wide(c_all, kv_mod_w, bias=kv_mod_b, prologue="silu", exact=True, tn=512)
    hpg = n_heads // N_KV
    w_gate = nsa_w_qg[:, :, n_heads * HEAD_DIM:].reshape(nsa_w_qg.shape[0], D, N_KV, 3 * hpg)
    w_gate_pad = jnp.pad(w_gate, ((0, 0), (0, 0), (0, 0), (0, 128 - 3 * hpg))).reshape(-1, D, N_KV * 128)
    p = dict(mod_w=mod_w, norm_pre=norm_pre, norm_post=norm_post, mlp_w1=mlp_w1, mlp_w2=mlp_w2,
             ssm_lam_re=ssm_lam_re, ssm_lam_im=ssm_lam_im, ssm_log_dt=ssm_log_dt, ssm_b_re=ssm_b_re,
             ssm_b_im=ssm_b_im, ssm_c_re=ssm_c_re, ssm_c_im=ssm_c_im, ssm_d=ssm_d, ssm_w_glu=ssm_w_glu,
             kv_norm=kv_norm, w_kv=w_kv, nsa_w_qg=nsa_w_qg, nsa_w_o=nsa_w_o, w_gate_pad=w_gate_pad)

    def make_prompt(kv2d, kv_rows):
        return prompt_attention(kv2d, kv_rows, cmp_pe, cmp_w1, cmp_b1, cmp_w2)

    def make_sample(kv2d, kv_rows):
        return sample_attention(kv2d, kv_rows, cache_cmp, cache_slc, cache_win, page_table,
                                cmp_pe, cmp_w1, cmp_b1, cmp_w2)

    n_a = depth // 2
    G = D // SSM_GROUP
    h0_prompt = jnp.zeros((n_a, bp, G, STATE_DIM, 2), F32)
    y_prompt, ssm_prompt, rows_prompt = trunk(
        x_prompt, [m[:bp] for m in mods], kv_mod[:bp], h0_prompt, make_prompt, p)
    y_sample, ssm_sample, rows_sample = trunk(
        x_sample, [m[bp:bp + bs] for m in mods], kv_mod[bp:bp + bs], state_ssm, make_sample, p)
    cmp_prompt, slc_prompt, win_rows_prompt = rows_prompt
    cmp_sample, slc_sample, win_sample = rows_sample
    win_prompt = win_rows_prompt[:, -min(WINDOW, x_prompt.shape[1]):]
    return (y_prompt, y_sample, ssm_prompt, ssm_sample, cmp_prompt, cmp_sample,
            slc_prompt, slc_sample, win_prompt, win_sample)
```

```python
import functools
import math

import jax
import jax.numpy as jnp
import numpy as np
from jax import lax
from jax.experimental import pallas as pl
from jax.experimental.pallas import tpu as pltpu

F32 = jnp.float32
BF16 = jnp.bfloat16

SSM_GROUP = 16
STATE_DIM = 64
HEAD_DIM = 128
N_KV = 4
CMP_STRIDE = 16
CMP_BLOCK = 2 * CMP_STRIDE
SEL_BLOCK = 64
SUBS_PER_SEL = SEL_BLOCK // CMP_STRIDE
TOPK = 16
WINDOW = 512
QBLK = 128
PAGE_SIZE = 128
EPS = 1e-6
NEG = -1e30
FORCE_BONUS = 1e4

V7X_VMEM_LIMIT_BYTES = 56 * 1024 * 1024
HIGHEST = lax.Precision.HIGHEST


def _cparams(n_axes):
    return pltpu.CompilerParams(dimension_semantics=("arbitrary",) * n_axes,
                                vmem_limit_bytes=V7X_VMEM_LIMIT_BYTES)


def _pick(n, pref):
    if n <= pref:
        return n
    t = pref
    while n % t:
        t //= 2
    return t


def _norm_mod_kernel(x_ref, g_ref, sc_ref, sh_ref, o_ref):
    x = x_ref[...]
    r = lax.rsqrt(jnp.mean(x * x, axis=-1, keepdims=True) + EPS)
    y = (x * r) * g_ref[...]
    o_ref[...] = (y * (1.0 + sc_ref[...]) + sh_ref[...]).astype(o_ref.dtype)


def norm_mod(x, g, scale, shift, out_dtype):
    M, D = x.shape
    tm = _pick(M, 512)
    per_row = scale.shape[0] != 1
    mod_spec = pl.BlockSpec((tm, D), lambda i: (i, 0)) if per_row else pl.BlockSpec((1, D), lambda i: (0, 0))
    return pl.pallas_call(
        _norm_mod_kernel,
        grid=(M // tm,),
        in_specs=[pl.BlockSpec((tm, D), lambda i: (i, 0)),
                  pl.BlockSpec((1, D), lambda i: (0, 0)),
                  mod_spec, mod_spec],
        out_specs=pl.BlockSpec((tm, D), lambda i: (i, 0)),
        out_shape=jax.ShapeDtypeStruct((M, D), out_dtype),
        compiler_params=_cparams(1),
        name="norm_mod",
    )(x, g.reshape(1, D), scale, shift)


def _mm_wide_kernel(*refs, prologue, epilogue, has_bias, exact, n_w):
    a_ref = refs[0]
    w_refs = refs[1:1 + n_w]
    pos = 1 + n_w
    b_ref = refs[pos] if has_bias else None
    pos += int(has_bias)
    o_ref = refs[pos]
    wbf_refs = refs[pos + 1:]

    a = a_ref[...]
    if prologue == "silu":
        a = a * jax.nn.sigmoid(a)
    if exact:
        zs = [jnp.dot(a, w[...], preferred_element_type=F32, precision=HIGHEST) for w in w_refs]
    else:
        @pl.when(pl.program_id(1) == 0)
        def _():
            for w, wbf in zip(w_refs, wbf_refs):
                wbf[...] = w[...].astype(BF16)

        a = a.astype(BF16)
        zs = [jnp.dot(a, wbf[...], preferred_element_type=F32) for wbf in wbf_refs]
    z = zs[0]
    if has_bias:
        z = z + b_ref[...]
    if epilogue == "sqrelu":
        z = jnp.square(jnp.maximum(z, 0.0))
    elif epilogue == "sigmoid":
        z = jax.nn.sigmoid(z)
    elif epilogue == "glu":
        z = z * jax.nn.sigmoid(zs[1])
    o_ref[...] = z.astype(o_ref.dtype)


def mm_wide(a, w, *, layer=None, col0=0, n_out=None, bias=None, prologue=None, epilogue=None,
            exact=False, out_dtype=F32, tm=1024, tn=512):
    M, K = a.shape
    n_out = n_out if n_out is not None else w.shape[-1] - col0
    tm = _pick(M, tm)
    tn = _pick(n_out, tn)
    assert col0 % tn == 0 and n_out % tn == 0
    n_w = 2 if epilogue == "glu" else 1
    jb = col0 // tn

    def w_spec(extra):
        if layer is None:
            return pl.BlockSpec((K, tn), lambda j, i: (0, jb + extra + j))
        return pl.BlockSpec((None, K, tn), lambda j, i: (layer, 0, jb + extra + j))

    in_specs = [pl.BlockSpec((tm, K), lambda j, i: (i, 0))] + [w_spec(e * (n_out // tn)) for e in range(n_w)]
    args = [a] + [w] * n_w
    if bias is not None:
        if layer is None:
            in_specs.append(pl.BlockSpec((1, tn), lambda j, i: (0, jb + j)))
            args.append(bias.reshape(1, -1))
        else:
            in_specs.append(pl.BlockSpec((None, 1, tn), lambda j, i: (layer, 0, jb + j)))
            args.append(bias.reshape(bias.shape[0], 1, -1))
    scratch = [] if exact else [pltpu.VMEM((K, tn), BF16) for _ in range(n_w)]
    return pl.pallas_call(
        functools.partial(_mm_wide_kernel, prologue=prologue, epilogue=epilogue,
                          has_bias=bias is not None, exact=exact, n_w=n_w),
        grid=(n_out // tn, M // tm),
        in_specs=in_specs,
        out_specs=pl.BlockSpec((tm, tn), lambda j, i: (i, j)),
        out_shape=jax.ShapeDtypeStruct((M, n_out), out_dtype),
        scratch_shapes=scratch,
        compiler_params=_cparams(2),
        name="mm_wide",
    )(*args)


def _mm_tall_kernel(a_ref, *refs, n_w, nk):
    w_refs = refs[:n_w]
    res_ref, g_ref, ga_ref, o_ref = refs[n_w:n_w + 4]
    acc_refs = refs[n_w + 4:]
    k = pl.program_id(1)

    @pl.when(k == 0)
    def _():
        for acc in acc_refs:
            acc[...] = jnp.zeros_like(acc)

    a = a_ref[...].astype(BF16)
    for w, acc in zip(w_refs, acc_refs):
        acc[...] += jnp.dot(a, w[...].astype(BF16), preferred_element_type=F32)

    @pl.when(k == nk - 1)
    def _():
        m = acc_refs[0][...]
        if n_w == 2:
            m = m * jax.nn.sigmoid(acc_refs[1][...])
        r = lax.rsqrt(jnp.mean(m * m, axis=-1, keepdims=True) + EPS)
        o_ref[...] = res_ref[...] + ga_ref[...] * ((m * r) * g_ref[...])


def mm_tall(a, w, res, g, gate, *, layer=None, glu=False, tm=512, tk=256):
    M, K = a.shape
    N = res.shape[1]
    tm = _pick(M, tm)
    tk = _pick(K, tk)
    nk = K // tk
    n_w = 2 if glu else 1

    def w_spec(e):
        if layer is None:
            return pl.BlockSpec((tk, N), lambda i, k: (k, e))
        return pl.BlockSpec((None, tk, N), lambda i, k: (layer, k, e))

    per_row = gate.shape[0] != 1
    ga_spec = pl.BlockSpec((tm, N), lambda i, k: (i, 0)) if per_row else pl.BlockSpec((1, N), lambda i, k: (0, 0))
    return pl.pallas_call(
        functools.partial(_mm_tall_kernel, n_w=n_w, nk=nk),
        grid=(M // tm, nk),
        in_specs=[pl.BlockSpec((tm, tk), lambda i, k: (i, k))] + [w_spec(e) for e in range(n_w)]
        + [pl.BlockSpec((tm, N), lambda i, k: (i, 0)),
           pl.BlockSpec((1, N), lambda i, k: (0, 0)),
           ga_spec],
        out_specs=pl.BlockSpec((tm, N), lambda i, k: (i, 0)),
        out_shape=jax.ShapeDtypeStruct((M, N), F32),
        scratch_shapes=[pltpu.VMEM((tm, N), F32) for _ in range(n_w)],
        compiler_params=_cparams(2),
        name="mm_tall",
    )(a, *([w] * n_w), res, g.reshape(1, N), gate)


S5_GROUPS_PER_STEP = 8
P2 = 2 * STATE_DIM


def _s5_prep_kernel(lam_re_ref, lam_im_ref, ldt_ref, btr_ref, bti_ref, cr_ref, ci_ref,
                    mt_ref, bt_ref, cs_ref, at_ref, *, tc, gb):
    C = SSM_GROUP
    tcc = tc * C
    wk = max(tcc, 128)
    lane = lax.broadcasted_iota(jnp.int32, (C, P2), 1)
    is_re = lane < STATE_DIM
    kk = lax.broadcasted_iota(jnp.int32, (tc + 1, P2), 0).astype(F32)
    lane_k = lax.broadcasted_iota(jnp.int32, (C, wk), 1)
    for gg in range(gb):
        lr = lam_re_ref[gg]
        li = lam_im_ref[gg]
        dt = jnp.exp(ldt_ref[gg])
        mag = jnp.exp(kk * (lr * dt))
        ang = kk * (li * dt)
        pr = mag * jnp.cos(ang)
        pi = mag * jnp.sin(ang)
        x = pr[1:2] - 1.0
        y = pi[1:2]
        den = lr * lr + li * li
        cfr = (x * lr + y * li) / den
        cfi = (y * lr - x * li) / den
        btr = btr_ref[gg]
        bti = bti_ref[gg]
        bbr = cfr * btr - cfi * bti
        bbi = cfr * bti + cfi * btr
        cre = cr_ref[gg]
        cim = ci_ref[gg]

        def bm(k):
            return jnp.where(is_re, pr[k:k + 1] * bbr - pi[k:k + 1] * bbi, pr[k:k + 1] * bbi + pi[k:k + 1] * bbr)

        def cm(k):
            return jnp.where(is_re, cre * pr[k:k + 1] - cim * pi[k:k + 1], -(cre * pi[k:k + 1] + cim * pr[k:k + 1]))

        cms = [cm(k) for k in range(tc + 1)]
        cs0 = jnp.concatenate(cms[:tc] + [jnp.zeros((wk - tcc, P2), F32)] * (wk > tcc), axis=0)
        kst = lax.dot_general(bm(0), cs0, (((1,), (1,)), ((), ())), preferred_element_type=F32,
                              precision=HIGHEST)
        for s in range(tc):
            shifted = kst if s == 0 else jnp.where(lane_k >= s * C, pltpu.roll(kst, s * C, 1), 0.0)
            mt_ref[gg, s * C:(s + 1) * C, :] = shifted[:, :tcc]
            bt_ref[gg, s * C:(s + 1) * C, :] = bm(tc - 1 - s)
            cs_ref[gg, s * C:(s + 1) * C, :] = cms[s + 1]
        at_ref[gg, 0:1, :] = pr[tc:tc + 1]
        at_ref[gg, 1:2, :] = jnp.where(is_re[0:1], -pi[tc:tc + 1], pi[tc:tc + 1])


def s5_prep(lam_re, lam_im, log_dt, b_re, b_im, c_re, c_im, tc):
    G = lam_re.shape[0]
    gb = _pick(G, S5_GROUPS_PER_STEP)
    C = SSM_GROUP
    tcc = tc * C
    dup = lambda v: jnp.concatenate([v, v], axis=-1)
    lam_re2 = dup(lam_re)[:, None, :]
    lam_im2 = dup(lam_im)[:, None, :]
    ldt2 = jnp.broadcast_to(log_dt[:, None, None], (G, 1, P2))
    btr = dup(jnp.swapaxes(b_re, 1, 2))
    bti = dup(jnp.swapaxes(b_im, 1, 2))
    cr2 = dup(c_re)
    ci2 = dup(c_im)
    vec = pl.BlockSpec((gb, 1, P2), lambda i: (i, 0, 0))
    mat = pl.BlockSpec((gb, C, P2), lambda i: (i, 0, 0))
    return pl.pallas_call(
        functools.partial(_s5_prep_kernel, tc=tc, gb=gb),
        grid=(G // gb,),
        in_specs=[vec, vec, vec, mat, mat, mat, mat],
        out_specs=[pl.BlockSpec((gb, tcc, tcc), lambda i: (i, 0, 0)),
                   pl.BlockSpec((gb, tcc, P2), lambda i: (i, 0, 0)),
                   pl.BlockSpec((gb, tcc, P2), lambda i: (i, 0, 0)),
                   pl.BlockSpec((gb, 2, P2), lambda i: (i, 0, 0))],
        out_shape=[jax.ShapeDtypeStruct((G, tcc, tcc), F32),
                   jax.ShapeDtypeStruct((G, tcc, P2), F32),
                   jax.ShapeDtypeStruct((G, tcc, P2), F32),
                   jax.ShapeDtypeStruct((G, 2, P2), F32)],
        compiler_params=_cparams(1),
        name="s5_prep",
    )(lam_re2, lam_im2, ldt2, btr, bti, cr2, ci2)


def _gmm_kernel(*refs, gb, two):
    if two:
        a_ref, w_ref, a2_ref, w2_ref, o_ref = refs
    else:
        a_ref, w_ref, o_ref = refs
    for gg in range(gb):
        z = jnp.dot(a_ref[gg], w_ref[gg], preferred_element_type=F32, precision=HIGHEST)
        if two:
            z = z + lax.dot_general(a2_ref[gg], w2_ref[gg], (((1,), (1,)), ((), ())),
                                    preferred_element_type=F32, precision=HIGHEST)
        o_ref[gg] = z


def gmm(a, w, a2=None, w2=None):
    G, R, K = a.shape
    N = w.shape[2]
    gb = _pick(G, S5_GROUPS_PER_STEP)
    two = a2 is not None
    spec = lambda arr: pl.BlockSpec((gb,) + arr.shape[1:], lambda i: (i, 0, 0))
    args = [a, w] + ([a2, w2] if two else [])
    return pl.pallas_call(
        functools.partial(_gmm_kernel, gb=gb, two=two),
        grid=(G // gb,),
        in_specs=[spec(x) for x in args],
        out_specs=pl.BlockSpec((gb, R, N), lambda i: (i, 0, 0)),
        out_shape=jax.ShapeDtypeStruct((G, R, N), F32),
        compiler_params=_cparams(1),
        name="s5_gmm",
    )(*args)


S5_CHUNKS_PER_STEP = 64


def _s5_scan_kernel(s_ref, h0_ref, at_ref, hin_ref, hfin_ref, h_scr, *, cb, n_steps):
    j = pl.program_id(1)

    @pl.when(j == 0)
    def _():
        h_scr[...] = h0_ref[...]

    ar = at_ref[0]
    ai = at_ref[1]

    def step(c, h):
        hin_ref[c] = h
        return ar * h + ai * pltpu.roll(h, STATE_DIM, 1) + s_ref[c]

    h = lax.fori_loop(0, cb, step, h_scr[...])
    h_scr[...] = h

    @pl.when(j == n_steps - 1)
    def _():
        hfin_ref[...] = h


def s5_scan(s, h0, at):
    Bn, n_chunk, G, _ = s.shape
    cb = _pick(n_chunk, S5_CHUNKS_PER_STEP)
    n_steps = n_chunk // cb
    return pl.pallas_call(
        functools.partial(_s5_scan_kernel, cb=cb, n_steps=n_steps),
        grid=(Bn, n_steps),
        in_specs=[pl.BlockSpec((None, cb, G, P2), lambda b, j: (b, j, 0, 0)),
                  pl.BlockSpec((None, G, P2), lambda b, j: (b, 0, 0)),
                  pl.BlockSpec((2, G, P2), lambda b, j: (0, 0, 0))],
        out_specs=[pl.BlockSpec((None, cb, G, P2), lambda b, j: (b, j, 0, 0)),
                   pl.BlockSpec((None, G, P2), lambda b, j: (b, 0, 0))],
        out_shape=[jax.ShapeDtypeStruct((Bn, n_chunk, G, P2), F32),
                   jax.ShapeDtypeStruct((Bn, G, P2), F32)],
        scratch_shapes=[pltpu.VMEM((G, P2), F32)],
        compiler_params=_cparams(2),
        name="s5_scan",
    )(s, h0, at)


def _s5_out_kernel(y_ref, u_ref, d_ref, o_ref):
    o_ref[...] = jax.nn.gelu(y_ref[...] + d_ref[...] * u_ref[...]).astype(o_ref.dtype)


def s5_out(y, u, d_skip):
    M, D = y.shape
    tm = _pick(M, 512)
    row = pl.BlockSpec((tm, D), lambda i: (i, 0))
    return pl.pallas_call(
        _s5_out_kernel,
        grid=(M // tm,),
        in_specs=[row, row, pl.BlockSpec((1, D), lambda i: (0, 0))],
        out_specs=row,
        out_shape=jax.ShapeDtypeStruct((M, D), BF16),
        compiler_params=_cparams(1),
        name="s5_out",
    )(y, u, d_skip.reshape(1, D))


def s5_mixer_core(u, h0, lam_re, lam_im, log_dt, b_re, b_im, c_re, c_im, d_skip):
    Bn, L, D = u.shape
    G = D // SSM_GROUP
    C = SSM_GROUP
    tc = _pick(L, 16)
    n_chunk = L // tc
    mt, bt, cs, at = s5_prep(lam_re, lam_im, log_dt, b_re, b_im, c_re, c_im, tc)
    x = u.reshape(Bn, n_chunk, tc, G, C).transpose(3, 0, 1, 2, 4).reshape(G, Bn * n_chunk, tc * C)
    s = gmm(x, bt)
    s = s.reshape(G, Bn, n_chunk, P2).transpose(1, 2, 0, 3)
    h0v = jnp.concatenate([h0[..., 0], h0[..., 1]], axis=-1)
    hin, hfin = s5_scan(s, h0v, at.transpose(1, 0, 2))
    hin = hin.transpose(2, 0, 1, 3).reshape(G, Bn * n_chunk, P2)
    y = gmm(x, mt, hin, cs)
    y = y.reshape(G, Bn, n_chunk, tc, C).transpose(1, 2, 3, 0, 4).reshape(Bn * L, D)
    h_last = jnp.stack([hfin[..., :STATE_DIM], hfin[..., STATE_DIM:]], axis=-1)
    return s5_out(y, u.reshape(Bn * L, D), d_skip), h_last


SUBS_PER_PAGE = PAGE_SIZE // CMP_STRIDE
KV_COLS = 2 * N_KV * HEAD_DIM
CMP_PAGES_PER_STEP = 8


def _cmp_lohi_kernel(pt_ref, *refs, n_pg):
    x_refs = refs[:n_pg]
    w_ref = refs[n_pg]
    o_ref = refs[n_pg + 1]
    rows = n_pg * SUBS_PER_PAGE
    for c in range(2):
        acc = jnp.zeros((N_KV * rows, 2 * HEAD_DIM), F32)
        for rp in range(CMP_STRIDE // 2):
            parts = []
            for k in range(N_KV):
                col = (c * N_KV + k) * HEAD_DIM
                for p in range(n_pg):
                    a0 = x_refs[p][:, 2 * rp, col:col + HEAD_DIM]
                    a1 = x_refs[p][:, 2 * rp + 1, col:col + HEAD_DIM]
                    parts.append(jnp.concatenate([a0, a1], axis=1))
            xs = jnp.concatenate(parts, axis=0).astype(BF16)
            acc = acc + jnp.dot(xs, w_ref[c, rp], preferred_element_type=F32)
        for k in range(N_KV):
            o_ref[c, k] = acc[k * rows:(k + 1) * rows]


def cmp_lohi(pages, page_table, cmp_w1):
    Bn, n_pages = page_table.shape
    n_pg = _pick(n_pages, CMP_PAGES_PER_STEP)
    x = pages.reshape(pages.shape[0], SUBS_PER_PAGE, CMP_STRIDE, KV_COLS)
    half = CMP_STRIDE // 2
    w_lo = cmp_w1[:, :CMP_STRIDE].reshape(2, half, 2 * HEAD_DIM, HEAD_DIM)
    w_hi = cmp_w1[:, CMP_STRIDE:].reshape(2, half, 2 * HEAD_DIM, HEAD_DIM)
    w = jnp.concatenate([w_lo, w_hi], axis=-1).astype(BF16)
    n_sub = n_pages * SUBS_PER_PAGE

    def x_spec(p):
        return pl.BlockSpec((None, SUBS_PER_PAGE, CMP_STRIDE, KV_COLS),
                            lambda b, i, pt: (pt[b, i * n_pg + p], 0, 0, 0))

    grid_spec = pltpu.PrefetchScalarGridSpec(
        num_scalar_prefetch=1,
        grid=(Bn, n_pages // n_pg),
        in_specs=[x_spec(p) for p in range(n_pg)]
        + [pl.BlockSpec(w.shape, lambda b, i, pt: (0, 0, 0, 0))],
        out_specs=pl.BlockSpec((None, 2, N_KV, n_pg * SUBS_PER_PAGE, 2 * HEAD_DIM),
                               lambda b, i, pt: (b, 0, 0, i, 0)),
    )
    return pl.pallas_call(
        functools.partial(_cmp_lohi_kernel, n_pg=n_pg),
        grid_spec=grid_spec,
        out_shape=jax.ShapeDtypeStruct((Bn, 2, N_KV, n_sub, 2 * HEAD_DIM), F32),
        compiler_params=_cparams(2),
        name="cmp_lohi",
    )(page_table, *([x] * n_pg), w)


def _compress_kernel(x_ref, pe_ref, w1_ref, b1_ref, w2_ref, o_ref):
    n_sub = x_ref.shape[0]
    half = w1_ref.shape[0] // 2
    pe = jnp.broadcast_to(pe_ref[...], (8, 2 * half))
    pe_lo = jnp.dot(pe[:, :half], w1_ref[:half], preferred_element_type=F32, precision=HIGHEST)[0:1]
    pe_hi = jnp.dot(pe[:, half:], w1_ref[half:], preferred_element_type=F32, precision=HIGHEST)[0:1]
    x = x_ref[...]
    lo = x[:, :HEAD_DIM] + pe_lo
    hi = x[:, HEAD_DIM:] + pe_hi
    hi_next = pltpu.roll(hi, n_sub - 1, 0)
    h = jax.nn.gelu(lo + hi_next + b1_ref[...])
    o_ref[...] = jnp.dot(h.astype(BF16), w2_ref[...].astype(BF16), preferred_element_type=F32).astype(o_ref.dtype)


def compress_blocks(lohi, cmp_pe, cmp_w1, cmp_b1, cmp_w2):
    Bn, _, _, n_sub, _ = lohi.shape
    kdim = CMP_BLOCK * HEAD_DIM
    return pl.pallas_call(
        _compress_kernel,
        grid=(Bn, 2, N_KV),
        in_specs=[pl.BlockSpec((None, None, None, n_sub, 2 * HEAD_DIM), lambda b, c, k: (b, c, k, 0, 0)),
                  pl.BlockSpec((None, 1, kdim), lambda b, c, k: (c, 0, 0)),
                  pl.BlockSpec((None, kdim, HEAD_DIM), lambda b, c, k: (c, 0, 0)),
                  pl.BlockSpec((None, 1, HEAD_DIM), lambda b, c, k: (c, 0, 0)),
                  pl.BlockSpec((None, HEAD_DIM, HEAD_DIM), lambda b, c, k: (c, 0, 0))],
        out_specs=pl.BlockSpec((None, None, None, n_sub, HEAD_DIM), lambda b, c, k: (b, c, k, 0, 0)),
        out_shape=jax.ShapeDtypeStruct((Bn, 2, N_KV, n_sub, HEAD_DIM), BF16),
        compiler_params=_cparams(3),
        name="compress",
    )(lohi, cmp_pe.reshape(2, 1, kdim), cmp_w1.reshape(2, kdim, HEAD_DIM),
      cmp_b1.reshape(2, 1, HEAD_DIM), cmp_w2)


N_CAND = 128
NEG_TAKEN = -3e38


def _topk_mask(score, axis, k_sel):
    idx = lax.broadcasted_iota(jnp.int32, score.shape, axis)
    sel = jnp.zeros(score.shape, F32)
    for _ in range(k_sel):
        m = jnp.max(score, axis=axis, keepdims=True)
        first = jnp.min(jnp.where(score == m, idx, N_CAND), axis=axis, keepdims=True)
        hit = idx == first
        sel = jnp.where(hit & (m > 0.5 * NEG), 1.0, sel)
        score = jnp.where(hit, NEG_TAKEN, score)
    return sel


def _cmp_attn_kernel(slope_ref, q_ref, kc_ref, vc_ref, gate_ref, wsel_ref, oc_ref, sel_ref, *,
                     tq, hpg, pos0, pos_step, t_real, transposed, k_sel, n_cand):
    i = pl.program_id(1)
    g = pl.program_id(2)
    n_cmp = kc_ref.shape[0]
    scale = HEAD_DIM ** -0.5
    q = q_ref[...]
    qs = jnp.concatenate([q[:, h * HEAD_DIM:(h + 1) * HEAD_DIM] for h in range(hpg)], axis=0)
    qs = (qs * scale).astype(BF16)
    s = lax.dot_general(qs, kc_ref[...], (((1,), (1,)), ((), ())), preferred_element_type=F32)
    base = pos0 + i * pos_step
    t_idx = lax.broadcasted_iota(jnp.int32, (tq, n_cmp), 0)
    n_idx = lax.broadcasted_iota(jnp.int32, (tq, n_cmp), 1)
    dist_i = base + t_idx - (n_idx * CMP_STRIDE + (CMP_BLOCK - 1))
    valid = dist_i >= 0
    dist = dist_i.astype(F32)
    gates = gate_ref[...]
    vc = vc_ref[...]
    psum = jnp.zeros((tq, n_cmp), F32)
    for h in range(hpg):
        sh = s[h * tq:(h + 1) * tq] - slope_ref[g * hpg + h] * dist
        sh = jnp.where(valid, sh, NEG)
        m = jnp.max(sh, axis=-1, keepdims=True)
        e = jnp.where(valid, jnp.exp(sh - m), 0.0)
        p = e / jnp.maximum(jnp.sum(e, axis=-1, keepdims=True), 1e-30)
        psum = psum + p
        o_h = jnp.dot(p.astype(BF16), vc, preferred_element_type=F32)
        oc_ref[:, h * HEAD_DIM:(h + 1) * HEAD_DIM] = o_h * gates[:, 3 * h:3 * h + 1]
    if transposed:
        imp = lax.dot_general(wsel_ref[...], psum, (((1,), (1,)), ((), ())),
                              preferred_element_type=F32, precision=HIGHEST)
        shape, j_ax, t_ax = (N_CAND, tq), 0, 1
    else:
        imp = lax.dot_general(psum, wsel_ref[...], (((1,), (1,)), ((), ())),
                              preferred_element_type=F32, precision=HIGHEST)
        shape, j_ax, t_ax = (tq, N_CAND), 1, 0
    j = lax.broadcasted_iota(jnp.int32, shape, j_ax)
    blk = (base + lax.broadcasted_iota(jnp.int32, shape, t_ax)) // SEL_BLOCK
    forced = (j == 0) | (j == blk) | (j == blk - 1)
    visible = (j <= blk) & (j < n_cand)
    score = jnp.where(visible, imp + jnp.where(forced, FORCE_BONUS, 0.0), NEG)
    sel = _topk_mask(score, j_ax, k_sel)
    if transposed:
        sel = sel.T
    sel_ref[...] = sel.astype(sel_ref.dtype)


def _sel_weights(n_cmp_pad, n_cmp):
    j = np.arange(N_CAND)[:, None]
    n = np.arange(n_cmp_pad)[None, :]
    w = (n >= SUBS_PER_SEL * j - 1) & (n <= SUBS_PER_SEL * j + SUBS_PER_SEL - 1) & (n < n_cmp)
    return jnp.asarray(w.astype(np.float32))


def cmp_attention(q, kcvc, gates, *, tq, pos0, pos_step, transposed, k_sel, n_cand):
    Bn, T, D = q.shape
    n_heads = D // HEAD_DIM
    hpg = n_heads // N_KV
    n_sub = kcvc.shape[3]
    gw = hpg * HEAD_DIM
    slopes = alibi_slopes(n_heads).reshape(-1)
    wsel = _sel_weights(n_sub, n_sub - 1)
    kern = functools.partial(_cmp_attn_kernel, tq=tq, hpg=hpg, pos0=pos0, pos_step=pos_step, t_real=T,
                             transposed=transposed, k_sel=k_sel, n_cand=n_cand)
    return pl.pallas_call(
        kern,
        grid=(Bn, T // tq, N_KV),
        in_specs=[pl.BlockSpec(memory_space=pltpu.SMEM),
                  pl.BlockSpec((None, tq, gw), lambda b, i, g: (b, i, g)),
                  pl.BlockSpec((None, None, None, n_sub, HEAD_DIM), lambda b, i, g: (b, 0, g, 0, 0)),
                  pl.BlockSpec((None, None, None, n_sub, HEAD_DIM), lambda b, i, g: (b, 1, g, 0, 0)),
                  pl.BlockSpec((None, tq, 128), lambda b, i, g: (b, i, g)),
                  pl.BlockSpec((N_CAND, n_sub), lambda b, i, g: (0, 0))],
        out_specs=[pl.BlockSpec((None, tq, gw), lambda b, i, g: (b, i, g)),
                   pl.BlockSpec((None, tq, N_CAND), lambda b, i, g: (b, i, g))],
        out_shape=[jax.ShapeDtypeStruct((Bn, T, D), F32),
                   jax.ShapeDtypeStruct((Bn, T, N_KV * N_CAND), BF16)],
        compiler_params=_cparams(3),
        name="cmp_attn",
    )(slopes, q, kcvc, kcvc, gates, wsel)


SEL_TK = 512
MASK_BIG = 1e30


def _sel_win_kernel(slope_ref, q_ref, ks_ref, vs_ref, kw_ref, vw_ref, sel_ref, oc_ref, gate_ref, e_ref,
                    o_ref, ksb, vsb, kwb, vwb, *, tq, hpg):
    g = pl.program_id(0)
    i = pl.program_id(1)
    L = ks_ref.shape[0]
    s0 = i * tq
    scale = HEAD_DIM ** -0.5

    @pl.when(i == 0)
    def _():
        ksb[...] = ks_ref[...].astype(BF16)
        vsb[...] = vs_ref[...].astype(BF16)
        kwb[0:WINDOW] = jnp.zeros((WINDOW, HEAD_DIM), BF16)
        vwb[0:WINDOW] = jnp.zeros((WINDOW, HEAD_DIM), BF16)
        kwb[WINDOW:] = kw_ref[...].astype(BF16)
        vwb[WINDOW:] = vw_ref[...].astype(BF16)

    q = q_ref[...]
    qs = jnp.concatenate([q[:, h * HEAD_DIM:(h + 1) * HEAD_DIM] for h in range(hpg)], axis=0)
    qs = (qs * scale).astype(BF16)
    slopes = [slope_ref[g * hpg + h] for h in range(hpg)]
    neg_sel = sel_ref[...] - 1.0
    c_iota = lax.broadcasted_iota(jnp.int32, (1, SEL_TK), 1)
    kd = s0 // SEL_TK

    def sweep(kt, carry, diagonal):
        ms, ls, accs = carry
        k0 = pl.multiple_of(kt * SEL_TK, SEL_TK)
        k_t = ksb[pl.ds(k0, SEL_TK), :]
        v_t = vsb[pl.ds(k0, SEL_TK), :]
        s = lax.dot_general(qs, k_t, (((1,), (1,)), ((), ())), preferred_element_type=F32)
        addm = jnp.dot(neg_sel, e_ref[kt], preferred_element_type=F32)
        if diagonal:
            t_idx = lax.broadcasted_iota(jnp.int32, (tq, SEL_TK), 0)
            c_idx = lax.broadcasted_iota(jnp.int32, (tq, SEL_TK), 1)
            addm = jnp.where(s0 + t_idx >= k0 + c_idx, addm, -MASK_BIG)
        rel = (k0 - s0 + c_iota).astype(F32)
        new_m, new_l, new_acc = [], [], []
        for h in range(hpg):
            sh = s[h * tq:(h + 1) * tq] + slopes[h] * rel + addm
            m_new = jnp.maximum(ms[h], jnp.max(sh, axis=-1, keepdims=True))
            p = jnp.exp(sh - m_new)
            alpha = jnp.exp(ms[h] - m_new)
            new_l.append(alpha * ls[h] + jnp.sum(p, axis=-1, keepdims=True))
            new_acc.append(alpha * accs[h] + jnp.dot(p.astype(BF16), v_t, preferred_element_type=F32))
            new_m.append(m_new)
        return new_m, new_l, new_acc

    init = ([jnp.full((tq, 1), NEG, F32)] * hpg, [jnp.zeros((tq, 1), F32)] * hpg,
            [jnp.zeros((tq, HEAD_DIM), F32)] * hpg)
    carry = lax.fori_loop(0, kd, lambda kt, c: tuple(sweep(kt, c, False)), tuple(init))
    ms, ls, accs = sweep(kd, carry, True)

    wn = WINDOW + tq
    w0 = pl.multiple_of(s0, tq)
    kw_t = kwb[pl.ds(w0, wn), :]
    vw_t = vwb[pl.ds(w0, wn), :]
    sw = lax.dot_general(qs, kw_t, (((1,), (1,)), ((), ())), preferred_element_type=F32)
    t_idx = lax.broadcasted_iota(jnp.int32, (tq, wn), 0)
    c_idx = lax.broadcasted_iota(jnp.int32, (tq, wn), 1)
    dist_i = t_idx + WINDOW - c_idx
    valid = (dist_i >= 0) & (dist_i < WINDOW) & (c_idx + s0 >= WINDOW)
    dist = dist_i.astype(F32)
    gates = gate_ref[...]
    oc = oc_ref[...]
    for h in range(hpg):
        sh = jnp.where(valid, sw[h * tq:(h + 1) * tq] - slopes[h] * dist, NEG)
        m = jnp.max(sh, axis=-1, keepdims=True)
        e = jnp.where(valid, jnp.exp(sh - m), 0.0)
        p = e / jnp.maximum(jnp.sum(e, axis=-1, keepdims=True), 1e-30)
        o_w = jnp.dot(p.astype(BF16), vw_t, preferred_element_type=F32)
        o_s = accs[h] / jnp.maximum(ls[h], 1e-30)
        out = (oc[:, h * HEAD_DIM:(h + 1) * HEAD_DIM] + gates[:, 3 * h + 1:3 * h + 2] * o_s
               + gates[:, 3 * h + 2:3 * h + 3] * o_w)
        o_ref[:, h * HEAD_DIM:(h + 1) * HEAD_DIM] = out.astype(o_ref.dtype)


def _block_expand(n_tiles):
    t = np.arange(n_tiles)[:, None, None]
    j = np.arange(N_CAND)[None, :, None]
    c = np.arange(SEL_TK)[None, None, :]
    return jnp.asarray(((t * SEL_TK + c) // SEL_BLOCK == j).astype(np.float32) * MASK_BIG, dtype=BF16)


def sel_win_attention(q, kv, sel, oc, gates, *, tq):
    _, L, D = q.shape
    n_heads = D // HEAD_DIM
    hpg = n_heads // N_KV
    gw = hpg * HEAD_DIM
    assert L % SEL_TK == 0 and SEL_TK % tq == 0 and L // SEL_BLOCK <= N_CAND
    slopes = alibi_slopes(n_heads).reshape(-1)
    e = _block_expand(L // SEL_TK)

    def kv_spec(branch, which):
        cb = (branch * 2 + which) * N_KV
        return pl.BlockSpec((L, HEAD_DIM), lambda g, i: (0, cb + g))

    return pl.pallas_call(
        functools.partial(_sel_win_kernel, tq=tq, hpg=hpg),
        grid=(N_KV, L // tq),
        in_specs=[pl.BlockSpec(memory_space=pltpu.SMEM),
                  pl.BlockSpec((None, tq, gw), lambda g, i: (0, i, g)),
                  kv_spec(1, 0), kv_spec(1, 1), kv_spec(2, 0), kv_spec(2, 1),
                  pl.BlockSpec((None, tq, N_CAND), lambda g, i: (0, i, g)),
                  pl.BlockSpec((None, tq, gw), lambda g, i: (0, i, g)),
                  pl.BlockSpec((None, tq, 128), lambda g, i: (0, i, g)),
                  pl.BlockSpec(e.shape, lambda g, i: (0, 0, 0))],
        out_specs=pl.BlockSpec((tq, gw), lambda g, i: (i, g)),
        out_shape=jax.ShapeDtypeStruct((L, D), BF16),
        scratch_shapes=[pltpu.VMEM((L, HEAD_DIM), BF16), pltpu.VMEM((L, HEAD_DIM), BF16),
                        pltpu.VMEM((WINDOW + L, HEAD_DIM), BF16), pltpu.VMEM((WINDOW + L, HEAD_DIM), BF16)],
        compiler_params=_cparams(2),
        name="sel_win_attn",
    )(slopes, q, kv, kv, kv, kv, sel, oc, gates, e)


def prompt_nsa(q, gates, kv, kcvc):
    L = q.shape[1]
    n_sel = L // SEL_BLOCK
    oc, sel = cmp_attention(q, kcvc, gates, tq=QBLK, pos0=0, pos_step=QBLK, transposed=True,
                            k_sel=min(TOPK, n_sel), n_cand=n_sel)
    return sel_win_attention(q, kv, sel, oc, gates, tq=QBLK)


SAMPLE_TQ = 8
SAMPLE_PAGES_PER_STEP = 8
NEW_ROWS_PAD = 128


def _sample_sel_win_kernel(pt_ref, slope_ref, q_ref, sel_ref, oc_ref, gate_ref, e_ref, *refs,
                           n_pg, hpg, t_real, n_steps, past_len, win_len):
    pages = refs[:n_pg]
    nslc_ref, nwin_ref, cwin_ref, o_ref, m_scr, l_scr, acc_scr = refs[n_pg:]
    j = pl.program_id(1)
    tq = SAMPLE_TQ
    tk = n_pg * PAGE_SIZE
    scale = HEAD_DIM ** -0.5

    @pl.when(j == 0)
    def _():
        m_scr[...] = jnp.full(m_scr.shape, NEG, F32)
        l_scr[...] = jnp.zeros(l_scr.shape, F32)
        acc_scr[...] = jnp.zeros(acc_scr.shape, F32)

    q = q_ref[...]
    neg_sel = sel_ref[...] - 1.0
    rel = (j * tk - past_len + lax.broadcasted_iota(jnp.int32, (1, tk), 1)).astype(F32)
    e_t = e_ref[j]

    def heads_of(g):
        gw = hpg * HEAD_DIM
        qg = q[:, g * gw:(g + 1) * gw]
        qs = jnp.concatenate([qg[:, h * HEAD_DIM:(h + 1) * HEAD_DIM] for h in range(hpg)], axis=0)
        return (qs * scale).astype(BF16)

    def online(g, h, sh, v):
        r0 = h * tq
        m_old = m_scr[g, r0:r0 + tq]
        m_new = jnp.maximum(m_old, jnp.max(sh, axis=-1, keepdims=True))
        p = jnp.exp(sh - m_new)
        alpha = jnp.exp(m_old - m_new)
        l_scr[g, r0:r0 + tq] = alpha * l_scr[g, r0:r0 + tq] + jnp.sum(p, axis=-1, keepdims=True)
        acc_scr[g, r0:r0 + tq] = alpha * acc_scr[g, r0:r0 + tq] + jnp.dot(p.astype(BF16), v,
                                                                         preferred_element_type=F32)
        m_scr[g, r0:r0 + tq] = m_new

    for g in range(N_KV):
        kc0 = g * HEAD_DIM
        vc0 = (N_KV + g) * HEAD_DIM
        k = jnp.concatenate([pg[:, kc0:kc0 + HEAD_DIM] for pg in pages], axis=0).astype(BF16)
        v = jnp.concatenate([pg[:, vc0:vc0 + HEAD_DIM] for pg in pages], axis=0).astype(BF16)
        qs = heads_of(g)
        s = lax.dot_general(qs, k, (((1,), (1,)), ((), ())), preferred_element_type=F32)
        addm = jnp.dot(neg_sel[:, g * N_CAND:(g + 1) * N_CAND], e_t, preferred_element_type=F32)
        for h in range(hpg):
            sh = s[h * tq:(h + 1) * tq] + slope_ref[g * hpg + h] * rel + addm
            online(g, h, sh, v)

    @pl.when(j == n_steps - 1)
    def _():
        gates = gate_ref[...]
        oc = oc_ref[...]
        t_n = lax.broadcasted_iota(jnp.int32, (tq, NEW_ROWS_PAD), 0)
        r_n = lax.broadcasted_iota(jnp.int32, (tq, NEW_ROWS_PAD), 1)
        valid_n = (r_n <= t_n) & (r_n < t_real)
        wn = win_len + NEW_ROWS_PAD
        t_w = lax.broadcasted_iota(jnp.int32, (tq, wn), 0)
        c_w = lax.broadcasted_iota(jnp.int32, (tq, wn), 1)
        dist_w = jnp.where(c_w < win_len, t_w + win_len - c_w, t_w - (c_w - win_len))
        cached = c_w < win_len
        valid_w = ((cached & (dist_w < WINDOW) & (c_w + past_len - win_len >= 0))
                   | ((c_w >= win_len) & (dist_w >= 0) & (c_w - win_len < t_real)))
        dist_wf = dist_w.astype(F32)
        for g in range(N_KV):
            kc0 = g * HEAD_DIM
            vc0 = (N_KV + g) * HEAD_DIM
            qs = heads_of(g)
            k_n = nslc_ref[:, kc0:kc0 + HEAD_DIM].astype(BF16)
            v_n = nslc_ref[:, vc0:vc0 + HEAD_DIM].astype(BF16)
            s_n = lax.dot_general(qs, k_n, (((1,), (1,)), ((), ())), preferred_element_type=F32)
            k_w = jnp.concatenate([cwin_ref[:, kc0:kc0 + HEAD_DIM], nwin_ref[:, kc0:kc0 + HEAD_DIM]],
                                  axis=0).astype(BF16)
            v_w = jnp.concatenate([cwin_ref[:, vc0:vc0 + HEAD_DIM], nwin_ref[:, vc0:vc0 + HEAD_DIM]],
                                  axis=0).astype(BF16)
            s_w = lax.dot_general(qs, k_w, (((1,), (1,)), ((), ())), preferred_element_type=F32)
            for h in range(hpg):
                slope = slope_ref[g * hpg + h]
                r0 = h * tq
                sh = jnp.where(valid_n, s_n[r0:r0 + tq] + slope * r_n.astype(F32), -MASK_BIG)
                online(g, h, sh, v_n)
                o_s = acc_scr[g, r0:r0 + tq] / jnp.maximum(l_scr[g, r0:r0 + tq], 1e-30)
                sw = jnp.where(valid_w, s_w[r0:r0 + tq] - slope * dist_wf, NEG)
                m = jnp.max(sw, axis=-1, keepdims=True)
                e = jnp.where(valid_w, jnp.exp(sw - m), 0.0)
                p = e / jnp.maximum(jnp.sum(e, axis=-1, keepdims=True), 1e-30)
                o_w = jnp.dot(p.astype(BF16), v_w, preferred_element_type=F32)
                c0 = (g * hpg + h) * HEAD_DIM
                gc = g * 128 + 3 * h
                out = (oc[:, c0:c0 + HEAD_DIM] + gates[:, gc + 1:gc + 2] * o_s + gates[:, gc + 2:gc + 3] * o_w)
                o_ref[:, c0:c0 + HEAD_DIM] = out.astype(o_ref.dtype)


def _block_expand_tiles(n_tiles, tk):
    t = np.arange(n_tiles)[:, None, None]
    j = np.arange(N_CAND)[None, :, None]
    c = np.arange(tk)[None, None, :]
    return jnp.asarray(((t * tk + c) // SEL_BLOCK == j).astype(np.float32) * MASK_BIG, dtype=BF16)


def sample_sel_win_attention(q, sel, oc, gates, cache_slc, page_table, new_slc, new_win, cache_win, t_real):
    Bn, tq, D = q.shape
    n_heads = D // HEAD_DIM
    hpg = n_heads // N_KV
    n_pages = page_table.shape[1]
    n_pg = _pick(n_pages, SAMPLE_PAGES_PER_STEP)
    n_steps = n_pages // n_pg
    past_len = n_pages * PAGE_SIZE
    win_len = cache_win.shape[1]
    assert past_len // SEL_BLOCK <= N_CAND and past_len % SEL_BLOCK == 0 and t_real <= min(tq, SEL_BLOCK)
    slopes = alibi_slopes(n_heads).reshape(-1)
    e = _block_expand_tiles(n_steps, n_pg * PAGE_SIZE)
    rows = hpg * tq

    def pg_spec(p):
        return pl.BlockSpec((None, PAGE_SIZE, KV_COLS), lambda b, j, pt: (pt[b, j * n_pg + p], 0, 0))

    per_b = lambda shape: pl.BlockSpec((None,) + shape, lambda b, j, pt: (b, 0, 0))
    grid_spec = pltpu.PrefetchScalarGridSpec(
        num_scalar_prefetch=1,
        grid=(Bn, n_steps),
        in_specs=[pl.BlockSpec(memory_space=pltpu.SMEM),
                  per_b((tq, D)), per_b((tq, N_KV * N_CAND)), per_b((tq, D)), per_b((tq, N_KV * 128)),
                  pl.BlockSpec(e.shape, lambda b, j, pt: (0, 0, 0))]
        + [pg_spec(p) for p in range(n_pg)]
        + [per_b((NEW_ROWS_PAD, KV_COLS)), per_b((NEW_ROWS_PAD, KV_COLS)), per_b((win_len, KV_COLS))],
        out_specs=per_b((tq, D)),
        scratch_shapes=[pltpu.VMEM((N_KV, rows, 1), F32), pltpu.VMEM((N_KV, rows, 1), F32),
                        pltpu.VMEM((N_KV, rows, HEAD_DIM), F32)],
    )
    return pl.pallas_call(
        functools.partial(_sample_sel_win_kernel, n_pg=n_pg, hpg=hpg, t_real=t_real, n_steps=n_steps,
                          past_len=past_len, win_len=win_len),
        grid_spec=grid_spec,
        out_shape=jax.ShapeDtypeStruct((Bn, tq, D), BF16),
        compiler_params=_cparams(2),
        name="sample_sel_win_attn",
    )(page_table, slopes, q, sel, oc, gates, e, *([cache_slc] * n_pg), new_slc, new_win, cache_win)


def sample_nsa(q2d, gates_pad, kv_rows, kcvc, cache_slc, cache_win, page_table):
    _, kv_slc, kv_win = kv_rows
    Bn, T = kv_slc.shape[:2]
    D = q2d.shape[1]
    past_len = page_table.shape[1] * PAGE_SIZE
    n_past_blk = past_len // SEL_BLOCK
    pad_t = lambda a: jnp.pad(a.reshape(Bn, T, -1), ((0, 0), (0, SAMPLE_TQ - T), (0, 0)))
    q = pad_t(q2d)
    gates = pad_t(gates_pad)
    oc, sel = cmp_attention(q, kcvc, gates, tq=SAMPLE_TQ, pos0=past_len, pos_step=0, transposed=False,
                            k_sel=min(TOPK, n_past_blk + 1) - 1, n_cand=n_past_blk)
    pad_rows = lambda a: jnp.pad(a.reshape(Bn, T, KV_COLS), ((0, 0), (0, NEW_ROWS_PAD - T), (0, 0)))
    o = sample_sel_win_attention(q, sel, oc, gates, cache_slc.reshape(-1, PAGE_SIZE, KV_COLS), page_table,
                                 pad_rows(kv_slc), pad_rows(kv_win),
                                 cache_win.reshape(Bn, -1, KV_COLS), T)
    return o[:, :T].reshape(Bn * T, D)


def sample_attention_paged(kv_rows, cache_cmp, cache_slc, cache_win, page_table, cmp_pe, cmp_w1, cmp_b1, cmp_w2):
    T = kv_rows[0].shape[1]
    assert T < CMP_STRIDE, "new rows never complete a compression sub-block"
    pages = cache_cmp.reshape(-1, PAGE_SIZE, KV_COLS)
    kcvc = compress_blocks(cmp_lohi(pages, page_table, cmp_w1), cmp_pe, cmp_w1, cmp_b1, cmp_w2)

    def attend(q2d, gates_pad):
        return sample_nsa(q2d, gates_pad, kv_rows, kcvc, cache_slc, cache_win, page_table)

    return attend


def alibi_slopes(n_heads):
    exps = np.arange(1, n_heads + 1, dtype=np.float32) * np.float32(-8.0 / n_heads)
    return jnp.asarray(np.exp2(exps), dtype=F32).reshape(N_KV, n_heads // N_KV)


def masked_softmax(s, mask):
    s = jnp.where(mask, s, NEG)
    m = jnp.max(s, axis=-1, keepdims=True)
    p = jnp.where(mask, jnp.exp(s - m), 0.0)
    return p / jnp.maximum(jnp.sum(p, axis=-1, keepdims=True), 1e-30)


def s5_core(u, h0, lam_re, lam_im, log_dt, b_re, b_im, c_re, c_im, d_skip):
    Bn, L, D = u.shape
    G = D // SSM_GROUP
    uf = u.reshape(Bn, L, G, SSM_GROUP)
    lam = lax.complex(lam_re, lam_im)
    dt = jnp.exp(log_dt)[:, None]
    a_bar = jnp.exp(lam * dt)
    b_c = lax.complex(b_re, b_im)
    b_bar = ((a_bar - 1.0) / lam)[..., None] * b_c
    bu = lax.complex(jnp.einsum('blgc,gpc->blgp', uf, b_bar.real),
                     jnp.einsum('blgc,gpc->blgp', uf, b_bar.imag))
    h_init = lax.complex(h0[..., 0], h0[..., 1])
    bu = bu.at[:, 0].add(a_bar * h_init)
    a = jnp.broadcast_to(a_bar, bu.shape)

    def combine(e1, e2):
        a1, b1 = e1
        a2, b2 = e2
        return a1 * a2, a2 * b1 + b2

    _, h = lax.associative_scan(combine, (a, bu), axis=1)
    y = (jnp.einsum('blgp,gcp->blgc', h.real, c_re) - jnp.einsum('blgp,gcp->blgc', h.imag, c_im))
    y = y.reshape(Bn, L, D) + d_skip * u
    h_last = jnp.stack([h[:, -1].real, h[:, -1].imag], axis=-1)
    return jax.nn.gelu(y), h_last


def cmp_partials(rows, pe, w1):
    Bn, L = rows.shape[:2]
    sub = rows.reshape(Bn, L // CMP_STRIDE, CMP_STRIDE, 2, N_KV, HEAD_DIM)
    w_lo = w1[:, :CMP_STRIDE]
    w_hi = w1[:, CMP_STRIDE:]
    pe_lo = jnp.einsum('crd,crde->ce', pe[:, :CMP_STRIDE], w_lo)
    pe_hi = jnp.einsum('crd,crde->ce', pe[:, CMP_STRIDE:], w_hi)
    lo = jnp.einsum('bnrckd,crde->bncke', sub, w_lo) + pe_lo[:, None, :]
    hi = jnp.einsum('bnrckd,crde->bncke', sub, w_hi) + pe_hi[:, None, :]
    return lo, hi


def compress(lo, hi, b1, w2):
    h = jax.nn.gelu(lo[:, :-1] + hi[:, 1:] + b1[:, None, :])
    kv = jnp.einsum('bncke,ced->bnckd', h, w2)
    return kv[:, :, 0], kv[:, :, 1]


def nsa_attend(q, gates, pos, kc, vc, c_end, gather_sel, n_sel, kw, vw, w_pos):
    Bn, T = q.shape[:2]
    hpg = q.shape[3]
    slopes = alibi_slopes(N_KV * hpg)
    scale = HEAD_DIM ** -0.5
    dist_c = (pos[:, None] - c_end[None, :]).astype(F32)
    s_c = (jnp.einsum('btghd,bngd->btghn', q, kc) * scale
           - slopes[None, None, :, :, None] * dist_c[None, :, None, None, :])
    p_c = masked_softmax(s_c, (dist_c >= 0)[None, :, None, None, :])
    o_c = jnp.einsum('btghn,bngd->btghd', p_c, vc)
    imp = p_c.sum(axis=3)
    n_cmp = imp.shape[-1]
    imp = jnp.pad(imp, ((0, 0), (0, 0), (0, 0), (1, SUBS_PER_SEL * (n_sel + 1) - 1 - n_cmp)))
    r = imp.reshape(Bn, T, N_KV, n_sel + 1, SUBS_PER_SEL)
    imp_sel = r[..., :n_sel, :].sum(-1) + r[..., 1:, 0]
    blk = pos // SEL_BLOCK
    j = jnp.arange(n_sel)
    forced = (j[None, :] == 0) | (j[None, :] == blk[:, None]) | (j[None, :] == blk[:, None] - 1)
    visible = j[None, :] <= blk[:, None]
    score = jnp.where(visible[None, :, None, :], imp_sel + FORCE_BONUS * forced[None, :, None, :], NEG)
    top_s, idx = lax.top_k(score, min(TOPK, n_sel))
    valid = top_s > 0.5 * NEG
    ks, vs = gather_sel(idx)
    n_k = idx.shape[-1]
    s_pos = idx[..., None] * SEL_BLOCK + jnp.arange(SEL_BLOCK)
    dist_s = (pos[None, :, None, None, None] - s_pos).astype(F32)
    mask_s = valid[..., None] & (dist_s >= 0)
    s_s = (jnp.einsum('btghd,btgksd->btghks', q, ks) * scale
           - slopes[None, None, :, :, None, None] * dist_s[:, :, :, None])
    p_s = masked_softmax(s_s.reshape(Bn, T, N_KV, hpg, n_k * SEL_BLOCK),
                         mask_s[:, :, :, None].reshape(Bn, T, N_KV, 1, n_k * SEL_BLOCK))
    o_s = jnp.einsum('btghm,btgmd->btghd', p_s, vs.reshape(Bn, T, N_KV, n_k * SEL_BLOCK, HEAD_DIM))
    dist_w = pos[:, None] - w_pos[None, :]
    mask_w = (dist_w >= 0) & (dist_w < WINDOW) & (w_pos[None, :] >= 0)
    s_w = (jnp.einsum('btghd,bmgd->btghm', q, kw) * scale
           - slopes[None, None, :, :, None] * dist_w.astype(F32)[None, :, None, None, :])
    p_w = masked_softmax(s_w, mask_w[None, :, None, None, :])
    o_w = jnp.einsum('btghm,bmgd->btghd', p_w, vw)
    return gates[..., 0:1] * o_c + gates[..., 1:2] * o_s + gates[..., 2:3] * o_w


def prompt_attention(kv2d, kv_rows, cmp_pe, cmp_w1, cmp_b1, cmp_w2):
    kv_cmp = kv_rows[0]
    L = kv_cmp.shape[1]
    pages = kv_cmp.reshape(L // PAGE_SIZE, PAGE_SIZE, KV_COLS)
    table = jnp.arange(L // PAGE_SIZE, dtype=jnp.int32)[None]
    kcvc = compress_blocks(cmp_lohi(pages, table, cmp_w1), cmp_pe, cmp_w1, cmp_b1, cmp_w2)

    def attend(q2d, gates_pad):
        return prompt_nsa(q2d[None], gates_pad[None], kv2d, kcvc)

    return attend


def sample_attention(kv2d, kv_rows, cache_cmp, cache_slc, cache_win, page_table,
                     cmp_pe, cmp_w1, cmp_b1, cmp_w2):
    kv_cmp, kv_slc, kv_win = kv_rows
    Bn, T = kv_cmp.shape[:2]
    n_pages = page_table.shape[1]
    past_len = n_pages * PAGE_SIZE
    past_cmp = cache_cmp[page_table].reshape(Bn, past_len, 2, N_KV, HEAD_DIM)
    lo, hi = cmp_partials(past_cmp, cmp_pe, cmp_w1)
    n_new_sub = T // CMP_STRIDE
    if n_new_sub > 0:
        lo_new, hi_new = cmp_partials(kv_cmp[:, :n_new_sub * CMP_STRIDE], cmp_pe, cmp_w1)
        lo = jnp.concatenate([lo, lo_new], axis=1)
        hi = jnp.concatenate([hi, hi_new], axis=1)
    kc, vc = compress(lo, hi, cmp_b1, cmp_w2)
    c_end = jnp.arange(kc.shape[1]) * CMP_STRIDE + (CMP_BLOCK - 1)
    n_past_blk = past_len // SEL_BLOCK
    n_new_blk = -(-T // SEL_BLOCK)
    n_sel = n_past_blk + n_new_blk
    blk_per_page = PAGE_SIZE // SEL_BLOCK
    pool = cache_slc.reshape(cache_slc.shape[0], blk_per_page, SEL_BLOCK, 2, N_KV, HEAD_DIM)
    new_blocks = jnp.pad(kv_slc, ((0, 0), (0, n_new_blk * SEL_BLOCK - T), (0, 0), (0, 0), (0, 0)))
    new_blocks = new_blocks.reshape(Bn, n_new_blk, SEL_BLOCK, 2, N_KV, HEAD_DIM)
    b_i = jnp.arange(Bn)[:, None, None, None]
    g_i = jnp.arange(N_KV)[None, None, :, None]

    def gather_sel(idx):
        jp = jnp.minimum(idx, n_past_blk - 1)
        page = page_table[b_i, jp // blk_per_page]
        sub = jp % blk_per_page
        jn = jnp.clip(idx - n_past_blk, 0, n_new_blk - 1)
        is_new = (idx >= n_past_blk)[..., None, None]
        k = jnp.where(is_new, new_blocks[b_i, jn, :, 0, g_i, :], pool[page, sub, :, 0, g_i, :])
        v = jnp.where(is_new, new_blocks[b_i, jn, :, 1, g_i, :], pool[page, sub, :, 1, g_i, :])
        return k, v

    w_rows = jnp.concatenate([cache_win, kv_win], axis=1)
    w_pos = past_len - cache_win.shape[1] + jnp.arange(w_rows.shape[1])
    pos = past_len + jnp.arange(T)

    def attend(q2d, gates_pad):
        hpg = q2d.shape[1] // (N_KV * HEAD_DIM)
        q = q2d.reshape(Bn, T, N_KV, hpg, HEAD_DIM)
        gates = gates_pad.reshape(Bn, T, N_KV, 128)[..., :3 * hpg].reshape(Bn, T, N_KV, hpg, 3)
        o = nsa_attend(q, gates, pos, kc, vc, c_end, gather_sel, n_sel,
                       w_rows[:, :, 0], w_rows[:, :, 1], w_pos)
        return o.reshape(Bn * T, -1)

    return attend


def _rows(v, per_tok):
    return v if v.shape[0] == 1 else jnp.repeat(v, per_tok, axis=0)


def trunk(x, mods, kv_mod, h0, make_attend, p):
    Bn, L, D = x.shape
    M = Bn * L
    n_heads = D // HEAD_DIM
    hpg = n_heads // N_KV
    depth = p["mod_w"].shape[0]
    n_a = depth // 2
    xr = x.reshape(M, D)
    new_h = []
    kv_rows = None
    attend = None
    for l in range(depth):
        sh1, sc1, ga1, sh2, sc2, ga2 = [_rows(m, L) for m in jnp.split(mods[l], 6, axis=-1)]
        if l == n_a:
            shift, scale = [_rows(m, L) for m in jnp.split(kv_mod, 2, axis=-1)]
            hk = norm_mod(xr, p["kv_norm"], scale, shift, BF16)
            kv2d = mm_wide(hk, p["w_kv"], tn=512)
            kv = kv2d.reshape(Bn, L, 3, 2, N_KV, HEAD_DIM)
            kv_rows = (kv[:, :, 0], kv[:, :, 1], kv[:, :, 2])
            attend = make_attend(kv2d, kv_rows)
        if l < n_a:
            u = norm_mod(xr, p["norm_pre"][l, 0], sc1, sh1, F32)
            gy, h_last = s5_mixer_core(u.reshape(Bn, L, D), h0[l], p["ssm_lam_re"][l], p["ssm_lam_im"][l],
                                       p["ssm_log_dt"][l], p["ssm_b_re"][l], p["ssm_b_im"][l],
                                       p["ssm_c_re"][l], p["ssm_c_im"][l], p["ssm_d"][l])
            new_h.append(h_last)
            xr = mm_tall(gy, p["ssm_w_glu"], xr, p["norm_post"][l, 0], ga1, layer=l, glu=True)
        else:
            lb = l - n_a
            h = norm_mod(xr, p["norm_pre"][l, 0], sc1, sh1, BF16)
            q = mm_wide(h, p["nsa_w_qg"], layer=lb, n_out=n_heads * HEAD_DIM, tn=512)
            gates_pad = mm_wide(h, p["w_gate_pad"], layer=lb, epilogue="sigmoid", tn=512)
            o = attend(q, gates_pad)
            xr = mm_tall(o, p["nsa_w_o"], xr, p["norm_post"][l, 0], ga1, layer=lb)
        h = norm_mod(xr, p["norm_pre"][l, 1], sc2, sh2, BF16)
        f = mm_wide(h, p["mlp_w1"], layer=l, epilogue="sqrelu", out_dtype=BF16, tn=1024)
        xr = mm_tall(f, p["mlp_w2"], xr, p["norm_post"][l, 1], ga2, layer=l, tk=512)
    return xr.reshape(Bn, L, D), jnp.stack(new_h), kv_rows


def kernel(x_prompt, x_sample, c_prompt, c_sample, state_ssm, cache_cmp, cache_slc, cache_win, page_table, mod_w, mod_b, norm_pre, norm_post, mlp_w1, mlp_w2, ssm_lam_re, ssm_lam_im, ssm_log_dt, ssm_b_re, ssm_b_im, ssm_c_re, ssm_c_im, ssm_d, ssm_w_glu, kv_norm, kv_mod_w, kv_mod_b, w_kv, cmp_pe, cmp_w1, cmp_b1, cmp_w2, nsa_w_qg, nsa_w_o):
    D = x_prompt.shape[-1]
    depth = mod_w.shape[0]
    n_heads = D // HEAD_DIM
    bp, bs = c_prompt.shape[0], c_sample.shape[0]
    c_all = jnp.concatenate([c_prompt, c_sample], axis=0)
    n_c = c_all.shape[0]
    c_all = jnp.pad(c_all, ((0, -n_c % 8), (0, 0)))
    mods = [mm_wide(c_all, mod_w, layer=l, bias=mod_b, prologue="silu", exact=True, tn=512) for l in range(depth)]
    kv_mod = mm_wide(c_all, kv_mod_w, bias=kv_mod_b, prologue="silu", exact=True, tn=512)
    hpg = n_heads // N_KV
    w_gate = nsa_w_qg[:, :, n_heads * HEAD_DIM:].reshape(nsa_w_qg.shape[0], D, N_KV, 3 * hpg)
    w_gate_pad = jnp.pad(w_gate, ((0, 0), (0, 0), (0, 0), (0, 128 - 3 * hpg))).reshape(-1, D, N_KV * 128)
    p = dict(mod_w=mod_w, norm_pre=norm_pre, norm_post=norm_post, mlp_w1=mlp_w1, mlp_w2=mlp_w2,
             ssm_lam_re=ssm_lam_re, ssm_lam_im=ssm_lam_im, ssm_log_dt=ssm_log_dt, ssm_b_re=ssm_b_re,
             ssm_b_im=ssm_b_im, ssm_c_re=ssm_c_re, ssm_c_im=ssm_c_im, ssm_d=ssm_d, ssm_w_glu=ssm_w_glu,
             kv_norm=kv_norm, w_kv=w_kv, nsa_w_qg=nsa_w_qg, nsa_w_o=nsa_w_o, w_gate_pad=w_gate_pad)

    def make_prompt(kv2d, kv_rows):
        return prompt_attention(kv2d, kv_rows, cmp_pe, cmp_w1, cmp_b1, cmp_w2)

    def make_sample(kv2d, kv_rows):
        return sample_attention_paged(kv_rows, cache_cmp, cache_slc, cache_win, page_table,
                                      cmp_pe, cmp_w1, cmp_b1, cmp_w2)

    n_a = depth // 2
    G = D // SSM_GROUP
    h0_prompt = jnp.zeros((n_a, bp, G, STATE_DIM, 2), F32)
    y_prompt, ssm_prompt, rows_prompt = trunk(
        x_prompt, [m[:bp] for m in mods], kv_mod[:bp], h0_prompt, make_prompt, p)
    y_sample, ssm_sample, rows_sample = trunk(
        x_sample, [m[bp:bp + bs] for m in mods], kv_mod[bp:bp + bs], state_ssm, make_sample, p)
    cmp_prompt, slc_prompt, win_rows_prompt = rows_prompt
    cmp_sample, slc_sample, win_sample = rows_sample
    win_prompt = win_rows_prompt[:, -min(WINDOW, x_prompt.shape[1]):]
    return (y_prompt, y_sample, ssm_prompt, ssm_sample, cmp_prompt, cmp_sample,
            slc_prompt, slc_sample, win_prompt, win_sample)
```

```python
import functools
import math

import jax
import jax.numpy as jnp
import numpy as np
from jax import lax
from jax.experimental import pallas as pl
from jax.experimental.pallas import tpu as pltpu

F32 = jnp.float32
BF16 = jnp.bfloat16

SSM_GROUP = 16
STATE_DIM = 64
HEAD_DIM = 128
N_KV = 4
CMP_STRIDE = 16
CMP_BLOCK = 2 * CMP_STRIDE
SEL_BLOCK = 64
SUBS_PER_SEL = SEL_BLOCK // CMP_STRIDE
TOPK = 16
WINDOW = 512
QBLK = 128
PAGE_SIZE = 128
EPS = 1e-6
NEG = -1e30
FORCE_BONUS = 1e4

V7X_VMEM_LIMIT_BYTES = 56 * 1024 * 1024
HIGHEST = lax.Precision.HIGHEST
LOG2E = math.log2(math.e)


def _cparams(n_axes):
    return pltpu.CompilerParams(dimension_semantics=("arbitrary",) * n_axes,
                                vmem_limit_bytes=V7X_VMEM_LIMIT_BYTES)


def _pick(n, pref):
    if n <= pref:
        return n
    t = pref
    while n % t:
        t //= 2
    return t


def _norm_mod_kernel(x_ref, g_ref, sc_ref, sh_ref, o_ref):
    x = x_ref[...]
    r = lax.rsqrt(jnp.mean(x * x, axis=-1, keepdims=True) + EPS)
    y = (x * r) * g_ref[...]
    o_ref[...] = (y * (1.0 + sc_ref[...]) + sh_ref[...]).astype(o_ref.dtype)


def norm_mod(x, g, scale, shift, out_dtype):
    M, D = x.shape
    tm = _pick(M, 512)
    per_row = scale.shape[0] != 1
    mod_spec = pl.BlockSpec((tm, D), lambda i: (i, 0)) if per_row else pl.BlockSpec((1, D), lambda i: (0, 0))
    return pl.pallas_call(
        _norm_mod_kernel,
        grid=(M // tm,),
        in_specs=[pl.BlockSpec((tm, D), lambda i: (i, 0)),
                  pl.BlockSpec((1, D), lambda i: (0, 0)),
                  mod_spec, mod_spec],
        out_specs=pl.BlockSpec((tm, D), lambda i: (i, 0)),
        out_shape=jax.ShapeDtypeStruct((M, D), out_dtype),
        compiler_params=_cparams(1),
        name="norm_mod",
    )(x, g.reshape(1, D), scale, shift)


def _mm_wide_kernel(*refs, prologue, epilogue, has_bias, exact, n_w):
    a_ref = refs[0]
    w_refs = refs[1:1 + n_w]
    pos = 1 + n_w
    b_ref = refs[pos] if has_bias else None
    pos += int(has_bias)
    o_ref = refs[pos]
    wbf_refs = refs[pos + 1:]

    a = a_ref[...]
    if prologue == "silu":
        a = a * jax.nn.sigmoid(a)
    if exact:
        zs = [jnp.dot(a, w[...], preferred_element_type=F32, precision=HIGHEST) for w in w_refs]
    else:
        @pl.when(pl.program_id(1) == 0)
        def _():
            for w, wbf in zip(w_refs, wbf_refs):
                wbf[...] = w[...].astype(BF16)

        a = a.astype(BF16)
        zs = [jnp.dot(a, wbf[...], preferred_element_type=F32) for wbf in wbf_refs]
    z = zs[0]
    if has_bias:
        z = z + b_ref[...]
    if epilogue == "sqrelu":
        z = jnp.square(jnp.maximum(z, 0.0))
    elif epilogue == "sigmoid":
        z = jax.nn.sigmoid(z)
    elif epilogue == "glu":
        z = z * jax.nn.sigmoid(zs[1])
    o_ref[...] = z.astype(o_ref.dtype)


def mm_wide(a, w, *, layer=None, col0=0, n_out=None, bias=None, prologue=None, epilogue=None,
            exact=False, out_dtype=F32, tm=1024, tn=512):
    M, K = a.shape
    n_out = n_out if n_out is not None else w.shape[-1] - col0
    tm = _pick(M, tm)
    tn = _pick(n_out, tn)
    assert col0 % tn == 0 and n_out % tn == 0
    n_w = 2 if epilogue == "glu" else 1
    jb = col0 // tn

    def w_spec(extra):
        if layer is None:
            return pl.BlockSpec((K, tn), lambda j, i: (0, jb + extra + j))
        return pl.BlockSpec((None, K, tn), lambda j, i: (layer, 0, jb + extra + j))

    in_specs = [pl.BlockSpec((tm, K), lambda j, i: (i, 0))] + [w_spec(e * (n_out // tn)) for e in range(n_w)]
    args = [a] + [w] * n_w
    if bias is not None:
        if layer is None:
            in_specs.append(pl.BlockSpec((1, tn), lambda j, i: (0, jb + j)))
            args.append(bias.reshape(1, -1))
        else:
            in_specs.append(pl.BlockSpec((None, 1, tn), lambda j, i: (layer, 0, jb + j)))
            args.append(bias.reshape(bias.shape[0], 1, -1))
    scratch = [] if exact else [pltpu.VMEM((K, tn), BF16) for _ in range(n_w)]
    return pl.pallas_call(
        functools.partial(_mm_wide_kernel, prologue=prologue, epilogue=epilogue,
                          has_bias=bias is not None, exact=exact, n_w=n_w),
        grid=(n_out // tn, M // tm),
        in_specs=in_specs,
        out_specs=pl.BlockSpec((tm, tn), lambda j, i: (i, j)),
        out_shape=jax.ShapeDtypeStruct((M, n_out), out_dtype),
        scratch_shapes=scratch,
        compiler_params=_cparams(2),
        name="mm_wide",
    )(*args)


def _mm_tall_kernel(a_ref, *refs, n_w, nk):
    w_refs = refs[:n_w]
    res_ref, g_ref, ga_ref, o_ref = refs[n_w:n_w + 4]
    acc_refs = refs[n_w + 4:]
    k = pl.program_id(1)

    @pl.when(k == 0)
    def _():
        for acc in acc_refs:
            acc[...] = jnp.zeros_like(acc)

    a = a_ref[...].astype(BF16)
    for w, acc in zip(w_refs, acc_refs):
        acc[...] += jnp.dot(a, w[...].astype(BF16), preferred_element_type=F32)

    @pl.when(k == nk - 1)
    def _():
        m = acc_refs[0][...]
        if n_w == 2:
            m = m * jax.nn.sigmoid(acc_refs[1][...])
        r = lax.rsqrt(jnp.mean(m * m, axis=-1, keepdims=True) + EPS)
        o_ref[...] = res_ref[...] + ga_ref[...] * ((m * r) * g_ref[...])


def mm_tall(a, w, res, g, gate, *, layer=None, glu=False, tm=512, tk=256):
    M, K = a.shape
    N = res.shape[1]
    tm = _pick(M, tm)
    tk = _pick(K, tk)
    nk = K // tk
    n_w = 2 if glu else 1

    def w_spec(e):
        if layer is None:
            return pl.BlockSpec((tk, N), lambda i, k: (k, e))
        return pl.BlockSpec((None, tk, N), lambda i, k: (layer, k, e))

    per_row = gate.shape[0] != 1
    ga_spec = pl.BlockSpec((tm, N), lambda i, k: (i, 0)) if per_row else pl.BlockSpec((1, N), lambda i, k: (0, 0))
    return pl.pallas_call(
        functools.partial(_mm_tall_kernel, n_w=n_w, nk=nk),
        grid=(M // tm, nk),
        in_specs=[pl.BlockSpec((tm, tk), lambda i, k: (i, k))] + [w_spec(e) for e in range(n_w)]
        + [pl.BlockSpec((tm, N), lambda i, k: (i, 0)),
           pl.BlockSpec((1, N), lambda i, k: (0, 0)),
           ga_spec],
        out_specs=pl.BlockSpec((tm, N), lambda i, k: (i, 0)),
        out_shape=jax.ShapeDtypeStruct((M, N), F32),
        scratch_shapes=[pltpu.VMEM((tm, N), F32) for _ in range(n_w)],
        compiler_params=_cparams(2),
        name="mm_tall",
    )(a, *([w] * n_w), res, g.reshape(1, N), gate)


S5_GROUPS_PER_STEP = 8
P2 = 2 * STATE_DIM


def _s5_prep_kernel(lam_re_ref, lam_im_ref, ldt_ref, btr_ref, bti_ref, cr_ref, ci_ref,
                    mt_ref, bt_ref, cs_ref, at_ref, *, tc, gb):
    C = SSM_GROUP
    tcc = tc * C
    wk = max(tcc, 128)
    lane = lax.broadcasted_iota(jnp.int32, (C, P2), 1)
    is_re = lane < STATE_DIM
    kk = lax.broadcasted_iota(jnp.int32, (tc + 1, P2), 0).astype(F32)
    lane_k = lax.broadcasted_iota(jnp.int32, (C, wk), 1)
    for gg in range(gb):
        lr = lam_re_ref[gg]
        li = lam_im_ref[gg]
        dt = jnp.exp(ldt_ref[gg])
        mag = jnp.exp(kk * (lr * dt))
        ang = kk * (li * dt)
        pr = mag * jnp.cos(ang)
        pi = mag * jnp.sin(ang)
        x = pr[1:2] - 1.0
        y = pi[1:2]
        den = lr * lr + li * li
        cfr = (x * lr + y * li) / den
        cfi = (y * lr - x * li) / den
        btr = btr_ref[gg]
        bti = bti_ref[gg]
        bbr = cfr * btr - cfi * bti
        bbi = cfr * bti + cfi * btr
        cre = cr_ref[gg]
        cim = ci_ref[gg]

        def bm(k):
            return jnp.where(is_re, pr[k:k + 1] * bbr - pi[k:k + 1] * bbi, pr[k:k + 1] * bbi + pi[k:k + 1] * bbr)

        def cm(k):
            return jnp.where(is_re, cre * pr[k:k + 1] - cim * pi[k:k + 1], -(cre * pi[k:k + 1] + cim * pr[k:k + 1]))

        cms = [cm(k) for k in range(tc + 1)]
        cs0 = jnp.concatenate(cms[:tc] + [jnp.zeros((wk - tcc, P2), F32)] * (wk > tcc), axis=0)
        kst = lax.dot_general(bm(0), cs0, (((1,), (1,)), ((), ())), preferred_element_type=F32,
                              precision=HIGHEST)
        for s in range(tc):
            shifted = kst if s == 0 else jnp.where(lane_k >= s * C, pltpu.roll(kst, s * C, 1), 0.0)
            mt_ref[gg, s * C:(s + 1) * C, :] = shifted[:, :tcc]
            bt_ref[gg, s * C:(s + 1) * C, :] = bm(tc - 1 - s)
            cs_ref[gg, s * C:(s + 1) * C, :] = cms[s + 1]
        at_ref[gg, 0:1, :] = pr[tc:tc + 1]
        at_ref[gg, 1:2, :] = jnp.where(is_re[0:1], -pi[tc:tc + 1], pi[tc:tc + 1])


def s5_prep(lam_re, lam_im, log_dt, b_re, b_im, c_re, c_im, tc):
    G = lam_re.shape[0]
    gb = _pick(G, S5_GROUPS_PER_STEP)
    C = SSM_GROUP
    tcc = tc * C
    dup = lambda v: jnp.concatenate([v, v], axis=-1)
    lam_re2 = dup(lam_re)[:, None, :]
    lam_im2 = dup(lam_im)[:, None, :]
    ldt2 = jnp.broadcast_to(log_dt[:, None, None], (G, 1, P2))
    btr = dup(jnp.swapaxes(b_re, 1, 2))
    bti = dup(jnp.swapaxes(b_im, 1, 2))
    cr2 = dup(c_re)
    ci2 = dup(c_im)
    vec = pl.BlockSpec((gb, 1, P2), lambda i: (i, 0, 0))
    mat = pl.BlockSpec((gb, C, P2), lambda i: (i, 0, 0))
    return pl.pallas_call(
        functools.partial(_s5_prep_kernel, tc=tc, gb=gb),
        grid=(G // gb,),
        in_specs=[vec, vec, vec, mat, mat, mat, mat],
        out_specs=[pl.BlockSpec((gb, tcc, tcc), lambda i: (i, 0, 0)),
                   pl.BlockSpec((gb, tcc, P2), lambda i: (i, 0, 0)),
                   pl.BlockSpec((gb, tcc, P2), lambda i: (i, 0, 0)),
                   pl.BlockSpec((gb, 2, P2), lambda i: (i, 0, 0))],
        out_shape=[jax.ShapeDtypeStruct((G, tcc, tcc), F32),
                   jax.ShapeDtypeStruct((G, tcc, P2), F32),
                   jax.ShapeDtypeStruct((G, tcc, P2), F32),
                   jax.ShapeDtypeStruct((G, 2, P2), F32)],
        compiler_params=_cparams(1),
        name="s5_prep",
    )(lam_re2, lam_im2, ldt2, btr, bti, cr2, ci2)


def _gmm_kernel(*refs, gb, two):
    if two:
        a_ref, w_ref, a2_ref, w2_ref, o_ref = refs
    else:
        a_ref, w_ref, o_ref = refs
    for gg in range(gb):
        z = jnp.dot(a_ref[gg], w_ref[gg], preferred_element_type=F32, precision=HIGHEST)
        if two:
            z = z + lax.dot_general(a2_ref[gg], w2_ref[gg], (((1,), (1,)), ((), ())),
                                    preferred_element_type=F32, precision=HIGHEST)
        o_ref[gg] = z


def gmm(a, w, a2=None, w2=None):
    G, R, K = a.shape
    N = w.shape[2]
    gb = _pick(G, S5_GROUPS_PER_STEP)
    two = a2 is not None
    spec = lambda arr: pl.BlockSpec((gb,) + arr.shape[1:], lambda i: (i, 0, 0))
    args = [a, w] + ([a2, w2] if two else [])
    return pl.pallas_call(
        functools.partial(_gmm_kernel, gb=gb, two=two),
        grid=(G // gb,),
        in_specs=[spec(x) for x in args],
        out_specs=pl.BlockSpec((gb, R, N), lambda i: (i, 0, 0)),
        out_shape=jax.ShapeDtypeStruct((G, R, N), F32),
        compiler_params=_cparams(1),
        name="s5_gmm",
    )(*args)


S5_CHUNKS_PER_STEP = 64


def _s5_scan_kernel(s_ref, h0_ref, at_ref, hin_ref, hfin_ref, h_scr, *, cb, n_steps):
    j = pl.program_id(1)

    @pl.when(j == 0)
    def _():
        h_scr[...] = h0_ref[...]

    ar = at_ref[0]
    ai = at_ref[1]

    def step(c, h):
        hin_ref[c] = h
        return ar * h + ai * pltpu.roll(h, STATE_DIM, 1) + s_ref[c]

    h = lax.fori_loop(0, cb, step, h_scr[...])
    h_scr[...] = h

    @pl.when(j == n_steps - 1)
    def _():
        hfin_ref[...] = h


def s5_scan(s, h0, at):
    Bn, n_chunk, G, _ = s.shape
    cb = _pick(n_chunk, S5_CHUNKS_PER_STEP)
    n_steps = n_chunk // cb
    return pl.pallas_call(
        functools.partial(_s5_scan_kernel, cb=cb, n_steps=n_steps),
        grid=(Bn, n_steps),
        in_specs=[pl.BlockSpec((None, cb, G, P2), lambda b, j: (b, j, 0, 0)),
                  pl.BlockSpec((None, G, P2), lambda b, j: (b, 0, 0)),
                  pl.BlockSpec((2, G, P2), lambda b, j: (0, 0, 0))],
        out_specs=[pl.BlockSpec((None, cb, G, P2), lambda b, j: (b, j, 0, 0)),
                   pl.BlockSpec((None, G, P2), lambda b, j: (b, 0, 0))],
        out_shape=[jax.ShapeDtypeStruct((Bn, n_chunk, G, P2), F32),
                   jax.ShapeDtypeStruct((Bn, G, P2), F32)],
        scratch_shapes=[pltpu.VMEM((G, P2), F32)],
        compiler_params=_cparams(2),
        name="s5_scan",
    )(s, h0, at)


def _s5_out_kernel(y_ref, u_ref, d_ref, o_ref):
    o_ref[...] = jax.nn.gelu(y_ref[...] + d_ref[...] * u_ref[...]).astype(o_ref.dtype)


def s5_out(y, u, d_skip):
    M, D = y.shape
    tm = _pick(M, 512)
    row = pl.BlockSpec((tm, D), lambda i: (i, 0))
    return pl.pallas_call(
        _s5_out_kernel,
        grid=(M // tm,),
        in_specs=[row, row, pl.BlockSpec((1, D), lambda i: (0, 0))],
        out_specs=row,
        out_shape=jax.ShapeDtypeStruct((M, D), BF16),
        compiler_params=_cparams(1),
        name="s5_out",
    )(y, u, d_skip.reshape(1, D))


def s5_mixer_core(u, h0, lam_re, lam_im, log_dt, b_re, b_im, c_re, c_im, d_skip):
    Bn, L, D = u.shape
    G = D // SSM_GROUP
    C = SSM_GROUP
    tc = _pick(L, 16)
    n_chunk = L // tc
    mt, bt, cs, at = s5_prep(lam_re, lam_im, log_dt, b_re, b_im, c_re, c_im, tc)
    x = u.reshape(Bn, n_chunk, tc, G, C).transpose(3, 0, 1, 2, 4).reshape(G, Bn * n_chunk, tc * C)
    s = gmm(x, bt)
    s = s.reshape(G, Bn, n_chunk, P2).transpose(1, 2, 0, 3)
    h0v = jnp.concatenate([h0[..., 0], h0[..., 1]], axis=-1)
    hin, hfin = s5_scan(s, h0v, at.transpose(1, 0, 2))
    hin = hin.transpose(2, 0, 1, 3).reshape(G, Bn * n_chunk, P2)
    y = gmm(x, mt, hin, cs)
    y = y.reshape(G, Bn, n_chunk, tc, C).transpose(1, 2, 3, 0, 4).reshape(Bn * L, D)
    h_last = jnp.stack([hfin[..., :STATE_DIM], hfin[..., STATE_DIM:]], axis=-1)
    return s5_out(y, u.reshape(Bn * L, D), d_skip), h_last


SUBS_PER_PAGE = PAGE_SIZE // CMP_STRIDE
KV_COLS = 2 * N_KV * HEAD_DIM
CMP_PAGES_PER_STEP = 8


def _cmp_lohi_kernel(pt_ref, *refs, n_pg):
    x_refs = refs[:2 * n_pg]
    w_ref = refs[2 * n_pg]
    o_ref = refs[2 * n_pg + 1]
    rows = n_pg * SUBS_PER_PAGE
    for c in range(2):
        acc = jnp.zeros((N_KV * rows, 2 * HEAD_DIM), F32)
        for rp in range(CMP_STRIDE // 2):
            parts = []
            for k in range(N_KV):
                for p in range(n_pg):
                    a0 = x_refs[2 * p + c][:, 2 * rp, k, :]
                    a1 = x_refs[2 * p + c][:, 2 * rp + 1, k, :]
                    parts.append(jnp.concatenate([a0, a1], axis=1))
            xs = jnp.concatenate(parts, axis=0).astype(BF16)
            acc = acc + jnp.dot(xs, w_ref[c, rp], preferred_element_type=F32)
        for k in range(N_KV):
            o_ref[c, k] = acc[k * rows:(k + 1) * rows]


def cmp_lohi(pages, page_table, cmp_w1):
    Bn, n_pages = page_table.shape
    n_pg = _pick(n_pages, CMP_PAGES_PER_STEP)
    x = pages.reshape(pages.shape[0], SUBS_PER_PAGE, CMP_STRIDE, 2, N_KV, HEAD_DIM)
    half = CMP_STRIDE // 2
    w_lo = cmp_w1[:, :CMP_STRIDE].reshape(2, half, 2 * HEAD_DIM, HEAD_DIM)
    w_hi = cmp_w1[:, CMP_STRIDE:].reshape(2, half, 2 * HEAD_DIM, HEAD_DIM)
    w = jnp.concatenate([w_lo, w_hi], axis=-1).astype(BF16)
    n_sub = n_pages * SUBS_PER_PAGE

    def x_spec(p, c):
        return pl.BlockSpec((None, SUBS_PER_PAGE, CMP_STRIDE, None, N_KV, HEAD_DIM),
                            lambda b, i, pt: (pt[b, i * n_pg + p], 0, 0, c, 0, 0))

    grid_spec = pltpu.PrefetchScalarGridSpec(
        num_scalar_prefetch=1,
        grid=(Bn, n_pages // n_pg),
        in_specs=[x_spec(p, c) for p in range(n_pg) for c in range(2)]
        + [pl.BlockSpec(w.shape, lambda b, i, pt: (0, 0, 0, 0))],
        out_specs=pl.BlockSpec((None, 2, N_KV, n_pg * SUBS_PER_PAGE, 2 * HEAD_DIM),
                               lambda b, i, pt: (b, 0, 0, i, 0)),
    )
    return pl.pallas_call(
        functools.partial(_cmp_lohi_kernel, n_pg=n_pg),
        grid_spec=grid_spec,
        out_shape=jax.ShapeDtypeStruct((Bn, 2, N_KV, n_sub, 2 * HEAD_DIM), F32),
        compiler_params=_cparams(2),
        name="cmp_lohi",
    )(page_table, *([x] * (2 * n_pg)), w)


def _compress_kernel(x_ref, pe_ref, w1_ref, b1_ref, w2_ref, o_ref, pe_scr):
    n_sub = x_ref.shape[0]
    half = w1_ref.shape[0] // 2
    c = pl.program_id(1)

    @pl.when((pl.program_id(0) == 0) & (pl.program_id(2) == 0))
    def _():
        pe = jnp.broadcast_to(pe_ref[...], (8, 2 * half))
        pe_scr[c, 0] = jnp.dot(pe[:, :half], w1_ref[:half], preferred_element_type=F32, precision=HIGHEST)
        pe_scr[c, 1] = jnp.dot(pe[:, half:], w1_ref[half:], preferred_element_type=F32, precision=HIGHEST)

    x = x_ref[...]
    lo = x[:, :HEAD_DIM] + pe_scr[c, 0][0:1]
    hi = x[:, HEAD_DIM:] + pe_scr[c, 1][0:1]
    hi_next = pltpu.roll(hi, n_sub - 1, 0)
    h = jax.nn.gelu(lo + hi_next + b1_ref[...])
    o_ref[...] = jnp.dot(h.astype(BF16), w2_ref[...].astype(BF16), preferred_element_type=F32).astype(o_ref.dtype)


def compress_blocks(lohi, cmp_pe, cmp_w1, cmp_b1, cmp_w2):
    Bn, _, _, n_sub, _ = lohi.shape
    kdim = CMP_BLOCK * HEAD_DIM
    return pl.pallas_call(
        _compress_kernel,
        grid=(Bn, 2, N_KV),
        in_specs=[pl.BlockSpec((None, None, None, n_sub, 2 * HEAD_DIM), lambda b, c, k: (b, c, k, 0, 0)),
                  pl.BlockSpec((None, 1, kdim), lambda b, c, k: (c, 0, 0)),
                  pl.BlockSpec((None, kdim, HEAD_DIM), lambda b, c, k: (c, 0, 0)),
                  pl.BlockSpec((None, 1, HEAD_DIM), lambda b, c, k: (c, 0, 0)),
                  pl.BlockSpec((None, HEAD_DIM, HEAD_DIM), lambda b, c, k: (c, 0, 0))],
        out_specs=pl.BlockSpec((None, None, None, n_sub, HEAD_DIM), lambda b, c, k: (b, c, k, 0, 0)),
        out_shape=jax.ShapeDtypeStruct((Bn, 2, N_KV, n_sub, HEAD_DIM), BF16),
        scratch_shapes=[pltpu.VMEM((2, 2, 8, HEAD_DIM), F32)],
        compiler_params=_cparams(3),
        name="compress",
    )(lohi, cmp_pe.reshape(2, 1, kdim), cmp_w1.reshape(2, kdim, HEAD_DIM),
      cmp_b1.reshape(2, 1, HEAD_DIM), cmp_w2)


N_CAND = 128
NEG_TAKEN = -3e38


def _topk_mask(score, axis, k_sel):
    idx = lax.broadcasted_iota(jnp.int32, score.shape, axis)
    sel = jnp.zeros(score.shape, F32)
    for _ in range(k_sel):
        m = jnp.max(score, axis=axis, keepdims=True)
        first = jnp.min(jnp.where(score == m, idx, N_CAND), axis=axis, keepdims=True)
        hit = idx == first
        sel = jnp.where(hit & (m > 0.5 * NEG), 1.0, sel)
        score = jnp.where(hit, NEG_TAKEN, score)
    return sel


def _cmp_attn_kernel(slope_ref, q_ref, kc_ref, vc_ref, gate_ref, wsel_ref, oc_ref, sel_ref, *,
                     tq, hpg, gps, pos0, pos_step, transposed, k_sel, n_cand):
    i = pl.program_id(1)
    n_cmp = kc_ref.shape[1]
    gw = hpg * HEAD_DIM
    scale = HEAD_DIM ** -0.5
    base = pos0 + i * pos_step
    t_idx = lax.broadcasted_iota(jnp.int32, (tq, n_cmp), 0)
    n_idx = lax.broadcasted_iota(jnp.int32, (tq, n_cmp), 1)
    dist_i = base + t_idx - (n_idx * CMP_STRIDE + (CMP_BLOCK - 1))
    valid = dist_i >= 0
    dist = dist_i.astype(F32)
    gates = gate_ref[...]
    shape, j_ax, t_ax = ((N_CAND, tq), 0, 1) if transposed else ((tq, N_CAND), 1, 0)
    j = lax.broadcasted_iota(jnp.int32, shape, j_ax)
    blk = (base + lax.broadcasted_iota(jnp.int32, shape, t_ax)) // SEL_BLOCK
    forced = (j == 0) | (j == blk) | (j == blk - 1)
    visible = (j <= blk) & (j < n_cand)
    scores = []
    for gi in range(gps):
        g = pl.program_id(2) * gps + gi
        q = q_ref[:, gi * gw:(gi + 1) * gw]
        qs = jnp.concatenate([q[:, h * HEAD_DIM:(h + 1) * HEAD_DIM] for h in range(hpg)], axis=0)
        qs = (qs * scale).astype(BF16)
        s = lax.dot_general(qs, kc_ref[gi], (((1,), (1,)), ((), ())), preferred_element_type=F32)
        vc = vc_ref[gi]
        psum = jnp.zeros((tq, n_cmp), F32)
        for h in range(hpg):
            sh = s[h * tq:(h + 1) * tq] - slope_ref[g * hpg + h] * dist
            sh = jnp.where(valid, sh, NEG)
            m = jnp.max(sh, axis=-1, keepdims=True)
            e = jnp.where(valid, jnp.exp(sh - m), 0.0)
            p = e / jnp.maximum(jnp.sum(e, axis=-1, keepdims=True), 1e-30)
            psum = psum + p
            o_h = jnp.dot(p.astype(BF16), vc, preferred_element_type=F32)
            c0 = gi * gw + h * HEAD_DIM
            oc_ref[:, c0:c0 + HEAD_DIM] = o_h * gates[:, gi * 128 + 3 * h:gi * 128 + 3 * h + 1]
        if transposed:
            imp = lax.dot_general(wsel_ref[...], psum, (((1,), (1,)), ((), ())),
                                  preferred_element_type=F32, precision=HIGHEST)
        else:
            imp = lax.dot_general(psum, wsel_ref[...], (((1,), (1,)), ((), ())),
                                  preferred_element_type=F32, precision=HIGHEST)
        scores.append(jnp.where(visible, imp + jnp.where(forced, FORCE_BONUS, 0.0), NEG))
    if transposed:
        for gi in range(gps):
            sel = _topk_mask(scores[gi], 0, k_sel).T
            sel_ref[:, gi * N_CAND:(gi + 1) * N_CAND] = sel.astype(sel_ref.dtype)
    else:
        sel = _topk_mask(jnp.concatenate(scores, axis=0), 1, k_sel)
        for gi in range(gps):
            sel_ref[:, gi * N_CAND:(gi + 1) * N_CAND] = sel[gi * tq:(gi + 1) * tq].astype(sel_ref.dtype)


def _sel_weights(n_cmp_pad, n_cmp):
    j = np.arange(N_CAND)[:, None]
    n = np.arange(n_cmp_pad)[None, :]
    w = (n >= SUBS_PER_SEL * j - 1) & (n <= SUBS_PER_SEL * j + SUBS_PER_SEL - 1) & (n < n_cmp)
    return jnp.asarray(w.astype(np.float32))


def cmp_attention(q, kcvc, gates, *, tq, pos0, pos_step, transposed, k_sel, n_cand):
    Bn, T, D = q.shape
    n_heads = D // HEAD_DIM
    hpg = n_heads // N_KV
    n_sub = kcvc.shape[3]
    gw = hpg * HEAD_DIM
    slopes = alibi_slopes(n_heads).reshape(-1)
    wsel = _sel_weights(n_sub, n_sub - 1)
    gps = N_KV if tq < 128 else 1
    kern = functools.partial(_cmp_attn_kernel, tq=tq, hpg=hpg, gps=gps, pos0=pos0, pos_step=pos_step,
                             transposed=transposed, k_sel=k_sel, n_cand=n_cand)
    return pl.pallas_call(
        kern,
        grid=(Bn, T // tq, N_KV // gps),
        in_specs=[pl.BlockSpec(memory_space=pltpu.SMEM),
                  pl.BlockSpec((None, tq, gps * gw), lambda b, i, g: (b, i, g)),
                  pl.BlockSpec((None, None, gps, n_sub, HEAD_DIM), lambda b, i, g: (b, 0, g, 0, 0)),
                  pl.BlockSpec((None, None, gps, n_sub, HEAD_DIM), lambda b, i, g: (b, 1, g, 0, 0)),
                  pl.BlockSpec((None, tq, gps * 128), lambda b, i, g: (b, i, g)),
                  pl.BlockSpec((N_CAND, n_sub), lambda b, i, g: (0, 0))],
        out_specs=[pl.BlockSpec((None, tq, gps * gw), lambda b, i, g: (b, i, g)),
                   pl.BlockSpec((None, tq, gps * N_CAND), lambda b, i, g: (b, i, g))],
        out_shape=[jax.ShapeDtypeStruct((Bn, T, D), F32),
                   jax.ShapeDtypeStruct((Bn, T, N_KV * N_CAND), BF16)],
        compiler_params=_cparams(3),
        name="cmp_attn",
    )(slopes, q, kcvc, kcvc, gates, wsel)


SEL_TK = 512
MASK_BIG = 1e30


def _sel_win_kernel(slope_ref, q_ref, ks_ref, vs_ref, kw_ref, vw_ref, sel_ref, oc_ref, gate_ref, e_ref,
                    o_ref, ksb, vsb, kwb, vwb, *, tq, hpg):
    g = pl.program_id(0)
    i = pl.program_id(1)
    L = ks_ref.shape[0]
    s0 = i * tq
    scale = HEAD_DIM ** -0.5

    @pl.when(i == 0)
    def _():
        ksb[:, :HEAD_DIM] = ks_ref[...].astype(BF16)
        ksb[:, HEAD_DIM:] = e_ref[...]
        vsb[...] = vs_ref[...].astype(BF16)
        kwb[0:WINDOW] = jnp.zeros((WINDOW, HEAD_DIM), BF16)
        vwb[0:WINDOW] = jnp.zeros((WINDOW, HEAD_DIM), BF16)
        kwb[WINDOW:] = kw_ref[...].astype(BF16)
        vwb[WINDOW:] = vw_ref[...].astype(BF16)

    q = q_ref[...]
    qs = jnp.concatenate([q[:, h * HEAD_DIM:(h + 1) * HEAD_DIM] for h in range(hpg)], axis=0)
    qs = (qs * (scale * LOG2E)).astype(BF16)
    slopes = [slope_ref[g * hpg + h] * LOG2E for h in range(hpg)]
    neg_sel = sel_ref[...] - 1.0
    qaug = jnp.concatenate([qs, jnp.concatenate([neg_sel] * hpg, axis=0)], axis=1)
    c_iota = lax.broadcasted_iota(jnp.int32, (1, SEL_TK), 1)
    kd = s0 // SEL_TK

    def sweep(kt, carry, diagonal):
        ms, ls, accs = carry
        k0 = pl.multiple_of(kt * SEL_TK, SEL_TK)
        k_t = ksb[pl.ds(k0, SEL_TK), :]
        v_t = vsb[pl.ds(k0, SEL_TK), :]
        if diagonal:
            t_idx = lax.broadcasted_iota(jnp.int32, (tq, SEL_TK), 0)
            c_idx = lax.broadcasted_iota(jnp.int32, (tq, SEL_TK), 1)
            causal = s0 + t_idx >= k0 + c_idx
        rel = (k0 - s0 + c_iota).astype(F32)
        new_m, new_l, new_acc = [], [], []
        s = lax.dot_general(qaug, k_t, (((1,), (1,)), ((), ())), preferred_element_type=F32)
        for h in range(hpg):
            sh = s[h * tq:(h + 1) * tq] + slopes[h] * rel
            if diagonal:
                sh = jnp.where(causal, sh, -MASK_BIG)
            m_new = jnp.maximum(ms[h], jnp.max(sh, axis=-1, keepdims=True))
            p = jnp.exp2(sh - m_new)
            alpha = jnp.exp2(ms[h] - m_new)
            new_l.append(alpha * ls[h] + jnp.sum(p, axis=-1, keepdims=True))
            new_acc.append(alpha * accs[h] + jnp.dot(p.astype(BF16), v_t, preferred_element_type=F32))
            new_m.append(m_new)
        return new_m, new_l, new_acc

    init = ([jnp.full((tq, 1), NEG, F32)] * hpg, [jnp.zeros((tq, 1), F32)] * hpg,
            [jnp.zeros((tq, HEAD_DIM), F32)] * hpg)
    carry = lax.fori_loop(0, kd, lambda kt, c: tuple(sweep(kt, c, False)), tuple(init))
    ms, ls, accs = sweep(kd, carry, True)

    wn = WINDOW + tq
    w0 = pl.multiple_of(s0, tq)
    kw_t = kwb[pl.ds(w0, wn), :]
    vw_t = vwb[pl.ds(w0, wn), :]
    sw = lax.dot_general(qs, kw_t, (((1,), (1,)), ((), ())), preferred_element_type=F32)
    t_idx = lax.broadcasted_iota(jnp.int32, (tq, wn), 0)
    c_idx = lax.broadcasted_iota(jnp.int32, (tq, wn), 1)
    dist_i = t_idx + WINDOW - c_idx
    valid = (dist_i >= 0) & (dist_i < WINDOW) & (c_idx + s0 >= WINDOW)
    dist = dist_i.astype(F32)
    gates = gate_ref[...]
    oc = oc_ref[...]
    for h in range(hpg):
        sh = jnp.where(valid, sw[h * tq:(h + 1) * tq] - slopes[h] * dist, NEG)
        m = jnp.max(sh, axis=-1, keepdims=True)
        e = jnp.where(valid, jnp.exp2(sh - m), 0.0)
        p = e / jnp.maximum(jnp.sum(e, axis=-1, keepdims=True), 1e-30)
        o_w = jnp.dot(p.astype(BF16), vw_t, preferred_element_type=F32)
        o_s = accs[h] / jnp.maximum(ls[h], 1e-30)
        out = (oc[:, h * HEAD_DIM:(h + 1) * HEAD_DIM] + gates[:, 3 * h + 1:3 * h + 2] * o_s
               + gates[:, 3 * h + 2:3 * h + 3] * o_w)
        o_ref[:, h * HEAD_DIM:(h + 1) * HEAD_DIM] = out.astype(o_ref.dtype)


def _block_onehot(n_keys):
    key = np.arange(n_keys)[:, None]
    j = np.arange(N_CAND)[None, :]
    return jnp.asarray((key // SEL_BLOCK == j).astype(np.float32) * MASK_BIG, dtype=BF16)


def sel_win_attention(q, kv, sel, oc, gates, *, tq):
    _, L, D = q.shape
    n_heads = D // HEAD_DIM
    hpg = n_heads // N_KV
    gw = hpg * HEAD_DIM
    assert L % SEL_TK == 0 and SEL_TK % tq == 0 and L // SEL_BLOCK <= N_CAND
    slopes = alibi_slopes(n_heads).reshape(-1)
    e = _block_onehot(L)

    def kv_spec(branch, which):
        cb = (branch * 2 + which) * N_KV
        return pl.BlockSpec((L, HEAD_DIM), lambda g, i: (0, cb + g))

    return pl.pallas_call(
        functools.partial(_sel_win_kernel, tq=tq, hpg=hpg),
        grid=(N_KV, L // tq),
        in_specs=[pl.BlockSpec(memory_space=pltpu.SMEM),
                  pl.BlockSpec((None, tq, gw), lambda g, i: (0, i, g)),
                  kv_spec(1, 0), kv_spec(1, 1), kv_spec(2, 0), kv_spec(2, 1),
                  pl.BlockSpec((None, tq, N_CAND), lambda g, i: (0, i, g)),
                  pl.BlockSpec((None, tq, gw), lambda g, i: (0, i, g)),
                  pl.BlockSpec((None, tq, 128), lambda g, i: (0, i, g)),
                  pl.BlockSpec(e.shape, lambda g, i: (0, 0))],
        out_specs=pl.BlockSpec((tq, gw), lambda g, i: (i, g)),
        out_shape=jax.ShapeDtypeStruct((L, D), BF16),
        scratch_shapes=[pltpu.VMEM((L, 2 * HEAD_DIM), BF16), pltpu.VMEM((L, HEAD_DIM), BF16),
                        pltpu.VMEM((WINDOW + L, HEAD_DIM), BF16), pltpu.VMEM((WINDOW + L, HEAD_DIM), BF16)],
        compiler_params=_cparams(2),
        name="sel_win_attn",
    )(slopes, q, kv, kv, kv, kv, sel, oc, gates, e)


def prompt_nsa(q, gates, kv, kcvc):
    L = q.shape[1]
    n_sel = L // SEL_BLOCK
    oc, sel = cmp_attention(q, kcvc, gates, tq=QBLK, pos0=0, pos_step=QBLK, transposed=True,
                            k_sel=min(TOPK, n_sel), n_cand=n_sel)
    return sel_win_attention(q, kv, sel, oc, gates, tq=QBLK)


SAMPLE_TQ = 8
SAMPLE_PAGES_PER_STEP = 8
NEW_ROWS_PAD = 128


def _sample_sel_win_kernel(pt_ref, slope_ref, q_ref, sel_ref, oc_ref, gate_ref, e_ref, *refs,
                           n_pg, hpg, t_real, n_steps, past_len, win_len):
    kpages = refs[0:2 * n_pg:2]
    vpages = refs[1:2 * n_pg:2]
    nslc_ref, nwin_ref, cwk_ref, cwv_ref, o_ref, m_scr, l_scr, acc_scr = refs[2 * n_pg:]
    j = pl.program_id(1)
    tq = SAMPLE_TQ
    tk = n_pg * PAGE_SIZE
    scale = HEAD_DIM ** -0.5

    @pl.when(j == 0)
    def _():
        m_scr[...] = jnp.full(m_scr.shape, NEG, F32)
        l_scr[...] = jnp.zeros(l_scr.shape, F32)
        acc_scr[...] = jnp.zeros(acc_scr.shape, F32)

    q = q_ref[...]
    neg_sel = sel_ref[...] - 1.0
    rel = (j * tk - past_len + lax.broadcasted_iota(jnp.int32, (1, tk), 1)).astype(F32)
    e_t = e_ref[j]

    def heads_of(g):
        gw = hpg * HEAD_DIM
        qg = q[:, g * gw:(g + 1) * gw]
        qs = jnp.concatenate([qg[:, h * HEAD_DIM:(h + 1) * HEAD_DIM] for h in range(hpg)], axis=0)
        return (qs * scale).astype(BF16)

    def online(g, h, sh, v):
        r0 = h * tq
        m_old = m_scr[g, r0:r0 + tq]
        m_new = jnp.maximum(m_old, jnp.max(sh, axis=-1, keepdims=True))
        p = jnp.exp(sh - m_new)
        alpha = jnp.exp(m_old - m_new)
        l_scr[g, r0:r0 + tq] = alpha * l_scr[g, r0:r0 + tq] + jnp.sum(p, axis=-1, keepdims=True)
        acc_scr[g, r0:r0 + tq] = alpha * acc_scr[g, r0:r0 + tq] + jnp.dot(p.astype(BF16), v,
                                                                         preferred_element_type=F32)
        m_scr[g, r0:r0 + tq] = m_new

    for g in range(N_KV):
        k = jnp.concatenate([pg[:, g, :] for pg in kpages], axis=0).astype(BF16)
        v = jnp.concatenate([pg[:, g, :] for pg in vpages], axis=0).astype(BF16)
        qs = heads_of(g)
        s = lax.dot_general(qs, k, (((1,), (1,)), ((), ())), preferred_element_type=F32)
        addm = jnp.dot(neg_sel[:, g * N_CAND:(g + 1) * N_CAND], e_t, preferred_element_type=F32)
        for h in range(hpg):
            sh = s[h * tq:(h + 1) * tq] + slope_ref[g * hpg + h] * rel + addm
            online(g, h, sh, v)

    @pl.when(j == n_steps - 1)
    def _():
        gates = gate_ref[...]
        oc = oc_ref[...]
        t_n = lax.broadcasted_iota(jnp.int32, (tq, NEW_ROWS_PAD), 0)
        r_n = lax.broadcasted_iota(jnp.int32, (tq, NEW_ROWS_PAD), 1)
        valid_n = (r_n <= t_n) & (r_n < t_real)
        wn = win_len + NEW_ROWS_PAD
        t_w = lax.broadcasted_iota(jnp.int32, (tq, wn), 0)
        c_w = lax.broadcasted_iota(jnp.int32, (tq, wn), 1)
        dist_w = jnp.where(c_w < win_len, t_w + win_len - c_w, t_w - (c_w - win_len))
        cached = c_w < win_len
        valid_w = ((cached & (dist_w < WINDOW) & (c_w + past_len - win_len >= 0))
                   | ((c_w >= win_len) & (dist_w >= 0) & (c_w - win_len < t_real)))
        dist_wf = dist_w.astype(F32)
        for g in range(N_KV):
            kc0 = g * HEAD_DIM
            vc0 = (N_KV + g) * HEAD_DIM
            qs = heads_of(g)
            k_n = nslc_ref[:, kc0:kc0 + HEAD_DIM].astype(BF16)
            v_n = nslc_ref[:, vc0:vc0 + HEAD_DIM].astype(BF16)
            s_n = lax.dot_general(qs, k_n, (((1,), (1,)), ((), ())), preferred_element_type=F32)
            k_w = jnp.concatenate([cwk_ref[:, g, :], nwin_ref[:, kc0:kc0 + HEAD_DIM]], axis=0).astype(BF16)
            v_w = jnp.concatenate([cwv_ref[:, g, :], nwin_ref[:, vc0:vc0 + HEAD_DIM]], axis=0).astype(BF16)
            s_w = lax.dot_general(qs, k_w, (((1,), (1,)), ((), ())), preferred_element_type=F32)
            for h in range(hpg):
                slope = slope_ref[g * hpg + h]
                r0 = h * tq
                sh = jnp.where(valid_n, s_n[r0:r0 + tq] + slope * r_n.astype(F32), -MASK_BIG)
                online(g, h, sh, v_n)
                o_s = acc_scr[g, r0:r0 + tq] / jnp.maximum(l_scr[g, r0:r0 + tq], 1e-30)
                sw = jnp.where(valid_w, s_w[r0:r0 + tq] - slope * dist_wf, NEG)
                m = jnp.max(sw, axis=-1, keepdims=True)
                e = jnp.where(valid_w, jnp.exp(sw - m), 0.0)
                p = e / jnp.maximum(jnp.sum(e, axis=-1, keepdims=True), 1e-30)
                o_w = jnp.dot(p.astype(BF16), v_w, preferred_element_type=F32)
                c0 = (g * hpg + h) * HEAD_DIM
                gc = g * 128 + 3 * h
                out = (oc[:, c0:c0 + HEAD_DIM] + gates[:, gc + 1:gc + 2] * o_s + gates[:, gc + 2:gc + 3] * o_w)
                o_ref[:, c0:c0 + HEAD_DIM] = out.astype(o_ref.dtype)


def _block_expand_tiles(n_tiles, tk):
    t = np.arange(n_tiles)[:, None, None]
    j = np.arange(N_CAND)[None, :, None]
    c = np.arange(tk)[None, None, :]
    return jnp.asarray(((t * tk + c) // SEL_BLOCK == j).astype(np.float32) * MASK_BIG, dtype=BF16)


def sample_sel_win_attention(q, sel, oc, gates, cache_slc, page_table, new_slc, new_win, cache_win, t_real):
    Bn, tq, D = q.shape
    n_heads = D // HEAD_DIM
    hpg = n_heads // N_KV
    n_pages = page_table.shape[1]
    n_pg = _pick(n_pages, SAMPLE_PAGES_PER_STEP)
    n_steps = n_pages // n_pg
    past_len = n_pages * PAGE_SIZE
    win_len = cache_win.shape[1]
    assert past_len // SEL_BLOCK <= N_CAND and past_len % SEL_BLOCK == 0 and t_real <= min(tq, SEL_BLOCK)
    slopes = alibi_slopes(n_heads).reshape(-1)
    e = _block_expand_tiles(n_steps, n_pg * PAGE_SIZE)
    rows = hpg * tq

    def pg_spec(p, c):
        return pl.BlockSpec((None, PAGE_SIZE, None, N_KV, HEAD_DIM),
                            lambda b, j, pt: (pt[b, j * n_pg + p], 0, c, 0, 0))

    def win_spec(c):
        return pl.BlockSpec((None, win_len, None, N_KV, HEAD_DIM), lambda b, j, pt: (b, 0, c, 0, 0))

    per_b = lambda shape: pl.BlockSpec((None,) + shape, lambda b, j, pt: (b, 0, 0))
    grid_spec = pltpu.PrefetchScalarGridSpec(
        num_scalar_prefetch=1,
        grid=(Bn, n_steps),
        in_specs=[pl.BlockSpec(memory_space=pltpu.SMEM),
                  per_b((tq, D)), per_b((tq, N_KV * N_CAND)), per_b((tq, D)), per_b((tq, N_KV * 128)),
                  pl.BlockSpec(e.shape, lambda b, j, pt: (0, 0, 0))]
        + [pg_spec(p, c) for p in range(n_pg) for c in range(2)]
        + [per_b((NEW_ROWS_PAD, KV_COLS)), per_b((NEW_ROWS_PAD, KV_COLS)), win_spec(0), win_spec(1)],
        out_specs=per_b((tq, D)),
        scratch_shapes=[pltpu.VMEM((N_KV, rows, 1), F32), pltpu.VMEM((N_KV, rows, 1), F32),
                        pltpu.VMEM((N_KV, rows, HEAD_DIM), F32)],
    )
    return pl.pallas_call(
        functools.partial(_sample_sel_win_kernel, n_pg=n_pg, hpg=hpg, t_real=t_real, n_steps=n_steps,
                          past_len=past_len, win_len=win_len),
        grid_spec=grid_spec,
        out_shape=jax.ShapeDtypeStruct((Bn, tq, D), BF16),
        compiler_params=_cparams(2),
        name="sample_sel_win_attn",
    )(page_table, slopes, q, sel, oc, gates, e, *([cache_slc] * (2 * n_pg)), new_slc, new_win, cache_win, cache_win)


def sample_nsa(q2d, gates_pad, kv_rows, kcvc, cache_slc, cache_win, page_table):
    _, kv_slc, kv_win = kv_rows
    Bn, T = kv_slc.shape[:2]
    D = q2d.shape[1]
    past_len = page_table.shape[1] * PAGE_SIZE
    n_past_blk = past_len // SEL_BLOCK
    pad_t = lambda a: jnp.pad(a.reshape(Bn, T, -1), ((0, 0), (0, SAMPLE_TQ - T), (0, 0)))
    q = pad_t(q2d)
    gates = pad_t(gates_pad)
    oc, sel = cmp_attention(q, kcvc, gates, tq=SAMPLE_TQ, pos0=past_len, pos_step=0, transposed=False,
                            k_sel=min(TOPK, n_past_blk + 1) - 1, n_cand=n_past_blk)
    pad_rows = lambda a: jnp.pad(a.reshape(Bn, T, KV_COLS), ((0, 0), (0, NEW_ROWS_PAD - T), (0, 0)))
    o = sample_sel_win_attention(q, sel, oc, gates, cache_slc, page_table,
                                 pad_rows(kv_slc), pad_rows(kv_win), cache_win, T)
    return o[:, :T].reshape(Bn * T, D)


def sample_attention_paged(kv_rows, cache_cmp, cache_slc, cache_win, page_table, cmp_pe, cmp_w1, cmp_b1, cmp_w2):
    T = kv_rows[0].shape[1]
    assert T < CMP_STRIDE, "new rows never complete a compression sub-block"
    kcvc = compress_blocks(cmp_lohi(cache_cmp, page_table, cmp_w1), cmp_pe, cmp_w1, cmp_b1, cmp_w2)

    def attend(q2d, gates_pad):
        return sample_nsa(q2d, gates_pad, kv_rows, kcvc, cache_slc, cache_win, page_table)

    return attend


def alibi_slopes(n_heads):
    exps = np.arange(1, n_heads + 1, dtype=np.float32) * np.float32(-8.0 / n_heads)
    return jnp.asarray(np.exp2(exps), dtype=F32).reshape(N_KV, n_heads // N_KV)


def masked_softmax(s, mask):
    s = jnp.where(mask, s, NEG)
    m = jnp.max(s, axis=-1, keepdims=True)
    p = jnp.where(mask, jnp.exp(s - m), 0.0)
    return p / jnp.maximum(jnp.sum(p, axis=-1, keepdims=True), 1e-30)


def s5_core(u, h0, lam_re, lam_im, log_dt, b_re, b_im, c_re, c_im, d_skip):
    Bn, L, D = u.shape
    G = D // SSM_GROUP
    uf = u.reshape(Bn, L, G, SSM_GROUP)
    lam = lax.complex(lam_re, lam_im)
    dt = jnp.exp(log_dt)[:, None]
    a_bar = jnp.exp(lam * dt)
    b_c = lax.complex(b_re, b_im)
    b_bar = ((a_bar - 1.0) / lam)[..., None] * b_c
    bu = lax.complex(jnp.einsum('blgc,gpc->blgp', uf, b_bar.real),
                     jnp.einsum('blgc,gpc->blgp', uf, b_bar.imag))
    h_init = lax.complex(h0[..., 0], h0[..., 1])
    bu = bu.at[:, 0].add(a_bar * h_init)
    a = jnp.broadcast_to(a_bar, bu.shape)

    def combine(e1, e2):
        a1, b1 = e1
        a2, b2 = e2
        return a1 * a2, a2 * b1 + b2

    _, h = lax.associative_scan(combine, (a, bu), axis=1)
    y = (jnp.einsum('blgp,gcp->blgc', h.real, c_re) - jnp.einsum('blgp,gcp->blgc', h.imag, c_im))
    y = y.reshape(Bn, L, D) + d_skip * u
    h_last = jnp.stack([h[:, -1].real, h[:, -1].imag], axis=-1)
    return jax.nn.gelu(y), h_last


def cmp_partials(rows, pe, w1):
    Bn, L = rows.shape[:2]
    sub = rows.reshape(Bn, L // CMP_STRIDE, CMP_STRIDE, 2, N_KV, HEAD_DIM)
    w_lo = w1[:, :CMP_STRIDE]
    w_hi = w1[:, CMP_STRIDE:]
    pe_lo = jnp.einsum('crd,crde->ce', pe[:, :CMP_STRIDE], w_lo)
    pe_hi = jnp.einsum('crd,crde->ce', pe[:, CMP_STRIDE:], w_hi)
    lo = jnp.einsum('bnrckd,crde->bncke', sub, w_lo) + pe_lo[:, None, :]
    hi = jnp.einsum('bnrckd,crde->bncke', sub, w_hi) + pe_hi[:, None, :]
    return lo, hi


def compress(lo, hi, b1, w2):
    h = jax.nn.gelu(lo[:, :-1] + hi[:, 1:] + b1[:, None, :])
    kv = jnp.einsum('bncke,ced->bnckd', h, w2)
    return kv[:, :, 0], kv[:, :, 1]


def nsa_attend(q, gates, pos, kc, vc, c_end, gather_sel, n_sel, kw, vw, w_pos):
    Bn, T = q.shape[:2]
    hpg = q.shape[3]
    slopes = alibi_slopes(N_KV * hpg)
    scale = HEAD_DIM ** -0.5
    dist_c = (pos[:, None] - c_end[None, :]).astype(F32)
    s_c = (jnp.einsum('btghd,bngd->btghn', q, kc) * scale
           - slopes[None, None, :, :, None] * dist_c[None, :, None, None, :])
    p_c = masked_softmax(s_c, (dist_c >= 0)[None, :, None, None, :])
    o_c = jnp.einsum('btghn,bngd->btghd', p_c, vc)
    imp = p_c.sum(axis=3)
    n_cmp = imp.shape[-1]
    imp = jnp.pad(imp, ((0, 0), (0, 0), (0, 0), (1, SUBS_PER_SEL * (n_sel + 1) - 1 - n_cmp)))
    r = imp.reshape(Bn, T, N_KV, n_sel + 1, SUBS_PER_SEL)
    imp_sel = r[..., :n_sel, :].sum(-1) + r[..., 1:, 0]
    blk = pos // SEL_BLOCK
    j = jnp.arange(n_sel)
    forced = (j[None, :] == 0) | (j[None, :] == blk[:, None]) | (j[None, :] == blk[:, None] - 1)
    visible = j[None, :] <= blk[:, None]
    score = jnp.where(visible[None, :, None, :], imp_sel + FORCE_BONUS * forced[None, :, None, :], NEG)
    top_s, idx = lax.top_k(score, min(TOPK, n_sel))
    valid = top_s > 0.5 * NEG
    ks, vs = gather_sel(idx)
    n_k = idx.shape[-1]
    s_pos = idx[..., None] * SEL_BLOCK + jnp.arange(SEL_BLOCK)
    dist_s = (pos[None, :, None, None, None] - s_pos).astype(F32)
    mask_s = valid[..., None] & (dist_s >= 0)
    s_s = (jnp.einsum('btghd,btgksd->btghks', q, ks) * scale
           - slopes[None, None, :, :, None, None] * dist_s[:, :, :, None])
    p_s = masked_softmax(s_s.reshape(Bn, T, N_KV, hpg, n_k * SEL_BLOCK),
                         mask_s[:, :, :, None].reshape(Bn, T, N_KV, 1, n_k * SEL_BLOCK))
    o_s = jnp.einsum('btghm,btgmd->btghd', p_s, vs.reshape(Bn, T, N_KV, n_k * SEL_BLOCK, HEAD_DIM))
    dist_w = pos[:, None] - w_pos[None, :]
    mask_w = (dist_w >= 0) & (dist_w < WINDOW) & (w_pos[None, :] >= 0)
    s_w = (jnp.einsum('btghd,bmgd->btghm', q, kw) * scale
           - slopes[None, None, :, :, None] * dist_w.astype(F32)[None, :, None, None, :])
    p_w = masked_softmax(s_w, mask_w[None, :, None, None, :])
    o_w = jnp.einsum('btghm,bmgd->btghd', p_w, vw)
    return gates[..., 0:1] * o_c + gates[..., 1:2] * o_s + gates[..., 2:3] * o_w


def prompt_attention(kv2d, kv_rows, cmp_pe, cmp_w1, cmp_b1, cmp_w2):
    kv_cmp = kv_rows[0]
    L = kv_cmp.shape[1]
    pages = kv_cmp.reshape(L // PAGE_SIZE, PAGE_SIZE, 2, N_KV, HEAD_DIM)
    table = jnp.arange(L // PAGE_SIZE, dtype=jnp.int32)[None]
    kcvc = compress_blocks(cmp_lohi(pages, table, cmp_w1), cmp_pe, cmp_w1, cmp_b1, cmp_w2)

    def attend(q2d, gates_pad):
        return prompt_nsa(q2d[None], gates_pad[None], kv2d, kcvc)

    return attend


def sample_attention(kv2d, kv_rows, cache_cmp, cache_slc, cache_win, page_table,
                     cmp_pe, cmp_w1, cmp_b1, cmp_w2):
    kv_cmp, kv_slc, kv_win = kv_rows
    Bn, T = kv_cmp.shape[:2]
    n_pages = page_table.shape[1]
    past_len = n_pages * PAGE_SIZE
    past_cmp = cache_cmp[page_table].reshape(Bn, past_len, 2, N_KV, HEAD_DIM)
    lo, hi = cmp_partials(past_cmp, cmp_pe, cmp_w1)
    n_new_sub = T // CMP_STRIDE
    if n_new_sub > 0:
        lo_new, hi_new = cmp_partials(kv_cmp[:, :n_new_sub * CMP_STRIDE], cmp_pe, cmp_w1)
        lo = jnp.concatenate([lo, lo_new], axis=1)
        hi = jnp.concatenate([hi, hi_new], axis=1)
    kc, vc = compress(lo, hi, cmp_b1, cmp_w2)
    c_end = jnp.arange(kc.shape[1]) * CMP_STRIDE + (CMP_BLOCK - 1)
    n_past_blk = past_len // SEL_BLOCK
    n_new_blk = -(-T // SEL_BLOCK)
    n_sel = n_past_blk + n_new_blk
    blk_per_page = PAGE_SIZE // SEL_BLOCK
    pool = cache_slc.reshape(cache_slc.shape[0], blk_per_page, SEL_BLOCK, 2, N_KV, HEAD_DIM)
    new_blocks = jnp.pad(kv_slc, ((0, 0), (0, n_new_blk * SEL_BLOCK - T), (0, 0), (0, 0), (0, 0)))
    new_blocks = new_blocks.reshape(Bn, n_new_blk, SEL_BLOCK, 2, N_KV, HEAD_DIM)
    b_i = jnp.arange(Bn)[:, None, None, None]
    g_i = jnp.arange(N_KV)[None, None, :, None]

    def gather_sel(idx):
        jp = jnp.minimum(idx, n_past_blk - 1)
        page = page_table[b_i, jp // blk_per_page]
        sub = jp % blk_per_page
        jn = jnp.clip(idx - n_past_blk, 0, n_new_blk - 1)
        is_new = (idx >= n_past_blk)[..., None, None]
        k = jnp.where(is_new, new_blocks[b_i, jn, :, 0, g_i, :], pool[page, sub, :, 0, g_i, :])
        v = jnp.where(is_new, new_blocks[b_i, jn, :, 1, g_i, :], pool[page, sub, :, 1, g_i, :])
        return k, v

    w_rows = jnp.concatenate([cache_win, kv_win], axis=1)
    w_pos = past_len - cache_win.shape[1] + jnp.arange(w_rows.shape[1])
    pos = past_len + jnp.arange(T)

    def attend(q2d, gates_pad):
        hpg = q2d.shape[1] // (N_KV * HEAD_DIM)
        q = q2d.reshape(Bn, T, N_KV, hpg, HEAD_DIM)
        gates = gates_pad.reshape(Bn, T, N_KV, 128)[..., :3 * hpg].reshape(Bn, T, N_KV, hpg, 3)
        o = nsa_attend(q, gates, pos, kc, vc, c_end, gather_sel, n_sel,
                       w_rows[:, :, 0], w_rows[:, :, 1], w_pos)
        return o.reshape(Bn * T, -1)

    return attend


def _rows(v, per_tok):
    return v if v.shape[0] == 1 else jnp.repeat(v, per_tok, axis=0)


def trunk(x, mods, kv_mod, h0, make_attend, p):
    Bn, L, D = x.shape
    M = Bn * L
    n_heads = D // HEAD_DIM
    hpg = n_heads // N_KV
    depth = p["mod_w"].shape[0]
    n_a = depth // 2
    xr = x.reshape(M, D)
    new_h = []
    kv_rows = None
    attend = None
    for l in range(depth):
        sh1, sc1, ga1, sh2, sc2, ga2 = [_rows(m, L) for m in jnp.split(mods[l], 6, axis=-1)]
        if l == n_a:
            shift, scale = [_rows(m, L) for m in jnp.split(kv_mod, 2, axis=-1)]
            hk = norm_mod(xr, p["kv_norm"], scale, shift, BF16)
            kv2d = mm_wide(hk, p["w_kv"], tn=512)
            kv = kv2d.reshape(Bn, L, 3, 2, N_KV, HEAD_DIM)
            kv_rows = (kv[:, :, 0], kv[:, :, 1], kv[:, :, 2])
            attend = make_attend(kv2d, kv_rows)
        if l < n_a:
            u = norm_mod(xr, p["norm_pre"][l, 0], sc1, sh1, F32)
            gy, h_last = s5_mixer_core(u.reshape(Bn, L, D), h0[l], p["ssm_lam_re"][l], p["ssm_lam_im"][l],
                                       p["ssm_log_dt"][l], p["ssm_b_re"][l], p["ssm_b_im"][l],
                                       p["ssm_c_re"][l], p["ssm_c_im"][l], p["ssm_d"][l])
            new_h.append(h_last)
            xr = mm_tall(gy, p["ssm_w_glu"], xr, p["norm_post"][l, 0], ga1, layer=l, glu=True)
        else:
            lb = l - n_a
            h = norm_mod(xr, p["norm_pre"][l, 0], sc1, sh1, BF16)
            q = mm_wide(h, p["nsa_w_qg"], layer=lb, n_out=n_heads * HEAD_DIM, tn=512)
            gates_pad = mm_wide(h, p["w_gate_pad"], layer=lb, epilogue="sigmoid", tn=512)
            o = attend(q, gates_pad)
            xr = mm_tall(o, p["nsa_w_o"], xr, p["norm_post"][l, 0], ga1, layer=lb)
        h = norm_mod(xr, p["norm_pre"][l, 1], sc2, sh2, BF16)
        f = mm_wide(h, p["mlp_w1"], layer=l, epilogue="sqrelu", out_dtype=BF16, tn=1024)
        xr = mm_tall(f, p["mlp_w2"], xr, p["norm_post"][l, 1], ga2, layer=l, tk=512)
    return xr.reshape(Bn, L, D), jnp.stack(new_h), kv_rows


def kernel(x_prompt, x_sample, c_prompt, c_sample, state_ssm, cache_cmp, cache_slc, cache_win, page_table, mod_w, mod_b, norm_pre, norm_post, mlp_w1, mlp_w2, ssm_lam_re, ssm_lam_im, ssm_log_dt, ssm_b_re, ssm_b_im, ssm_c_re, ssm_c_im, ssm_d, ssm_w_glu, kv_norm, kv_mod_w, kv_mod_b, w_kv, cmp_pe, cmp_w1, cmp_b1, cmp_w2, nsa_w_qg, nsa_w_o):
    D = x_prompt.shape[-1]
    depth = mod_w.shape[0]
    n_heads = D // HEAD_DIM
    bp, bs = c_prompt.shape[0], c_sample.shape[0]
    c_all = jnp.concatenate([c_prompt, c_sample], axis=0)
    n_c = c_all.shape[0]
    c_all = jnp.pad(c_all, ((0, -n_c % 8), (0, 0)))
    mods = [mm_wide(c_all, mod_w, layer=l, bias=mod_b, prologue="silu", exact=True, tn=512) for l in range(depth)]
    kv_mod = mm_wide(c_all, kv_mod_w, bias=kv_mod_b, prologue="silu", exact=True, tn=512)
    hpg = n_heads // N_KV
    w_gate = nsa_w_qg[:, :, n_heads * HEAD_DIM:].reshape(nsa_w_qg.shape[0], D, N_KV, 3 * hpg)
    w_gate_pad = jnp.pad(w_gate, ((0, 0), (0, 0), (0, 0), (0, 128 - 3 * hpg))).reshape(-1, D, N_KV * 128)
    p = dict(mod_w=mod_w, norm_pre=norm_pre, norm_post=norm_post, mlp_w1=mlp_w1, mlp_w2=mlp_w2,
             ssm_lam_re=ssm_lam_re, ssm_lam_im=ssm_lam_im, ssm_log_dt=ssm_log_dt, ssm_b_re=ssm_b_re,
             ssm_b_im=ssm_b_im, ssm_c_re=ssm_c_re, ssm_c_im=ssm_c_im, ssm_d=ssm_d, ssm_w_glu=ssm_w_glu,
             kv_norm=kv_norm, w_kv=w_kv, nsa_w_qg=nsa_w_qg, nsa_w_o=nsa_w_o, w_gate_pad=w_gate_pad)

    def make_prompt(kv2d, kv_rows):
        return prompt_attention(kv2d, kv_rows, cmp_pe, cmp_w1, cmp_b1, cmp_w2)

    def make_sample(kv2d, kv_rows):
        return sample_attention_paged(kv_rows, cache_cmp, cache_slc, cache_win, page_table,
                                      cmp_pe, cmp_w1, cmp_b1, cmp_w2)

    n_a = depth // 2
    G = D // SSM_GROUP
    h0_prompt = jnp.zeros((n_a, bp, G, STATE_DIM, 2), F32)
    y_prompt, ssm_prompt, rows_prompt = trunk(
        x_prompt, [m[:bp] for m in mods], kv_mod[:bp], h0_prompt, make_prompt, p)
    y_sample, ssm_sample, rows_sample = trunk(
        x_sample, [m[bp:bp + bs] for m in mods], kv_mod[bp:bp + bs], state_ssm, make_sample, p)
    cmp_prompt, slc_prompt, win_rows_prompt = rows_prompt
    cmp_sample, slc_sample, win_sample = rows_sample
    win_prompt = win_rows_prompt[:, -min(WINDOW, x_prompt.shape[1]):]
    return (y_prompt, y_sample, ssm_prompt, ssm_sample, cmp_prompt, cmp_sample,
            slc_prompt, slc_sample, win_prompt, win_sample)
```

```python
import functools
import math

import jax
import jax.numpy as jnp
import numpy as np
from jax import lax
from jax.experimental import pallas as pl
from jax.experimental.pallas import tpu as pltpu

F32 = jnp.float32
BF16 = jnp.bfloat16

SSM_GROUP = 16
STATE_DIM = 64
HEAD_DIM = 128
N_KV = 4
CMP_STRIDE = 16
CMP_BLOCK = 2 * CMP_STRIDE
SEL_BLOCK = 64
SUBS_PER_SEL = SEL_BLOCK // CMP_STRIDE
TOPK = 16
WINDOW = 512
QBLK = 128
PAGE_SIZE = 128
EPS = 1e-6
NEG = -1e30
FORCE_BONUS = 1e4

V7X_VMEM_LIMIT_BYTES = 56 * 1024 * 1024
HIGHEST = lax.Precision.HIGHEST
LOG2E = math.log2(math.e)


def _cparams(n_axes):
    return pltpu.CompilerParams(dimension_semantics=("arbitrary",) * n_axes,
                                vmem_limit_bytes=V7X_VMEM_LIMIT_BYTES)


def _pick(n, pref):
    if n <= pref:
        return n
    t = pref
    while n % t:
        t //= 2
    return t


def _norm_mod_kernel(x_ref, g_ref, sc_ref, sh_ref, o_ref):
    x = x_ref[...]
    r = lax.rsqrt(jnp.mean(x * x, axis=-1, keepdims=True) + EPS)
    y = (x * r) * g_ref[...]
    o_ref[...] = (y * (1.0 + sc_ref[...]) + sh_ref[...]).astype(o_ref.dtype)


def norm_mod(x, g, scale, shift, out_dtype):
    M, D = x.shape
    tm = _pick(M, 512)
    per_row = scale.shape[0] != 1
    mod_spec = pl.BlockSpec((tm, D), lambda i: (i, 0)) if per_row else pl.BlockSpec((1, D), lambda i: (0, 0))
    return pl.pallas_call(
        _norm_mod_kernel,
        grid=(M // tm,),
        in_specs=[pl.BlockSpec((tm, D), lambda i: (i, 0)),
                  pl.BlockSpec((1, D), lambda i: (0, 0)),
                  mod_spec, mod_spec],
        out_specs=pl.BlockSpec((tm, D), lambda i: (i, 0)),
        out_shape=jax.ShapeDtypeStruct((M, D), out_dtype),
        compiler_params=_cparams(1),
        name="norm_mod",
    )(x, g.reshape(1, D), scale, shift)


def _mm_wide_kernel(*refs, prologue, epilogue, has_bias, exact, n_w):
    a_ref = refs[0]
    w_refs = refs[1:1 + n_w]
    pos = 1 + n_w
    b_ref = refs[pos] if has_bias else None
    pos += int(has_bias)
    o_ref = refs[pos]
    wbf_refs = refs[pos + 1:]

    a = a_ref[...]
    if prologue == "silu":
        a = a * jax.nn.sigmoid(a)
    if exact:
        zs = [jnp.dot(a, w[...], preferred_element_type=F32, precision=HIGHEST) for w in w_refs]
    else:
        @pl.when(pl.program_id(1) == 0)
        def _():
            for w, wbf in zip(w_refs, wbf_refs):
                wbf[...] = w[...].astype(BF16)

        a = a.astype(BF16)
        zs = [jnp.dot(a, wbf[...], preferred_element_type=F32) for wbf in wbf_refs]
    z = zs[0]
    if has_bias:
        z = z + b_ref[...]
    if epilogue == "sqrelu":
        z = jnp.square(jnp.maximum(z, 0.0))
    elif epilogue == "sigmoid":
        z = jax.nn.sigmoid(z)
    elif epilogue == "glu":
        z = z * jax.nn.sigmoid(zs[1])
    o_ref[...] = z.astype(o_ref.dtype)


def mm_wide(a, w, *, layer=None, col0=0, n_out=None, bias=None, prologue=None, epilogue=None,
            exact=False, out_dtype=F32, tm=1024, tn=512):
    M, K = a.shape
    n_out = n_out if n_out is not None else w.shape[-1] - col0
    tm = _pick(M, tm)
    tn = _pick(n_out, tn)
    assert col0 % tn == 0 and n_out % tn == 0
    n_w = 2 if epilogue == "glu" else 1
    jb = col0 // tn

    def w_spec(extra):
        if layer is None:
            return pl.BlockSpec((K, tn), lambda j, i: (0, jb + extra + j))
        return pl.BlockSpec((None, K, tn), lambda j, i: (layer, 0, jb + extra + j))

    in_specs = [pl.BlockSpec((tm, K), lambda j, i: (i, 0))] + [w_spec(e * (n_out // tn)) for e in range(n_w)]
    args = [a] + [w] * n_w
    if bias is not None:
        if layer is None:
            in_specs.append(pl.BlockSpec((1, tn), lambda j, i: (0, jb + j)))
            args.append(bias.reshape(1, -1))
        else:
            in_specs.append(pl.BlockSpec((None, 1, tn), lambda j, i: (layer, 0, jb + j)))
            args.append(bias.reshape(bias.shape[0], 1, -1))
    scratch = [] if exact else [pltpu.VMEM((K, tn), BF16) for _ in range(n_w)]
    return pl.pallas_call(
        functools.partial(_mm_wide_kernel, prologue=prologue, epilogue=epilogue,
                          has_bias=bias is not None, exact=exact, n_w=n_w),
        grid=(n_out // tn, M // tm),
        in_specs=in_specs,
        out_specs=pl.BlockSpec((tm, tn), lambda j, i: (i, j)),
        out_shape=jax.ShapeDtypeStruct((M, n_out), out_dtype),
        scratch_shapes=scratch,
        compiler_params=_cparams(2),
        name="mm_wide",
    )(*args)


def _mm_tall_kernel(a_ref, *refs, n_w, nk):
    w_refs = refs[:n_w]
    res_ref, g_ref, ga_ref, o_ref = refs[n_w:n_w + 4]
    acc_refs = refs[n_w + 4:]
    k = pl.program_id(1)

    @pl.when(k == 0)
    def _():
        for acc in acc_refs:
            acc[...] = jnp.zeros_like(acc)

    a = a_ref[...].astype(BF16)
    for w, acc in zip(w_refs, acc_refs):
        acc[...] += jnp.dot(a, w[...].astype(BF16), preferred_element_type=F32)

    @pl.when(k == nk - 1)
    def _():
        m = acc_refs[0][...]
        if n_w == 2:
            m = m * jax.nn.sigmoid(acc_refs[1][...])
        r = lax.rsqrt(jnp.mean(m * m, axis=-1, keepdims=True) + EPS)
        o_ref[...] = res_ref[...] + ga_ref[...] * ((m * r) * g_ref[...])


def mm_tall(a, w, res, g, gate, *, layer=None, glu=False, tm=512, tk=512):
    M, K = a.shape
    N = res.shape[1]
    tm = _pick(M, tm)
    tk = _pick(K, tk)
    nk = K // tk
    n_w = 2 if glu else 1

    def w_spec(e):
        if layer is None:
            return pl.BlockSpec((tk, N), lambda i, k: (k, e))
        return pl.BlockSpec((None, tk, N), lambda i, k: (layer, k, e))

    per_row = gate.shape[0] != 1
    ga_spec = pl.BlockSpec((tm, N), lambda i, k: (i, 0)) if per_row else pl.BlockSpec((1, N), lambda i, k: (0, 0))
    return pl.pallas_call(
        functools.partial(_mm_tall_kernel, n_w=n_w, nk=nk),
        grid=(M // tm, nk),
        in_specs=[pl.BlockSpec((tm, tk), lambda i, k: (i, k))] + [w_spec(e) for e in range(n_w)]
        + [pl.BlockSpec((tm, N), lambda i, k: (i, 0)),
           pl.BlockSpec((1, N), lambda i, k: (0, 0)),
           ga_spec],
        out_specs=pl.BlockSpec((tm, N), lambda i, k: (i, 0)),
        out_shape=jax.ShapeDtypeStruct((M, N), F32),
        scratch_shapes=[pltpu.VMEM((tm, N), F32) for _ in range(n_w)],
        compiler_params=_cparams(2),
        name="mm_tall",
    )(a, *([w] * n_w), res, g.reshape(1, N), gate)


S5_GROUPS_PER_STEP = 128 // SSM_GROUP
P2 = 2 * STATE_DIM


def _s5_prep_kernel(lam_re_ref, lam_im_ref, ldt_ref, btr_ref, bti_ref, cr_ref, ci_ref,
                    ws_ref, wct_ref, wm_ref, at_ref, *, tc, gb):
    ws_ref[...] = jnp.zeros(ws_ref.shape, ws_ref.dtype)
    wct_ref[...] = jnp.zeros(wct_ref.shape, wct_ref.dtype)
    C = SSM_GROUP
    tcc = tc * C
    wk = max(tcc, 128)
    lane = lax.broadcasted_iota(jnp.int32, (C, P2), 1)
    is_re = lane < STATE_DIM
    kk = lax.broadcasted_iota(jnp.int32, (tc + 1, P2), 0).astype(F32)
    lane_k = lax.broadcasted_iota(jnp.int32, (C, wk), 1)
    for gg in range(gb):
        lr = lam_re_ref[gg]
        li = lam_im_ref[gg]
        dt = jnp.exp(ldt_ref[gg])
        mag = jnp.exp(kk * (lr * dt))
        ang = kk * (li * dt)
        pr = mag * jnp.cos(ang)
        pi = mag * jnp.sin(ang)
        x = pr[1:2] - 1.0
        y = pi[1:2]
        den = lr * lr + li * li
        cfr = (x * lr + y * li) / den
        cfi = (y * lr - x * li) / den
        btr = btr_ref[gg]
        bti = bti_ref[gg]
        bbr = cfr * btr - cfi * bti
        bbi = cfr * bti + cfi * btr
        cre = cr_ref[gg]
        cim = ci_ref[gg]

        def bm(k):
            return jnp.where(is_re, pr[k:k + 1] * bbr - pi[k:k + 1] * bbi, pr[k:k + 1] * bbi + pi[k:k + 1] * bbr)

        def cm(k):
            return jnp.where(is_re, cre * pr[k:k + 1] - cim * pi[k:k + 1], -(cre * pi[k:k + 1] + cim * pr[k:k + 1]))

        cms = [cm(k) for k in range(tc + 1)]
        cs0 = jnp.concatenate(cms[:tc] + [jnp.zeros((wk - tcc, P2), F32)] * (wk > tcc), axis=0)
        kst = lax.dot_general(bm(0), cs0, (((1,), (1,)), ((), ())), preferred_element_type=F32,
                              precision=HIGHEST)
        r0 = gg * C
        for s in range(tc):
            ws_ref[s, r0:r0 + C, gg * P2:(gg + 1) * P2] = bm(tc - 1 - s).astype(ws_ref.dtype)
            wct_ref[s, r0:r0 + C, gg * P2:(gg + 1) * P2] = cms[s + 1].astype(wct_ref.dtype)
            shift = (r0 - s * C) % wk
            moved = kst if shift == 0 else pltpu.roll(kst, shift, 1)
            blockdiag = jnp.where((lane_k >= r0) & (lane_k < r0 + C), moved, 0.0)
            wm_ref[s, r0:r0 + C, :] = blockdiag[:, :128].astype(wm_ref.dtype)
        at_ref[gg, 0:1, :] = pr[tc:tc + 1]
        at_ref[gg, 1:2, :] = jnp.where(is_re[0:1], -pi[tc:tc + 1], pi[tc:tc + 1])


def s5_prep(lam_re, lam_im, log_dt, b_re, b_im, c_re, c_im, tc):
    G = lam_re.shape[0]
    gb = _pick(G, S5_GROUPS_PER_STEP)
    C = SSM_GROUP
    tcc = tc * C
    dup = lambda v: jnp.concatenate([v, v], axis=-1)
    lam_re2 = dup(lam_re)[:, None, :]
    lam_im2 = dup(lam_im)[:, None, :]
    ldt2 = jnp.broadcast_to(log_dt[:, None, None], (G, 1, P2))
    btr = dup(jnp.swapaxes(b_re, 1, 2))
    bti = dup(jnp.swapaxes(b_im, 1, 2))
    cr2 = dup(c_re)
    ci2 = dup(c_im)
    vec = pl.BlockSpec((gb, 1, P2), lambda i: (i, 0, 0))
    mat = pl.BlockSpec((gb, C, P2), lambda i: (i, 0, 0))
    return pl.pallas_call(
        functools.partial(_s5_prep_kernel, tc=tc, gb=gb),
        grid=(G // gb,),
        in_specs=[vec, vec, vec, mat, mat, mat, mat],
        out_specs=[pl.BlockSpec((None, tc, 128, gb * P2), lambda i: (i, 0, 0, 0)),
                   pl.BlockSpec((None, tc, 128, gb * P2), lambda i: (i, 0, 0, 0)),
                   pl.BlockSpec((None, tc, 128, 128), lambda i: (i, 0, 0, 0)),
                   pl.BlockSpec((gb, 2, P2), lambda i: (i, 0, 0))],
        out_shape=[jax.ShapeDtypeStruct((G // gb, tc, 128, gb * P2), BF16),
                   jax.ShapeDtypeStruct((G // gb, tc, 128, gb * P2), BF16),
                   jax.ShapeDtypeStruct((G // gb, tc, 128, 128), BF16),
                   jax.ShapeDtypeStruct((G, 2, P2), F32)],
        compiler_params=_cparams(1),
        name="s5_prep",
    )(lam_re2, lam_im2, ldt2, btr, bti, cr2, ci2)


def _s5_sums_kernel(u_ref, ws_ref, s_ref, *, tc):
    acc = None
    for s in range(tc):
        z = jnp.dot(u_ref[:, s, :].astype(BF16), ws_ref[s], preferred_element_type=F32)
        acc = z if acc is None else acc + z
    s_ref[...] = acc


def s5_chunk_sums(u3, ws):
    R, tc, D = u3.shape
    nb = D // 128
    sw = ws.shape[-1]
    return pl.pallas_call(
        functools.partial(_s5_sums_kernel, tc=tc),
        grid=(nb,),
        in_specs=[pl.BlockSpec((R, tc, 128), lambda i: (0, 0, i)),
                  pl.BlockSpec((None, tc, 128, sw), lambda i: (i, 0, 0, 0))],
        out_specs=pl.BlockSpec((R, sw), lambda i: (0, i)),
        out_shape=jax.ShapeDtypeStruct((R, nb * sw), F32),
        compiler_params=_cparams(1),
        name="s5_sums",
    )(u3, ws)


def _s5_out_kernel(u_ref, hin_ref, wm_ref, wct_ref, d_ref, o_ref, *, tc):
    us = [u_ref[:, s, :] for s in range(tc)]
    ub = [x.astype(BF16) for x in us]
    hb = hin_ref[...].astype(BF16)
    d = d_ref[...]
    for t in range(tc):
        y = lax.dot_general(hb, wct_ref[t], (((1,), (1,)), ((), ())), preferred_element_type=F32)
        for s in range(t + 1):
            y = y + jnp.dot(ub[s], wm_ref[t - s], preferred_element_type=F32)
        o_ref[:, t, :] = jax.nn.gelu(y + d * us[t])


def s5_chunk_out(u3, hin, wm, wct, d_skip):
    R, tc, D = u3.shape
    nb = D // 128
    sw = wct.shape[-1]
    return pl.pallas_call(
        functools.partial(_s5_out_kernel, tc=tc),
        grid=(nb,),
        in_specs=[pl.BlockSpec((R, tc, 128), lambda i: (0, 0, i)),
                  pl.BlockSpec((R, sw), lambda i: (0, i)),
                  pl.BlockSpec((None, tc, 128, 128), lambda i: (i, 0, 0, 0)),
                  pl.BlockSpec((None, tc, 128, sw), lambda i: (i, 0, 0, 0)),
                  pl.BlockSpec((1, 128), lambda i: (0, i))],
        out_specs=pl.BlockSpec((R, tc, 128), lambda i: (0, 0, i)),
        out_shape=jax.ShapeDtypeStruct((R, tc, D), F32),
        compiler_params=_cparams(1),
        name="s5_chunk_out",
    )(u3, hin, wm, wct, d_skip.reshape(1, D))


S5_CHUNKS_PER_STEP = 64


def _s5_scan_kernel(s_ref, h0_ref, at_ref, hin_ref, hfin_ref, h_scr, *, cb, n_steps):
    j = pl.program_id(1)

    @pl.when(j == 0)
    def _():
        h_scr[...] = h0_ref[...]

    ar = at_ref[0]
    ai = at_ref[1]

    def step(c, h):
        hin_ref[c] = h
        return ar * h + ai * pltpu.roll(h, STATE_DIM, 1) + s_ref[c]

    h = lax.fori_loop(0, cb, step, h_scr[...])
    h_scr[...] = h

    @pl.when(j == n_steps - 1)
    def _():
        hfin_ref[...] = h


def s5_scan(s, h0, at):
    Bn, n_chunk, G, _ = s.shape
    cb = _pick(n_chunk, S5_CHUNKS_PER_STEP)
    n_steps = n_chunk // cb
    return pl.pallas_call(
        functools.partial(_s5_scan_kernel, cb=cb, n_steps=n_steps),
        grid=(Bn, n_steps),
        in_specs=[pl.BlockSpec((None, cb, G, P2), lambda b, j: (b, j, 0, 0)),
                  pl.BlockSpec((None, G, P2), lambda b, j: (b, 0, 0)),
                  pl.BlockSpec((2, G, P2), lambda b, j: (0, 0, 0))],
        out_specs=[pl.BlockSpec((None, cb, G, P2), lambda b, j: (b, j, 0, 0)),
                   pl.BlockSpec((None, G, P2), lambda b, j: (b, 0, 0))],
        out_shape=[jax.ShapeDtypeStruct((Bn, n_chunk, G, P2), F32),
                   jax.ShapeDtypeStruct((Bn, G, P2), F32)],
        scratch_shapes=[pltpu.VMEM((G, P2), F32)],
        compiler_params=_cparams(2),
        name="s5_scan",
    )(s, h0, at)


def s5_mixer_core(u, h0, lam_re, lam_im, log_dt, b_re, b_im, c_re, c_im, d_skip):
    Bn, L, D = u.shape
    G = D // SSM_GROUP
    tc = _pick(L, 16)
    n_chunk = L // tc
    ws, wct, wm, at = s5_prep(lam_re, lam_im, log_dt, b_re, b_im, c_re, c_im, tc)
    u3 = u.reshape(Bn * n_chunk, tc, D)
    s = s5_chunk_sums(u3, ws).reshape(Bn, n_chunk, G, P2)
    h0v = jnp.concatenate([h0[..., 0], h0[..., 1]], axis=-1)
    hin, hfin = s5_scan(s, h0v, at.transpose(1, 0, 2))
    gy = s5_chunk_out(u3, hin.reshape(Bn * n_chunk, G * P2), wm, wct, d_skip)
    h_last = jnp.stack([hfin[..., :STATE_DIM], hfin[..., STATE_DIM:]], axis=-1)
    return gy.reshape(Bn * L, D), h_last


SUBS_PER_PAGE = PAGE_SIZE // CMP_STRIDE
KV_COLS = 2 * N_KV * HEAD_DIM
CMP_PAGES_PER_STEP = 8


def _cmp_lohi_kernel(pt_ref, *refs, n_pg):
    x_refs = refs[:2 * n_pg]
    w_ref = refs[2 * n_pg]
    o_ref = refs[2 * n_pg + 1]
    rows = SUBS_PER_PAGE * N_KV
    for c in range(2):
        acc = jnp.zeros((n_pg * rows, 2 * HEAD_DIM), F32)
        for rp in range(CMP_STRIDE // 2):
            parts = []
            for p in range(n_pg):
                a0 = x_refs[2 * p + c][:, 2 * rp].reshape(rows, HEAD_DIM)
                a1 = x_refs[2 * p + c][:, 2 * rp + 1].reshape(rows, HEAD_DIM)
                parts.append(jnp.concatenate([a0, a1], axis=1))
            xs = jnp.concatenate(parts, axis=0).astype(BF16)
            acc = acc + jnp.dot(xs, w_ref[c, rp], preferred_element_type=F32)
        o_ref[c] = acc


def cmp_lohi(pages, page_table, cmp_w1):
    Bn, n_pages = page_table.shape
    n_pg = _pick(n_pages, CMP_PAGES_PER_STEP)
    x = pages.reshape(pages.shape[0], SUBS_PER_PAGE, CMP_STRIDE, 2, N_KV, HEAD_DIM)
    half = CMP_STRIDE // 2
    w_lo = cmp_w1[:, :CMP_STRIDE].reshape(2, half, 2 * HEAD_DIM, HEAD_DIM)
    w_hi = cmp_w1[:, CMP_STRIDE:].reshape(2, half, 2 * HEAD_DIM, HEAD_DIM)
    w = jnp.concatenate([w_lo, w_hi], axis=-1).astype(BF16)
    n_sub = n_pages * SUBS_PER_PAGE

    def x_spec(p, c):
        return pl.BlockSpec((None, SUBS_PER_PAGE, CMP_STRIDE, None, N_KV, HEAD_DIM),
                            lambda b, i, pt: (pt[b, i * n_pg + p], 0, 0, c, 0, 0))

    grid_spec = pltpu.PrefetchScalarGridSpec(
        num_scalar_prefetch=1,
        grid=(Bn, n_pages // n_pg),
        in_specs=[x_spec(p, c) for p in range(n_pg) for c in range(2)]
        + [pl.BlockSpec(w.shape, lambda b, i, pt: (0, 0, 0, 0))],
        out_specs=pl.BlockSpec((None, 2, n_pg * SUBS_PER_PAGE * N_KV, 2 * HEAD_DIM),
                               lambda b, i, pt: (b, 0, i, 0)),
    )
    return pl.pallas_call(
        functools.partial(_cmp_lohi_kernel, n_pg=n_pg),
        grid_spec=grid_spec,
        out_shape=jax.ShapeDtypeStruct((Bn, 2, n_sub * N_KV, 2 * HEAD_DIM), F32),
        compiler_params=_cparams(2),
        name="cmp_lohi",
    )(page_table, *([x] * (2 * n_pg)), w)


def _compress_kernel(x_ref, pe_ref, w1_ref, b1_ref, w2_ref, o_ref, pe_scr):
    n_rows = x_ref.shape[0]
    half = w1_ref.shape[0] // 2
    c = pl.program_id(1)

    @pl.when(pl.program_id(0) == 0)
    def _():
        pe = jnp.broadcast_to(pe_ref[...], (8, 2 * half))
        pe_scr[c, 0] = jnp.dot(pe[:, :half], w1_ref[:half], preferred_element_type=F32, precision=HIGHEST)
        pe_scr[c, 1] = jnp.dot(pe[:, half:], w1_ref[half:], preferred_element_type=F32, precision=HIGHEST)

    x = x_ref[...]
    lo = x[:, :HEAD_DIM] + pe_scr[c, 0][0:1]
    hi = x[:, HEAD_DIM:] + pe_scr[c, 1][0:1]
    hi_next = pltpu.roll(hi, n_rows - N_KV, 0)
    h = jax.nn.gelu(lo + hi_next + b1_ref[...])
    o_ref[...] = jnp.dot(h.astype(BF16), w2_ref[...].astype(BF16), preferred_element_type=F32).astype(o_ref.dtype)


def compress_blocks(lohi, cmp_pe, cmp_w1, cmp_b1, cmp_w2):
    Bn, _, n_rows, _ = lohi.shape
    kdim = CMP_BLOCK * HEAD_DIM
    out = pl.pallas_call(
        _compress_kernel,
        grid=(Bn, 2),
        in_specs=[pl.BlockSpec((None, None, n_rows, 2 * HEAD_DIM), lambda b, c: (b, c, 0, 0)),
                  pl.BlockSpec((None, 1, kdim), lambda b, c: (c, 0, 0)),
                  pl.BlockSpec((None, kdim, HEAD_DIM), lambda b, c: (c, 0, 0)),
                  pl.BlockSpec((None, 1, HEAD_DIM), lambda b, c: (c, 0, 0)),
                  pl.BlockSpec((None, HEAD_DIM, HEAD_DIM), lambda b, c: (c, 0, 0))],
        out_specs=pl.BlockSpec((None, None, n_rows, HEAD_DIM), lambda b, c: (b, c, 0, 0)),
        out_shape=jax.ShapeDtypeStruct((Bn, 2, n_rows, HEAD_DIM), BF16),
        scratch_shapes=[pltpu.VMEM((2, 2, 8, HEAD_DIM), F32)],
        compiler_params=_cparams(2),
        name="compress",
    )(lohi, cmp_pe.reshape(2, 1, kdim), cmp_w1.reshape(2, kdim, HEAD_DIM),
      cmp_b1.reshape(2, 1, HEAD_DIM), cmp_w2)
    return out.reshape(Bn, 2, n_rows // N_KV, N_KV, HEAD_DIM).transpose(0, 1, 3, 2, 4)


N_CAND = 128
NEG_TAKEN = -3e38


def _topk_mask(score, axis, k_sel):
    idx = lax.broadcasted_iota(jnp.int32, score.shape, axis)
    sel = jnp.zeros(score.shape, F32)
    for _ in range(k_sel):
        m = jnp.max(score, axis=axis, keepdims=True)
        first = jnp.min(jnp.where(score == m, idx, N_CAND), axis=axis, keepdims=True)
        hit = idx == first
        sel = jnp.where(hit & (m > 0.5 * NEG), 1.0, sel)
        score = jnp.where(hit, NEG_TAKEN, score)
    return sel


def _cmp_attn_kernel(slope_ref, q_ref, kc_ref, vc_ref, gate_ref, wsel_ref, oc_ref, sel_ref, *,
                     tq, hpg, gps, pos0, pos_step, transposed, k_sel, n_cand):
    i = pl.program_id(1)
    n_cmp = kc_ref.shape[1]
    gw = hpg * HEAD_DIM
    scale = HEAD_DIM ** -0.5
    base = pos0 + i * pos_step
    t_idx = lax.broadcasted_iota(jnp.int32, (tq, n_cmp), 0)
    n_idx = lax.broadcasted_iota(jnp.int32, (tq, n_cmp), 1)
    dist_i = base + t_idx - (n_idx * CMP_STRIDE + (CMP_BLOCK - 1))
    valid = dist_i >= 0
    dist = dist_i.astype(F32)
    gates = gate_ref[...]
    shape, j_ax, t_ax = ((N_CAND, tq), 0, 1) if transposed else ((tq, N_CAND), 1, 0)
    j = lax.broadcasted_iota(jnp.int32, shape, j_ax)
    blk = (base + lax.broadcasted_iota(jnp.int32, shape, t_ax)) // SEL_BLOCK
    forced = (j == 0) | (j == blk) | (j == blk - 1)
    visible = (j <= blk) & (j < n_cand)
    scores = []
    for gi in range(gps):
        g = pl.program_id(2) * gps + gi
        q = q_ref[:, gi * gw:(gi + 1) * gw]
        qs = jnp.concatenate([q[:, h * HEAD_DIM:(h + 1) * HEAD_DIM] for h in range(hpg)], axis=0)
        qs = (qs * scale).astype(BF16)
        s = lax.dot_general(qs, kc_ref[gi], (((1,), (1,)), ((), ())), preferred_element_type=F32)
        vc = vc_ref[gi]
        psum = jnp.zeros((tq, n_cmp), F32)
        for h in range(hpg):
            sh = s[h * tq:(h + 1) * tq] - slope_ref[g * hpg + h] * dist
            sh = jnp.where(valid, sh, NEG)
            m = jnp.max(sh, axis=-1, keepdims=True)
            e = jnp.where(valid, jnp.exp(sh - m), 0.0)
            p = e / jnp.maximum(jnp.sum(e, axis=-1, keepdims=True), 1e-30)
            psum = psum + p
            o_h = jnp.dot(p.astype(BF16), vc, preferred_element_type=F32)
            c0 = gi * gw + h * HEAD_DIM
            oc_ref[:, c0:c0 + HEAD_DIM] = o_h * gates[:, gi * 128 + 3 * h:gi * 128 + 3 * h + 1]
        if transposed:
            imp = lax.dot_general(wsel_ref[...], psum, (((1,), (1,)), ((), ())),
                                  preferred_element_type=F32, precision=HIGHEST)
        else:
            imp = lax.dot_general(psum, wsel_ref[...], (((1,), (1,)), ((), ())),
                                  preferred_element_type=F32, precision=HIGHEST)
        scores.append(jnp.where(visible, imp + jnp.where(forced, FORCE_BONUS, 0.0), NEG))
    if transposed:
        for gi in range(gps):
            sel = _topk_mask(scores[gi], 0, k_sel).T
            sel_ref[:, gi * N_CAND:(gi + 1) * N_CAND] = sel.astype(sel_ref.dtype)
    else:
        sel = _topk_mask(jnp.concatenate(scores, axis=0), 1, k_sel)
        for gi in range(gps):
            sel_ref[:, gi * N_CAND:(gi + 1) * N_CAND] = sel[gi * tq:(gi + 1) * tq].astype(sel_ref.dtype)


def _sel_weights(n_cmp_pad, n_cmp):
    j = np.arange(N_CAND)[:, None]
    n = np.arange(n_cmp_pad)[None, :]
    w = (n >= SUBS_PER_SEL * j - 1) & (n <= SUBS_PER_SEL * j + SUBS_PER_SEL - 1) & (n < n_cmp)
    return jnp.asarray(w.astype(np.float32))


def cmp_attention(q, kcvc, gates, *, tq, pos0, pos_step, transposed, k_sel, n_cand):
    Bn, T, D = q.shape
    n_heads = D // HEAD_DIM
    hpg = n_heads // N_KV
    n_sub = kcvc.shape[3]
    gw = hpg * HEAD_DIM
    slopes = alibi_slopes(n_heads).reshape(-1)
    wsel = _sel_weights(n_sub, n_sub - 1)
    gps = N_KV if tq < 128 else 1
    kern = functools.partial(_cmp_attn_kernel, tq=tq, hpg=hpg, gps=gps, pos0=pos0, pos_step=pos_step,
                             transposed=transposed, k_sel=k_sel, n_cand=n_cand)
    return pl.pallas_call(
        kern,
        grid=(Bn, T // tq, N_KV // gps),
        in_specs=[pl.BlockSpec(memory_space=pltpu.SMEM),
                  pl.BlockSpec((None, tq, gps * gw), lambda b, i, g: (b, i, g)),
                  pl.BlockSpec((None, None, gps, n_sub, HEAD_DIM), lambda b, i, g: (b, 0, g, 0, 0)),
                  pl.BlockSpec((None, None, gps, n_sub, HEAD_DIM), lambda b, i, g: (b, 1, g, 0, 0)),
                  pl.BlockSpec((None, tq, gps * 128), lambda b, i, g: (b, i, g)),
                  pl.BlockSpec((N_CAND, n_sub), lambda b, i, g: (0, 0))],
        out_specs=[pl.BlockSpec((None, tq, gps * gw), lambda b, i, g: (b, i, g)),
                   pl.BlockSpec((None, tq, gps * N_CAND), lambda b, i, g: (b, i, g))],
        out_shape=[jax.ShapeDtypeStruct((Bn, T, D), F32),
                   jax.ShapeDtypeStruct((Bn, T, N_KV * N_CAND), BF16)],
        compiler_params=_cparams(3),
        name="cmp_attn",
    )(slopes, q, kcvc, kcvc, gates, wsel)


SEL_TK = 512
MASK_BIG = 1e30


def _sel_win_kernel(slope_ref, q_ref, ks_ref, vs_ref, kw_ref, vw_ref, sel_ref, oc_ref, gate_ref, e_ref,
                    o_ref, ksb, vsb, kwb, vwb, *, tq, hpg):
    g = pl.program_id(0)
    i = pl.program_id(1)
    L = ks_ref.shape[0]
    s0 = i * tq
    scale = HEAD_DIM ** -0.5

    @pl.when(i == 0)
    def _():
        ksb[:, :HEAD_DIM] = ks_ref[...].astype(BF16)
        ksb[:, HEAD_DIM:] = e_ref[...]
        vsb[...] = vs_ref[...].astype(BF16)
        kwb[0:WINDOW] = jnp.zeros((WINDOW, HEAD_DIM), BF16)
        vwb[0:WINDOW] = jnp.zeros((WINDOW, HEAD_DIM), BF16)
        kwb[WINDOW:] = kw_ref[...].astype(BF16)
        vwb[WINDOW:] = vw_ref[...].astype(BF16)

    q = q_ref[...]
    qs = jnp.concatenate([q[:, h * HEAD_DIM:(h + 1) * HEAD_DIM] for h in range(hpg)], axis=0)
    qs = (qs * (scale * LOG2E)).astype(BF16)
    slopes = [slope_ref[g * hpg + h] * LOG2E for h in range(hpg)]
    neg_sel = sel_ref[...] - 1.0
    qaug = jnp.concatenate([qs, jnp.concatenate([neg_sel] * hpg, axis=0)], axis=1)
    c_iota = lax.broadcasted_iota(jnp.int32, (1, SEL_TK), 1)
    kd = s0 // SEL_TK

    def scores(kt):
        k0 = pl.multiple_of(kt * SEL_TK, SEL_TK)
        return lax.dot_general(qaug, ksb[pl.ds(k0, SEL_TK), :], (((1,), (1,)), ((), ())),
                               preferred_element_type=F32)

    def sweep(kt, carry, diagonal):
        ms, ls, accs = carry
        k0 = pl.multiple_of(kt * SEL_TK, SEL_TK)
        v_t = vsb[pl.ds(k0, SEL_TK), :]
        if diagonal:
            t_idx = lax.broadcasted_iota(jnp.int32, (tq, SEL_TK), 0)
            c_idx = lax.broadcasted_iota(jnp.int32, (tq, SEL_TK), 1)
            causal = s0 + t_idx >= k0 + c_idx
        s = scores(kt)
        rel = (k0 - s0 + c_iota).astype(F32)
        new_m, new_l, new_acc = [], [], []
        for h in range(hpg):
            sh = s[h * tq:(h + 1) * tq] + slopes[h] * rel
            if diagonal:
                sh = jnp.where(causal, sh, -MASK_BIG)
            m_new = jnp.maximum(ms[h], jnp.max(sh, axis=-1, keepdims=True))
            p = jnp.exp2(sh - m_new)
            alpha = jnp.exp2(ms[h] - m_new)
            new_l.append(alpha * ls[h] + jnp.sum(p, axis=-1, keepdims=True))
            new_acc.append(alpha * accs[h] + jnp.dot(p.astype(BF16), v_t, preferred_element_type=F32))
            new_m.append(m_new)
        return new_m, new_l, new_acc

    init = ([jnp.full((tq, 1), NEG, F32)] * hpg, [jnp.zeros((tq, 1), F32)] * hpg,
            [jnp.zeros((tq, HEAD_DIM), F32)] * hpg)
    carry = lax.fori_loop(0, kd, lambda kt, c: tuple(sweep(kt, c, False)), tuple(init))
    ms, ls, accs = sweep(kd, carry, True)

    wn = WINDOW + tq
    w0 = pl.multiple_of(s0, tq)
    kw_t = kwb[pl.ds(w0, wn), :]
    vw_t = vwb[pl.ds(w0, wn), :]
    sw = lax.dot_general(qs, kw_t, (((1,), (1,)), ((), ())), preferred_element_type=F32)
    t_idx = lax.broadcasted_iota(jnp.int32, (tq, wn), 0)
    c_idx = lax.broadcasted_iota(jnp.int32, (tq, wn), 1)
    dist_i = t_idx + WINDOW - c_idx
    valid = (dist_i >= 0) & (dist_i < WINDOW) & (c_idx + s0 >= WINDOW)
    dist = dist_i.astype(F32)
    gates = gate_ref[...]
    oc = oc_ref[...]
    for h in range(hpg):
        sh = jnp.where(valid, sw[h * tq:(h + 1) * tq] - slopes[h] * dist, NEG)
        m = jnp.max(sh, axis=-1, keepdims=True)
        e = jnp.where(valid, jnp.exp2(sh - m), 0.0)
        p = e / jnp.maximum(jnp.sum(e, axis=-1, keepdims=True), 1e-30)
        o_w = jnp.dot(p.astype(BF16), vw_t, preferred_element_type=F32)
        o_s = accs[h] / jnp.maximum(ls[h], 1e-30)
        out = (oc[:, h * HEAD_DIM:(h + 1) * HEAD_DIM] + gates[:, 3 * h + 1:3 * h + 2] * o_s
               + gates[:, 3 * h + 2:3 * h + 3] * o_w)
        o_ref[:, h * HEAD_DIM:(h + 1) * HEAD_DIM] = out.astype(o_ref.dtype)


def _block_onehot(n_keys):
    key = np.arange(n_keys)[:, None]
    j = np.arange(N_CAND)[None, :]
    return jnp.asarray((key // SEL_BLOCK == j).astype(np.float32) * MASK_BIG, dtype=BF16)


def sel_win_attention(q, kv, sel, oc, gates, *, tq):
    _, L, D = q.shape
    n_heads = D // HEAD_DIM
    hpg = n_heads // N_KV
    gw = hpg * HEAD_DIM
    assert L % SEL_TK == 0 and SEL_TK % tq == 0 and L // SEL_BLOCK <= N_CAND
    slopes = alibi_slopes(n_heads).reshape(-1)
    e = _block_onehot(L)

    def kv_spec(branch, which):
        cb = (branch * 2 + which) * N_KV
        return pl.BlockSpec((L, HEAD_DIM), lambda g, i: (0, cb + g))

    return pl.pallas_call(
        functools.partial(_sel_win_kernel, tq=tq, hpg=hpg),
        grid=(N_KV, L // tq),
        in_specs=[pl.BlockSpec(memory_space=pltpu.SMEM),
                  pl.BlockSpec((None, tq, gw), lambda g, i: (0, i, g)),
                  kv_spec(1, 0), kv_spec(1, 1), kv_spec(2, 0), kv_spec(2, 1),
                  pl.BlockSpec((None, tq, N_CAND), lambda g, i: (0, i, g)),
                  pl.BlockSpec((None, tq, gw), lambda g, i: (0, i, g)),
                  pl.BlockSpec((None, tq, 128), lambda g, i: (0, i, g)),
                  pl.BlockSpec(e.shape, lambda g, i: (0, 0))],
        out_specs=pl.BlockSpec((tq, gw), lambda g, i: (i, g)),
        out_shape=jax.ShapeDtypeStruct((L, D), BF16),
        scratch_shapes=[pltpu.VMEM((L, 2 * HEAD_DIM), BF16), pltpu.VMEM((L, HEAD_DIM), BF16),
                        pltpu.VMEM((WINDOW + L, HEAD_DIM), BF16), pltpu.VMEM((WINDOW + L, HEAD_DIM), BF16)],
        compiler_params=_cparams(2),
        name="sel_win_attn",
    )(slopes, q, kv, kv, kv, kv, sel, oc, gates, e)


def prompt_nsa(q, gates, kv, kcvc):
    L = q.shape[1]
    n_sel = L // SEL_BLOCK
    oc, sel = cmp_attention(q, kcvc, gates, tq=QBLK, pos0=0, pos_step=QBLK, transposed=True,
                            k_sel=min(TOPK, n_sel), n_cand=n_sel)
    return sel_win_attention(q, kv, sel, oc, gates, tq=QBLK)


SAMPLE_TQ = 8
SAMPLE_PAGES_PER_STEP = 8
NEW_ROWS_PAD = 128


def _sample_sel_win_kernel(pt_ref, slope_ref, q_ref, sel_ref, oc_ref, gate_ref, e_ref, *refs,
                           n_pg, hpg, t_real, n_steps, past_len, win_len):
    kpages = refs[0:2 * n_pg:2]
    vpages = refs[1:2 * n_pg:2]
    nslc_ref, nwin_ref, cwk_ref, cwv_ref, o_ref, m_scr, l_scr, acc_scr = refs[2 * n_pg:]
    j = pl.program_id(1)
    tq = SAMPLE_TQ
    tk = n_pg * PAGE_SIZE
    scale = HEAD_DIM ** -0.5

    @pl.when(j == 0)
    def _():
        m_scr[...] = jnp.full(m_scr.shape, NEG, F32)
        l_scr[...] = jnp.zeros(l_scr.shape, F32)
        acc_scr[...] = jnp.zeros(acc_scr.shape, F32)

    q = q_ref[...]
    neg_sel = sel_ref[...] - 1.0

    def heads_of(g):
        gw = hpg * HEAD_DIM
        qg = q[:, g * gw:(g + 1) * gw]
        qs = jnp.concatenate([qg[:, h * HEAD_DIM:(h + 1) * HEAD_DIM] for h in range(hpg)], axis=0)
        return (qs * scale).astype(BF16)

    def online(g, h, sh, v):
        r0 = h * tq
        m_old = m_scr[g, r0:r0 + tq]
        m_new = jnp.maximum(m_old, jnp.max(sh, axis=-1, keepdims=True))
        p = jnp.exp(sh - m_new)
        alpha = jnp.exp(m_old - m_new)
        l_scr[g, r0:r0 + tq] = alpha * l_scr[g, r0:r0 + tq] + jnp.sum(p, axis=-1, keepdims=True)
        acc_scr[g, r0:r0 + tq] = alpha * acc_scr[g, r0:r0 + tq] + jnp.dot(p.astype(BF16), v,
                                                                         preferred_element_type=F32)
        m_scr[g, r0:r0 + tq] = m_new

    rows = hpg * tq
    n_rows = N_KV * rows
    n_lane = tk * N_KV
    kf = jnp.concatenate([pg[...].reshape(PAGE_SIZE * N_KV, HEAD_DIM) for pg in kpages], axis=0).astype(BF16)
    vf = jnp.concatenate([pg[...].reshape(PAGE_SIZE * N_KV, HEAD_DIM) for pg in vpages], axis=0).astype(BF16)
    kaug = jnp.concatenate([kf, e_ref[...]], axis=1)
    qaug = jnp.concatenate(
        [jnp.concatenate([heads_of(g) for g in range(N_KV)], axis=0),
         jnp.concatenate([neg_sel[:, g * N_CAND:(g + 1) * N_CAND] for g in range(N_KV) for _ in range(hpg)],
                         axis=0)], axis=1)
    s = lax.dot_general(qaug, kaug, (((1,), (1,)), ((), ())), preferred_element_type=F32)
    lane = lax.broadcasted_iota(jnp.int32, (1, n_lane), 1)
    rel = (j * tk - past_len + lane // N_KV).astype(F32)
    slope_col = jnp.concatenate([jnp.full((tq, 1), slope_ref[gh], F32) for gh in range(N_KV * hpg)], axis=0)
    row_g = lax.broadcasted_iota(jnp.int32, (n_rows, 1), 0) // rows
    sh = jnp.where(lane % N_KV == row_g, s + slope_col * rel, -MASK_BIG)
    m_old = m_scr[...].reshape(n_rows, 1)
    m_new = jnp.maximum(m_old, jnp.max(sh, axis=-1, keepdims=True))
    p = jnp.exp(sh - m_new)
    alpha = jnp.exp(m_old - m_new)
    l_new = alpha * l_scr[...].reshape(n_rows, 1) + jnp.sum(p, axis=-1, keepdims=True)
    acc_new = alpha * acc_scr[...].reshape(n_rows, HEAD_DIM) + jnp.dot(p.astype(BF16), vf,
                                                                    preferred_element_type=F32)
    m_scr[...] = m_new.reshape(m_scr.shape)
    l_scr[...] = l_new.reshape(l_scr.shape)
    acc_scr[...] = acc_new.reshape(acc_scr.shape)

    @pl.when(j == n_steps - 1)
    def _():
        gates = gate_ref[...]
        oc = oc_ref[...]
        t_n = lax.broadcasted_iota(jnp.int32, (tq, NEW_ROWS_PAD), 0)
        r_n = lax.broadcasted_iota(jnp.int32, (tq, NEW_ROWS_PAD), 1)
        valid_n = (r_n <= t_n) & (r_n < t_real)
        wn = win_len + NEW_ROWS_PAD
        t_w = lax.broadcasted_iota(jnp.int32, (tq, wn), 0)
        c_w = lax.broadcasted_iota(jnp.int32, (tq, wn), 1)
        dist_w = jnp.where(c_w < win_len, t_w + win_len - c_w, t_w - (c_w - win_len))
        cached = c_w < win_len
        valid_w = ((cached & (dist_w < WINDOW) & (c_w + past_len - win_len >= 0))
                   | ((c_w >= win_len) & (dist_w >= 0) & (c_w - win_len < t_real)))
        dist_wf = dist_w.astype(F32)
        for g in range(N_KV):
            kc0 = g * HEAD_DIM
            vc0 = (N_KV + g) * HEAD_DIM
            qs = heads_of(g)
            k_n = nslc_ref[:, kc0:kc0 + HEAD_DIM].astype(BF16)
            v_n = nslc_ref[:, vc0:vc0 + HEAD_DIM].astype(BF16)
            s_n = lax.dot_general(qs, k_n, (((1,), (1,)), ((), ())), preferred_element_type=F32)
            k_w = jnp.concatenate([cwk_ref[:, g, :], nwin_ref[:, kc0:kc0 + HEAD_DIM]], axis=0).astype(BF16)
            v_w = jnp.concatenate([cwv_ref[:, g, :], nwin_ref[:, vc0:vc0 + HEAD_DIM]], axis=0).astype(BF16)
            s_w = lax.dot_general(qs, k_w, (((1,), (1,)), ((), ())), preferred_element_type=F32)
            for h in range(hpg):
                slope = slope_ref[g * hpg + h]
                r0 = h * tq
                sh = jnp.where(valid_n, s_n[r0:r0 + tq] + slope * r_n.astype(F32), -MASK_BIG)
                online(g, h, sh, v_n)
                o_s = acc_scr[g, r0:r0 + tq] / jnp.maximum(l_scr[g, r0:r0 + tq], 1e-30)
                sw = jnp.where(valid_w, s_w[r0:r0 + tq] - slope * dist_wf, NEG)
                m = jnp.max(sw, axis=-1, keepdims=True)
                e = jnp.where(valid_w, jnp.exp(sw - m), 0.0)
                p = e / jnp.maximum(jnp.sum(e, axis=-1, keepdims=True), 1e-30)
                o_w = jnp.dot(p.astype(BF16), v_w, preferred_element_type=F32)
                c0 = (g * hpg + h) * HEAD_DIM
                gc = g * 128 + 3 * h
                out = (oc[:, c0:c0 + HEAD_DIM] + gates[:, gc + 1:gc + 2] * o_s + gates[:, gc + 2:gc + 3] * o_w)
                o_ref[:, c0:c0 + HEAD_DIM] = out.astype(o_ref.dtype)


def _block_onehot_rows(n_keys):
    key = np.repeat(np.arange(n_keys), N_KV)[:, None]
    j = np.arange(N_CAND)[None, :]
    return jnp.asarray((key // SEL_BLOCK == j).astype(np.float32) * MASK_BIG, dtype=BF16)


def sample_sel_win_attention(q, sel, oc, gates, cache_slc, page_table, new_slc, new_win, cache_win, t_real):
    Bn, tq, D = q.shape
    n_heads = D // HEAD_DIM
    hpg = n_heads // N_KV
    n_pages = page_table.shape[1]
    n_pg = _pick(n_pages, SAMPLE_PAGES_PER_STEP)
    n_steps = n_pages // n_pg
    past_len = n_pages * PAGE_SIZE
    win_len = cache_win.shape[1]
    assert past_len // SEL_BLOCK <= N_CAND and past_len % SEL_BLOCK == 0 and t_real <= min(tq, SEL_BLOCK)
    slopes = alibi_slopes(n_heads).reshape(-1)
    e = _block_onehot_rows(past_len)
    e_rows = n_pg * PAGE_SIZE * N_KV
    rows = hpg * tq

    def pg_spec(p, c):
        return pl.BlockSpec((None, PAGE_SIZE, None, N_KV, HEAD_DIM),
                            lambda b, j, pt: (pt[b, j * n_pg + p], 0, c, 0, 0))

    def win_spec(c):
        return pl.BlockSpec((None, win_len, None, N_KV, HEAD_DIM), lambda b, j, pt: (b, 0, c, 0, 0))

    per_b = lambda shape: pl.BlockSpec((None,) + shape, lambda b, j, pt: (b, 0, 0))
    grid_spec = pltpu.PrefetchScalarGridSpec(
        num_scalar_prefetch=1,
        grid=(Bn, n_steps),
        in_specs=[pl.BlockSpec(memory_space=pltpu.SMEM),
                  per_b((tq, D)), per_b((tq, N_KV * N_CAND)), per_b((tq, D)), per_b((tq, N_KV * 128)),
                  pl.BlockSpec((e_rows, N_CAND), lambda b, j, pt: (j, 0))]
        + [pg_spec(p, c) for p in range(n_pg) for c in range(2)]
        + [per_b((NEW_ROWS_PAD, KV_COLS)), per_b((NEW_ROWS_PAD, KV_COLS)), win_spec(0), win_spec(1)],
        out_specs=per_b((tq, D)),
        scratch_shapes=[pltpu.VMEM((N_KV, rows, 1), F32), pltpu.VMEM((N_KV, rows, 1), F32),
                        pltpu.VMEM((N_KV, rows, HEAD_DIM), F32)],
    )
    return pl.pallas_call(
        functools.partial(_sample_sel_win_kernel, n_pg=n_pg, hpg=hpg, t_real=t_real, n_steps=n_steps,
                          past_len=past_len, win_len=win_len),
        grid_spec=grid_spec,
        out_shape=jax.ShapeDtypeStruct((Bn, tq, D), BF16),
        compiler_params=_cparams(2),
        name="sample_sel_win_attn",
    )(page_table, slopes, q, sel, oc, gates, e, *([cache_slc] * (2 * n_pg)), new_slc, new_win, cache_win, cache_win)


def sample_nsa(q2d, gates_pad, kv_rows, kcvc, cache_slc, cache_win, page_table):
    _, kv_slc, kv_win = kv_rows
    Bn, T = kv_slc.shape[:2]
    D = q2d.shape[1]
    past_len = page_table.shape[1] * PAGE_SIZE
    n_past_blk = past_len // SEL_BLOCK
    pad_t = lambda a: jnp.pad(a.reshape(Bn, T, -1), ((0, 0), (0, SAMPLE_TQ - T), (0, 0)))
    q = pad_t(q2d)
    gates = pad_t(gates_pad)
    oc, sel = cmp_attention(q, kcvc, gates, tq=SAMPLE_TQ, pos0=past_len, pos_step=0, transposed=False,
                            k_sel=min(TOPK, n_past_blk + 1) - 1, n_cand=n_past_blk)
    pad_rows = lambda a: jnp.pad(a.reshape(Bn, T, KV_COLS), ((0, 0), (0, NEW_ROWS_PAD - T), (0, 0)))
    o = sample_sel_win_attention(q, sel, oc, gates, cache_slc, page_table,
                                 pad_rows(kv_slc), pad_rows(kv_win), cache_win, T)
    return o[:, :T].reshape(Bn * T, D)


def sample_attention_paged(kv_rows, cache_cmp, cache_slc, cache_win, page_table, cmp_pe, cmp_w1, cmp_b1, cmp_w2):
    T = kv_rows[0].shape[1]
    assert T < CMP_STRIDE, "new rows never complete a compression sub-block"
    kcvc = compress_blocks(cmp_lohi(cache_cmp, page_table, cmp_w1), cmp_pe, cmp_w1, cmp_b1, cmp_w2)

    def attend(q2d, gates_pad):
        return sample_nsa(q2d, gates_pad, kv_rows, kcvc, cache_slc, cache_win, page_table)

    return attend


def alibi_slopes(n_heads):
    exps = np.arange(1, n_heads + 1, dtype=np.float32) * np.float32(-8.0 / n_heads)
    return jnp.asarray(np.exp2(exps), dtype=F32).reshape(N_KV, n_heads // N_KV)


def masked_softmax(s, mask):
    s = jnp.where(mask, s, NEG)
    m = jnp.max(s, axis=-1, keepdims=True)
    p = jnp.where(mask, jnp.exp(s - m), 0.0)
    return p / jnp.maximum(jnp.sum(p, axis=-1, keepdims=True), 1e-30)


def s5_core(u, h0, lam_re, lam_im, log_dt, b_re, b_im, c_re, c_im, d_skip):
    Bn, L, D = u.shape
    G = D // SSM_GROUP
    uf = u.reshape(Bn, L, G, SSM_GROUP)
    lam = lax.complex(lam_re, lam_im)
    dt = jnp.exp(log_dt)[:, None]
    a_bar = jnp.exp(lam * dt)
    b_c = lax.complex(b_re, b_im)
    b_bar = ((a_bar - 1.0) / lam)[..., None] * b_c
    bu = lax.complex(jnp.einsum('blgc,gpc->blgp', uf, b_bar.real),
                     jnp.einsum('blgc,gpc->blgp', uf, b_bar.imag))
    h_init = lax.complex(h0[..., 0], h0[..., 1])
    bu = bu.at[:, 0].add(a_bar * h_init)
    a = jnp.broadcast_to(a_bar, bu.shape)

    def combine(e1, e2):
        a1, b1 = e1
        a2, b2 = e2
        return a1 * a2, a2 * b1 + b2

    _, h = lax.associative_scan(combine, (a, bu), axis=1)
    y = (jnp.einsum('blgp,gcp->blgc', h.real, c_re) - jnp.einsum('blgp,gcp->blgc', h.imag, c_im))
    y = y.reshape(Bn, L, D) + d_skip * u
    h_last = jnp.stack([h[:, -1].real, h[:, -1].imag], axis=-1)
    return jax.nn.gelu(y), h_last


def cmp_partials(rows, pe, w1):
    Bn, L = rows.shape[:2]
    sub = rows.reshape(Bn, L // CMP_STRIDE, CMP_STRIDE, 2, N_KV, HEAD_DIM)
    w_lo = w1[:, :CMP_STRIDE]
    w_hi = w1[:, CMP_STRIDE:]
    pe_lo = jnp.einsum('crd,crde->ce', pe[:, :CMP_STRIDE], w_lo)
    pe_hi = jnp.einsum('crd,crde->ce', pe[:, CMP_STRIDE:], w_hi)
    lo = jnp.einsum('bnrckd,crde->bncke', sub, w_lo) + pe_lo[:, None, :]
    hi = jnp.einsum('bnrckd,crde->bncke', sub, w_hi) + pe_hi[:, None, :]
    return lo, hi


def compress(lo, hi, b1, w2):
    h = jax.nn.gelu(lo[:, :-1] + hi[:, 1:] + b1[:, None, :])
    kv = jnp.einsum('bncke,ced->bnckd', h, w2)
    return kv[:, :, 0], kv[:, :, 1]


def nsa_attend(q, gates, pos, kc, vc, c_end, gather_sel, n_sel, kw, vw, w_pos):
    Bn, T = q.shape[:2]
    hpg = q.shape[3]
    slopes = alibi_slopes(N_KV * hpg)
    scale = HEAD_DIM ** -0.5
    dist_c = (pos[:, None] - c_end[None, :]).astype(F32)
    s_c = (jnp.einsum('btghd,bngd->btghn', q, kc) * scale
           - slopes[None, None, :, :, None] * dist_c[None, :, None, None, :])
    p_c = masked_softmax(s_c, (dist_c >= 0)[None, :, None, None, :])
    o_c = jnp.einsum('btghn,bngd->btghd', p_c, vc)
    imp = p_c.sum(axis=3)
    n_cmp = imp.shape[-1]
    imp = jnp.pad(imp, ((0, 0), (0, 0), (0, 0), (1, SUBS_PER_SEL * (n_sel + 1) - 1 - n_cmp)))
    r = imp.reshape(Bn, T, N_KV, n_sel + 1, SUBS_PER_SEL)
    imp_sel = r[..., :n_sel, :].sum(-1) + r[..., 1:, 0]
    blk = pos // SEL_BLOCK
    j = jnp.arange(n_sel)
    forced = (j[None, :] == 0) | (j[None, :] == blk[:, None]) | (j[None, :] == blk[:, None] - 1)
    visible = j[None, :] <= blk[:, None]
    score = jnp.where(visible[None, :, None, :], imp_sel + FORCE_BONUS * forced[None, :, None, :], NEG)
    top_s, idx = lax.top_k(score, min(TOPK, n_sel))
    valid = top_s > 0.5 * NEG
    ks, vs = gather_sel(idx)
    n_k = idx.shape[-1]
    s_pos = idx[..., None] * SEL_BLOCK + jnp.arange(SEL_BLOCK)
    dist_s = (pos[None, :, None, None, None] - s_pos).astype(F32)
    mask_s = valid[..., None] & (dist_s >= 0)
    s_s = (jnp.einsum('btghd,btgksd->btghks', q, ks) * scale
           - slopes[None, None, :, :, None, None] * dist_s[:, :, :, None])
    p_s = masked_softmax(s_s.reshape(Bn, T, N_KV, hpg, n_k * SEL_BLOCK),
                         mask_s[:, :, :, None].reshape(Bn, T, N_KV, 1, n_k * SEL_BLOCK))
    o_s = jnp.einsum('btghm,btgmd->btghd', p_s, vs.reshape(Bn, T, N_KV, n_k * SEL_BLOCK, HEAD_DIM))
    dist_w = pos[:, None] - w_pos[None, :]
    mask_w = (dist_w >= 0) & (dist_w < WINDOW) & (w_pos[None, :] >= 0)
    s_w = (jnp.einsum('btghd,bmgd->btghm', q, kw) * scale
           - slopes[None, None, :, :, None] * dist_w.astype(F32)[None, :, None, None, :])
    p_w = masked_softmax(s_w, mask_w[None, :, None, None, :])
    o_w = jnp.einsum('btghm,bmgd->btghd', p_w, vw)
    return gates[..., 0:1] * o_c + gates[..., 1:2] * o_s + gates[..., 2:3] * o_w


def prompt_attention(kv2d, kv_rows, cmp_pe, cmp_w1, cmp_b1, cmp_w2):
    kv_cmp = kv_rows[0]
    L = kv_cmp.shape[1]
    pages = kv_cmp.reshape(L // PAGE_SIZE, PAGE_SIZE, 2, N_KV, HEAD_DIM)
    table = jnp.arange(L // PAGE_SIZE, dtype=jnp.int32)[None]
    kcvc = compress_blocks(cmp_lohi(pages, table, cmp_w1), cmp_pe, cmp_w1, cmp_b1, cmp_w2)

    def attend(q2d, gates_pad):
        return prompt_nsa(q2d[None], gates_pad[None], kv2d, kcvc)

    return attend


def sample_attention(kv2d, kv_rows, cache_cmp, cache_slc, cache_win, page_table,
                     cmp_pe, cmp_w1, cmp_b1, cmp_w2):
    kv_cmp, kv_slc, kv_win = kv_rows
    Bn, T = kv_cmp.shape[:2]
    n_pages = page_table.shape[1]
    past_len = n_pages * PAGE_SIZE
    past_cmp = cache_cmp[page_table].reshape(Bn, past_len, 2, N_KV, HEAD_DIM)
    lo, hi = cmp_partials(past_cmp, cmp_pe, cmp_w1)
    n_new_sub = T // CMP_STRIDE
    if n_new_sub > 0:
        lo_new, hi_new = cmp_partials(kv_cmp[:, :n_new_sub * CMP_STRIDE], cmp_pe, cmp_w1)
        lo = jnp.concatenate([lo, lo_new], axis=1)
        hi = jnp.concatenate([hi, hi_new], axis=1)
    kc, vc = compress(lo, hi, cmp_b1, cmp_w2)
    c_end = jnp.arange(kc.shape[1]) * CMP_STRIDE + (CMP_BLOCK - 1)
    n_past_blk = past_len // SEL_BLOCK
    n_new_blk = -(-T // SEL_BLOCK)
    n_sel = n_past_blk + n_new_blk
    blk_per_page = PAGE_SIZE // SEL_BLOCK
    pool = cache_slc.reshape(cache_slc.shape[0], blk_per_page, SEL_BLOCK, 2, N_KV, HEAD_DIM)
    new_blocks = jnp.pad(kv_slc, ((0, 0), (0, n_new_blk * SEL_BLOCK - T), (0, 0), (0, 0), (0, 0)))
    new_blocks = new_blocks.reshape(Bn, n_new_blk, SEL_BLOCK, 2, N_KV, HEAD_DIM)
    b_i = jnp.arange(Bn)[:, None, None, None]
    g_i = jnp.arange(N_KV)[None, None, :, None]

    def gather_sel(idx):
        jp = jnp.minimum(idx, n_past_blk - 1)
        page = page_table[b_i, jp // blk_per_page]
        sub = jp % blk_per_page
        jn = jnp.clip(idx - n_past_blk, 0, n_new_blk - 1)
        is_new = (idx >= n_past_blk)[..., None, None]
        k = jnp.where(is_new, new_blocks[b_i, jn, :, 0, g_i, :], pool[page, sub, :, 0, g_i, :])
        v = jnp.where(is_new, new_blocks[b_i, jn, :, 1, g_i, :], pool[page, sub, :, 1, g_i, :])
        return k, v

    w_rows = jnp.concatenate([cache_win, kv_win], axis=1)
    w_pos = past_len - cache_win.shape[1] + jnp.arange(w_rows.shape[1])
    pos = past_len + jnp.arange(T)

    def attend(q2d, gates_pad):
        hpg = q2d.shape[1] // (N_KV * HEAD_DIM)
        q = q2d.reshape(Bn, T, N_KV, hpg, HEAD_DIM)
        gates = gates_pad.reshape(Bn, T, N_KV, 128)[..., :3 * hpg].reshape(Bn, T, N_KV, hpg, 3)
        o = nsa_attend(q, gates, pos, kc, vc, c_end, gather_sel, n_sel,
                       w_rows[:, :, 0], w_rows[:, :, 1], w_pos)
        return o.reshape(Bn * T, -1)

    return attend


def _rows(v, per_tok):
    return v if v.shape[0] == 1 else jnp.repeat(v, per_tok, axis=0)


def trunk(x, mods, kv_mod, h0, make_attend, p):
    Bn, L, D = x.shape
    M = Bn * L
    n_heads = D // HEAD_DIM
    hpg = n_heads // N_KV
    depth = p["mod_w"].shape[0]
    n_a = depth // 2
    xr = x.reshape(M, D)
    new_h = []
    kv_rows = None
    attend = None
    for l in range(depth):
        sh1, sc1, ga1, sh2, sc2, ga2 = [_rows(m, L) for m in jnp.split(mods[l], 6, axis=-1)]
        if l == n_a:
            shift, scale = [_rows(m, L) for m in jnp.split(kv_mod, 2, axis=-1)]
            hk = norm_mod(xr, p["kv_norm"], scale, shift, BF16)
            kv2d = mm_wide(hk, p["w_kv"], tn=512)
            kv = kv2d.reshape(Bn, L, 3, 2, N_KV, HEAD_DIM)
            kv_rows = (kv[:, :, 0], kv[:, :, 1], kv[:, :, 2])
            attend = make_attend(kv2d, kv_rows)
        if l < n_a:
            u = norm_mod(xr, p["norm_pre"][l, 0], sc1, sh1, F32)
            gy, h_last = s5_mixer_core(u.reshape(Bn, L, D), h0[l], p["ssm_lam_re"][l], p["ssm_lam_im"][l],
                                       p["ssm_log_dt"][l], p["ssm_b_re"][l], p["ssm_b_im"][l],
                                       p["ssm_c_re"][l], p["ssm_c_im"][l], p["ssm_d"][l])
            new_h.append(h_last)
            xr = mm_tall(gy, p["ssm_w_glu"], xr, p["norm_post"][l, 0], ga1, layer=l, glu=True)
        else:
            lb = l - n_a
            h = norm_mod(xr, p["norm_pre"][l, 0], sc1, sh1, BF16)
            q = mm_wide(h, p["nsa_w_qg"], layer=lb, n_out=n_heads * HEAD_DIM, tn=512)
            gates_pad = mm_wide(h, p["w_gate_pad"], layer=lb, epilogue="sigmoid", tn=512)
            o = attend(q, gates_pad)
            xr = mm_tall(o, p["nsa_w_o"], xr, p["norm_post"][l, 0], ga1, layer=lb)
        h = norm_mod(xr, p["norm_pre"][l, 1], sc2, sh2, BF16)
        f = mm_wide(h, p["mlp_w1"], layer=l, epilogue="sqrelu", out_dtype=BF16, tn=1024)
        xr = mm_tall(f, p["mlp_w2"], xr, p["norm_post"][l, 1], ga2, layer=l, tk=512)
    return xr.reshape(Bn, L, D), jnp.stack(new_h), kv_rows


def kernel(x_prompt, x_sample, c_prompt, c_sample, state_ssm, cache_cmp, cache_slc, cache_win, page_table, mod_w, mod_b, norm_pre, norm_post, mlp_w1, mlp_w2, ssm_lam_re, ssm_lam_im, ssm_log_dt, ssm_b_re, ssm_b_im, ssm_c_re, ssm_c_im, ssm_d, ssm_w_glu, kv_norm, kv_mod_w, kv_mod_b, w_kv, cmp_pe, cmp_w1, cmp_b1, cmp_w2, nsa_w_qg, nsa_w_o):
    D = x_prompt.shape[-1]
    depth = mod_w.shape[0]
    n_heads = D // HEAD_DIM
    bp, bs = c_prompt.shape[0], c_sample.shape[0]
    c_all = jnp.concatenate([c_prompt, c_sample], axis=0)
    n_c = c_all.shape[0]
    c_all = jnp.pad(c_all, ((0, -n_c % 8), (0, 0)))
    mods = [mm_wide(c_all, mod_w, layer=l, bias=mod_b, prologue="silu", exact=True, tn=512) for l in range(depth)]
    kv_mod = mm_wide(c_all, kv_mod_w, bias=kv_mod_b, prologue="silu", exact=True, tn=512)
    hpg = n_heads // N_KV
    w_gate = nsa_w_qg[:, :, n_heads * HEAD_DIM:].reshape(nsa_w_qg.shape[0], D, N_KV, 3 * hpg)
    w_gate_pad = jnp.pad(w_gate, ((0, 0), (0, 0), (0, 0), (0, 128 - 3 * hpg))).reshape(-1, D, N_KV * 128)
    p = dict(mod_w=mod_w, norm_pre=norm_pre, norm_post=norm_post, mlp_w1=mlp_w1, mlp_w2=mlp_w2.astype(BF16),
             ssm_lam_re=ssm_lam_re, ssm_lam_im=ssm_lam_im, ssm_log_dt=ssm_log_dt, ssm_b_re=ssm_b_re,
             ssm_b_im=ssm_b_im, ssm_c_re=ssm_c_re, ssm_c_im=ssm_c_im, ssm_d=ssm_d,
             ssm_w_glu=ssm_w_glu.astype(BF16), kv_norm=kv_norm, w_kv=w_kv, nsa_w_qg=nsa_w_qg,
             nsa_w_o=nsa_w_o.astype(BF16), w_gate_pad=w_gate_pad)

    def make_prompt(kv2d, kv_rows):
        return prompt_attention(kv2d, kv_rows, cmp_pe, cmp_w1, cmp_b1, cmp_w2)

    def make_sample(kv2d, kv_rows):
        return sample_attention_paged(kv_rows, cache_cmp, cache_slc, cache_win, page_table,
                                      cmp_pe, cmp_w1, cmp_b1, cmp_w2)

    n_a = depth // 2
    G = D // SSM_GROUP
    h0_prompt = jnp.zeros((n_a, bp, G, STATE_DIM, 2), F32)
    y_prompt, ssm_prompt, rows_prompt = trunk(
        x_prompt, [m[:bp] for m in mods], kv_mod[:bp], h0_prompt, make_prompt, p)
    y_sample, ssm_sample, rows_sample = trunk(
        x_sample, [m[bp:bp + bs] for m in mods], kv_mod[bp:bp + bs], state_ssm, make_sample, p)
    cmp_prompt, slc_prompt, win_rows_prompt = rows_prompt
    cmp_sample, slc_sample, win_sample = rows_sample
    win_prompt = win_rows_prompt[:, -min(WINDOW, x_prompt.shape[1]):]
    return (y_prompt, y_sample, ssm_prompt, ssm_sample, cmp_prompt, cmp_sample,
            slc_prompt, slc_sample, win_prompt, win_sample)
```

```python
import functools
import math

import jax
import jax.numpy as jnp
import numpy as np
from jax import lax
from jax.experimental import pallas as pl
from jax.experimental.pallas import tpu as pltpu

F32 = jnp.float32
BF16 = jnp.bfloat16

SSM_GROUP = 16
STATE_DIM = 64
HEAD_DIM = 128
N_KV = 4
CMP_STRIDE = 16
CMP_BLOCK = 2 * CMP_STRIDE
SEL_BLOCK = 64
SUBS_PER_SEL = SEL_BLOCK // CMP_STRIDE
TOPK = 16
WINDOW = 512
QBLK = 128
PAGE_SIZE = 128
EPS = 1e-6
NEG = -1e30
FORCE_BONUS = 1e4

V7X_VMEM_LIMIT_BYTES = 56 * 1024 * 1024
HIGHEST = lax.Precision.HIGHEST
LOG2E = math.log2(math.e)


def _cparams(n_axes):
    return pltpu.CompilerParams(dimension_semantics=("arbitrary",) * n_axes,
                                vmem_limit_bytes=V7X_VMEM_LIMIT_BYTES)


def _pick(n, pref):
    if n <= pref:
        return n
    t = pref
    while n % t:
        t //= 2
    return t


def _norm_mod_kernel(x_ref, g_ref, sc_ref, sh_ref, o_ref):
    x = x_ref[...]
    r = lax.rsqrt(jnp.mean(x * x, axis=-1, keepdims=True) + EPS)
    y = (x * r) * g_ref[...]
    o_ref[...] = (y * (1.0 + sc_ref[...]) + sh_ref[...]).astype(o_ref.dtype)


def norm_mod(x, g, scale, shift, out_dtype):
    M, D = x.shape
    tm = _pick(M, 512)
    per_row = scale.shape[0] != 1
    mod_spec = pl.BlockSpec((tm, D), lambda i: (i, 0)) if per_row else pl.BlockSpec((1, D), lambda i: (0, 0))
    return pl.pallas_call(
        _norm_mod_kernel,
        grid=(M // tm,),
        in_specs=[pl.BlockSpec((tm, D), lambda i: (i, 0)),
                  pl.BlockSpec((1, D), lambda i: (0, 0)),
                  mod_spec, mod_spec],
        out_specs=pl.BlockSpec((tm, D), lambda i: (i, 0)),
        out_shape=jax.ShapeDtypeStruct((M, D), out_dtype),
        compiler_params=_cparams(1),
        name="norm_mod",
    )(x, g.reshape(1, D), scale, shift)


def _mm_wide_kernel(*refs, prologue, epilogue, has_bias, exact, n_w):
    a_ref = refs[0]
    w_refs = refs[1:1 + n_w]
    pos = 1 + n_w
    b_ref = refs[pos] if has_bias else None
    pos += int(has_bias)
    o_ref = refs[pos]
    wbf_refs = refs[pos + 1:]

    a = a_ref[...]
    if prologue == "silu":
        a = a * jax.nn.sigmoid(a)
    if exact:
        zs = [jnp.dot(a, w[...], preferred_element_type=F32, precision=HIGHEST) for w in w_refs]
    else:
        @pl.when(pl.program_id(1) == 0)
        def _():
            for w, wbf in zip(w_refs, wbf_refs):
                wbf[...] = w[...].astype(BF16)

        a = a.astype(BF16)
        zs = [jnp.dot(a, wbf[...], preferred_element_type=F32) for wbf in wbf_refs]
    z = zs[0]
    if has_bias:
        z = z + b_ref[...]
    if epilogue == "sqrelu":
        z = jnp.square(jnp.maximum(z, 0.0))
    elif epilogue == "sigmoid":
        z = jax.nn.sigmoid(z)
    elif epilogue == "glu":
        z = z * jax.nn.sigmoid(zs[1])
    o_ref[...] = z.astype(o_ref.dtype)


def mm_wide(a, w, *, layer=None, col0=0, n_out=None, bias=None, prologue=None, epilogue=None,
            exact=False, out_dtype=F32, tm=1024, tn=512):
    M, K = a.shape
    n_out = n_out if n_out is not None else w.shape[-1] - col0
    tm = _pick(M, tm)
    tn = _pick(n_out, tn)
    assert col0 % tn == 0 and n_out % tn == 0
    n_w = 2 if epilogue == "glu" else 1
    jb = col0 // tn

    def w_spec(extra):
        if layer is None:
            return pl.BlockSpec((K, tn), lambda j, i: (0, jb + extra + j))
        return pl.BlockSpec((None, K, tn), lambda j, i: (layer, 0, jb + extra + j))

    in_specs = [pl.BlockSpec((tm, K), lambda j, i: (i, 0))] + [w_spec(e * (n_out // tn)) for e in range(n_w)]
    args = [a] + [w] * n_w
    if bias is not None:
        if layer is None:
            in_specs.append(pl.BlockSpec((1, tn), lambda j, i: (0, jb + j)))
            args.append(bias.reshape(1, -1))
        else:
            in_specs.append(pl.BlockSpec((None, 1, tn), lambda j, i: (layer, 0, jb + j)))
            args.append(bias.reshape(bias.shape[0], 1, -1))
    scratch = [] if exact else [pltpu.VMEM((K, tn), BF16) for _ in range(n_w)]
    return pl.pallas_call(
        functools.partial(_mm_wide_kernel, prologue=prologue, epilogue=epilogue,
                          has_bias=bias is not None, exact=exact, n_w=n_w),
        grid=(n_out // tn, M // tm),
        in_specs=in_specs,
        out_specs=pl.BlockSpec((tm, tn), lambda j, i: (i, j)),
        out_shape=jax.ShapeDtypeStruct((M, n_out), out_dtype),
        scratch_shapes=scratch,
        compiler_params=_cparams(2),
        name="mm_wide",
    )(*args)


def _mm_tall_kernel(a_ref, *refs, n_w, nk):
    w_refs = refs[:n_w]
    res_ref, g_ref, ga_ref, o_ref = refs[n_w:n_w + 4]
    acc_refs = refs[n_w + 4:]
    k = pl.program_id(1)

    @pl.when(k == 0)
    def _():
        for acc in acc_refs:
            acc[...] = jnp.zeros_like(acc)

    a = a_ref[...].astype(BF16)
    for w, acc in zip(w_refs, acc_refs):
        acc[...] += jnp.dot(a, w[...].astype(BF16), preferred_element_type=F32)

    @pl.when(k == nk - 1)
    def _():
        m = acc_refs[0][...]
        if n_w == 2:
            m = m * jax.nn.sigmoid(acc_refs[1][...])
        r = lax.rsqrt(jnp.mean(m * m, axis=-1, keepdims=True) + EPS)
        o_ref[...] = res_ref[...] + ga_ref[...] * ((m * r) * g_ref[...])


def mm_tall(a, w, res, g, gate, *, layer=None, glu=False, tm=512, tk=512):
    M, K = a.shape
    N = res.shape[1]
    tm = _pick(M, tm)
    tk = _pick(K, tk)
    nk = K // tk
    n_w = 2 if glu else 1

    def w_spec(e):
        if layer is None:
            return pl.BlockSpec((tk, N), lambda i, k: (k, e))
        return pl.BlockSpec((None, tk, N), lambda i, k: (layer, k, e))

    per_row = gate.shape[0] != 1
    ga_spec = pl.BlockSpec((tm, N), lambda i, k: (i, 0)) if per_row else pl.BlockSpec((1, N), lambda i, k: (0, 0))
    return pl.pallas_call(
        functools.partial(_mm_tall_kernel, n_w=n_w, nk=nk),
        grid=(M // tm, nk),
        in_specs=[pl.BlockSpec((tm, tk), lambda i, k: (i, k))] + [w_spec(e) for e in range(n_w)]
        + [pl.BlockSpec((tm, N), lambda i, k: (i, 0)),
           pl.BlockSpec((1, N), lambda i, k: (0, 0)),
           ga_spec],
        out_specs=pl.BlockSpec((tm, N), lambda i, k: (i, 0)),
        out_shape=jax.ShapeDtypeStruct((M, N), F32),
        scratch_shapes=[pltpu.VMEM((tm, N), F32) for _ in range(n_w)],
        compiler_params=_cparams(2),
        name="mm_tall",
    )(a, *([w] * n_w), res, g.reshape(1, N), gate)


S5_GROUPS_PER_STEP = 128 // SSM_GROUP
P2 = 2 * STATE_DIM


def _s5_prep_kernel(lam_re_ref, lam_im_ref, ldt_ref, btr_ref, bti_ref, cr_ref, ci_ref,
                    ws_ref, wct_ref, wm_ref, at_ref, *, tc, gb):
    ws_ref[...] = jnp.zeros(ws_ref.shape, ws_ref.dtype)
    wct_ref[...] = jnp.zeros(wct_ref.shape, wct_ref.dtype)
    C = SSM_GROUP
    tcc = tc * C
    wk = max(tcc, 128)
    lane = lax.broadcasted_iota(jnp.int32, (C, P2), 1)
    is_re = lane < STATE_DIM
    kk = lax.broadcasted_iota(jnp.int32, (tc + 1, P2), 0).astype(F32)
    lane_k = lax.broadcasted_iota(jnp.int32, (C, wk), 1)
    for gg in range(gb):
        lr = lam_re_ref[gg]
        li = lam_im_ref[gg]
        dt = jnp.exp(ldt_ref[gg])
        mag = jnp.exp(kk * (lr * dt))
        ang = kk * (li * dt)
        pr = mag * jnp.cos(ang)
        pi = mag * jnp.sin(ang)
        x = pr[1:2] - 1.0
        y = pi[1:2]
        den = lr * lr + li * li
        cfr = (x * lr + y * li) / den
        cfi = (y * lr - x * li) / den
        btr = btr_ref[gg]
        bti = bti_ref[gg]
        bbr = cfr * btr - cfi * bti
        bbi = cfr * bti + cfi * btr
        cre = cr_ref[gg]
        cim = ci_ref[gg]

        def bm(k):
            return jnp.where(is_re, pr[k:k + 1] * bbr - pi[k:k + 1] * bbi, pr[k:k + 1] * bbi + pi[k:k + 1] * bbr)

        def cm(k):
            return jnp.where(is_re, cre * pr[k:k + 1] - cim * pi[k:k + 1], -(cre * pi[k:k + 1] + cim * pr[k:k + 1]))

        cms = [cm(k) for k in range(tc + 1)]
        cs0 = jnp.concatenate(cms[:tc] + [jnp.zeros((wk - tcc, P2), F32)] * (wk > tcc), axis=0)
        kst = lax.dot_general(bm(0), cs0, (((1,), (1,)), ((), ())), preferred_element_type=F32,
                              precision=HIGHEST)
        r0 = gg * C
        for s in range(tc):
            ws_ref[s, r0:r0 + C, gg * P2:(gg + 1) * P2] = bm(tc - 1 - s).astype(ws_ref.dtype)
            wct_ref[s, r0:r0 + C, gg * P2:(gg + 1) * P2] = cms[s + 1].astype(wct_ref.dtype)
            shift = (r0 - s * C) % wk
            moved = kst if shift == 0 else pltpu.roll(kst, shift, 1)
            blockdiag = jnp.where((lane_k >= r0) & (lane_k < r0 + C), moved, 0.0)
            wm_ref[s, r0:r0 + C, :] = blockdiag[:, :128].astype(wm_ref.dtype)
        at_ref[gg, 0:1, :] = pr[tc:tc + 1]
        at_ref[gg, 1:2, :] = jnp.where(is_re[0:1], -pi[tc:tc + 1], pi[tc:tc + 1])


def s5_prep(lam_re, lam_im, log_dt, b_re, b_im, c_re, c_im, tc):
    G = lam_re.shape[0]
    gb = _pick(G, S5_GROUPS_PER_STEP)
    C = SSM_GROUP
    tcc = tc * C
    dup = lambda v: jnp.concatenate([v, v], axis=-1)
    lam_re2 = dup(lam_re)[:, None, :]
    lam_im2 = dup(lam_im)[:, None, :]
    ldt2 = jnp.broadcast_to(log_dt[:, None, None], (G, 1, P2))
    btr = dup(jnp.swapaxes(b_re, 1, 2))
    bti = dup(jnp.swapaxes(b_im, 1, 2))
    cr2 = dup(c_re)
    ci2 = dup(c_im)
    vec = pl.BlockSpec((gb, 1, P2), lambda i: (i, 0, 0))
    mat = pl.BlockSpec((gb, C, P2), lambda i: (i, 0, 0))
    return pl.pallas_call(
        functools.partial(_s5_prep_kernel, tc=tc, gb=gb),
        grid=(G // gb,),
        in_specs=[vec, vec, vec, mat, mat, mat, mat],
        out_specs=[pl.BlockSpec((None, tc, 128, gb * P2), lambda i: (i, 0, 0, 0)),
                   pl.BlockSpec((None, tc, 128, gb * P2), lambda i: (i, 0, 0, 0)),
                   pl.BlockSpec((None, tc, 128, 128), lambda i: (i, 0, 0, 0)),
                   pl.BlockSpec((gb, 2, P2), lambda i: (i, 0, 0))],
        out_shape=[jax.ShapeDtypeStruct((G // gb, tc, 128, gb * P2), BF16),
                   jax.ShapeDtypeStruct((G // gb, tc, 128, gb * P2), BF16),
                   jax.ShapeDtypeStruct((G // gb, tc, 128, 128), BF16),
                   jax.ShapeDtypeStruct((G, 2, P2), F32)],
        compiler_params=_cparams(1),
        name="s5_prep",
    )(lam_re2, lam_im2, ldt2, btr, bti, cr2, ci2)


def _s5_sums_kernel(u_ref, ws_ref, s_ref, *, tc):
    acc = None
    for s in range(tc):
        z = jnp.dot(u_ref[:, s, :].astype(BF16), ws_ref[s], preferred_element_type=F32)
        acc = z if acc is None else acc + z
    s_ref[...] = acc


def s5_chunk_sums(u3, ws):
    R, tc, D = u3.shape
    nb = D // 128
    sw = ws.shape[-1]
    return pl.pallas_call(
        functools.partial(_s5_sums_kernel, tc=tc),
        grid=(nb,),
        in_specs=[pl.BlockSpec((R, tc, 128), lambda i: (0, 0, i)),
                  pl.BlockSpec((None, tc, 128, sw), lambda i: (i, 0, 0, 0))],
        out_specs=pl.BlockSpec((R, sw), lambda i: (0, i)),
        out_shape=jax.ShapeDtypeStruct((R, nb * sw), F32),
        compiler_params=_cparams(1),
        name="s5_sums",
    )(u3, ws)


def _s5_out_kernel(u_ref, hin_ref, wm_ref, wct_ref, d_ref, o_ref, *, tc):
    us = [u_ref[:, s, :] for s in range(tc)]
    ub = [x.astype(BF16) for x in us]
    hb = hin_ref[...].astype(BF16)
    d = d_ref[...]
    for t in range(tc):
        y = lax.dot_general(hb, wct_ref[t], (((1,), (1,)), ((), ())), preferred_element_type=F32)
        for s in range(t + 1):
            y = y + jnp.dot(ub[s], wm_ref[t - s], preferred_element_type=F32)
        o_ref[:, t, :] = jax.nn.gelu(y + d * us[t])


def s5_chunk_out(u3, hin, wm, wct, d_skip):
    R, tc, D = u3.shape
    nb = D // 128
    sw = wct.shape[-1]
    return pl.pallas_call(
        functools.partial(_s5_out_kernel, tc=tc),
        grid=(nb,),
        in_specs=[pl.BlockSpec((R, tc, 128), lambda i: (0, 0, i)),
                  pl.BlockSpec((R, sw), lambda i: (0, i)),
                  pl.BlockSpec((None, tc, 128, 128), lambda i: (i, 0, 0, 0)),
                  pl.BlockSpec((None, tc, 128, sw), lambda i: (i, 0, 0, 0)),
                  pl.BlockSpec((1, 128), lambda i: (0, i))],
        out_specs=pl.BlockSpec((R, tc, 128), lambda i: (0, 0, i)),
        out_shape=jax.ShapeDtypeStruct((R, tc, D), F32),
        compiler_params=_cparams(1),
        name="s5_chunk_out",
    )(u3, hin, wm, wct, d_skip.reshape(1, D))


S5_CHUNKS_PER_STEP = 64


def _s5_scan_kernel(s_ref, h0_ref, at_ref, hin_ref, hfin_ref, h_scr, *, cb, n_steps):
    j = pl.program_id(1)

    @pl.when(j == 0)
    def _():
        h_scr[...] = h0_ref[...]

    ar = at_ref[0]
    ai = at_ref[1]

    def step(c, h):
        hin_ref[c] = h
        return ar * h + ai * pltpu.roll(h, STATE_DIM, 1) + s_ref[c]

    h = lax.fori_loop(0, cb, step, h_scr[...])
    h_scr[...] = h

    @pl.when(j == n_steps - 1)
    def _():
        hfin_ref[...] = h


def s5_scan(s, h0, at):
    Bn, n_chunk, G, _ = s.shape
    cb = _pick(n_chunk, S5_CHUNKS_PER_STEP)
    n_steps = n_chunk // cb
    return pl.pallas_call(
        functools.partial(_s5_scan_kernel, cb=cb, n_steps=n_steps),
        grid=(Bn, n_steps),
        in_specs=[pl.BlockSpec((None, cb, G, P2), lambda b, j: (b, j, 0, 0)),
                  pl.BlockSpec((None, G, P2), lambda b, j: (b, 0, 0)),
                  pl.BlockSpec((2, G, P2), lambda b, j: (0, 0, 0))],
        out_specs=[pl.BlockSpec((None, cb, G, P2), lambda b, j: (b, j, 0, 0)),
                   pl.BlockSpec((None, G, P2), lambda b, j: (b, 0, 0))],
        out_shape=[jax.ShapeDtypeStruct((Bn, n_chunk, G, P2), F32),
                   jax.ShapeDtypeStruct((Bn, G, P2), F32)],
        scratch_shapes=[pltpu.VMEM((G, P2), F32)],
        compiler_params=_cparams(2),
        name="s5_scan",
    )(s, h0, at)


def s5_mixer_core(u, h0, lam_re, lam_im, log_dt, b_re, b_im, c_re, c_im, d_skip):
    Bn, L, D = u.shape
    G = D // SSM_GROUP
    tc = _pick(L, 16)
    n_chunk = L // tc
    ws, wct, wm, at = s5_prep(lam_re, lam_im, log_dt, b_re, b_im, c_re, c_im, tc)
    u3 = u.reshape(Bn * n_chunk, tc, D)
    s = s5_chunk_sums(u3, ws).reshape(Bn, n_chunk, G, P2)
    h0v = jnp.concatenate([h0[..., 0], h0[..., 1]], axis=-1)
    hin, hfin = s5_scan(s, h0v, at.transpose(1, 0, 2))
    gy = s5_chunk_out(u3, hin.reshape(Bn * n_chunk, G * P2), wm, wct, d_skip)
    h_last = jnp.stack([hfin[..., :STATE_DIM], hfin[..., STATE_DIM:]], axis=-1)
    return gy.reshape(Bn * L, D), h_last


SUBS_PER_PAGE = PAGE_SIZE // CMP_STRIDE
KV_COLS = 2 * N_KV * HEAD_DIM
CMP_PAGES_PER_STEP = 8


def _cmp_lohi_kernel(pt_ref, *refs, n_pg):
    x_refs = refs[:n_pg]
    w_ref = refs[n_pg]
    o_ref = refs[n_pg + 1]
    rows = SUBS_PER_PAGE * N_KV
    for c in range(2):
        acc = jnp.zeros((n_pg * rows, 2 * HEAD_DIM), F32)
        for rp in range(CMP_STRIDE // 2):
            parts = []
            for p in range(n_pg):
                a0 = x_refs[p][:, 2 * rp, c].reshape(rows, HEAD_DIM)
                a1 = x_refs[p][:, 2 * rp + 1, c].reshape(rows, HEAD_DIM)
                parts.append(jnp.concatenate([a0, a1], axis=1))
            xs = jnp.concatenate(parts, axis=0).astype(BF16)
            acc = acc + jnp.dot(xs, w_ref[c, rp], preferred_element_type=F32)
        o_ref[c] = acc


def cmp_lohi(pages, page_table, cmp_w1):
    Bn, n_pages = page_table.shape
    n_pg = _pick(n_pages, CMP_PAGES_PER_STEP)
    x = pages.reshape(pages.shape[0], SUBS_PER_PAGE, CMP_STRIDE, 2, N_KV, HEAD_DIM)
    half = CMP_STRIDE // 2
    w_lo = cmp_w1[:, :CMP_STRIDE].reshape(2, half, 2 * HEAD_DIM, HEAD_DIM)
    w_hi = cmp_w1[:, CMP_STRIDE:].reshape(2, half, 2 * HEAD_DIM, HEAD_DIM)
    w = jnp.concatenate([w_lo, w_hi], axis=-1).astype(BF16)
    n_sub = n_pages * SUBS_PER_PAGE

    def x_spec(p):
        return pl.BlockSpec((None, SUBS_PER_PAGE, CMP_STRIDE, 2, N_KV, HEAD_DIM),
                            lambda b, i, pt: (pt[b, i * n_pg + p], 0, 0, 0, 0, 0))

    grid_spec = pltpu.PrefetchScalarGridSpec(
        num_scalar_prefetch=1,
        grid=(Bn, n_pages // n_pg),
        in_specs=[x_spec(p) for p in range(n_pg)]
        + [pl.BlockSpec(w.shape, lambda b, i, pt: (0, 0, 0, 0))],
        out_specs=pl.BlockSpec((None, 2, n_pg * SUBS_PER_PAGE * N_KV, 2 * HEAD_DIM),
                               lambda b, i, pt: (b, 0, i, 0)),
    )
    return pl.pallas_call(
        functools.partial(_cmp_lohi_kernel, n_pg=n_pg),
        grid_spec=grid_spec,
        out_shape=jax.ShapeDtypeStruct((Bn, 2, n_sub * N_KV, 2 * HEAD_DIM), F32),
        compiler_params=_cparams(2),
        name="cmp_lohi",
    )(page_table, *([x] * n_pg), w)


def _compress_kernel(x_ref, pe_ref, w1_ref, b1_ref, w2_ref, o_ref, pe_scr):
    n_rows = x_ref.shape[0]
    half = w1_ref.shape[0] // 2
    c = pl.program_id(1)

    @pl.when(pl.program_id(0) == 0)
    def _():
        pe = jnp.broadcast_to(pe_ref[...], (8, 2 * half))
        pe_scr[c, 0] = jnp.dot(pe[:, :half], w1_ref[:half], preferred_element_type=F32, precision=HIGHEST)
        pe_scr[c, 1] = jnp.dot(pe[:, half:], w1_ref[half:], preferred_element_type=F32, precision=HIGHEST)

    x = x_ref[...]
    lo = x[:, :HEAD_DIM] + pe_scr[c, 0][0:1]
    hi = x[:, HEAD_DIM:] + pe_scr[c, 1][0:1]
    hi_next = pltpu.roll(hi, n_rows - N_KV, 0)
    h = jax.nn.gelu(lo + hi_next + b1_ref[...])
    o_ref[...] = jnp.dot(h.astype(BF16), w2_ref[...].astype(BF16), preferred_element_type=F32).astype(o_ref.dtype)


def compress_blocks(lohi, cmp_pe, cmp_w1, cmp_b1, cmp_w2):
    Bn, _, n_rows, _ = lohi.shape
    kdim = CMP_BLOCK * HEAD_DIM
    out = pl.pallas_call(
        _compress_kernel,
        grid=(Bn, 2),
        in_specs=[pl.BlockSpec((None, None, n_rows, 2 * HEAD_DIM), lambda b, c: (b, c, 0, 0)),
                  pl.BlockSpec((None, 1, kdim), lambda b, c: (c, 0, 0)),
                  pl.BlockSpec((None, kdim, HEAD_DIM), lambda b, c: (c, 0, 0)),
                  pl.BlockSpec((None, 1, HEAD_DIM), lambda b, c: (c, 0, 0)),
                  pl.BlockSpec((None, HEAD_DIM, HEAD_DIM), lambda b, c: (c, 0, 0))],
        out_specs=pl.BlockSpec((None, None, n_rows, HEAD_DIM), lambda b, c: (b, c, 0, 0)),
        out_shape=jax.ShapeDtypeStruct((Bn, 2, n_rows, HEAD_DIM), BF16),
        scratch_shapes=[pltpu.VMEM((2, 2, 8, HEAD_DIM), F32)],
        compiler_params=_cparams(2),
        name="compress",
    )(lohi, cmp_pe.reshape(2, 1, kdim), cmp_w1.reshape(2, kdim, HEAD_DIM),
      cmp_b1.reshape(2, 1, HEAD_DIM), cmp_w2)
    return out.reshape(Bn, 2, n_rows // N_KV, N_KV, HEAD_DIM).transpose(0, 1, 3, 2, 4)


N_CAND = 128
NEG_TAKEN = -3e38


def _topk_mask(score, axis, k_sel):
    idx = lax.broadcasted_iota(jnp.int32, score.shape, axis)
    sel = jnp.zeros(score.shape, F32)
    for _ in range(k_sel):
        m = jnp.max(score, axis=axis, keepdims=True)
        first = jnp.min(jnp.where(score == m, idx, N_CAND), axis=axis, keepdims=True)
        hit = idx == first
        sel = jnp.where(hit & (m > 0.5 * NEG), 1.0, sel)
        score = jnp.where(hit, NEG_TAKEN, score)
    return sel


def _cmp_attn_kernel(slope_ref, q_ref, kc_ref, vc_ref, gate_ref, wsel_ref, oc_ref, sel_ref, *,
                     tq, hpg, gps, pos0, pos_step, transposed, k_sel, n_cand):
    i = pl.program_id(1)
    n_cmp = kc_ref.shape[1]
    gw = hpg * HEAD_DIM
    scale = HEAD_DIM ** -0.5
    base = pos0 + i * pos_step
    t_idx = lax.broadcasted_iota(jnp.int32, (tq, n_cmp), 0)
    n_idx = lax.broadcasted_iota(jnp.int32, (tq, n_cmp), 1)
    dist_i = base + t_idx - (n_idx * CMP_STRIDE + (CMP_BLOCK - 1))
    valid = dist_i >= 0
    dist = dist_i.astype(F32)
    gates = gate_ref[...]
    shape, j_ax, t_ax = ((N_CAND, tq), 0, 1) if transposed else ((tq, N_CAND), 1, 0)
    j = lax.broadcasted_iota(jnp.int32, shape, j_ax)
    blk = (base + lax.broadcasted_iota(jnp.int32, shape, t_ax)) // SEL_BLOCK
    forced = (j == 0) | (j == blk) | (j == blk - 1)
    visible = (j <= blk) & (j < n_cand)
    scores = []
    for gi in range(gps):
        g = pl.program_id(2) * gps + gi
        q = q_ref[:, gi * gw:(gi + 1) * gw]
        qs = jnp.concatenate([q[:, h * HEAD_DIM:(h + 1) * HEAD_DIM] for h in range(hpg)], axis=0)
        qs = (qs * scale).astype(BF16)
        s = lax.dot_general(qs, kc_ref[gi], (((1,), (1,)), ((), ())), preferred_element_type=F32)
        vc = vc_ref[gi]
        psum = jnp.zeros((tq, n_cmp), F32)
        for h in range(hpg):
            sh = s[h * tq:(h + 1) * tq] - slope_ref[g * hpg + h] * dist
            sh = jnp.where(valid, sh, NEG)
            m = jnp.max(sh, axis=-1, keepdims=True)
            e = jnp.where(valid, jnp.exp(sh - m), 0.0)
            p = e / jnp.maximum(jnp.sum(e, axis=-1, keepdims=True), 1e-30)
            psum = psum + p
            o_h = jnp.dot(p.astype(BF16), vc, preferred_element_type=F32)
            c0 = gi * gw + h * HEAD_DIM
            oc_ref[:, c0:c0 + HEAD_DIM] = o_h * gates[:, gi * 128 + 3 * h:gi * 128 + 3 * h + 1]
        if transposed:
            imp = lax.dot_general(wsel_ref[...], psum, (((1,), (1,)), ((), ())),
                                  preferred_element_type=F32, precision=HIGHEST)
        else:
            imp = lax.dot_general(psum, wsel_ref[...], (((1,), (1,)), ((), ())),
                                  preferred_element_type=F32, precision=HIGHEST)
        scores.append(jnp.where(visible, imp + jnp.where(forced, FORCE_BONUS, 0.0), NEG))
    if transposed:
        for gi in range(gps):
            sel = _topk_mask(scores[gi], 0, k_sel).T
            sel_ref[:, gi * N_CAND:(gi + 1) * N_CAND] = sel.astype(sel_ref.dtype)
    else:
        sel = _topk_mask(jnp.concatenate(scores, axis=0), 1, k_sel)
        for gi in range(gps):
            sel_ref[:, gi * N_CAND:(gi + 1) * N_CAND] = sel[gi * tq:(gi + 1) * tq].astype(sel_ref.dtype)


def _sel_weights(n_cmp_pad, n_cmp):
    j = np.arange(N_CAND)[:, None]
    n = np.arange(n_cmp_pad)[None, :]
    w = (n >= SUBS_PER_SEL * j - 1) & (n <= SUBS_PER_SEL * j + SUBS_PER_SEL - 1) & (n < n_cmp)
    return jnp.asarray(w.astype(np.float32))


def cmp_attention(q, kcvc, gates, *, tq, pos0, pos_step, transposed, k_sel, n_cand):
    Bn, T, D = q.shape
    n_heads = D // HEAD_DIM
    hpg = n_heads // N_KV
    n_sub = kcvc.shape[3]
    gw = hpg * HEAD_DIM
    slopes = alibi_slopes(n_heads).reshape(-1)
    wsel = _sel_weights(n_sub, n_sub - 1)
    gps = N_KV if tq < 128 else 1
    kern = functools.partial(_cmp_attn_kernel, tq=tq, hpg=hpg, gps=gps, pos0=pos0, pos_step=pos_step,
                             transposed=transposed, k_sel=k_sel, n_cand=n_cand)
    return pl.pallas_call(
        kern,
        grid=(Bn, T // tq, N_KV // gps),
        in_specs=[pl.BlockSpec(memory_space=pltpu.SMEM),
                  pl.BlockSpec((None, tq, gps * gw), lambda b, i, g: (b, i, g)),
                  pl.BlockSpec((None, None, gps, n_sub, HEAD_DIM), lambda b, i, g: (b, 0, g, 0, 0)),
                  pl.BlockSpec((None, None, gps, n_sub, HEAD_DIM), lambda b, i, g: (b, 1, g, 0, 0)),
                  pl.BlockSpec((None, tq, gps * 128), lambda b, i, g: (b, i, g)),
                  pl.BlockSpec((N_CAND, n_sub), lambda b, i, g: (0, 0))],
        out_specs=[pl.BlockSpec((None, tq, gps * gw), lambda b, i, g: (b, i, g)),
                   pl.BlockSpec((None, tq, gps * N_CAND), lambda b, i, g: (b, i, g))],
        out_shape=[jax.ShapeDtypeStruct((Bn, T, D), F32),
                   jax.ShapeDtypeStruct((Bn, T, N_KV * N_CAND), BF16)],
        compiler_params=_cparams(3),
        name="cmp_attn",
    )(slopes, q, kcvc, kcvc, gates, wsel)


SEL_TK = 512
MASK_BIG = 1e30


def _sel_win_kernel(slope_ref, q_ref, ks_ref, vs_ref, kw_ref, vw_ref, sel_ref, oc_ref, gate_ref, e_ref,
                    o_ref, ksb, vsb, kwb, vwb, *, tq, hpg):
    g = pl.program_id(0)
    i = pl.program_id(1)
    L = ks_ref.shape[0]
    s0 = i * tq
    scale = HEAD_DIM ** -0.5

    @pl.when(i == 0)
    def _():
        ksb[:, :HEAD_DIM] = ks_ref[...].astype(BF16)
        ksb[:, HEAD_DIM:] = e_ref[...]
        vsb[...] = vs_ref[...].astype(BF16)
        kwb[0:WINDOW] = jnp.zeros((WINDOW, HEAD_DIM), BF16)
        vwb[0:WINDOW] = jnp.zeros((WINDOW, HEAD_DIM), BF16)
        kwb[WINDOW:] = kw_ref[...].astype(BF16)
        vwb[WINDOW:] = vw_ref[...].astype(BF16)

    q = q_ref[...]
    qs = jnp.concatenate([q[:, h * HEAD_DIM:(h + 1) * HEAD_DIM] for h in range(hpg)], axis=0)
    qs = (qs * (scale * LOG2E)).astype(BF16)
    slopes = [slope_ref[g * hpg + h] * LOG2E for h in range(hpg)]
    neg_sel = sel_ref[...] - 1.0
    qaug = jnp.concatenate([qs, jnp.concatenate([neg_sel] * hpg, axis=0)], axis=1)
    n_tiles = s0 // SEL_TK + 1

    def sweep(k0, carry, width, diagonal):
        ms, ls, accs = carry
        k0 = pl.multiple_of(k0, SEL_TK)
        v_t = vsb[pl.ds(k0, width), :]
        if diagonal:
            t_idx = lax.broadcasted_iota(jnp.int32, (tq, width), 0)
            c_idx = lax.broadcasted_iota(jnp.int32, (tq, width), 1)
            causal = s0 + t_idx >= k0 + c_idx
        s = lax.dot_general(qaug, ksb[pl.ds(k0, width), :], (((1,), (1,)), ((), ())),
                            preferred_element_type=F32)
        rel = (k0 - s0 + lax.broadcasted_iota(jnp.int32, (1, width), 1)).astype(F32)
        new_m, new_l, new_acc = [], [], []
        for h in range(hpg):
            sh = s[h * tq:(h + 1) * tq] + slopes[h] * rel
            if diagonal:
                sh = jnp.where(causal, sh, -MASK_BIG)
            m_new = jnp.maximum(ms[h], jnp.max(sh, axis=-1, keepdims=True))
            p = jnp.exp2(sh - m_new)
            alpha = jnp.exp2(ms[h] - m_new)
            new_l.append(alpha * ls[h] + jnp.sum(p, axis=-1, keepdims=True))
            new_acc.append(alpha * accs[h] + jnp.dot(p.astype(BF16), v_t, preferred_element_type=F32))
            new_m.append(m_new)
        return tuple(new_m), tuple(new_l), tuple(new_acc)

    init = ((jnp.full((tq, 1), NEG, F32),) * hpg, (jnp.zeros((tq, 1), F32),) * hpg,
            (jnp.zeros((tq, HEAD_DIM), F32),) * hpg)
    n_pairs = (n_tiles - 1) // 2
    carry = lax.fori_loop(0, n_pairs, lambda i2, c: sweep(i2 * (2 * SEL_TK), c, 2 * SEL_TK, False), init)
    k_last = n_pairs * (2 * SEL_TK)
    ms, ls, accs = lax.cond(n_tiles % 2 == 0,
                            lambda c: sweep(k_last, c, 2 * SEL_TK, True),
                            lambda c: sweep(k_last, c, SEL_TK, True), carry)

    wn = WINDOW + tq
    w0 = pl.multiple_of(s0, tq)
    kw_t = kwb[pl.ds(w0, wn), :]
    vw_t = vwb[pl.ds(w0, wn), :]
    sw = lax.dot_general(qs, kw_t, (((1,), (1,)), ((), ())), preferred_element_type=F32)
    t_idx = lax.broadcasted_iota(jnp.int32, (tq, wn), 0)
    c_idx = lax.broadcasted_iota(jnp.int32, (tq, wn), 1)
    dist_i = t_idx + WINDOW - c_idx
    valid = (dist_i >= 0) & (dist_i < WINDOW) & (c_idx + s0 >= WINDOW)
    dist = dist_i.astype(F32)
    gates = gate_ref[...]
    oc = oc_ref[...]
    for h in range(hpg):
        sh = jnp.where(valid, sw[h * tq:(h + 1) * tq] - slopes[h] * dist, NEG)
        m = jnp.max(sh, axis=-1, keepdims=True)
        e = jnp.where(valid, jnp.exp2(sh - m), 0.0)
        p = e / jnp.maximum(jnp.sum(e, axis=-1, keepdims=True), 1e-30)
        o_w = jnp.dot(p.astype(BF16), vw_t, preferred_element_type=F32)
        o_s = accs[h] / jnp.maximum(ls[h], 1e-30)
        out = (oc[:, h * HEAD_DIM:(h + 1) * HEAD_DIM] + gates[:, 3 * h + 1:3 * h + 2] * o_s
               + gates[:, 3 * h + 2:3 * h + 3] * o_w)
        o_ref[:, h * HEAD_DIM:(h + 1) * HEAD_DIM] = out.astype(o_ref.dtype)


def _block_onehot(n_keys):
    key = np.arange(n_keys)[:, None]
    j = np.arange(N_CAND)[None, :]
    return jnp.asarray((key // SEL_BLOCK == j).astype(np.float32) * MASK_BIG, dtype=BF16)


def sel_win_attention(q, kv, sel, oc, gates, *, tq):
    _, L, D = q.shape
    n_heads = D // HEAD_DIM
    hpg = n_heads // N_KV
    gw = hpg * HEAD_DIM
    assert L % SEL_TK == 0 and SEL_TK % tq == 0 and L // SEL_BLOCK <= N_CAND
    slopes = alibi_slopes(n_heads).reshape(-1)
    e = _block_onehot(L)

    def kv_spec(branch, which):
        cb = (branch * 2 + which) * N_KV
        return pl.BlockSpec((L, HEAD_DIM), lambda g, i: (0, cb + g))

    return pl.pallas_call(
        functools.partial(_sel_win_kernel, tq=tq, hpg=hpg),
        grid=(N_KV, L // tq),
        in_specs=[pl.BlockSpec(memory_space=pltpu.SMEM),
                  pl.BlockSpec((None, tq, gw), lambda g, i: (0, i, g)),
                  kv_spec(1, 0), kv_spec(1, 1), kv_spec(2, 0), kv_spec(2, 1),
                  pl.BlockSpec((None, tq, N_CAND), lambda g, i: (0, i, g)),
                  pl.BlockSpec((None, tq, gw), lambda g, i: (0, i, g)),
                  pl.BlockSpec((None, tq, 128), lambda g, i: (0, i, g)),
                  pl.BlockSpec(e.shape, lambda g, i: (0, 0))],
        out_specs=pl.BlockSpec((tq, gw), lambda g, i: (i, g)),
        out_shape=jax.ShapeDtypeStruct((L, D), BF16),
        scratch_shapes=[pltpu.VMEM((L, 2 * HEAD_DIM), BF16), pltpu.VMEM((L, HEAD_DIM), BF16),
                        pltpu.VMEM((WINDOW + L, HEAD_DIM), BF16), pltpu.VMEM((WINDOW + L, HEAD_DIM), BF16)],
        compiler_params=_cparams(2),
        name="sel_win_attn",
    )(slopes, q, kv, kv, kv, kv, sel, oc, gates, e)


def prompt_nsa(q, gates, kv, kcvc):
    L = q.shape[1]
    n_sel = L // SEL_BLOCK
    oc, sel = cmp_attention(q, kcvc, gates, tq=QBLK, pos0=0, pos_step=QBLK, transposed=True,
                            k_sel=min(TOPK, n_sel), n_cand=n_sel)
    return sel_win_attention(q, kv, sel, oc, gates, tq=QBLK)


SAMPLE_TQ = 8
SAMPLE_PAGES_PER_STEP = 8
NEW_ROWS_PAD = 128


def _sample_sel_win_kernel(pt_ref, slope_ref, q_ref, sel_ref, oc_ref, gate_ref, e_ref, *refs,
                           n_pg, hpg, t_real, n_steps, past_len, win_len):
    pages = refs[:n_pg]
    nslc_ref, nwin_ref, cwk_ref, cwv_ref, o_ref, m_scr, l_scr, acc_scr = refs[n_pg:]
    j = pl.program_id(1)
    tq = SAMPLE_TQ
    tk = n_pg * PAGE_SIZE
    scale = HEAD_DIM ** -0.5

    @pl.when(j == 0)
    def _():
        m_scr[...] = jnp.full(m_scr.shape, NEG, F32)
        l_scr[...] = jnp.zeros(l_scr.shape, F32)
        acc_scr[...] = jnp.zeros(acc_scr.shape, F32)

    q = q_ref[...]
    neg_sel = sel_ref[...] - 1.0

    def heads_of(g):
        gw = hpg * HEAD_DIM
        qg = q[:, g * gw:(g + 1) * gw]
        qs = jnp.concatenate([qg[:, h * HEAD_DIM:(h + 1) * HEAD_DIM] for h in range(hpg)], axis=0)
        return (qs * scale).astype(BF16)

    def online(g, h, sh, v):
        r0 = h * tq
        m_old = m_scr[g, r0:r0 + tq]
        m_new = jnp.maximum(m_old, jnp.max(sh, axis=-1, keepdims=True))
        p = jnp.exp(sh - m_new)
        alpha = jnp.exp(m_old - m_new)
        l_scr[g, r0:r0 + tq] = alpha * l_scr[g, r0:r0 + tq] + jnp.sum(p, axis=-1, keepdims=True)
        acc_scr[g, r0:r0 + tq] = alpha * acc_scr[g, r0:r0 + tq] + jnp.dot(p.astype(BF16), v,
                                                                         preferred_element_type=F32)
        m_scr[g, r0:r0 + tq] = m_new

    rows = hpg * tq
    n_rows = N_KV * rows
    n_lane = tk * N_KV
    kf = jnp.concatenate([pg[:, 0].reshape(PAGE_SIZE * N_KV, HEAD_DIM) for pg in pages], axis=0).astype(BF16)
    vf = jnp.concatenate([pg[:, 1].reshape(PAGE_SIZE * N_KV, HEAD_DIM) for pg in pages], axis=0).astype(BF16)
    kaug = jnp.concatenate([kf, e_ref[...]], axis=1)
    qaug = jnp.concatenate(
        [jnp.concatenate([heads_of(g) for g in range(N_KV)], axis=0),
         jnp.concatenate([neg_sel[:, g * N_CAND:(g + 1) * N_CAND] for g in range(N_KV) for _ in range(hpg)],
                         axis=0)], axis=1)
    s = lax.dot_general(qaug, kaug, (((1,), (1,)), ((), ())), preferred_element_type=F32)
    lane = lax.broadcasted_iota(jnp.int32, (1, n_lane), 1)
    rel = (j * tk - past_len + lane // N_KV).astype(F32)
    slope_col = jnp.concatenate([jnp.full((tq, 1), slope_ref[gh], F32) for gh in range(N_KV * hpg)], axis=0)
    row_g = lax.broadcasted_iota(jnp.int32, (n_rows, 1), 0) // rows
    sh = jnp.where(lane % N_KV == row_g, s + slope_col * rel, -MASK_BIG)
    m_old = m_scr[...].reshape(n_rows, 1)
    m_new = jnp.maximum(m_old, jnp.max(sh, axis=-1, keepdims=True))
    p = jnp.exp(sh - m_new)
    alpha = jnp.exp(m_old - m_new)
    l_new = alpha * l_scr[...].reshape(n_rows, 1) + jnp.sum(p, axis=-1, keepdims=True)
    acc_new = alpha * acc_scr[...].reshape(n_rows, HEAD_DIM) + jnp.dot(p.astype(BF16), vf,
                                                                    preferred_element_type=F32)
    m_scr[...] = m_new.reshape(m_scr.shape)
    l_scr[...] = l_new.reshape(l_scr.shape)
    acc_scr[...] = acc_new.reshape(acc_scr.shape)

    @pl.when(j == n_steps - 1)
    def _():
        gates = gate_ref[...]
        oc = oc_ref[...]
        t_n = lax.broadcasted_iota(jnp.int32, (tq, NEW_ROWS_PAD), 0)
        r_n = lax.broadcasted_iota(jnp.int32, (tq, NEW_ROWS_PAD), 1)
        valid_n = (r_n <= t_n) & (r_n < t_real)
        wn = win_len + NEW_ROWS_PAD
        t_w = lax.broadcasted_iota(jnp.int32, (tq, wn), 0)
        c_w = lax.broadcasted_iota(jnp.int32, (tq, wn), 1)
        dist_w = jnp.where(c_w < win_len, t_w + win_len - c_w, t_w - (c_w - win_len))
        cached = c_w < win_len
        valid_w = ((cached & (dist_w < WINDOW) & (c_w + past_len - win_len >= 0))
                   | ((c_w >= win_len) & (dist_w >= 0) & (c_w - win_len < t_real)))
        dist_wf = dist_w.astype(F32)
        for g in range(N_KV):
            kc0 = g * HEAD_DIM
            vc0 = (N_KV + g) * HEAD_DIM
            qs = heads_of(g)
            k_n = nslc_ref[:, kc0:kc0 + HEAD_DIM].astype(BF16)
            v_n = nslc_ref[:, vc0:vc0 + HEAD_DIM].astype(BF16)
            s_n = lax.dot_general(qs, k_n, (((1,), (1,)), ((), ())), preferred_element_type=F32)
            k_w = jnp.concatenate([cwk_ref[:, g, :], nwin_ref[:, kc0:kc0 + HEAD_DIM]], axis=0).astype(BF16)
            v_w = jnp.concatenate([cwv_ref[:, g, :], nwin_ref[:, vc0:vc0 + HEAD_DIM]], axis=0).astype(BF16)
            s_w = lax.dot_general(qs, k_w, (((1,), (1,)), ((), ())), preferred_element_type=F32)
            for h in range(hpg):
                slope = slope_ref[g * hpg + h]
                r0 = h * tq
                sh = jnp.where(valid_n, s_n[r0:r0 + tq] + slope * r_n.astype(F32), -MASK_BIG)
                online(g, h, sh, v_n)
                o_s = acc_scr[g, r0:r0 + tq] / jnp.maximum(l_scr[g, r0:r0 + tq], 1e-30)
                sw = jnp.where(valid_w, s_w[r0:r0 + tq] - slope * dist_wf, NEG)
                m = jnp.max(sw, axis=-1, keepdims=True)
                e = jnp.where(valid_w, jnp.exp(sw - m), 0.0)
                p = e / jnp.maximum(jnp.sum(e, axis=-1, keepdims=True), 1e-30)
                o_w = jnp.dot(p.astype(BF16), v_w, preferred_element_type=F32)
                c0 = (g * hpg + h) * HEAD_DIM
                gc = g * 128 + 3 * h
                out = (oc[:, c0:c0 + HEAD_DIM] + gates[:, gc + 1:gc + 2] * o_s + gates[:, gc + 2:gc + 3] * o_w)
                o_ref[:, c0:c0 + HEAD_DIM] = out.astype(o_ref.dtype)


def _block_onehot_rows(n_keys):
    key = np.repeat(np.arange(n_keys), N_KV)[:, None]
    j = np.arange(N_CAND)[None, :]
    return jnp.asarray((key // SEL_BLOCK == j).astype(np.float32) * MASK_BIG, dtype=BF16)


def sample_sel_win_attention(q, sel, oc, gates, cache_slc, page_table, new_slc, new_win, cache_win, t_real):
    Bn, tq, D = q.shape
    n_heads = D // HEAD_DIM
    hpg = n_heads // N_KV
    n_pages = page_table.shape[1]
    n_pg = _pick(n_pages, SAMPLE_PAGES_PER_STEP)
    n_steps = n_pages // n_pg
    past_len = n_pages * PAGE_SIZE
    win_len = cache_win.shape[1]
    assert past_len // SEL_BLOCK <= N_CAND and past_len % SEL_BLOCK == 0 and t_real <= min(tq, SEL_BLOCK)
    slopes = alibi_slopes(n_heads).reshape(-1)
    e = _block_onehot_rows(past_len)
    e_rows = n_pg * PAGE_SIZE * N_KV
    rows = hpg * tq

    def pg_spec(p):
        return pl.BlockSpec((None, PAGE_SIZE, 2, N_KV, HEAD_DIM),
                            lambda b, j, pt: (pt[b, j * n_pg + p], 0, 0, 0, 0))

    def win_spec(c):
        return pl.BlockSpec((None, win_len, None, N_KV, HEAD_DIM), lambda b, j, pt: (b, 0, c, 0, 0))

    per_b = lambda shape: pl.BlockSpec((None,) + shape, lambda b, j, pt: (b, 0, 0))
    grid_spec = pltpu.PrefetchScalarGridSpec(
        num_scalar_prefetch=1,
        grid=(Bn, n_steps),
        in_specs=[pl.BlockSpec(memory_space=pltpu.SMEM),
                  per_b((tq, D)), per_b((tq, N_KV * N_CAND)), per_b((tq, D)), per_b((tq, N_KV * 128)),
                  pl.BlockSpec((e_rows, N_CAND), lambda b, j, pt: (j, 0))]
        + [pg_spec(p) for p in range(n_pg)]
        + [per_b((NEW_ROWS_PAD, KV_COLS)), per_b((NEW_ROWS_PAD, KV_COLS)), win_spec(0), win_spec(1)],
        out_specs=per_b((tq, D)),
        scratch_shapes=[pltpu.VMEM((N_KV, rows, 1), F32), pltpu.VMEM((N_KV, rows, 1), F32),
                        pltpu.VMEM((N_KV, rows, HEAD_DIM), F32)],
    )
    return pl.pallas_call(
        functools.partial(_sample_sel_win_kernel, n_pg=n_pg, hpg=hpg, t_real=t_real, n_steps=n_steps,
                          past_len=past_len, win_len=win_len),
        grid_spec=grid_spec,
        out_shape=jax.ShapeDtypeStruct((Bn, tq, D), BF16),
        compiler_params=_cparams(2),
        name="sample_sel_win_attn",
    )(page_table, slopes, q, sel, oc, gates, e, *([cache_slc] * n_pg), new_slc, new_win, cache_win, cache_win)


def sample_nsa(q2d, gates_pad, kv_rows, kcvc, cache_slc, cache_win, page_table):
    _, kv_slc, kv_win = kv_rows
    Bn, T = kv_slc.shape[:2]
    D = q2d.shape[1]
    past_len = page_table.shape[1] * PAGE_SIZE
    n_past_blk = past_len // SEL_BLOCK
    pad_t = lambda a: jnp.pad(a.reshape(Bn, T, -1), ((0, 0), (0, SAMPLE_TQ - T), (0, 0)))
    q = pad_t(q2d)
    gates = pad_t(gates_pad)
    oc, sel = cmp_attention(q, kcvc, gates, tq=SAMPLE_TQ, pos0=past_len, pos_step=0, transposed=False,
                            k_sel=min(TOPK, n_past_blk + 1) - 1, n_cand=n_past_blk)
    pad_rows = lambda a: jnp.pad(a.reshape(Bn, T, KV_COLS), ((0, 0), (0, NEW_ROWS_PAD - T), (0, 0)))
    o = sample_sel_win_attention(q, sel, oc, gates, cache_slc, page_table,
                                 pad_rows(kv_slc), pad_rows(kv_win), cache_win, T)
    return o[:, :T].reshape(Bn * T, D)


def sample_attention_paged(kv_rows, cache_cmp, cache_slc, cache_win, page_table, cmp_pe, cmp_w1, cmp_b1, cmp_w2):
    T = kv_rows[0].shape[1]
    assert T < CMP_STRIDE, "new rows never complete a compression sub-block"
    kcvc = compress_blocks(cmp_lohi(cache_cmp, page_table, cmp_w1), cmp_pe, cmp_w1, cmp_b1, cmp_w2)

    def attend(q2d, gates_pad):
        return sample_nsa(q2d, gates_pad, kv_rows, kcvc, cache_slc, cache_win, page_table)

    return attend


def alibi_slopes(n_heads):
    exps = np.arange(1, n_heads + 1, dtype=np.float32) * np.float32(-8.0 / n_heads)
    return jnp.asarray(np.exp2(exps), dtype=F32).reshape(N_KV, n_heads // N_KV)


def masked_softmax(s, mask):
    s = jnp.where(mask, s, NEG)
    m = jnp.max(s, axis=-1, keepdims=True)
    p = jnp.where(mask, jnp.exp(s - m), 0.0)
    return p / jnp.maximum(jnp.sum(p, axis=-1, keepdims=True), 1e-30)


def s5_core(u, h0, lam_re, lam_im, log_dt, b_re, b_im, c_re, c_im, d_skip):
    Bn, L, D = u.shape
    G = D // SSM_GROUP
    uf = u.reshape(Bn, L, G, SSM_GROUP)
    lam = lax.complex(lam_re, lam_im)
    dt = jnp.exp(log_dt)[:, None]
    a_bar = jnp.exp(lam * dt)
    b_c = lax.complex(b_re, b_im)
    b_bar = ((a_bar - 1.0) / lam)[..., None] * b_c
    bu = lax.complex(jnp.einsum('blgc,gpc->blgp', uf, b_bar.real),
                     jnp.einsum('blgc,gpc->blgp', uf, b_bar.imag))
    h_init = lax.complex(h0[..., 0], h0[..., 1])
    bu = bu.at[:, 0].add(a_bar * h_init)
    a = jnp.broadcast_to(a_bar, bu.shape)

    def combine(e1, e2):
        a1, b1 = e1
        a2, b2 = e2
        return a1 * a2, a2 * b1 + b2

    _, h = lax.associative_scan(combine, (a, bu), axis=1)
    y = (jnp.einsum('blgp,gcp->blgc', h.real, c_re) - jnp.einsum('blgp,gcp->blgc', h.imag, c_im))
    y = y.reshape(Bn, L, D) + d_skip * u
    h_last = jnp.stack([h[:, -1].real, h[:, -1].imag], axis=-1)
    return jax.nn.gelu(y), h_last


def cmp_partials(rows, pe, w1):
    Bn, L = rows.shape[:2]
    sub = rows.reshape(Bn, L // CMP_STRIDE, CMP_STRIDE, 2, N_KV, HEAD_DIM)
    w_lo = w1[:, :CMP_STRIDE]
    w_hi = w1[:, CMP_STRIDE:]
    pe_lo = jnp.einsum('crd,crde->ce', pe[:, :CMP_STRIDE], w_lo)
    pe_hi = jnp.einsum('crd,crde->ce', pe[:, CMP_STRIDE:], w_hi)
    lo = jnp.einsum('bnrckd,crde->bncke', sub, w_lo) + pe_lo[:, None, :]
    hi = jnp.einsum('bnrckd,crde->bncke', sub, w_hi) + pe_hi[:, None, :]
    return lo, hi


def compress(lo, hi, b1, w2):
    h = jax.nn.gelu(lo[:, :-1] + hi[:, 1:] + b1[:, None, :])
    kv = jnp.einsum('bncke,ced->bnckd', h, w2)
    return kv[:, :, 0], kv[:, :, 1]


def nsa_attend(q, gates, pos, kc, vc, c_end, gather_sel, n_sel, kw, vw, w_pos):
    Bn, T = q.shape[:2]
    hpg = q.shape[3]
    slopes = alibi_slopes(N_KV * hpg)
    scale = HEAD_DIM ** -0.5
    dist_c = (pos[:, None] - c_end[None, :]).astype(F32)
    s_c = (jnp.einsum('btghd,bngd->btghn', q, kc) * scale
           - slopes[None, None, :, :, None] * dist_c[None, :, None, None, :])
    p_c = masked_softmax(s_c, (dist_c >= 0)[None, :, None, None, :])
    o_c = jnp.einsum('btghn,bngd->btghd', p_c, vc)
    imp = p_c.sum(axis=3)
    n_cmp = imp.shape[-1]
    imp = jnp.pad(imp, ((0, 0), (0, 0), (0, 0), (1, SUBS_PER_SEL * (n_sel + 1) - 1 - n_cmp)))
    r = imp.reshape(Bn, T, N_KV, n_sel + 1, SUBS_PER_SEL)
    imp_sel = r[..., :n_sel, :].sum(-1) + r[..., 1:, 0]
    blk = pos // SEL_BLOCK
    j = jnp.arange(n_sel)
    forced = (j[None, :] == 0) | (j[None, :] == blk[:, None]) | (j[None, :] == blk[:, None] - 1)
    visible = j[None, :] <= blk[:, None]
    score = jnp.where(visible[None, :, None, :], imp_sel + FORCE_BONUS * forced[None, :, None, :], NEG)
    top_s, idx = lax.top_k(score, min(TOPK, n_sel))
    valid = top_s > 0.5 * NEG
    ks, vs = gather_sel(idx)
    n_k = idx.shape[-1]
    s_pos = idx[..., None] * SEL_BLOCK + jnp.arange(SEL_BLOCK)
    dist_s = (pos[None, :, None, None, None] - s_pos).astype(F32)
    mask_s = valid[..., None] & (dist_s >= 0)
    s_s = (jnp.einsum('btghd,btgksd->btghks', q, ks) * scale
           - slopes[None, None, :, :, None, None] * dist_s[:, :, :, None])
    p_s = masked_softmax(s_s.reshape(Bn, T, N_KV, hpg, n_k * SEL_BLOCK),
                         mask_s[:, :, :, None].reshape(Bn, T, N_KV, 1, n_k * SEL_BLOCK))
    o_s = jnp.einsum('btghm,btgmd->btghd', p_s, vs.reshape(Bn, T, N_KV, n_k * SEL_BLOCK, HEAD_DIM))
    dist_w = pos[:, None] - w_pos[None, :]
    mask_w = (dist_w >= 0) & (dist_w < WINDOW) & (w_pos[None, :] >= 0)
    s_w = (jnp.einsum('btghd,bmgd->btghm', q, kw) * scale
           - slopes[None, None, :, :, None] * dist_w.astype(F32)[None, :, None, None, :])
    p_w = masked_softmax(s_w, mask_w[None, :, None, None, :])
    o_w = jnp.einsum('btghm,bmgd->btghd', p_w, vw)
    return gates[..., 0:1] * o_c + gates[..., 1:2] * o_s + gates[..., 2:3] * o_w


def prompt_attention(kv2d, kv_rows, cmp_pe, cmp_w1, cmp_b1, cmp_w2):
    kv_cmp = kv_rows[0]
    L = kv_cmp.shape[1]
    pages = kv_cmp.reshape(L // PAGE_SIZE, PAGE_SIZE, 2, N_KV, HEAD_DIM)
    table = jnp.arange(L // PAGE_SIZE, dtype=jnp.int32)[None]
    kcvc = compress_blocks(cmp_lohi(pages, table, cmp_w1), cmp_pe, cmp_w1, cmp_b1, cmp_w2)

    def attend(q2d, gates_pad):
        return prompt_nsa(q2d[None], gates_pad[None], kv2d, kcvc)

    return attend


def sample_attention(kv2d, kv_rows, cache_cmp, cache_slc, cache_win, page_table,
                     cmp_pe, cmp_w1, cmp_b1, cmp_w2):
    kv_cmp, kv_slc, kv_win = kv_rows
    Bn, T = kv_cmp.shape[:2]
    n_pages = page_table.shape[1]
    past_len = n_pages * PAGE_SIZE
    past_cmp = cache_cmp[page_table].reshape(Bn, past_len, 2, N_KV, HEAD_DIM)
    lo, hi = cmp_partials(past_cmp, cmp_pe, cmp_w1)
    n_new_sub = T // CMP_STRIDE
    if n_new_sub > 0:
        lo_new, hi_new = cmp_partials(kv_cmp[:, :n_new_sub * CMP_STRIDE], cmp_pe, cmp_w1)
        lo = jnp.concatenate([lo, lo_new], axis=1)
        hi = jnp.concatenate([hi, hi_new], axis=1)
    kc, vc = compress(lo, hi, cmp_b1, cmp_w2)
    c_end = jnp.arange(kc.shape[1]) * CMP_STRIDE + (CMP_BLOCK - 1)
    n_past_blk = past_len // SEL_BLOCK
    n_new_blk = -(-T // SEL_BLOCK)
    n_sel = n_past_blk + n_new_blk
    blk_per_page = PAGE_SIZE // SEL_BLOCK
    pool = cache_slc.reshape(cache_slc.shape[0], blk_per_page, SEL_BLOCK, 2, N_KV, HEAD_DIM)
    new_blocks = jnp.pad(kv_slc, ((0, 0), (0, n_new_blk * SEL_BLOCK - T), (0, 0), (0, 0), (0, 0)))
    new_blocks = new_blocks.reshape(Bn, n_new_blk, SEL_BLOCK, 2, N_KV, HEAD_DIM)
    b_i = jnp.arange(Bn)[:, None, None, None]
    g_i = jnp.arange(N_KV)[None, None, :, None]

    def gather_sel(idx):
        jp = jnp.minimum(idx, n_past_blk - 1)
        page = page_table[b_i, jp // blk_per_page]
        sub = jp % blk_per_page
        jn = jnp.clip(idx - n_past_blk, 0, n_new_blk - 1)
        is_new = (idx >= n_past_blk)[..., None, None]
        k = jnp.where(is_new, new_blocks[b_i, jn, :, 0, g_i, :], pool[page, sub, :, 0, g_i, :])
        v = jnp.where(is_new, new_blocks[b_i, jn, :, 1, g_i, :], pool[page, sub, :, 1, g_i, :])
        return k, v

    w_rows = jnp.concatenate([cache_win, kv_win], axis=1)
    w_pos = past_len - cache_win.shape[1] + jnp.arange(w_rows.shape[1])
    pos = past_len + jnp.arange(T)

    def attend(q2d, gates_pad):
        hpg = q2d.shape[1] // (N_KV * HEAD_DIM)
        q = q2d.reshape(Bn, T, N_KV, hpg, HEAD_DIM)
        gates = gates_pad.reshape(Bn, T, N_KV, 128)[..., :3 * hpg].reshape(Bn, T, N_KV, hpg, 3)
        o = nsa_attend(q, gates, pos, kc, vc, c_end, gather_sel, n_sel,
                       w_rows[:, :, 0], w_rows[:, :, 1], w_pos)
        return o.reshape(Bn * T, -1)

    return attend


def _rows(v, per_tok):
    return v if v.shape[0] == 1 else jnp.repeat(v, per_tok, axis=0)


def trunk(x, mods, kv_mod, h0, make_attend, p):
    Bn, L, D = x.shape
    M = Bn * L
    n_heads = D // HEAD_DIM
    hpg = n_heads // N_KV
    depth = p["mod_w"].shape[0]
    n_a = depth // 2
    xr = x.reshape(M, D)
    new_h = []
    kv_rows = None
    attend = None
    for l in range(depth):
        sh1, sc1, ga1, sh2, sc2, ga2 = [_rows(m, L) for m in jnp.split(mods[l], 6, axis=-1)]
        if l == n_a:
            shift, scale = [_rows(m, L) for m in jnp.split(kv_mod, 2, axis=-1)]
            hk = norm_mod(xr, p["kv_norm"], scale, shift, BF16)
            kv2d = mm_wide(hk, p["w_kv"], tn=512)
            kv = kv2d.reshape(Bn, L, 3, 2, N_KV, HEAD_DIM)
            kv_rows = (kv[:, :, 0], kv[:, :, 1], kv[:, :, 2])
            attend = make_attend(kv2d, kv_rows)
        if l < n_a:
            u = norm_mod(xr, p["norm_pre"][l, 0], sc1, sh1, F32)
            gy, h_last = s5_mixer_core(u.reshape(Bn, L, D), h0[l], p["ssm_lam_re"][l], p["ssm_lam_im"][l],
                                       p["ssm_log_dt"][l], p["ssm_b_re"][l], p["ssm_b_im"][l],
                                       p["ssm_c_re"][l], p["ssm_c_im"][l], p["ssm_d"][l])
            new_h.append(h_last)
            xr = mm_tall(gy, p["ssm_w_glu"], xr, p["norm_post"][l, 0], ga1, layer=l, glu=True)
        else:
            lb = l - n_a
            h = norm_mod(xr, p["norm_pre"][l, 0], sc1, sh1, BF16)
            q = mm_wide(h, p["nsa_w_qg"], layer=lb, n_out=n_heads * HEAD_DIM, tn=512)
            gates_pad = mm_wide(h, p["w_gate_pad"], layer=lb, epilogue="sigmoid", tn=512)
            o = attend(q, gates_pad)
            xr = mm_tall(o, p["nsa_w_o"], xr, p["norm_post"][l, 0], ga1, layer=lb)
        h = norm_mod(xr, p["norm_pre"][l, 1], sc2, sh2, BF16)
        f = mm_wide(h, p["mlp_w1"], layer=l, epilogue="sqrelu", out_dtype=BF16, tn=1024)
        xr = mm_tall(f, p["mlp_w2"], xr, p["norm_post"][l, 1], ga2, layer=l, tk=1024)
    return xr.reshape(Bn, L, D), jnp.stack(new_h), kv_rows


def kernel(x_prompt, x_sample, c_prompt, c_sample, state_ssm, cache_cmp, cache_slc, cache_win, page_table, mod_w, mod_b, norm_pre, norm_post, mlp_w1, mlp_w2, ssm_lam_re, ssm_lam_im, ssm_log_dt, ssm_b_re, ssm_b_im, ssm_c_re, ssm_c_im, ssm_d, ssm_w_glu, kv_norm, kv_mod_w, kv_mod_b, w_kv, cmp_pe, cmp_w1, cmp_b1, cmp_w2, nsa_w_qg, nsa_w_o):
    D = x_prompt.shape[-1]
    depth = mod_w.shape[0]
    n_heads = D // HEAD_DIM
    bp, bs = c_prompt.shape[0], c_sample.shape[0]
    c_all = jnp.concatenate([c_prompt, c_sample], axis=0)
    n_c = c_all.shape[0]
    c_all = jnp.pad(c_all, ((0, -n_c % 8), (0, 0)))
    mods = [mm_wide(c_all, mod_w, layer=l, bias=mod_b, prologue="silu", exact=True, tn=512) for l in range(depth)]
    kv_mod = mm_wide(c_all, kv_mod_w, bias=kv_mod_b, prologue="silu", exact=True, tn=512)
    hpg = n_heads // N_KV
    w_gate = nsa_w_qg[:, :, n_heads * HEAD_DIM:].reshape(nsa_w_qg.shape[0], D, N_KV, 3 * hpg)
    w_gate_pad = jnp.pad(w_gate, ((0, 0), (0, 0), (0, 0), (0, 128 - 3 * hpg))).reshape(-1, D, N_KV * 128)
    p = dict(mod_w=mod_w, norm_pre=norm_pre, norm_post=norm_post, mlp_w1=mlp_w1, mlp_w2=mlp_w2.astype(BF16),
             ssm_lam_re=ssm_lam_re, ssm_lam_im=ssm_lam_im, ssm_log_dt=ssm_log_dt, ssm_b_re=ssm_b_re,
             ssm_b_im=ssm_b_im, ssm_c_re=ssm_c_re, ssm_c_im=ssm_c_im, ssm_d=ssm_d,
             ssm_w_glu=ssm_w_glu.astype(BF16), kv_norm=kv_norm, w_kv=w_kv, nsa_w_qg=nsa_w_qg,
             nsa_w_o=nsa_w_o.astype(BF16), w_gate_pad=w_gate_pad)

    def make_prompt(kv2d, kv_rows):
        return prompt_attention(kv2d, kv_rows, cmp_pe, cmp_w1, cmp_b1, cmp_w2)

    def make_sample(kv2d, kv_rows):
        return sample_attention_paged(kv_rows, cache_cmp, cache_slc, cache_win, page_table,
                                      cmp_pe, cmp_w1, cmp_b1, cmp_w2)

    n_a = depth // 2
    G = D // SSM_GROUP
    h0_prompt = jnp.zeros((n_a, bp, G, STATE_DIM, 2), F32)
    y_prompt, ssm_prompt, rows_prompt = trunk(
        x_prompt, [m[:bp] for m in mods], kv_mod[:bp], h0_prompt, make_prompt, p)
    y_sample, ssm_sample, rows_sample = trunk(
        x_sample, [m[bp:bp + bs] for m in mods], kv_mod[bp:bp + bs], state_ssm, make_sample, p)
    cmp_prompt, slc_prompt, win_rows_prompt = rows_prompt
    cmp_sample, slc_sample, win_sample = rows_sample
    win_prompt = win_rows_prompt[:, -min(WINDOW, x_prompt.shape[1]):]
    return (y_prompt, y_sample, ssm_prompt, ssm_sample, cmp_prompt, cmp_sample,
            slc_prompt, slc_sample, win_prompt, win_sample)
```

```python
import functools
import math

import jax
import jax.numpy as jnp
import numpy as np
from jax import lax
from jax.experimental import pallas as pl
from jax.experimental.pallas import tpu as pltpu

F32 = jnp.float32
BF16 = jnp.bfloat16

SSM_GROUP = 16
STATE_DIM = 64
HEAD_DIM = 128
N_KV = 4
CMP_STRIDE = 16
CMP_BLOCK = 2 * CMP_STRIDE
SEL_BLOCK = 64
SUBS_PER_SEL = SEL_BLOCK // CMP_STRIDE
TOPK = 16
WINDOW = 512
QBLK = 128
PAGE_SIZE = 128
EPS = 1e-6
NEG = -1e30
FORCE_BONUS = 1e4

V7X_VMEM_LIMIT_BYTES = 56 * 1024 * 1024
HIGHEST = lax.Precision.HIGHEST
LOG2E = math.log2(math.e)


def _cparams(n_axes):
    return pltpu.CompilerParams(dimension_semantics=("arbitrary",) * n_axes,
                                vmem_limit_bytes=V7X_VMEM_LIMIT_BYTES)


def _pick(n, pref):
    if n <= pref:
        return n
    t = pref
    while n % t:
        t //= 2
    return t


def _norm_mod_kernel(x_ref, g_ref, sc_ref, sh_ref, o_ref):
    x = x_ref[...]
    r = lax.rsqrt(jnp.mean(x * x, axis=-1, keepdims=True) + EPS)
    y = (x * r) * g_ref[...]
    o_ref[...] = (y * (1.0 + sc_ref[...]) + sh_ref[...]).astype(o_ref.dtype)


def norm_mod(x, g, scale, shift, out_dtype):
    M, D = x.shape
    tm = _pick(M, 512)
    per_row = scale.shape[0] != 1
    mod_spec = pl.BlockSpec((tm, D), lambda i: (i, 0)) if per_row else pl.BlockSpec((1, D), lambda i: (0, 0))
    return pl.pallas_call(
        _norm_mod_kernel,
        grid=(M // tm,),
        in_specs=[pl.BlockSpec((tm, D), lambda i: (i, 0)),
                  pl.BlockSpec((1, D), lambda i: (0, 0)),
                  mod_spec, mod_spec],
        out_specs=pl.BlockSpec((tm, D), lambda i: (i, 0)),
        out_shape=jax.ShapeDtypeStruct((M, D), out_dtype),
        compiler_params=_cparams(1),
        name="norm_mod",
    )(x, g.reshape(1, D), scale, shift)


def _mm_wide_kernel(*refs, prologue, epilogue, has_bias, exact, n_w):
    a_ref = refs[0]
    w_refs = refs[1:1 + n_w]
    pos = 1 + n_w
    b_ref = refs[pos] if has_bias else None
    pos += int(has_bias)
    o_ref = refs[pos]
    wbf_refs = refs[pos + 1:]

    a = a_ref[...]
    if prologue == "silu":
        a = a * jax.nn.sigmoid(a)
    if exact:
        zs = [jnp.dot(a, w[...], preferred_element_type=F32, precision=HIGHEST) for w in w_refs]
    else:
        @pl.when(pl.program_id(1) == 0)
        def _():
            for w, wbf in zip(w_refs, wbf_refs):
                wbf[...] = w[...].astype(BF16)

        a = a.astype(BF16)
        zs = [jnp.dot(a, wbf[...], preferred_element_type=F32) for wbf in wbf_refs]
    z = zs[0]
    if has_bias:
        z = z + b_ref[...]
    if epilogue == "sqrelu":
        z = jnp.square(jnp.maximum(z, 0.0))
    elif epilogue == "sigmoid":
        z = jax.nn.sigmoid(z)
    elif epilogue == "glu":
        z = z * jax.nn.sigmoid(zs[1])
    o_ref[...] = z.astype(o_ref.dtype)


def mm_wide(a, w, *, layer=None, col0=0, n_out=None, bias=None, prologue=None, epilogue=None,
            exact=False, out_dtype=F32, tm=1024, tn=512):
    M, K = a.shape
    n_out = n_out if n_out is not None else w.shape[-1] - col0
    tm = _pick(M, tm)
    tn = _pick(n_out, tn)
    assert col0 % tn == 0 and n_out % tn == 0
    n_w = 2 if epilogue == "glu" else 1
    jb = col0 // tn

    def w_spec(extra):
        if layer is None:
            return pl.BlockSpec((K, tn), lambda j, i: (0, jb + extra + j))
        return pl.BlockSpec((None, K, tn), lambda j, i: (layer, 0, jb + extra + j))

    in_specs = [pl.BlockSpec((tm, K), lambda j, i: (i, 0))] + [w_spec(e * (n_out // tn)) for e in range(n_w)]
    args = [a] + [w] * n_w
    if bias is not None:
        if layer is None:
            in_specs.append(pl.BlockSpec((1, tn), lambda j, i: (0, jb + j)))
            args.append(bias.reshape(1, -1))
        else:
            in_specs.append(pl.BlockSpec((None, 1, tn), lambda j, i: (layer, 0, jb + j)))
            args.append(bias.reshape(bias.shape[0], 1, -1))
    scratch = [] if exact else [pltpu.VMEM((K, tn), BF16) for _ in range(n_w)]
    return pl.pallas_call(
        functools.partial(_mm_wide_kernel, prologue=prologue, epilogue=epilogue,
                          has_bias=bias is not None, exact=exact, n_w=n_w),
        grid=(n_out // tn, M // tm),
        in_specs=in_specs,
        out_specs=pl.BlockSpec((tm, tn), lambda j, i: (i, j)),
        out_shape=jax.ShapeDtypeStruct((M, n_out), out_dtype),
        scratch_shapes=scratch,
        compiler_params=_cparams(2),
        name="mm_wide",
    )(*args)


def _mm_tall_kernel(a_ref, *refs, n_w, nk):
    w_refs = refs[:n_w]
    res_ref, g_ref, ga_ref, o_ref = refs[n_w:n_w + 4]
    acc_refs = refs[n_w + 4:]
    k = pl.program_id(1)

    @pl.when(k == 0)
    def _():
        for acc in acc_refs:
            acc[...] = jnp.zeros_like(acc)

    a = a_ref[...].astype(BF16)
    for w, acc in zip(w_refs, acc_refs):
        acc[...] += jnp.dot(a, w[...].astype(BF16), preferred_element_type=F32)

    @pl.when(k == nk - 1)
    def _():
        m = acc_refs[0][...]
        if n_w == 2:
            m = m * jax.nn.sigmoid(acc_refs[1][...])
        r = lax.rsqrt(jnp.mean(m * m, axis=-1, keepdims=True) + EPS)
        o_ref[...] = res_ref[...] + ga_ref[...] * ((m * r) * g_ref[...])


def mm_tall(a, w, res, g, gate, *, layer=None, glu=False, tm=512, tk=512):
    M, K = a.shape
    N = res.shape[1]
    tm = _pick(M, tm)
    tk = _pick(K, tk)
    nk = K // tk
    n_w = 2 if glu else 1

    def w_spec(e):
        if layer is None:
            return pl.BlockSpec((tk, N), lambda i, k: (k, e))
        return pl.BlockSpec((None, tk, N), lambda i, k: (layer, k, e))

    per_row = gate.shape[0] != 1
    ga_spec = pl.BlockSpec((tm, N), lambda i, k: (i, 0)) if per_row else pl.BlockSpec((1, N), lambda i, k: (0, 0))
    return pl.pallas_call(
        functools.partial(_mm_tall_kernel, n_w=n_w, nk=nk),
        grid=(M // tm, nk),
        in_specs=[pl.BlockSpec((tm, tk), lambda i, k: (i, k))] + [w_spec(e) for e in range(n_w)]
        + [pl.BlockSpec((tm, N), lambda i, k: (i, 0)),
           pl.BlockSpec((1, N), lambda i, k: (0, 0)),
           ga_spec],
        out_specs=pl.BlockSpec((tm, N), lambda i, k: (i, 0)),
        out_shape=jax.ShapeDtypeStruct((M, N), F32),
        scratch_shapes=[pltpu.VMEM((tm, N), F32) for _ in range(n_w)],
        compiler_params=_cparams(2),
        name="mm_tall",
    )(a, *([w] * n_w), res, g.reshape(1, N), gate)


S5_GROUPS_PER_STEP = 128 // SSM_GROUP
P2 = 2 * STATE_DIM


def _s5_prep_kernel(lam_re_ref, lam_im_ref, ldt_ref, btr_ref, bti_ref, cr_ref, ci_ref,
                    ws_ref, wct_ref, wm_ref, at_ref, *, tc, gb):
    ws_ref[...] = jnp.zeros(ws_ref.shape, ws_ref.dtype)
    wct_ref[...] = jnp.zeros(wct_ref.shape, wct_ref.dtype)
    C = SSM_GROUP
    tcc = tc * C
    wk = max(tcc, 128)
    lane = lax.broadcasted_iota(jnp.int32, (C, P2), 1)
    is_re = lane < STATE_DIM
    kk = lax.broadcasted_iota(jnp.int32, (tc + 1, P2), 0).astype(F32)
    lane_k = lax.broadcasted_iota(jnp.int32, (C, wk), 1)
    for gg in range(gb):
        lr = lam_re_ref[gg]
        li = lam_im_ref[gg]
        dt = jnp.exp(ldt_ref[gg])
        mag = jnp.exp(kk * (lr * dt))
        ang = kk * (li * dt)
        pr = mag * jnp.cos(ang)
        pi = mag * jnp.sin(ang)
        x = pr[1:2] - 1.0
        y = pi[1:2]
        den = lr * lr + li * li
        cfr = (x * lr + y * li) / den
        cfi = (y * lr - x * li) / den
        btr = btr_ref[gg]
        bti = bti_ref[gg]
        bbr = cfr * btr - cfi * bti
        bbi = cfr * bti + cfi * btr
        cre = cr_ref[gg]
        cim = ci_ref[gg]

        def bm(k):
            return jnp.where(is_re, pr[k:k + 1] * bbr - pi[k:k + 1] * bbi, pr[k:k + 1] * bbi + pi[k:k + 1] * bbr)

        def cm(k):
            return jnp.where(is_re, cre * pr[k:k + 1] - cim * pi[k:k + 1], -(cre * pi[k:k + 1] + cim * pr[k:k + 1]))

        cms = [cm(k) for k in range(tc + 1)]
        cs0 = jnp.concatenate(cms[:tc] + [jnp.zeros((wk - tcc, P2), F32)] * (wk > tcc), axis=0)
        kst = lax.dot_general(bm(0), cs0, (((1,), (1,)), ((), ())), preferred_element_type=F32,
                              precision=HIGHEST)
        r0 = gg * C
        for s in range(tc):
            ws_ref[s, r0:r0 + C, gg * P2:(gg + 1) * P2] = bm(tc - 1 - s).astype(ws_ref.dtype)
            wct_ref[s, r0:r0 + C, gg * P2:(gg + 1) * P2] = cms[s + 1].astype(wct_ref.dtype)
            shift = (r0 - s * C) % wk
            moved = kst if shift == 0 else pltpu.roll(kst, shift, 1)
            blockdiag = jnp.where((lane_k >= r0) & (lane_k < r0 + C), moved, 0.0)
            wm_ref[s, r0:r0 + C, :] = blockdiag[:, :128].astype(wm_ref.dtype)
        at_ref[gg, 0:1, :] = pr[tc:tc + 1]
        at_ref[gg, 1:2, :] = jnp.where(is_re[0:1], -pi[tc:tc + 1], pi[tc:tc + 1])


def s5_prep(lam_re, lam_im, log_dt, b_re, b_im, c_re, c_im, tc):
    G = lam_re.shape[0]
    gb = _pick(G, S5_GROUPS_PER_STEP)
    C = SSM_GROUP
    tcc = tc * C
    dup = lambda v: jnp.concatenate([v, v], axis=-1)
    lam_re2 = dup(lam_re)[:, None, :]
    lam_im2 = dup(lam_im)[:, None, :]
    ldt2 = jnp.broadcast_to(log_dt[:, None, None], (G, 1, P2))
    btr = dup(jnp.swapaxes(b_re, 1, 2))
    bti = dup(jnp.swapaxes(b_im, 1, 2))
    cr2 = dup(c_re)
    ci2 = dup(c_im)
    vec = pl.BlockSpec((gb, 1, P2), lambda i: (i, 0, 0))
    mat = pl.BlockSpec((gb, C, P2), lambda i: (i, 0, 0))
    return pl.pallas_call(
        functools.partial(_s5_prep_kernel, tc=tc, gb=gb),
        grid=(G // gb,),
        in_specs=[vec, vec, vec, mat, mat, mat, mat],
        out_specs=[pl.BlockSpec((None, tc, 128, gb * P2), lambda i: (i, 0, 0, 0)),
                   pl.BlockSpec((None, tc, 128, gb * P2), lambda i: (i, 0, 0, 0)),
                   pl.BlockSpec((None, tc, 128, 128), lambda i: (i, 0, 0, 0)),
                   pl.BlockSpec((gb, 2, P2), lambda i: (i, 0, 0))],
        out_shape=[jax.ShapeDtypeStruct((G // gb, tc, 128, gb * P2), BF16),
                   jax.ShapeDtypeStruct((G // gb, tc, 128, gb * P2), BF16),
                   jax.ShapeDtypeStruct((G // gb, tc, 128, 128), BF16),
                   jax.ShapeDtypeStruct((G, 2, P2), F32)],
        compiler_params=_cparams(1),
        name="s5_prep",
    )(lam_re2, lam_im2, ldt2, btr, bti, cr2, ci2)


def _s5_sums_kernel(u_ref, ws_ref, s_ref, *, tc):
    acc = None
    for s in range(tc):
        z = jnp.dot(u_ref[:, s, :].astype(BF16), ws_ref[s], preferred_element_type=F32)
        acc = z if acc is None else acc + z
    s_ref[...] = acc


def s5_chunk_sums(u3, ws):
    R, tc, D = u3.shape
    nb = D // 128
    sw = ws.shape[-1]
    return pl.pallas_call(
        functools.partial(_s5_sums_kernel, tc=tc),
        grid=(nb,),
        in_specs=[pl.BlockSpec((R, tc, 128), lambda i: (0, 0, i)),
                  pl.BlockSpec((None, tc, 128, sw), lambda i: (i, 0, 0, 0))],
        out_specs=pl.BlockSpec((R, sw), lambda i: (0, i)),
        out_shape=jax.ShapeDtypeStruct((R, nb * sw), F32),
        compiler_params=_cparams(1),
        name="s5_sums",
    )(u3, ws)


def _s5_out_kernel(u_ref, hin_ref, wm_ref, wct_ref, d_ref, o_ref, *, tc):
    us = [u_ref[:, s, :] for s in range(tc)]
    ub = [x.astype(BF16) for x in us]
    hb = hin_ref[...].astype(BF16)
    d = d_ref[...]
    for t in range(tc):
        y = lax.dot_general(hb, wct_ref[t], (((1,), (1,)), ((), ())), preferred_element_type=F32)
        for s in range(t + 1):
            y = y + jnp.dot(ub[s], wm_ref[t - s], preferred_element_type=F32)
        o_ref[:, t, :] = jax.nn.gelu(y + d * us[t])


def s5_chunk_out(u3, hin, wm, wct, d_skip):
    R, tc, D = u3.shape
    nb = D // 128
    sw = wct.shape[-1]
    return pl.pallas_call(
        functools.partial(_s5_out_kernel, tc=tc),
        grid=(nb,),
        in_specs=[pl.BlockSpec((R, tc, 128), lambda i: (0, 0, i)),
                  pl.BlockSpec((R, sw), lambda i: (0, i)),
                  pl.BlockSpec((None, tc, 128, 128), lambda i: (i, 0, 0, 0)),
                  pl.BlockSpec((None, tc, 128, sw), lambda i: (i, 0, 0, 0)),
                  pl.BlockSpec((1, 128), lambda i: (0, i))],
        out_specs=pl.BlockSpec((R, tc, 128), lambda i: (0, 0, i)),
        out_shape=jax.ShapeDtypeStruct((R, tc, D), F32),
        compiler_params=_cparams(1),
        name="s5_chunk_out",
    )(u3, hin, wm, wct, d_skip.reshape(1, D))


S5_CHUNKS_PER_STEP = 64


def _s5_scan_kernel(s_ref, h0_ref, at_ref, hin_ref, hfin_ref, h_scr, *, cb, n_steps):
    j = pl.program_id(1)

    @pl.when(j == 0)
    def _():
        h_scr[...] = h0_ref[...]

    ar = at_ref[0]
    ai = at_ref[1]

    def step(c, h):
        hin_ref[c] = h
        return ar * h + ai * pltpu.roll(h, STATE_DIM, 1) + s_ref[c]

    h = lax.fori_loop(0, cb, step, h_scr[...])
    h_scr[...] = h

    @pl.when(j == n_steps - 1)
    def _():
        hfin_ref[...] = h


def s5_scan(s, h0, at):
    Bn, n_chunk, G, _ = s.shape
    cb = _pick(n_chunk, S5_CHUNKS_PER_STEP)
    n_steps = n_chunk // cb
    return pl.pallas_call(
        functools.partial(_s5_scan_kernel, cb=cb, n_steps=n_steps),
        grid=(Bn, n_steps),
        in_specs=[pl.BlockSpec((None, cb, G, P2), lambda b, j: (b, j, 0, 0)),
                  pl.BlockSpec((None, G, P2), lambda b, j: (b, 0, 0)),
                  pl.BlockSpec((2, G, P2), lambda b, j: (0, 0, 0))],
        out_specs=[pl.BlockSpec((None, cb, G, P2), lambda b, j: (b, j, 0, 0)),
                   pl.BlockSpec((None, G, P2), lambda b, j: (b, 0, 0))],
        out_shape=[jax.ShapeDtypeStruct((Bn, n_chunk, G, P2), F32),
                   jax.ShapeDtypeStruct((Bn, G, P2), F32)],
        scratch_shapes=[pltpu.VMEM((G, P2), F32)],
        compiler_params=_cparams(2),
        name="s5_scan",
    )(s, h0, at)


def s5_mixer_core(u, h0, lam_re, lam_im, log_dt, b_re, b_im, c_re, c_im, d_skip):
    Bn, L, D = u.shape
    G = D // SSM_GROUP
    tc = _pick(L, 16)
    n_chunk = L // tc
    ws, wct, wm, at = s5_prep(lam_re, lam_im, log_dt, b_re, b_im, c_re, c_im, tc)
    u3 = u.reshape(Bn * n_chunk, tc, D)
    s = s5_chunk_sums(u3, ws).reshape(Bn, n_chunk, G, P2)
    h0v = jnp.concatenate([h0[..., 0], h0[..., 1]], axis=-1)
    hin, hfin = s5_scan(s, h0v, at.transpose(1, 0, 2))
    gy = s5_chunk_out(u3, hin.reshape(Bn * n_chunk, G * P2), wm, wct, d_skip)
    h_last = jnp.stack([hfin[..., :STATE_DIM], hfin[..., STATE_DIM:]], axis=-1)
    return gy.reshape(Bn * L, D), h_last


SUBS_PER_PAGE = PAGE_SIZE // CMP_STRIDE
KV_COLS = 2 * N_KV * HEAD_DIM
CMP_PAGES_PER_STEP = 8


def _cmp_lohi_kernel(pt_ref, *refs, n_pg):
    x_refs = refs[:n_pg]
    w_ref = refs[n_pg]
    o_ref = refs[n_pg + 1]
    rows = SUBS_PER_PAGE * N_KV
    for c in range(2):
        acc = jnp.zeros((n_pg * rows, 2 * HEAD_DIM), F32)
        for rp in range(CMP_STRIDE // 2):
            parts = []
            for p in range(n_pg):
                a0 = x_refs[p][:, 2 * rp, c].reshape(rows, HEAD_DIM)
                a1 = x_refs[p][:, 2 * rp + 1, c].reshape(rows, HEAD_DIM)
                parts.append(jnp.concatenate([a0, a1], axis=1))
            xs = jnp.concatenate(parts, axis=0).astype(BF16)
            acc = acc + jnp.dot(xs, w_ref[c, rp], preferred_element_type=F32)
        o_ref[c] = acc


def cmp_lohi(pages, page_table, cmp_w1):
    Bn, n_pages = page_table.shape
    n_pg = _pick(n_pages, CMP_PAGES_PER_STEP)
    x = pages.reshape(pages.shape[0], SUBS_PER_PAGE, CMP_STRIDE, 2, N_KV, HEAD_DIM)
    half = CMP_STRIDE // 2
    w_lo = cmp_w1[:, :CMP_STRIDE].reshape(2, half, 2 * HEAD_DIM, HEAD_DIM)
    w_hi = cmp_w1[:, CMP_STRIDE:].reshape(2, half, 2 * HEAD_DIM, HEAD_DIM)
    w = jnp.concatenate([w_lo, w_hi], axis=-1).astype(BF16)
    n_sub = n_pages * SUBS_PER_PAGE

    def x_spec(p):
        return pl.BlockSpec((None, SUBS_PER_PAGE, CMP_STRIDE, 2, N_KV, HEAD_DIM),
                            lambda b, i, pt: (pt[b, i * n_pg + p], 0, 0, 0, 0, 0))

    grid_spec = pltpu.PrefetchScalarGridSpec(
        num_scalar_prefetch=1,
        grid=(Bn, n_pages // n_pg),
        in_specs=[x_spec(p) for p in range(n_pg)]
        + [pl.BlockSpec(w.shape, lambda b, i, pt: (0, 0, 0, 0))],
        out_specs=pl.BlockSpec((None, 2, n_pg * SUBS_PER_PAGE * N_KV, 2 * HEAD_DIM),
                               lambda b, i, pt: (b, 0, i, 0)),
    )
    return pl.pallas_call(
        functools.partial(_cmp_lohi_kernel, n_pg=n_pg),
        grid_spec=grid_spec,
        out_shape=jax.ShapeDtypeStruct((Bn, 2, n_sub * N_KV, 2 * HEAD_DIM), F32),
        compiler_params=_cparams(2),
        name="cmp_lohi",
    )(page_table, *([x] * n_pg), w)


def _compress_kernel(x_ref, pe_ref, w1_ref, b1_ref, w2_ref, o_ref, pe_scr):
    n_rows = x_ref.shape[0]
    half = w1_ref.shape[0] // 2
    c = pl.program_id(1)

    @pl.when(pl.program_id(0) == 0)
    def _():
        pe = jnp.broadcast_to(pe_ref[...], (8, 2 * half))
        pe_scr[c, 0] = jnp.dot(pe[:, :half], w1_ref[:half], preferred_element_type=F32, precision=HIGHEST)
        pe_scr[c, 1] = jnp.dot(pe[:, half:], w1_ref[half:], preferred_element_type=F32, precision=HIGHEST)

    x = x_ref[...]
    lo = x[:, :HEAD_DIM] + pe_scr[c, 0][0:1]
    hi = x[:, HEAD_DIM:] + pe_scr[c, 1][0:1]
    hi_next = pltpu.roll(hi, n_rows - N_KV, 0)
    h = jax.nn.gelu(lo + hi_next + b1_ref[...])
    o_ref[...] = jnp.dot(h.astype(BF16), w2_ref[...].astype(BF16), preferred_element_type=F32).astype(o_ref.dtype)


def compress_blocks(lohi, cmp_pe, cmp_w1, cmp_b1, cmp_w2):
    Bn, _, n_rows, _ = lohi.shape
    kdim = CMP_BLOCK * HEAD_DIM
    out = pl.pallas_call(
        _compress_kernel,
        grid=(Bn, 2),
        in_specs=[pl.BlockSpec((None, None, n_rows, 2 * HEAD_DIM), lambda b, c: (b, c, 0, 0)),
                  pl.BlockSpec((None, 1, kdim), lambda b, c: (c, 0, 0)),
                  pl.BlockSpec((None, kdim, HEAD_DIM), lambda b, c: (c, 0, 0)),
                  pl.BlockSpec((None, 1, HEAD_DIM), lambda b, c: (c, 0, 0)),
                  pl.BlockSpec((None, HEAD_DIM, HEAD_DIM), lambda b, c: (c, 0, 0))],
        out_specs=pl.BlockSpec((None, None, n_rows, HEAD_DIM), lambda b, c: (b, c, 0, 0)),
        out_shape=jax.ShapeDtypeStruct((Bn, 2, n_rows, HEAD_DIM), BF16),
        scratch_shapes=[pltpu.VMEM((2, 2, 8, HEAD_DIM), F32)],
        compiler_params=_cparams(2),
        name="compress",
    )(lohi, cmp_pe.reshape(2, 1, kdim), cmp_w1.reshape(2, kdim, HEAD_DIM),
      cmp_b1.reshape(2, 1, HEAD_DIM), cmp_w2)
    return out.reshape(Bn, 2, n_rows // N_KV, N_KV, HEAD_DIM).transpose(0, 1, 3, 2, 4)


N_CAND = 128
NEG_TAKEN = -3e38


def _topk_mask(score, axis, k_sel):
    idx = lax.broadcasted_iota(jnp.int32, score.shape, axis)
    sel = jnp.zeros(score.shape, F32)
    for _ in range(k_sel):
        m = jnp.max(score, axis=axis, keepdims=True)
        first = jnp.min(jnp.where(score == m, idx, N_CAND), axis=axis, keepdims=True)
        hit = idx == first
        sel = jnp.where(hit & (m > 0.5 * NEG), 1.0, sel)
        score = jnp.where(hit, NEG_TAKEN, score)
    return sel


def _cmp_attn_kernel(slope_ref, q_ref, kc_ref, vc_ref, gate_ref, wsel_ref, oc_ref, sel_ref, *,
                     tq, hpg, gps, pos0, pos_step, transposed, k_sel, n_cand):
    i = pl.program_id(1)
    n_cmp = kc_ref.shape[1]
    gw = hpg * HEAD_DIM
    scale = HEAD_DIM ** -0.5
    base = pos0 + i * pos_step
    t_idx = lax.broadcasted_iota(jnp.int32, (tq, n_cmp), 0)
    n_idx = lax.broadcasted_iota(jnp.int32, (tq, n_cmp), 1)
    dist_i = base + t_idx - (n_idx * CMP_STRIDE + (CMP_BLOCK - 1))
    valid = dist_i >= 0
    dist = dist_i.astype(F32)
    gates = gate_ref[...]
    shape, j_ax, t_ax = ((N_CAND, tq), 0, 1) if transposed else ((tq, N_CAND), 1, 0)
    j = lax.broadcasted_iota(jnp.int32, shape, j_ax)
    blk = (base + lax.broadcasted_iota(jnp.int32, shape, t_ax)) // SEL_BLOCK
    forced = (j == 0) | (j == blk) | (j == blk - 1)
    visible = (j <= blk) & (j < n_cand)
    scores = []
    for gi in range(gps):
        g = pl.program_id(2) * gps + gi
        q = q_ref[:, gi * gw:(gi + 1) * gw]
        qs = jnp.concatenate([q[:, h * HEAD_DIM:(h + 1) * HEAD_DIM] for h in range(hpg)], axis=0)
        qs = (qs * scale).astype(BF16)
        s = lax.dot_general(qs, kc_ref[gi], (((1,), (1,)), ((), ())), preferred_element_type=F32)
        vc = vc_ref[gi]
        psum = jnp.zeros((tq, n_cmp), F32)
        for h in range(hpg):
            sh = s[h * tq:(h + 1) * tq] - slope_ref[g * hpg + h] * dist
            sh = jnp.where(valid, sh, NEG)
            m = jnp.max(sh, axis=-1, keepdims=True)
            e = jnp.where(valid, jnp.exp(sh - m), 0.0)
            p = e / jnp.maximum(jnp.sum(e, axis=-1, keepdims=True), 1e-30)
            psum = psum + p
            o_h = jnp.dot(p.astype(BF16), vc, preferred_element_type=F32)
            c0 = gi * gw + h * HEAD_DIM
            oc_ref[:, c0:c0 + HEAD_DIM] = o_h * gates[:, gi * 128 + 3 * h:gi * 128 + 3 * h + 1]
        if transposed:
            imp = lax.dot_general(wsel_ref[...], psum, (((1,), (1,)), ((), ())),
                                  preferred_element_type=F32, precision=HIGHEST)
        else:
            imp = lax.dot_general(psum, wsel_ref[...], (((1,), (1,)), ((), ())),
                                  preferred_element_type=F32, precision=HIGHEST)
        scores.append(jnp.where(visible, imp + jnp.where(forced, FORCE_BONUS, 0.0), NEG))
    if transposed:
        for gi in range(gps):
            sel = _topk_mask(scores[gi], 0, k_sel).T
            sel_ref[:, gi * N_CAND:(gi + 1) * N_CAND] = sel.astype(sel_ref.dtype)
    else:
        sel = _topk_mask(jnp.concatenate(scores, axis=0), 1, k_sel)
        for gi in range(gps):
            sel_ref[:, gi * N_CAND:(gi + 1) * N_CAND] = sel[gi * tq:(gi + 1) * tq].astype(sel_ref.dtype)


def _sel_weights(n_cmp_pad, n_cmp):
    j = np.arange(N_CAND)[:, None]
    n = np.arange(n_cmp_pad)[None, :]
    w = (n >= SUBS_PER_SEL * j - 1) & (n <= SUBS_PER_SEL * j + SUBS_PER_SEL - 1) & (n < n_cmp)
    return jnp.asarray(w.astype(np.float32))


def cmp_attention(q, kcvc, gates, *, tq, pos0, pos_step, transposed, k_sel, n_cand):
    Bn, T, D = q.shape
    n_heads = D // HEAD_DIM
    hpg = n_heads // N_KV
    n_sub = kcvc.shape[3]
    gw = hpg * HEAD_DIM
    slopes = alibi_slopes(n_heads).reshape(-1)
    wsel = _sel_weights(n_sub, n_sub - 1)
    gps = N_KV if tq < 128 else 1
    kern = functools.partial(_cmp_attn_kernel, tq=tq, hpg=hpg, gps=gps, pos0=pos0, pos_step=pos_step,
                             transposed=transposed, k_sel=k_sel, n_cand=n_cand)
    return pl.pallas_call(
        kern,
        grid=(Bn, T // tq, N_KV // gps),
        in_specs=[pl.BlockSpec(memory_space=pltpu.SMEM),
                  pl.BlockSpec((None, tq, gps * gw), lambda b, i, g: (b, i, g)),
                  pl.BlockSpec((None, None, gps, n_sub, HEAD_DIM), lambda b, i, g: (b, 0, g, 0, 0)),
                  pl.BlockSpec((None, None, gps, n_sub, HEAD_DIM), lambda b, i, g: (b, 1, g, 0, 0)),
                  pl.BlockSpec((None, tq, gps * 128), lambda b, i, g: (b, i, g)),
                  pl.BlockSpec((N_CAND, n_sub), lambda b, i, g: (0, 0))],
        out_specs=[pl.BlockSpec((None, tq, gps * gw), lambda b, i, g: (b, i, g)),
                   pl.BlockSpec((None, tq, gps * N_CAND), lambda b, i, g: (b, i, g))],
        out_shape=[jax.ShapeDtypeStruct((Bn, T, D), F32),
                   jax.ShapeDtypeStruct((Bn, T, N_KV * N_CAND), BF16)],
        compiler_params=_cparams(3),
        name="cmp_attn",
    )(slopes, q, kcvc, kcvc, gates, wsel)


SEL_TK = 512
SEL_TQ = 256
MASK_BIG = 1e30


def _sel_win_kernel(slope_ref, q_ref, ks_ref, vs_ref, kw_ref, vw_ref, sel_ref, oc_ref, gate_ref, e_ref,
                    o_ref, ksb, vsb, kwb, vwb, *, tq, hpg):
    g = pl.program_id(0)
    i = pl.program_id(1)
    L = ks_ref.shape[0]
    s0 = i * tq
    scale = HEAD_DIM ** -0.5

    @pl.when(i == 0)
    def _():
        ksb[:, :HEAD_DIM] = ks_ref[...].astype(BF16)
        ksb[:, HEAD_DIM:] = e_ref[...]
        vsb[...] = vs_ref[...].astype(BF16)
        kwb[0:WINDOW] = jnp.zeros((WINDOW, HEAD_DIM), BF16)
        vwb[0:WINDOW] = jnp.zeros((WINDOW, HEAD_DIM), BF16)
        kwb[WINDOW:] = kw_ref[...].astype(BF16)
        vwb[WINDOW:] = vw_ref[...].astype(BF16)

    q = q_ref[...]
    qs = jnp.concatenate([q[:, h * HEAD_DIM:(h + 1) * HEAD_DIM] for h in range(hpg)], axis=0)
    qs = (qs * (scale * LOG2E)).astype(BF16)
    slopes = [slope_ref[g * hpg + h] * LOG2E for h in range(hpg)]
    neg_sel = sel_ref[...] - 1.0
    qaug = jnp.concatenate([qs, jnp.concatenate([neg_sel] * hpg, axis=0)], axis=1)
    n_tiles = s0 // SEL_TK + 1

    def sweep(k0, carry, width, diagonal):
        ms, ls, accs = carry
        k0 = pl.multiple_of(k0, SEL_TK)
        v_t = vsb[pl.ds(k0, width), :]
        if diagonal:
            t_idx = lax.broadcasted_iota(jnp.int32, (tq, width), 0)
            c_idx = lax.broadcasted_iota(jnp.int32, (tq, width), 1)
            causal = s0 + t_idx >= k0 + c_idx
        s = lax.dot_general(qaug, ksb[pl.ds(k0, width), :], (((1,), (1,)), ((), ())),
                            preferred_element_type=F32)
        rel = (k0 - s0 + lax.broadcasted_iota(jnp.int32, (1, width), 1)).astype(F32)
        new_m, new_l, new_acc = [], [], []
        for h in range(hpg):
            sh = s[h * tq:(h + 1) * tq] + slopes[h] * rel
            if diagonal:
                sh = jnp.where(causal, sh, -MASK_BIG)
            m_new = jnp.maximum(ms[h], jnp.max(sh, axis=-1, keepdims=True))
            p = jnp.exp2(sh - m_new)
            alpha = jnp.exp2(ms[h] - m_new)
            new_l.append(alpha * ls[h] + jnp.sum(p, axis=-1, keepdims=True))
            new_acc.append(alpha * accs[h] + jnp.dot(p.astype(BF16), v_t, preferred_element_type=F32))
            new_m.append(m_new)
        return tuple(new_m), tuple(new_l), tuple(new_acc)

    init = ((jnp.full((tq, 1), NEG, F32),) * hpg, (jnp.zeros((tq, 1), F32),) * hpg,
            (jnp.zeros((tq, HEAD_DIM), F32),) * hpg)
    n_pairs = (n_tiles - 1) // 2
    carry = lax.fori_loop(0, n_pairs, lambda i2, c: sweep(i2 * (2 * SEL_TK), c, 2 * SEL_TK, False), init)
    k_last = n_pairs * (2 * SEL_TK)
    ms, ls, accs = lax.cond(n_tiles % 2 == 0,
                            lambda c: sweep(k_last, c, 2 * SEL_TK, True),
                            lambda c: sweep(k_last, c, SEL_TK, True), carry)

    wn = WINDOW + tq
    w0 = pl.multiple_of(s0, tq)
    kw_t = kwb[pl.ds(w0, wn), :]
    vw_t = vwb[pl.ds(w0, wn), :]
    sw = lax.dot_general(qs, kw_t, (((1,), (1,)), ((), ())), preferred_element_type=F32)
    t_idx = lax.broadcasted_iota(jnp.int32, (tq, wn), 0)
    c_idx = lax.broadcasted_iota(jnp.int32, (tq, wn), 1)
    dist_i = t_idx + WINDOW - c_idx
    valid = (dist_i >= 0) & (dist_i < WINDOW) & (c_idx + s0 >= WINDOW)
    dist = dist_i.astype(F32)
    gates = gate_ref[...]
    oc = oc_ref[...]
    for h in range(hpg):
        sh = sw[h * tq:(h + 1) * tq] + jnp.where(valid, -slopes[h] * dist, -MASK_BIG)
        m = jnp.max(sh, axis=-1, keepdims=True)
        e = jnp.exp2(sh - m)
        o_w = (jnp.dot(e.astype(BF16), vw_t, preferred_element_type=F32)
               / jnp.maximum(jnp.sum(e, axis=-1, keepdims=True), 1e-30))
        o_s = accs[h] / jnp.maximum(ls[h], 1e-30)
        out = (oc[:, h * HEAD_DIM:(h + 1) * HEAD_DIM] + gates[:, 3 * h + 1:3 * h + 2] * o_s
               + gates[:, 3 * h + 2:3 * h + 3] * o_w)
        o_ref[:, h * HEAD_DIM:(h + 1) * HEAD_DIM] = out.astype(o_ref.dtype)


def _block_onehot(n_keys):
    key = np.arange(n_keys)[:, None]
    j = np.arange(N_CAND)[None, :]
    return jnp.asarray((key // SEL_BLOCK == j).astype(np.float32) * MASK_BIG, dtype=BF16)


def sel_win_attention(q, kv, sel, oc, gates, *, tq):
    _, L, D = q.shape
    n_heads = D // HEAD_DIM
    hpg = n_heads // N_KV
    gw = hpg * HEAD_DIM
    assert L % SEL_TK == 0 and SEL_TK % tq == 0 and L // SEL_BLOCK <= N_CAND
    slopes = alibi_slopes(n_heads).reshape(-1)
    e = _block_onehot(L)

    def kv_spec(branch, which):
        cb = (branch * 2 + which) * N_KV
        return pl.BlockSpec((L, HEAD_DIM), lambda g, i: (0, cb + g), pipeline_mode=pl.Buffered(1))

    return pl.pallas_call(
        functools.partial(_sel_win_kernel, tq=tq, hpg=hpg),
        grid=(N_KV, L // tq),
        in_specs=[pl.BlockSpec(memory_space=pltpu.SMEM),
                  pl.BlockSpec((None, tq, gw), lambda g, i: (0, i, g)),
                  kv_spec(1, 0), kv_spec(1, 1), kv_spec(2, 0), kv_spec(2, 1),
                  pl.BlockSpec((None, tq, N_CAND), lambda g, i: (0, i, g)),
                  pl.BlockSpec((None, tq, gw), lambda g, i: (0, i, g)),
                  pl.BlockSpec((None, tq, 128), lambda g, i: (0, i, g)),
                  pl.BlockSpec(e.shape, lambda g, i: (0, 0))],
        out_specs=pl.BlockSpec((tq, gw), lambda g, i: (i, g)),
        out_shape=jax.ShapeDtypeStruct((L, D), BF16),
        scratch_shapes=[pltpu.VMEM((L, 2 * HEAD_DIM), BF16), pltpu.VMEM((L, HEAD_DIM), BF16),
                        pltpu.VMEM((WINDOW + L, HEAD_DIM), BF16), pltpu.VMEM((WINDOW + L, HEAD_DIM), BF16)],
        compiler_params=_cparams(2),
        name="sel_win_attn",
    )(slopes, q, kv, kv, kv, kv, sel, oc, gates, e)


def prompt_nsa(q, gates, kv, kcvc):
    L = q.shape[1]
    n_sel = L // SEL_BLOCK
    oc, sel = cmp_attention(q, kcvc, gates, tq=QBLK, pos0=0, pos_step=QBLK, transposed=True,
                            k_sel=min(TOPK, n_sel), n_cand=n_sel)
    return sel_win_attention(q, kv, sel, oc, gates, tq=SEL_TQ)


SAMPLE_TQ = 8
SAMPLE_PAGES_PER_STEP = 8
NEW_ROWS_PAD = 128


def _sample_sel_win_kernel(pt_ref, slope_ref, q_ref, sel_ref, oc_ref, gate_ref, e_ref, *refs,
                           n_pg, hpg, t_real, n_steps, past_len, win_len):
    pages = refs[:n_pg]
    nslc_ref, nwin_ref, cwk_ref, cwv_ref, o_ref, m_scr, l_scr, acc_scr = refs[n_pg:]
    j = pl.program_id(1)
    tq = SAMPLE_TQ
    tk = n_pg * PAGE_SIZE
    scale = HEAD_DIM ** -0.5

    @pl.when(j == 0)
    def _():
        m_scr[...] = jnp.full(m_scr.shape, NEG, F32)
        l_scr[...] = jnp.zeros(l_scr.shape, F32)
        acc_scr[...] = jnp.zeros(acc_scr.shape, F32)

    q = q_ref[...]
    neg_sel = sel_ref[...] - 1.0

    def heads_of(g):
        gw = hpg * HEAD_DIM
        qg = q[:, g * gw:(g + 1) * gw]
        qs = jnp.concatenate([qg[:, h * HEAD_DIM:(h + 1) * HEAD_DIM] for h in range(hpg)], axis=0)
        return (qs * scale).astype(BF16)

    def online(g, h, sh, v):
        r0 = h * tq
        m_old = m_scr[g, r0:r0 + tq]
        m_new = jnp.maximum(m_old, jnp.max(sh, axis=-1, keepdims=True))
        p = jnp.exp(sh - m_new)
        alpha = jnp.exp(m_old - m_new)
        l_scr[g, r0:r0 + tq] = alpha * l_scr[g, r0:r0 + tq] + jnp.sum(p, axis=-1, keepdims=True)
        acc_scr[g, r0:r0 + tq] = alpha * acc_scr[g, r0:r0 + tq] + jnp.dot(p.astype(BF16), v,
                                                                         preferred_element_type=F32)
        m_scr[g, r0:r0 + tq] = m_new

    rows = hpg * tq
    n_rows = N_KV * rows
    n_lane = tk * N_KV
    kf = jnp.concatenate([pg[:, 0].reshape(PAGE_SIZE * N_KV, HEAD_DIM) for pg in pages], axis=0).astype(BF16)
    vf = jnp.concatenate([pg[:, 1].reshape(PAGE_SIZE * N_KV, HEAD_DIM) for pg in pages], axis=0).astype(BF16)
    kaug = jnp.concatenate([kf, e_ref[...]], axis=1)
    qaug = jnp.concatenate(
        [jnp.concatenate([heads_of(g) for g in range(N_KV)], axis=0),
         jnp.concatenate([neg_sel[:, g * N_CAND:(g + 1) * N_CAND] for g in range(N_KV) for _ in range(hpg)],
                         axis=0)], axis=1)
    s = lax.dot_general(qaug, kaug, (((1,), (1,)), ((), ())), preferred_element_type=F32)
    lane = lax.broadcasted_iota(jnp.int32, (1, n_lane), 1)
    rel = (j * tk - past_len + lane // N_KV).astype(F32)
    slope_col = jnp.concatenate([jnp.full((tq, 1), slope_ref[gh], F32) for gh in range(N_KV * hpg)], axis=0)
    row_g = lax.broadcasted_iota(jnp.int32, (n_rows, 1), 0) // rows
    sh = jnp.where(lane % N_KV == row_g, s + slope_col * rel, -MASK_BIG)
    m_old = m_scr[...].reshape(n_rows, 1)
    m_new = jnp.maximum(m_old, jnp.max(sh, axis=-1, keepdims=True))
    p = jnp.exp(sh - m_new)
    alpha = jnp.exp(m_old - m_new)
    l_new = alpha * l_scr[...].reshape(n_rows, 1) + jnp.sum(p, axis=-1, keepdims=True)
    acc_new = alpha * acc_scr[...].reshape(n_rows, HEAD_DIM) + jnp.dot(p.astype(BF16), vf,
                                                                    preferred_element_type=F32)
    m_scr[...] = m_new.reshape(m_scr.shape)
    l_scr[...] = l_new.reshape(l_scr.shape)
    acc_scr[...] = acc_new.reshape(acc_scr.shape)

    @pl.when(j == n_steps - 1)
    def _():
        gates = gate_ref[...]
        oc = oc_ref[...]
        t_n = lax.broadcasted_iota(jnp.int32, (tq, NEW_ROWS_PAD), 0)
        r_n = lax.broadcasted_iota(jnp.int32, (tq, NEW_ROWS_PAD), 1)
        valid_n = (r_n <= t_n) & (r_n < t_real)
        wn = win_len + NEW_ROWS_PAD
        t_w = lax.broadcasted_iota(jnp.int32, (tq, wn), 0)
        c_w = lax.broadcasted_iota(jnp.int32, (tq, wn), 1)
        dist_w = jnp.where(c_w < win_len, t_w + win_len - c_w, t_w - (c_w - win_len))
        cached = c_w < win_len
        valid_w = ((cached & (dist_w < WINDOW) & (c_w + past_len - win_len >= 0))
                   | ((c_w >= win_len) & (dist_w >= 0) & (c_w - win_len < t_real)))
        dist_wf = dist_w.astype(F32)
        for g in range(N_KV):
            kc0 = g * HEAD_DIM
            vc0 = (N_KV + g) * HEAD_DIM
            qs = heads_of(g)
            k_n = nslc_ref[:, kc0:kc0 + HEAD_DIM].astype(BF16)
            v_n = nslc_ref[:, vc0:vc0 + HEAD_DIM].astype(BF16)
            s_n = lax.dot_general(qs, k_n, (((1,), (1,)), ((), ())), preferred_element_type=F32)
            k_w = jnp.concatenate([cwk_ref[:, g, :], nwin_ref[:, kc0:kc0 + HEAD_DIM]], axis=0).astype(BF16)
            v_w = jnp.concatenate([cwv_ref[:, g, :], nwin_ref[:, vc0:vc0 + HEAD_DIM]], axis=0).astype(BF16)
            s_w = lax.dot_general(qs, k_w, (((1,), (1,)), ((), ())), preferred_element_type=F32)
            for h in range(hpg):
                slope = slope_ref[g * hpg + h]
                r0 = h * tq
                sh = jnp.where(valid_n, s_n[r0:r0 + tq] + slope * r_n.astype(F32), -MASK_BIG)
                online(g, h, sh, v_n)
                o_s = acc_scr[g, r0:r0 + tq] / jnp.maximum(l_scr[g, r0:r0 + tq], 1e-30)
                sw = jnp.where(valid_w, s_w[r0:r0 + tq] - slope * dist_wf, NEG)
                m = jnp.max(sw, axis=-1, keepdims=True)
                e = jnp.where(valid_w, jnp.exp(sw - m), 0.0)
                p = e / jnp.maximum(jnp.sum(e, axis=-1, keepdims=True), 1e-30)
                o_w = jnp.dot(p.astype(BF16), v_w, preferred_element_type=F32)
                c0 = (g * hpg + h) * HEAD_DIM
                gc = g * 128 + 3 * h
                out = (oc[:, c0:c0 + HEAD_DIM] + gates[:, gc + 1:gc + 2] * o_s + gates[:, gc + 2:gc + 3] * o_w)
                o_ref[:, c0:c0 + HEAD_DIM] = out.astype(o_ref.dtype)


def _block_onehot_rows(n_keys):
    key = np.repeat(np.arange(n_keys), N_KV)[:, None]
    j = np.arange(N_CAND)[None, :]
    return jnp.asarray((key // SEL_BLOCK == j).astype(np.float32) * MASK_BIG, dtype=BF16)


def sample_sel_win_attention(q, sel, oc, gates, cache_slc, page_table, new_slc, new_win, cache_win, t_real):
    Bn, tq, D = q.shape
    n_heads = D // HEAD_DIM
    hpg = n_heads // N_KV
    n_pages = page_table.shape[1]
    n_pg = _pick(n_pages, SAMPLE_PAGES_PER_STEP)
    n_steps = n_pages // n_pg
    past_len = n_pages * PAGE_SIZE
    win_len = cache_win.shape[1]
    assert past_len // SEL_BLOCK <= N_CAND and past_len % SEL_BLOCK == 0 and t_real <= min(tq, SEL_BLOCK)
    slopes = alibi_slopes(n_heads).reshape(-1)
    e = _block_onehot_rows(past_len)
    e_rows = n_pg * PAGE_SIZE * N_KV
    rows = hpg * tq

    def pg_spec(p):
        return pl.BlockSpec((None, PAGE_SIZE, 2, N_KV, HEAD_DIM),
                            lambda b, j, pt: (pt[b, j * n_pg + p], 0, 0, 0, 0))

    def win_spec(c):
        return pl.BlockSpec((None, win_len, None, N_KV, HEAD_DIM), lambda b, j, pt: (b, 0, c, 0, 0))

    per_b = lambda shape: pl.BlockSpec((None,) + shape, lambda b, j, pt: (b, 0, 0))
    grid_spec = pltpu.PrefetchScalarGridSpec(
        num_scalar_prefetch=1,
        grid=(Bn, n_steps),
        in_specs=[pl.BlockSpec(memory_space=pltpu.SMEM),
                  per_b((tq, D)), per_b((tq, N_KV * N_CAND)), per_b((tq, D)), per_b((tq, N_KV * 128)),
                  pl.BlockSpec((e_rows, N_CAND), lambda b, j, pt: (j, 0))]
        + [pg_spec(p) for p in range(n_pg)]
        + [per_b((NEW_ROWS_PAD, KV_COLS)), per_b((NEW_ROWS_PAD, KV_COLS)), win_spec(0), win_spec(1)],
        out_specs=per_b((tq, D)),
        scratch_shapes=[pltpu.VMEM((N_KV, rows, 1), F32), pltpu.VMEM((N_KV, rows, 1), F32),
                        pltpu.VMEM((N_KV, rows, HEAD_DIM), F32)],
    )
    return pl.pallas_call(
        functools.partial(_sample_sel_win_kernel, n_pg=n_pg, hpg=hpg, t_real=t_real, n_steps=n_steps,
                          past_len=past_len, win_len=win_len),
        grid_spec=grid_spec,
        out_shape=jax.ShapeDtypeStruct((Bn, tq, D), BF16),
        compiler_params=_cparams(2),
        name="sample_sel_win_attn",
    )(page_table, slopes, q, sel, oc, gates, e, *([cache_slc] * n_pg), new_slc, new_win, cache_win, cache_win)


def sample_nsa(q2d, gates_pad, kv_rows, kcvc, cache_slc, cache_win, page_table):
    _, kv_slc, kv_win = kv_rows
    Bn, T = kv_slc.shape[:2]
    D = q2d.shape[1]
    past_len = page_table.shape[1] * PAGE_SIZE
    n_past_blk = past_len // SEL_BLOCK
    pad_t = lambda a: jnp.pad(a.reshape(Bn, T, -1), ((0, 0), (0, SAMPLE_TQ - T), (0, 0)))
    q = pad_t(q2d)
    gates = pad_t(gates_pad)
    oc, sel = cmp_attention(q, kcvc, gates, tq=SAMPLE_TQ, pos0=past_len, pos_step=0, transposed=False,
                            k_sel=min(TOPK, n_past_blk + 1) - 1, n_cand=n_past_blk)
    pad_rows = lambda a: jnp.pad(a.reshape(Bn, T, KV_COLS), ((0, 0), (0, NEW_ROWS_PAD - T), (0, 0)))
    o = sample_sel_win_attention(q, sel, oc, gates, cache_slc, page_table,
                                 pad_rows(kv_slc), pad_rows(kv_win), cache_win, T)
    return o[:, :T].reshape(Bn * T, D)


def sample_attention_paged(kv_rows, cache_cmp, cache_slc, cache_win, page_table, cmp_pe, cmp_w1, cmp_b1, cmp_w2):
    T = kv_rows[0].shape[1]
    assert T < CMP_STRIDE, "new rows never complete a compression sub-block"
    kcvc = compress_blocks(cmp_lohi(cache_cmp, page_table, cmp_w1), cmp_pe, cmp_w1, cmp_b1, cmp_w2)

    def attend(q2d, gates_pad):
        return sample_nsa(q2d, gates_pad, kv_rows, kcvc, cache_slc, cache_win, page_table)

    return attend


def alibi_slopes(n_heads):
    exps = np.arange(1, n_heads + 1, dtype=np.float32) * np.float32(-8.0 / n_heads)
    return jnp.asarray(np.exp2(exps), dtype=F32).reshape(N_KV, n_heads // N_KV)


def prompt_attention(kv2d, kv_rows, cmp_pe, cmp_w1, cmp_b1, cmp_w2):
    kv_cmp = kv_rows[0]
    L = kv_cmp.shape[1]
    pages = kv_cmp.reshape(L // PAGE_SIZE, PAGE_SIZE, 2, N_KV, HEAD_DIM)
    table = jnp.arange(L // PAGE_SIZE, dtype=jnp.int32)[None]
    kcvc = compress_blocks(cmp_lohi(pages, table, cmp_w1), cmp_pe, cmp_w1, cmp_b1, cmp_w2)

    def attend(q2d, gates_pad):
        return prompt_nsa(q2d[None], gates_pad[None], kv2d, kcvc)

    return attend


def _rows(v, per_tok):
    return v if v.shape[0] == 1 else jnp.repeat(v, per_tok, axis=0)


def trunk(x, mods, kv_mod, h0, make_attend, p):
    Bn, L, D = x.shape
    M = Bn * L
    n_heads = D // HEAD_DIM
    hpg = n_heads // N_KV
    depth = p["mod_w"].shape[0]
    n_a = depth // 2
    xr = x.reshape(M, D)
    new_h = []
    kv_rows = None
    attend = None
    for l in range(depth):
        sh1, sc1, ga1, sh2, sc2, ga2 = [_rows(m, L) for m in jnp.split(mods[l], 6, axis=-1)]
        if l == n_a:
            shift, scale = [_rows(m, L) for m in jnp.split(kv_mod, 2, axis=-1)]
            hk = norm_mod(xr, p["kv_norm"], scale, shift, BF16)
            kv2d = mm_wide(hk, p["w_kv"], tn=512)
            kv = kv2d.reshape(Bn, L, 3, 2, N_KV, HEAD_DIM)
            kv_rows = (kv[:, :, 0], kv[:, :, 1], kv[:, :, 2])
            attend = make_attend(kv2d, kv_rows)
        if l < n_a:
            u = norm_mod(xr, p["norm_pre"][l, 0], sc1, sh1, F32)
            gy, h_last = s5_mixer_core(u.reshape(Bn, L, D), h0[l], p["ssm_lam_re"][l], p["ssm_lam_im"][l],
                                       p["ssm_log_dt"][l], p["ssm_b_re"][l], p["ssm_b_im"][l],
                                       p["ssm_c_re"][l], p["ssm_c_im"][l], p["ssm_d"][l])
            new_h.append(h_last)
            xr = mm_tall(gy, p["ssm_w_glu"], xr, p["norm_post"][l, 0], ga1, layer=l, glu=True)
        else:
            lb = l - n_a
            h = norm_mod(xr, p["norm_pre"][l, 0], sc1, sh1, BF16)
            q = mm_wide(h, p["nsa_w_qg"], layer=lb, n_out=n_heads * HEAD_DIM, tn=512)
            gates_pad = mm_wide(h, p["w_gate_pad"], layer=lb, epilogue="sigmoid", tn=512)
            o = attend(q, gates_pad)
            xr = mm_tall(o, p["nsa_w_o"], xr, p["norm_post"][l, 0], ga1, layer=lb)
        h = norm_mod(xr, p["norm_pre"][l, 1], sc2, sh2, BF16)
        f = mm_wide(h, p["mlp_w1"], layer=l, epilogue="sqrelu", out_dtype=BF16, tn=1024)
        xr = mm_tall(f, p["mlp_w2"], xr, p["norm_post"][l, 1], ga2, layer=l, tk=1024)
    return xr.reshape(Bn, L, D), jnp.stack(new_h), kv_rows


def kernel(x_prompt, x_sample, c_prompt, c_sample, state_ssm, cache_cmp, cache_slc, cache_win, page_table, mod_w, mod_b, norm_pre, norm_post, mlp_w1, mlp_w2, ssm_lam_re, ssm_lam_im, ssm_log_dt, ssm_b_re, ssm_b_im, ssm_c_re, ssm_c_im, ssm_d, ssm_w_glu, kv_norm, kv_mod_w, kv_mod_b, w_kv, cmp_pe, cmp_w1, cmp_b1, cmp_w2, nsa_w_qg, nsa_w_o):
    D = x_prompt.shape[-1]
    depth = mod_w.shape[0]
    n_heads = D // HEAD_DIM
    bp, bs = c_prompt.shape[0], c_sample.shape[0]
    c_all = jnp.concatenate([c_prompt, c_sample], axis=0)
    n_c = c_all.shape[0]
    c_all = jnp.pad(c_all, ((0, -n_c % 8), (0, 0)))
    mods = [mm_wide(c_all, mod_w, layer=l, bias=mod_b, prologue="silu", exact=True, tn=512) for l in range(depth)]
    kv_mod = mm_wide(c_all, kv_mod_w, bias=kv_mod_b, prologue="silu", exact=True, tn=512)
    hpg = n_heads // N_KV
    w_gate = nsa_w_qg[:, :, n_heads * HEAD_DIM:].reshape(nsa_w_qg.shape[0], D, N_KV, 3 * hpg)
    w_gate_pad = jnp.pad(w_gate, ((0, 0), (0, 0), (0, 0), (0, 128 - 3 * hpg))).reshape(-1, D, N_KV * 128)
    p = dict(mod_w=mod_w, norm_pre=norm_pre, norm_post=norm_post, mlp_w1=mlp_w1, mlp_w2=mlp_w2.astype(BF16),
             ssm_lam_re=ssm_lam_re, ssm_lam_im=ssm_lam_im, ssm_log_dt=ssm_log_dt, ssm_b_re=ssm_b_re,
             ssm_b_im=ssm_b_im, ssm_c_re=ssm_c_re, ssm_c_im=ssm_c_im, ssm_d=ssm_d,
             ssm_w_glu=ssm_w_glu.astype(BF16), kv_norm=kv_norm, w_kv=w_kv, nsa_w_qg=nsa_w_qg,
             nsa_w_o=nsa_w_o.astype(BF16), w_gate_pad=w_gate_pad)

    def make_prompt(kv2d, kv_rows):
        return prompt_attention(kv2d, kv_rows, cmp_pe, cmp_w1, cmp_b1, cmp_w2)

    def make_sample(kv2d, kv_rows):
        return sample_attention_paged(kv_rows, cache_cmp, cache_slc, cache_win, page_table,
                                      cmp_pe, cmp_w1, cmp_b1, cmp_w2)

    n_a = depth // 2
    G = D // SSM_GROUP
    h0_prompt = jnp.zeros((n_a, bp, G, STATE_DIM, 2), F32)
    y_prompt, ssm_prompt, rows_prompt = trunk(
        x_prompt, [m[:bp] for m in mods], kv_mod[:bp], h0_prompt, make_prompt, p)
    y_sample, ssm_sample, rows_sample = trunk(
        x_sample, [m[bp:bp + bs] for m in mods], kv_mod[bp:bp + bs], state_ssm, make_sample, p)
    cmp_prompt, slc_prompt, win_rows_prompt = rows_prompt
    cmp_sample, slc_sample, win_sample = rows_sample
    win_prompt = win_rows_prompt[:, -min(WINDOW, x_prompt.shape[1]):]
    return (y_prompt, y_sample, ssm_prompt, ssm_sample, cmp_prompt, cmp_sample,
            slc_prompt, slc_sample, win_prompt, win_sample)
```

```python
import functools
import math

import jax
import jax.numpy as jnp
import numpy as np
from jax import lax
from jax.experimental import pallas as pl
from jax.experimental.pallas import tpu as pltpu

F32 = jnp.float32
BF16 = jnp.bfloat16

SSM_GROUP = 16
STATE_DIM = 64
HEAD_DIM = 128
N_KV = 4
CMP_STRIDE = 16
CMP_BLOCK = 2 * CMP_STRIDE
SEL_BLOCK = 64
SUBS_PER_SEL = SEL_BLOCK // CMP_STRIDE
TOPK = 16
WINDOW = 512
QBLK = 128
PAGE_SIZE = 128
EPS = 1e-6
NEG = -1e30
FORCE_BONUS = 1e4

V7X_VMEM_LIMIT_BYTES = 56 * 1024 * 1024
HIGHEST = lax.Precision.HIGHEST
LOG2E = math.log2(math.e)


def _cparams(n_axes):
    return pltpu.CompilerParams(dimension_semantics=("arbitrary",) * n_axes,
                                vmem_limit_bytes=V7X_VMEM_LIMIT_BYTES)


def _pick(n, pref):
    if n <= pref:
        return n
    t = pref
    while n % t:
        t //= 2
    return t


def _norm_mod_kernel(x_ref, g_ref, sc_ref, sh_ref, o_ref):
    x = x_ref[...]
    r = lax.rsqrt(jnp.mean(x * x, axis=-1, keepdims=True) + EPS)
    y = (x * r) * g_ref[...]
    o_ref[...] = (y * (1.0 + sc_ref[...]) + sh_ref[...]).astype(o_ref.dtype)


def norm_mod(x, g, scale, shift, out_dtype):
    M, D = x.shape
    tm = _pick(M, 512)
    per_row = scale.shape[0] != 1
    mod_spec = pl.BlockSpec((tm, D), lambda i: (i, 0)) if per_row else pl.BlockSpec((1, D), lambda i: (0, 0))
    return pl.pallas_call(
        _norm_mod_kernel,
        grid=(M // tm,),
        in_specs=[pl.BlockSpec((tm, D), lambda i: (i, 0)),
                  pl.BlockSpec((1, D), lambda i: (0, 0)),
                  mod_spec, mod_spec],
        out_specs=pl.BlockSpec((tm, D), lambda i: (i, 0)),
        out_shape=jax.ShapeDtypeStruct((M, D), out_dtype),
        compiler_params=_cparams(1),
        name="norm_mod",
    )(x, g.reshape(1, D), scale, shift)


def _mm_wide_kernel(*refs, prologue, epilogue, has_bias, exact, n_w):
    a_ref = refs[0]
    w_refs = refs[1:1 + n_w]
    pos = 1 + n_w
    b_ref = refs[pos] if has_bias else None
    pos += int(has_bias)
    o_ref = refs[pos]
    wbf_refs = refs[pos + 1:]

    a = a_ref[...]
    if prologue == "silu":
        a = a * jax.nn.sigmoid(a)
    if exact:
        zs = [jnp.dot(a, w[...], preferred_element_type=F32, precision=HIGHEST) for w in w_refs]
    else:
        @pl.when(pl.program_id(1) == 0)
        def _():
            for w, wbf in zip(w_refs, wbf_refs):
                wbf[...] = w[...].astype(BF16)

        a = a.astype(BF16)
        zs = [jnp.dot(a, wbf[...], preferred_element_type=F32) for wbf in wbf_refs]
    z = zs[0]
    if has_bias:
        z = z + b_ref[...]
    if epilogue == "sqrelu":
        z = jnp.square(jnp.maximum(z, 0.0))
    elif epilogue == "sigmoid":
        z = jax.nn.sigmoid(z)
    elif epilogue == "glu":
        z = z * jax.nn.sigmoid(zs[1])
    o_ref[...] = z.astype(o_ref.dtype)


def mm_wide(a, w, *, layer=None, col0=0, n_out=None, bias=None, prologue=None, epilogue=None,
            exact=False, out_dtype=F32, tm=1024, tn=512):
    M, K = a.shape
    n_out = n_out if n_out is not None else w.shape[-1] - col0
    tm = _pick(M, tm)
    tn = _pick(n_out, tn)
    assert col0 % tn == 0 and n_out % tn == 0
    n_w = 2 if epilogue == "glu" else 1
    jb = col0 // tn

    def w_spec(extra):
        if layer is None:
            return pl.BlockSpec((K, tn), lambda j, i: (0, jb + extra + j))
        return pl.BlockSpec((None, K, tn), lambda j, i: (layer, 0, jb + extra + j))

    in_specs = [pl.BlockSpec((tm, K), lambda j, i: (i, 0))] + [w_spec(e * (n_out // tn)) for e in range(n_w)]
    args = [a] + [w] * n_w
    if bias is not None:
        if layer is None:
            in_specs.append(pl.BlockSpec((1, tn), lambda j, i: (0, jb + j)))
            args.append(bias.reshape(1, -1))
        else:
            in_specs.append(pl.BlockSpec((None, 1, tn), lambda j, i: (layer, 0, jb + j)))
            args.append(bias.reshape(bias.shape[0], 1, -1))
    scratch = [] if exact else [pltpu.VMEM((K, tn), BF16) for _ in range(n_w)]
    return pl.pallas_call(
        functools.partial(_mm_wide_kernel, prologue=prologue, epilogue=epilogue,
                          has_bias=bias is not None, exact=exact, n_w=n_w),
        grid=(n_out // tn, M // tm),
        in_specs=in_specs,
        out_specs=pl.BlockSpec((tm, tn), lambda j, i: (i, j)),
        out_shape=jax.ShapeDtypeStruct((M, n_out), out_dtype),
        scratch_shapes=scratch,
        compiler_params=_cparams(2),
        name="mm_wide",
    )(*args)


def _mm_tall_kernel(a_ref, *refs, n_w, nk, n_next):
    w_refs = refs[:n_w]
    res_ref, g_ref, ga_ref = refs[n_w:n_w + 3]
    nxt_in = refs[n_w + 3:n_w + 3 + 3 * n_next]
    o_ref = refs[n_w + 3 + 3 * n_next]
    nxt_out = refs[n_w + 4 + 3 * n_next:n_w + 4 + 4 * n_next]
    acc_refs = refs[n_w + 4 + 4 * n_next:]
    k = pl.program_id(1)

    @pl.when(k == 0)
    def _():
        for acc in acc_refs:
            acc[...] = jnp.zeros_like(acc)

    a = a_ref[...].astype(BF16)
    for w, acc in zip(w_refs, acc_refs):
        acc[...] += jnp.dot(a, w[...].astype(BF16), preferred_element_type=F32)

    @pl.when(k == nk - 1)
    def _():
        m = acc_refs[0][...]
        if n_w == 2:
            m = m * jax.nn.sigmoid(acc_refs[1][...])
        r = lax.rsqrt(jnp.mean(m * m, axis=-1, keepdims=True) + EPS)
        x = res_ref[...] + ga_ref[...] * ((m * r) * g_ref[...])
        o_ref[...] = x
        if n_next:
            xn = x * lax.rsqrt(jnp.mean(x * x, axis=-1, keepdims=True) + EPS)
            for j in range(n_next):
                gn, sc, sh = nxt_in[3 * j:3 * j + 3]
                nxt_out[j][...] = ((xn * gn[...]) * (1.0 + sc[...]) + sh[...]).astype(nxt_out[j].dtype)


def mm_tall(a, w, res, g, gate, *, layer=None, glu=False, tm=512, tk=512, next_norms=()):
    M, K = a.shape
    N = res.shape[1]
    tm = _pick(M, tm)
    tk = _pick(K, tk)
    nk = K // tk
    n_w = 2 if glu else 1

    w_mode = dict(pipeline_mode=pl.Buffered(1)) if nk == 1 else {}

    def w_spec(e):
        if layer is None:
            return pl.BlockSpec((tk, N), lambda i, k: (k, e), **w_mode)
        return pl.BlockSpec((None, tk, N), lambda i, k: (layer, k, e), **w_mode)

    per_row = gate.shape[0] != 1
    row_spec = pl.BlockSpec((tm, N), lambda i, k: (i, 0))
    vec_spec = pl.BlockSpec((1, N), lambda i, k: (0, 0))
    mod_spec = row_spec if per_row else vec_spec
    n_next = len(next_norms)
    nxt_args, nxt_specs = [], []
    for gn, sc, sh, _ in next_norms:
        nxt_args += [gn.reshape(1, N), sc, sh]
        nxt_specs += [vec_spec, mod_spec, mod_spec]
    outs = pl.pallas_call(
        functools.partial(_mm_tall_kernel, n_w=n_w, nk=nk, n_next=n_next),
        grid=(M // tm, nk),
        in_specs=[pl.BlockSpec((tm, tk), lambda i, k: (i, k))] + [w_spec(e) for e in range(n_w)]
        + [row_spec, vec_spec, mod_spec] + nxt_specs,
        out_specs=[row_spec] * (1 + n_next),
        out_shape=[jax.ShapeDtypeStruct((M, N), F32)]
        + [jax.ShapeDtypeStruct((M, N), dt) for _, _, _, dt in next_norms],
        scratch_shapes=[pltpu.VMEM((tm, N), F32) for _ in range(n_w)],
        compiler_params=_cparams(2),
        name="mm_tall",
    )(a, *([w] * n_w), res, g.reshape(1, N), gate, *nxt_args)
    return outs if n_next else outs[0]


S5_GROUPS_PER_STEP = 128 // SSM_GROUP
P2 = 2 * STATE_DIM


def _s5_prep_kernel(lam_re_ref, lam_im_ref, ldt_ref, btr_ref, bti_ref, cr_ref, ci_ref,
                    ws_ref, wct_ref, wm_ref, at_ref, *, tc, gb):
    ws_ref[...] = jnp.zeros(ws_ref.shape, ws_ref.dtype)
    wct_ref[...] = jnp.zeros(wct_ref.shape, wct_ref.dtype)
    C = SSM_GROUP
    tcc = tc * C
    wk = max(tcc, 128)
    lane = lax.broadcasted_iota(jnp.int32, (C, P2), 1)
    is_re = lane < STATE_DIM
    kk = lax.broadcasted_iota(jnp.int32, (tc + 1, P2), 0).astype(F32)
    lane_k = lax.broadcasted_iota(jnp.int32, (C, wk), 1)
    for gg in range(gb):
        lr = lam_re_ref[gg]
        li = lam_im_ref[gg]
        dt = jnp.exp(ldt_ref[gg])
        mag = jnp.exp(kk * (lr * dt))
        ang = kk * (li * dt)
        pr = mag * jnp.cos(ang)
        pi = mag * jnp.sin(ang)
        x = pr[1:2] - 1.0
        y = pi[1:2]
        den = lr * lr + li * li
        cfr = (x * lr + y * li) / den
        cfi = (y * lr - x * li) / den
        btr = btr_ref[gg]
        bti = bti_ref[gg]
        bbr = cfr * btr - cfi * bti
        bbi = cfr * bti + cfi * btr
        cre = cr_ref[gg]
        cim = ci_ref[gg]

        def bm(k):
            return jnp.where(is_re, pr[k:k + 1] * bbr - pi[k:k + 1] * bbi, pr[k:k + 1] * bbi + pi[k:k + 1] * bbr)

        def cm(k):
            return jnp.where(is_re, cre * pr[k:k + 1] - cim * pi[k:k + 1], -(cre * pi[k:k + 1] + cim * pr[k:k + 1]))

        cms = [cm(k) for k in range(tc + 1)]
        cs0 = jnp.concatenate(cms[:tc] + [jnp.zeros((wk - tcc, P2), F32)] * (wk > tcc), axis=0)
        kst = lax.dot_general(bm(0), cs0, (((1,), (1,)), ((), ())), preferred_element_type=F32,
                              precision=HIGHEST)
        r0 = gg * C
        for s in range(tc):
            ws_ref[s, r0:r0 + C, gg * P2:(gg + 1) * P2] = bm(tc - 1 - s).astype(ws_ref.dtype)
            wct_ref[s, r0:r0 + C, gg * P2:(gg + 1) * P2] = cms[s + 1].astype(wct_ref.dtype)
            shift = (r0 - s * C) % wk
            moved = kst if shift == 0 else pltpu.roll(kst, shift, 1)
            blockdiag = jnp.where((lane_k >= r0) & (lane_k < r0 + C), moved, 0.0)
            wm_ref[s, r0:r0 + C, :] = blockdiag[:, :128].astype(wm_ref.dtype)
        at_ref[gg, 0:1, :] = pr[tc:tc + 1]
        at_ref[gg, 1:2, :] = jnp.where(is_re[0:1], -pi[tc:tc + 1], pi[tc:tc + 1])


def s5_prep(lam_re, lam_im, log_dt, b_re, b_im, c_re, c_im, tc):
    G = lam_re.shape[0]
    gb = _pick(G, S5_GROUPS_PER_STEP)
    C = SSM_GROUP
    tcc = tc * C
    dup = lambda v: jnp.concatenate([v, v], axis=-1)
    lam_re2 = dup(lam_re)[:, None, :]
    lam_im2 = dup(lam_im)[:, None, :]
    ldt2 = jnp.broadcast_to(log_dt[:, None, None], (G, 1, P2))
    btr = dup(jnp.swapaxes(b_re, 1, 2))
    bti = dup(jnp.swapaxes(b_im, 1, 2))
    cr2 = dup(c_re)
    ci2 = dup(c_im)
    vec = pl.BlockSpec((gb, 1, P2), lambda i: (i, 0, 0))
    mat = pl.BlockSpec((gb, C, P2), lambda i: (i, 0, 0))
    return pl.pallas_call(
        functools.partial(_s5_prep_kernel, tc=tc, gb=gb),
        grid=(G // gb,),
        in_specs=[vec, vec, vec, mat, mat, mat, mat],
        out_specs=[pl.BlockSpec((None, tc, 128, gb * P2), lambda i: (i, 0, 0, 0)),
                   pl.BlockSpec((None, tc, 128, gb * P2), lambda i: (i, 0, 0, 0)),
                   pl.BlockSpec((None, tc, 128, 128), lambda i: (i, 0, 0, 0)),
                   pl.BlockSpec((gb, 2, P2), lambda i: (i, 0, 0))],
        out_shape=[jax.ShapeDtypeStruct((G // gb, tc, 128, gb * P2), BF16),
                   jax.ShapeDtypeStruct((G // gb, tc, 128, gb * P2), BF16),
                   jax.ShapeDtypeStruct((G // gb, tc, 128, 128), BF16),
                   jax.ShapeDtypeStruct((G, 2, P2), F32)],
        compiler_params=_cparams(1),
        name="s5_prep",
    )(lam_re2, lam_im2, ldt2, btr, bti, cr2, ci2)


def _s5_sums_kernel(u_ref, ws_ref, s_ref, *, tc):
    acc = None
    for s in range(tc):
        z = jnp.dot(u_ref[:, s, :].astype(BF16), ws_ref[s], preferred_element_type=F32)
        acc = z if acc is None else acc + z
    s_ref[...] = acc


def s5_chunk_sums(u3, ws):
    R, tc, D = u3.shape
    nb = D // 128
    sw = ws.shape[-1]
    return pl.pallas_call(
        functools.partial(_s5_sums_kernel, tc=tc),
        grid=(nb,),
        in_specs=[pl.BlockSpec((R, tc, 128), lambda i: (0, 0, i)),
                  pl.BlockSpec((None, tc, 128, sw), lambda i: (i, 0, 0, 0))],
        out_specs=pl.BlockSpec((R, sw), lambda i: (0, i)),
        out_shape=jax.ShapeDtypeStruct((R, nb * sw), F32),
        compiler_params=_cparams(1),
        name="s5_sums",
    )(u3, ws)


def _s5_out_kernel(u_ref, hin_ref, wm_ref, wct_ref, d_ref, o_ref, *, tc):
    us = [u_ref[:, s, :] for s in range(tc)]
    ub = [x.astype(BF16) for x in us]
    hb = hin_ref[...].astype(BF16)
    d = d_ref[...]
    for t in range(tc):
        y = lax.dot_general(hb, wct_ref[t], (((1,), (1,)), ((), ())), preferred_element_type=F32)
        for s in range(t + 1):
            y = y + jnp.dot(ub[s], wm_ref[t - s], preferred_element_type=F32)
        o_ref[:, t, :] = jax.nn.gelu(y + d * us[t])


def s5_chunk_out(u3, hin, wm, wct, d_skip):
    R, tc, D = u3.shape
    nb = D // 128
    sw = wct.shape[-1]
    return pl.pallas_call(
        functools.partial(_s5_out_kernel, tc=tc),
        grid=(nb,),
        in_specs=[pl.BlockSpec((R, tc, 128), lambda i: (0, 0, i)),
                  pl.BlockSpec((R, sw), lambda i: (0, i)),
                  pl.BlockSpec((None, tc, 128, 128), lambda i: (i, 0, 0, 0)),
                  pl.BlockSpec((None, tc, 128, sw), lambda i: (i, 0, 0, 0)),
                  pl.BlockSpec((1, 128), lambda i: (0, i))],
        out_specs=pl.BlockSpec((R, tc, 128), lambda i: (0, 0, i)),
        out_shape=jax.ShapeDtypeStruct((R, tc, D), F32),
        compiler_params=_cparams(1),
        name="s5_chunk_out",
    )(u3, hin, wm, wct, d_skip.reshape(1, D))


S5_CHUNKS_PER_STEP = 64


def _s5_scan_kernel(s_ref, h0_ref, at_ref, hin_ref, hfin_ref, h_scr, *, cb, n_steps):
    j = pl.program_id(1)

    @pl.when(j == 0)
    def _():
        h_scr[...] = h0_ref[...]

    ar = at_ref[0]
    ai = at_ref[1]

    def step(c, h):
        hin_ref[c] = h
        return ar * h + ai * pltpu.roll(h, STATE_DIM, 1) + s_ref[c]

    h = lax.fori_loop(0, cb, step, h_scr[...])
    h_scr[...] = h

    @pl.when(j == n_steps - 1)
    def _():
        hfin_ref[...] = h


def s5_scan(s, h0, at):
    Bn, n_chunk, G, _ = s.shape
    cb = _pick(n_chunk, S5_CHUNKS_PER_STEP)
    n_steps = n_chunk // cb
    return pl.pallas_call(
        functools.partial(_s5_scan_kernel, cb=cb, n_steps=n_steps),
        grid=(Bn, n_steps),
        in_specs=[pl.BlockSpec((None, cb, G, P2), lambda b, j: (b, j, 0, 0)),
                  pl.BlockSpec((None, G, P2), lambda b, j: (b, 0, 0)),
                  pl.BlockSpec((2, G, P2), lambda b, j: (0, 0, 0))],
        out_specs=[pl.BlockSpec((None, cb, G, P2), lambda b, j: (b, j, 0, 0)),
                   pl.BlockSpec((None, G, P2), lambda b, j: (b, 0, 0))],
        out_shape=[jax.ShapeDtypeStruct((Bn, n_chunk, G, P2), F32),
                   jax.ShapeDtypeStruct((Bn, G, P2), F32)],
        scratch_shapes=[pltpu.VMEM((G, P2), F32)],
        compiler_params=_cparams(2),
        name="s5_scan",
    )(s, h0, at)


def s5_mixer_core(u, h0, lam_re, lam_im, log_dt, b_re, b_im, c_re, c_im, d_skip):
    Bn, L, D = u.shape
    G = D // SSM_GROUP
    tc = _pick(L, 16)
    n_chunk = L // tc
    ws, wct, wm, at = s5_prep(lam_re, lam_im, log_dt, b_re, b_im, c_re, c_im, tc)
    u3 = u.reshape(Bn * n_chunk, tc, D)
    s = s5_chunk_sums(u3, ws).reshape(Bn, n_chunk, G, P2)
    h0v = jnp.concatenate([h0[..., 0], h0[..., 1]], axis=-1)
    hin, hfin = s5_scan(s, h0v, at.transpose(1, 0, 2))
    gy = s5_chunk_out(u3, hin.reshape(Bn * n_chunk, G * P2), wm, wct, d_skip)
    h_last = jnp.stack([hfin[..., :STATE_DIM], hfin[..., STATE_DIM:]], axis=-1)
    return gy.reshape(Bn * L, D), h_last


SUBS_PER_PAGE = PAGE_SIZE // CMP_STRIDE
KV_COLS = 2 * N_KV * HEAD_DIM
CMP_PAGES_PER_STEP = 8


def _cmp_lohi_kernel(pt_ref, *refs, n_pg):
    x_refs = refs[:n_pg]
    w_ref = refs[n_pg]
    o_ref = refs[n_pg + 1]
    rows = SUBS_PER_PAGE * N_KV
    for c in range(2):
        acc = jnp.zeros((n_pg * rows, 2 * HEAD_DIM), F32)
        for rp in range(CMP_STRIDE // 2):
            parts = []
            for p in range(n_pg):
                a0 = x_refs[p][:, 2 * rp, c].reshape(rows, HEAD_DIM)
                a1 = x_refs[p][:, 2 * rp + 1, c].reshape(rows, HEAD_DIM)
                parts.append(jnp.concatenate([a0, a1], axis=1))
            xs = jnp.concatenate(parts, axis=0).astype(BF16)
            acc = acc + jnp.dot(xs, w_ref[c, rp], preferred_element_type=F32)
        o_ref[c] = acc


def cmp_lohi(pages, page_table, cmp_w1):
    Bn, n_pages = page_table.shape
    n_pg = _pick(n_pages, CMP_PAGES_PER_STEP)
    x = pages.reshape(pages.shape[0], SUBS_PER_PAGE, CMP_STRIDE, 2, N_KV, HEAD_DIM)
    half = CMP_STRIDE // 2
    w_lo = cmp_w1[:, :CMP_STRIDE].reshape(2, half, 2 * HEAD_DIM, HEAD_DIM)
    w_hi = cmp_w1[:, CMP_STRIDE:].reshape(2, half, 2 * HEAD_DIM, HEAD_DIM)
    w = jnp.concatenate([w_lo, w_hi], axis=-1).astype(BF16)
    n_sub = n_pages * SUBS_PER_PAGE

    def x_spec(p):
        return pl.BlockSpec((None, SUBS_PER_PAGE, CMP_STRIDE, 2, N_KV, HEAD_DIM),
                            lambda b, i, pt: (pt[b, i * n_pg + p], 0, 0, 0, 0, 0))

    grid_spec = pltpu.PrefetchScalarGridSpec(
        num_scalar_prefetch=1,
        grid=(Bn, n_pages // n_pg),
        in_specs=[x_spec(p) for p in range(n_pg)]
        + [pl.BlockSpec(w.shape, lambda b, i, pt: (0, 0, 0, 0))],
        out_specs=pl.BlockSpec((None, 2, n_pg * SUBS_PER_PAGE * N_KV, 2 * HEAD_DIM),
                               lambda b, i, pt: (b, 0, i, 0)),
    )
    return pl.pallas_call(
        functools.partial(_cmp_lohi_kernel, n_pg=n_pg),
        grid_spec=grid_spec,
        out_shape=jax.ShapeDtypeStruct((Bn, 2, n_sub * N_KV, 2 * HEAD_DIM), F32),
        compiler_params=_cparams(2),
        name="cmp_lohi",
    )(page_table, *([x] * n_pg), w)


def _compress_kernel(x_ref, pe_ref, w1_ref, b1_ref, w2_ref, o_ref, pe_scr):
    n_rows = x_ref.shape[0]
    half = w1_ref.shape[0] // 2
    c = pl.program_id(1)

    @pl.when(pl.program_id(0) == 0)
    def _():
        pe = jnp.broadcast_to(pe_ref[...], (8, 2 * half))
        pe_scr[c, 0] = jnp.dot(pe[:, :half], w1_ref[:half], preferred_element_type=F32, precision=HIGHEST)
        pe_scr[c, 1] = jnp.dot(pe[:, half:], w1_ref[half:], preferred_element_type=F32, precision=HIGHEST)

    x = x_ref[...]
    lo = x[:, :HEAD_DIM] + pe_scr[c, 0][0:1]
    hi = x[:, HEAD_DIM:] + pe_scr[c, 1][0:1]
    hi_next = pltpu.roll(hi, n_rows - N_KV, 0)
    h = jax.nn.gelu(lo + hi_next + b1_ref[...])
    o_ref[...] = jnp.dot(h.astype(BF16), w2_ref[...].astype(BF16), preferred_element_type=F32).astype(o_ref.dtype)


def compress_blocks(lohi, cmp_pe, cmp_w1, cmp_b1, cmp_w2):
    Bn, _, n_rows, _ = lohi.shape
    kdim = CMP_BLOCK * HEAD_DIM
    out = pl.pallas_call(
        _compress_kernel,
        grid=(Bn, 2),
        in_specs=[pl.BlockSpec((None, None, n_rows, 2 * HEAD_DIM), lambda b, c: (b, c, 0, 0)),
                  pl.BlockSpec((None, 1, kdim), lambda b, c: (c, 0, 0)),
                  pl.BlockSpec((None, kdim, HEAD_DIM), lambda b, c: (c, 0, 0)),
                  pl.BlockSpec((None, 1, HEAD_DIM), lambda b, c: (c, 0, 0)),
                  pl.BlockSpec((None, HEAD_DIM, HEAD_DIM), lambda b, c: (c, 0, 0))],
        out_specs=pl.BlockSpec((None, None, n_rows, HEAD_DIM), lambda b, c: (b, c, 0, 0)),
        out_shape=jax.ShapeDtypeStruct((Bn, 2, n_rows, HEAD_DIM), BF16),
        scratch_shapes=[pltpu.VMEM((2, 2, 8, HEAD_DIM), F32)],
        compiler_params=_cparams(2),
        name="compress",
    )(lohi, cmp_pe.reshape(2, 1, kdim), cmp_w1.reshape(2, kdim, HEAD_DIM),
      cmp_b1.reshape(2, 1, HEAD_DIM), cmp_w2)
    return out.reshape(Bn, 2, n_rows // N_KV, N_KV, HEAD_DIM).transpose(0, 1, 3, 2, 4)


N_CAND = 128
NEG_TAKEN = -3e38


def _topk_mask(score, axis, k_sel):
    idx = lax.broadcasted_iota(jnp.int32, score.shape, axis)
    sel = jnp.zeros(score.shape, F32)
    for _ in range(k_sel):
        m = jnp.max(score, axis=axis, keepdims=True)
        first = jnp.min(jnp.where(score == m, idx, N_CAND), axis=axis, keepdims=True)
        hit = idx == first
        sel = jnp.where(hit & (m > 0.5 * NEG), 1.0, sel)
        score = jnp.where(hit, NEG_TAKEN, score)
    return sel


def _cmp_attn_kernel(slope_ref, q_ref, kc_ref, vc_ref, gate_ref, wsel_ref, oc_ref, sel_ref, *,
                     tq, hpg, gps, pos0, pos_step, transposed, k_sel, n_cand):
    i = pl.program_id(1)
    n_cmp = kc_ref.shape[1]
    gw = hpg * HEAD_DIM
    scale = HEAD_DIM ** -0.5
    base = pos0 + i * pos_step
    t_idx = lax.broadcasted_iota(jnp.int32, (tq, n_cmp), 0)
    n_idx = lax.broadcasted_iota(jnp.int32, (tq, n_cmp), 1)
    dist_i = base + t_idx - (n_idx * CMP_STRIDE + (CMP_BLOCK - 1))
    valid = dist_i >= 0
    dist = dist_i.astype(F32)
    gates = gate_ref[...]
    shape, j_ax, t_ax = ((N_CAND, tq), 0, 1) if transposed else ((tq, N_CAND), 1, 0)
    j = lax.broadcasted_iota(jnp.int32, shape, j_ax)
    blk = (base + lax.broadcasted_iota(jnp.int32, shape, t_ax)) // SEL_BLOCK
    forced = (j == 0) | (j == blk) | (j == blk - 1)
    visible = (j <= blk) & (j < n_cand)
    scores = []
    for gi in range(gps):
        g = pl.program_id(2) * gps + gi
        q = q_ref[:, gi * gw:(gi + 1) * gw]
        qs = jnp.concatenate([q[:, h * HEAD_DIM:(h + 1) * HEAD_DIM] for h in range(hpg)], axis=0)
        qs = (qs * scale).astype(BF16)
        s = lax.dot_general(qs, kc_ref[gi], (((1,), (1,)), ((), ())), preferred_element_type=F32)
        vc = vc_ref[gi]
        psum = jnp.zeros((tq, n_cmp), F32)
        for h in range(hpg):
            sh = s[h * tq:(h + 1) * tq] - slope_ref[g * hpg + h] * dist
            sh = jnp.where(valid, sh, NEG)
            m = jnp.max(sh, axis=-1, keepdims=True)
            e = jnp.where(valid, jnp.exp(sh - m), 0.0)
            p = e / jnp.maximum(jnp.sum(e, axis=-1, keepdims=True), 1e-30)
            psum = psum + p
            o_h = jnp.dot(p.astype(BF16), vc, preferred_element_type=F32)
            c0 = gi * gw + h * HEAD_DIM
            oc_ref[:, c0:c0 + HEAD_DIM] = o_h * gates[:, gi * 128 + 3 * h:gi * 128 + 3 * h + 1]
        if transposed:
            imp = lax.dot_general(wsel_ref[...], psum, (((1,), (1,)), ((), ())),
                                  preferred_element_type=F32, precision=HIGHEST)
        else:
            imp = lax.dot_general(psum, wsel_ref[...], (((1,), (1,)), ((), ())),
                                  preferred_element_type=F32, precision=HIGHEST)
        scores.append(jnp.where(visible, imp + jnp.where(forced, FORCE_BONUS, 0.0), NEG))
    if transposed:
        for gi in range(gps):
            sel = _topk_mask(scores[gi], 0, k_sel).T
            sel_ref[:, gi * N_CAND:(gi + 1) * N_CAND] = sel.astype(sel_ref.dtype)
    else:
        sel = _topk_mask(jnp.concatenate(scores, axis=0), 1, k_sel)
        for gi in range(gps):
            sel_ref[:, gi * N_CAND:(gi + 1) * N_CAND] = sel[gi * tq:(gi + 1) * tq].astype(sel_ref.dtype)


def _sel_weights(n_cmp_pad, n_cmp):
    j = np.arange(N_CAND)[:, None]
    n = np.arange(n_cmp_pad)[None, :]
    w = (n >= SUBS_PER_SEL * j - 1) & (n <= SUBS_PER_SEL * j + SUBS_PER_SEL - 1) & (n < n_cmp)
    return jnp.asarray(w.astype(np.float32))


def cmp_attention(q, kcvc, gates, *, tq, pos0, pos_step, transposed, k_sel, n_cand):
    Bn, T, D = q.shape
    n_heads = D // HEAD_DIM
    hpg = n_heads // N_KV
    n_sub = kcvc.shape[3]
    gw = hpg * HEAD_DIM
    slopes = alibi_slopes(n_heads).reshape(-1)
    wsel = _sel_weights(n_sub, n_sub - 1)
    gps = N_KV if tq < 128 else 1
    kern = functools.partial(_cmp_attn_kernel, tq=tq, hpg=hpg, gps=gps, pos0=pos0, pos_step=pos_step,
                             transposed=transposed, k_sel=k_sel, n_cand=n_cand)
    return pl.pallas_call(
        kern,
        grid=(Bn, T // tq, N_KV // gps),
        in_specs=[pl.BlockSpec(memory_space=pltpu.SMEM),
                  pl.BlockSpec((None, tq, gps * gw), lambda b, i, g: (b, i, g)),
                  pl.BlockSpec((None, None, gps, n_sub, HEAD_DIM), lambda b, i, g: (b, 0, g, 0, 0)),
                  pl.BlockSpec((None, None, gps, n_sub, HEAD_DIM), lambda b, i, g: (b, 1, g, 0, 0)),
                  pl.BlockSpec((None, tq, gps * 128), lambda b, i, g: (b, i, g)),
                  pl.BlockSpec((N_CAND, n_sub), lambda b, i, g: (0, 0))],
        out_specs=[pl.BlockSpec((None, tq, gps * gw), lambda b, i, g: (b, i, g)),
                   pl.BlockSpec((None, tq, gps * N_CAND), lambda b, i, g: (b, i, g))],
        out_shape=[jax.ShapeDtypeStruct((Bn, T, D), F32),
                   jax.ShapeDtypeStruct((Bn, T, N_KV * N_CAND), BF16)],
        compiler_params=_cparams(3),
        name="cmp_attn",
    )(slopes, q, kcvc, kcvc, gates, wsel)


SEL_TK = 512
SEL_TQ = 256
MASK_BIG = 1e30


def _sel_win_kernel(slope_ref, q_ref, ks_ref, vs_ref, kw_ref, vw_ref, sel_ref, oc_ref, gate_ref, e_ref,
                    o_ref, ksb, vsb, kwb, vwb, *, tq, hpg):
    g = pl.program_id(0)
    i = pl.program_id(1)
    L = ks_ref.shape[0]
    s0 = i * tq
    scale = HEAD_DIM ** -0.5

    @pl.when(i == 0)
    def _():
        ksb[:, :HEAD_DIM] = ks_ref[...].astype(BF16)
        ksb[:, HEAD_DIM:] = e_ref[...]
        vsb[...] = vs_ref[...].astype(BF16)
        kwb[0:WINDOW] = jnp.zeros((WINDOW, HEAD_DIM), BF16)
        vwb[0:WINDOW] = jnp.zeros((WINDOW, HEAD_DIM), BF16)
        kwb[WINDOW:] = kw_ref[...].astype(BF16)
        vwb[WINDOW:] = vw_ref[...].astype(BF16)

    q = q_ref[...]
    qs = jnp.concatenate([q[:, h * HEAD_DIM:(h + 1) * HEAD_DIM] for h in range(hpg)], axis=0)
    qs = (qs * (scale * LOG2E)).astype(BF16)
    slopes = [slope_ref[g * hpg + h] * LOG2E for h in range(hpg)]
    neg_sel = sel_ref[...] - 1.0
    qaug = jnp.concatenate([qs, jnp.concatenate([neg_sel] * hpg, axis=0)], axis=1)
    n_tiles = s0 // SEL_TK + 1

    def sweep(k0, carry, width, diagonal):
        ms, ls, accs = carry
        k0 = pl.multiple_of(k0, SEL_TK)
        v_t = vsb[pl.ds(k0, width), :]
        if diagonal:
            t_idx = lax.broadcasted_iota(jnp.int32, (tq, width), 0)
            c_idx = lax.broadcasted_iota(jnp.int32, (tq, width), 1)
            causal = s0 + t_idx >= k0 + c_idx
        s = lax.dot_general(qaug, ksb[pl.ds(k0, width), :], (((1,), (1,)), ((), ())),
                            preferred_element_type=F32)
        rel = (k0 - s0 + lax.broadcasted_iota(jnp.int32, (1, width), 1)).astype(F32)
        new_m, new_l, new_acc = [], [], []
        for h in range(hpg):
            sh = s[h * tq:(h + 1) * tq] + slopes[h] * rel
            if diagonal:
                sh = jnp.where(causal, sh, -MASK_BIG)
            m_new = jnp.maximum(ms[h], jnp.max(sh, axis=-1, keepdims=True))
            p = jnp.exp2(sh - m_new)
            alpha = jnp.exp2(ms[h] - m_new)
            new_l.append(alpha * ls[h] + jnp.sum(p, axis=-1, keepdims=True))
            new_acc.append(alpha * accs[h] + jnp.dot(p.astype(BF16), v_t, preferred_element_type=F32))
            new_m.append(m_new)
        return tuple(new_m), tuple(new_l), tuple(new_acc)

    init = ((jnp.full((tq, 1), NEG, F32),) * hpg, (jnp.zeros((tq, 1), F32),) * hpg,
            (jnp.zeros((tq, HEAD_DIM), F32),) * hpg)
    n_pairs = (n_tiles - 1) // 2
    carry = lax.fori_loop(0, n_pairs, lambda i2, c: sweep(i2 * (2 * SEL_TK), c, 2 * SEL_TK, False), init)
    k_last = n_pairs * (2 * SEL_TK)
    ms, ls, accs = lax.cond(n_tiles % 2 == 0,
                            lambda c: sweep(k_last, c, 2 * SEL_TK, True),
                            lambda c: sweep(k_last, c, SEL_TK, True), carry)

    wn = WINDOW + tq
    w0 = pl.multiple_of(s0, tq)
    kw_t = kwb[pl.ds(w0, wn), :]
    vw_t = vwb[pl.ds(w0, wn), :]
    sw = lax.dot_general(qs, kw_t, (((1,), (1,)), ((), ())), preferred_element_type=F32)
    t_idx = lax.broadcasted_iota(jnp.int32, (tq, wn), 0)
    c_idx = lax.broadcasted_iota(jnp.int32, (tq, wn), 1)
    dist_i = t_idx + WINDOW - c_idx
    valid = (dist_i >= 0) & (dist_i < WINDOW) & (c_idx + s0 >= WINDOW)
    dist = dist_i.astype(F32)
    gates = gate_ref[...]
    oc = oc_ref[...]
    for h in range(hpg):
        sh = sw[h * tq:(h + 1) * tq] + jnp.where(valid, -slopes[h] * dist, -MASK_BIG)
        m = jnp.max(sh, axis=-1, keepdims=True)
        e = jnp.exp2(sh - m)
        o_w = (jnp.dot(e.astype(BF16), vw_t, preferred_element_type=F32)
               / jnp.maximum(jnp.sum(e, axis=-1, keepdims=True), 1e-30))
        o_s = accs[h] / jnp.maximum(ls[h], 1e-30)
        out = (oc[:, h * HEAD_DIM:(h + 1) * HEAD_DIM] + gates[:, 3 * h + 1:3 * h + 2] * o_s
               + gates[:, 3 * h + 2:3 * h + 3] * o_w)
        o_ref[:, h * HEAD_DIM:(h + 1) * HEAD_DIM] = out.astype(o_ref.dtype)


def _block_onehot(n_keys):
    key = np.arange(n_keys)[:, None]
    j = np.arange(N_CAND)[None, :]
    return jnp.asarray((key // SEL_BLOCK == j).astype(np.float32) * MASK_BIG, dtype=BF16)


def sel_win_attention(q, kv, sel, oc, gates, *, tq):
    _, L, D = q.shape
    n_heads = D // HEAD_DIM
    hpg = n_heads // N_KV
    gw = hpg * HEAD_DIM
    assert L % SEL_TK == 0 and SEL_TK % tq == 0 and L // SEL_BLOCK <= N_CAND
    slopes = alibi_slopes(n_heads).reshape(-1)
    e = _block_onehot(L)

    def kv_spec(branch, which):
        cb = (branch * 2 + which) * N_KV
        return pl.BlockSpec((L, HEAD_DIM), lambda g, i: (0, cb + g), pipeline_mode=pl.Buffered(1))

    return pl.pallas_call(
        functools.partial(_sel_win_kernel, tq=tq, hpg=hpg),
        grid=(N_KV, L // tq),
        in_specs=[pl.BlockSpec(memory_space=pltpu.SMEM),
                  pl.BlockSpec((None, tq, gw), lambda g, i: (0, i, g)),
                  kv_spec(1, 0), kv_spec(1, 1), kv_spec(2, 0), kv_spec(2, 1),
                  pl.BlockSpec((None, tq, N_CAND), lambda g, i: (0, i, g)),
                  pl.BlockSpec((None, tq, gw), lambda g, i: (0, i, g)),
                  pl.BlockSpec((None, tq, 128), lambda g, i: (0, i, g)),
                  pl.BlockSpec(e.shape, lambda g, i: (0, 0))],
        out_specs=pl.BlockSpec((tq, gw), lambda g, i: (i, g)),
        out_shape=jax.ShapeDtypeStruct((L, D), BF16),
        scratch_shapes=[pltpu.VMEM((L, 2 * HEAD_DIM), BF16), pltpu.VMEM((L, HEAD_DIM), BF16),
                        pltpu.VMEM((WINDOW + L, HEAD_DIM), BF16), pltpu.VMEM((WINDOW + L, HEAD_DIM), BF16)],
        compiler_params=_cparams(2),
        name="sel_win_attn",
    )(slopes, q, kv, kv, kv, kv, sel, oc, gates, e)


def prompt_nsa(q, gates, kv, kcvc):
    L = q.shape[1]
    n_sel = L // SEL_BLOCK
    oc, sel = cmp_attention(q, kcvc, gates, tq=QBLK, pos0=0, pos_step=QBLK, transposed=True,
                            k_sel=min(TOPK, n_sel), n_cand=n_sel)
    return sel_win_attention(q, kv, sel, oc, gates, tq=SEL_TQ)


SAMPLE_TQ = 8
SAMPLE_PAGES_PER_STEP = 8
NEW_ROWS_PAD = 128


def _sample_sel_win_kernel(pt_ref, slope_ref, q_ref, sel_ref, oc_ref, gate_ref, e_ref, *refs,
                           n_pg, hpg, t_real, n_steps, past_len, win_len):
    pages = refs[:n_pg]
    nslc_ref, nwin_ref, cwk_ref, cwv_ref, o_ref, m_scr, l_scr, acc_scr = refs[n_pg:]
    j = pl.program_id(1)
    tq = SAMPLE_TQ
    tk = n_pg * PAGE_SIZE
    scale = HEAD_DIM ** -0.5

    @pl.when(j == 0)
    def _():
        m_scr[...] = jnp.full(m_scr.shape, NEG, F32)
        l_scr[...] = jnp.zeros(l_scr.shape, F32)
        acc_scr[...] = jnp.zeros(acc_scr.shape, F32)

    q = q_ref[...]
    neg_sel = sel_ref[...] - 1.0

    def heads_of(g):
        gw = hpg * HEAD_DIM
        qg = q[:, g * gw:(g + 1) * gw]
        qs = jnp.concatenate([qg[:, h * HEAD_DIM:(h + 1) * HEAD_DIM] for h in range(hpg)], axis=0)
        return (qs * scale).astype(BF16)

    def online(g, h, sh, v):
        r0 = h * tq
        m_old = m_scr[g, r0:r0 + tq]
        m_new = jnp.maximum(m_old, jnp.max(sh, axis=-1, keepdims=True))
        p = jnp.exp(sh - m_new)
        alpha = jnp.exp(m_old - m_new)
        l_scr[g, r0:r0 + tq] = alpha * l_scr[g, r0:r0 + tq] + jnp.sum(p, axis=-1, keepdims=True)
        acc_scr[g, r0:r0 + tq] = alpha * acc_scr[g, r0:r0 + tq] + jnp.dot(p.astype(BF16), v,
                                                                         preferred_element_type=F32)
        m_scr[g, r0:r0 + tq] = m_new

    rows = hpg * tq
    n_rows = N_KV * rows
    n_lane = tk * N_KV
    kf = jnp.concatenate([pg[:, 0].reshape(PAGE_SIZE * N_KV, HEAD_DIM) for pg in pages], axis=0).astype(BF16)
    vf = jnp.concatenate([pg[:, 1].reshape(PAGE_SIZE * N_KV, HEAD_DIM) for pg in pages], axis=0).astype(BF16)
    kaug = jnp.concatenate([kf, e_ref[...]], axis=1)
    qaug = jnp.concatenate(
        [jnp.concatenate([heads_of(g) for g in range(N_KV)], axis=0),
         jnp.concatenate([neg_sel[:, g * N_CAND:(g + 1) * N_CAND] for g in range(N_KV) for _ in range(hpg)],
                         axis=0)], axis=1)
    s = lax.dot_general(qaug, kaug, (((1,), (1,)), ((), ())), preferred_element_type=F32)
    lane = lax.broadcasted_iota(jnp.int32, (1, n_lane), 1)
    rel = (j * tk - past_len + lane // N_KV).astype(F32)
    slope_col = jnp.concatenate([jnp.full((tq, 1), slope_ref[gh], F32) for gh in range(N_KV * hpg)], axis=0)
    row_g = lax.broadcasted_iota(jnp.int32, (n_rows, 1), 0) // rows
    sh = jnp.where(lane % N_KV == row_g, s + slope_col * rel, -MASK_BIG)
    m_old = m_scr[...].reshape(n_rows, 1)
    m_new = jnp.maximum(m_old, jnp.max(sh, axis=-1, keepdims=True))
    p = jnp.exp(sh - m_new)
    alpha = jnp.exp(m_old - m_new)
    l_new = alpha * l_scr[...].reshape(n_rows, 1) + jnp.sum(p, axis=-1, keepdims=True)
    acc_new = alpha * acc_scr[...].reshape(n_rows, HEAD_DIM) + jnp.dot(p.astype(BF16), vf,
                                                                    preferred_element_type=F32)
    m_scr[...] = m_new.reshape(m_scr.shape)
    l_scr[...] = l_new.reshape(l_scr.shape)
    acc_scr[...] = acc_new.reshape(acc_scr.shape)

    @pl.when(j == n_steps - 1)
    def _():
        gates = gate_ref[...]
        oc = oc_ref[...]
        t_n = lax.broadcasted_iota(jnp.int32, (tq, NEW_ROWS_PAD), 0)
        r_n = lax.broadcasted_iota(jnp.int32, (tq, NEW_ROWS_PAD), 1)
        valid_n = (r_n <= t_n) & (r_n < t_real)
        wn = win_len + NEW_ROWS_PAD
        t_w = lax.broadcasted_iota(jnp.int32, (tq, wn), 0)
        c_w = lax.broadcasted_iota(jnp.int32, (tq, wn), 1)
        dist_w = jnp.where(c_w < win_len, t_w + win_len - c_w, t_w - (c_w - win_len))
        cached = c_w < win_len
        valid_w = ((cached & (dist_w < WINDOW) & (c_w + past_len - win_len >= 0))
                   | ((c_w >= win_len) & (dist_w >= 0) & (c_w - win_len < t_real)))
        dist_wf = dist_w.astype(F32)
        for g in range(N_KV):
            kc0 = g * HEAD_DIM
            vc0 = (N_KV + g) * HEAD_DIM
            qs = heads_of(g)
            k_n = nslc_ref[:, kc0:kc0 + HEAD_DIM].astype(BF16)
            v_n = nslc_ref[:, vc0:vc0 + HEAD_DIM].astype(BF16)
            s_n = lax.dot_general(qs, k_n, (((1,), (1,)), ((), ())), preferred_element_type=F32)
            k_w = jnp.concatenate([cwk_ref[:, g, :], nwin_ref[:, kc0:kc0 + HEAD_DIM]], axis=0).astype(BF16)
            v_w = jnp.concatenate([cwv_ref[:, g, :], nwin_ref[:, vc0:vc0 + HEAD_DIM]], axis=0).astype(BF16)
            s_w = lax.dot_general(qs, k_w, (((1,), (1,)), ((), ())), preferred_element_type=F32)
            for h in range(hpg):
                slope = slope_ref[g * hpg + h]
                r0 = h * tq
                sh = jnp.where(valid_n, s_n[r0:r0 + tq] + slope * r_n.astype(F32), -MASK_BIG)
                online(g, h, sh, v_n)
                o_s = acc_scr[g, r0:r0 + tq] / jnp.maximum(l_scr[g, r0:r0 + tq], 1e-30)
                sw = jnp.where(valid_w, s_w[r0:r0 + tq] - slope * dist_wf, NEG)
                m = jnp.max(sw, axis=-1, keepdims=True)
                e = jnp.where(valid_w, jnp.exp(sw - m), 0.0)
                p = e / jnp.maximum(jnp.sum(e, axis=-1, keepdims=True), 1e-30)
                o_w = jnp.dot(p.astype(BF16), v_w, preferred_element_type=F32)
                c0 = (g * hpg + h) * HEAD_DIM
                gc = g * 128 + 3 * h
                out = (oc[:, c0:c0 + HEAD_DIM] + gates[:, gc + 1:gc + 2] * o_s + gates[:, gc + 2:gc + 3] * o_w)
                o_ref[:, c0:c0 + HEAD_DIM] = out.astype(o_ref.dtype)


def _block_onehot_rows(n_keys):
    key = np.repeat(np.arange(n_keys), N_KV)[:, None]
    j = np.arange(N_CAND)[None, :]
    return jnp.asarray((key // SEL_BLOCK == j).astype(np.float32) * MASK_BIG, dtype=BF16)


def sample_sel_win_attention(q, sel, oc, gates, cache_slc, page_table, new_slc, new_win, cache_win, t_real):
    Bn, tq, D = q.shape
    n_heads = D // HEAD_DIM
    hpg = n_heads // N_KV
    n_pages = page_table.shape[1]
    n_pg = _pick(n_pages, SAMPLE_PAGES_PER_STEP)
    n_steps = n_pages // n_pg
    past_len = n_pages * PAGE_SIZE
    win_len = cache_win.shape[1]
    assert past_len // SEL_BLOCK <= N_CAND and past_len % SEL_BLOCK == 0 and t_real <= min(tq, SEL_BLOCK)
    slopes = alibi_slopes(n_heads).reshape(-1)
    e = _block_onehot_rows(past_len)
    e_rows = n_pg * PAGE_SIZE * N_KV
    rows = hpg * tq

    def pg_spec(p):
        return pl.BlockSpec((None, PAGE_SIZE, 2, N_KV, HEAD_DIM),
                            lambda b, j, pt: (pt[b, j * n_pg + p], 0, 0, 0, 0))

    def win_spec(c):
        return pl.BlockSpec((None, win_len, None, N_KV, HEAD_DIM), lambda b, j, pt: (b, 0, c, 0, 0))

    per_b = lambda shape: pl.BlockSpec((None,) + shape, lambda b, j, pt: (b, 0, 0))
    grid_spec = pltpu.PrefetchScalarGridSpec(
        num_scalar_prefetch=1,
        grid=(Bn, n_steps),
        in_specs=[pl.BlockSpec(memory_space=pltpu.SMEM),
                  per_b((tq, D)), per_b((tq, N_KV * N_CAND)), per_b((tq, D)), per_b((tq, N_KV * 128)),
                  pl.BlockSpec((e_rows, N_CAND), lambda b, j, pt: (j, 0))]
        + [pg_spec(p) for p in range(n_pg)]
        + [per_b((NEW_ROWS_PAD, KV_COLS)), per_b((NEW_ROWS_PAD, KV_COLS)), win_spec(0), win_spec(1)],
        out_specs=per_b((tq, D)),
        scratch_shapes=[pltpu.VMEM((N_KV, rows, 1), F32), pltpu.VMEM((N_KV, rows, 1), F32),
                        pltpu.VMEM((N_KV, rows, HEAD_DIM), F32)],
    )
    return pl.pallas_call(
        functools.partial(_sample_sel_win_kernel, n_pg=n_pg, hpg=hpg, t_real=t_real, n_steps=n_steps,
                          past_len=past_len, win_len=win_len),
        grid_spec=grid_spec,
        out_shape=jax.ShapeDtypeStruct((Bn, tq, D), BF16),
        compiler_params=_cparams(2),
        name="sample_sel_win_attn",
    )(page_table, slopes, q, sel, oc, gates, e, *([cache_slc] * n_pg), new_slc, new_win, cache_win, cache_win)


def sample_nsa(q2d, gates_pad, kv_rows, kcvc, cache_slc, cache_win, page_table):
    _, kv_slc, kv_win = kv_rows
    Bn, T = kv_slc.shape[:2]
    D = q2d.shape[1]
    past_len = page_table.shape[1] * PAGE_SIZE
    n_past_blk = past_len // SEL_BLOCK
    pad_t = lambda a: jnp.pad(a.reshape(Bn, T, -1), ((0, 0), (0, SAMPLE_TQ - T), (0, 0)))
    q = pad_t(q2d)
    gates = pad_t(gates_pad)
    oc, sel = cmp_attention(q, kcvc, gates, tq=SAMPLE_TQ, pos0=past_len, pos_step=0, transposed=False,
                            k_sel=min(TOPK, n_past_blk + 1) - 1, n_cand=n_past_blk)
    pad_rows = lambda a: jnp.pad(a.reshape(Bn, T, KV_COLS), ((0, 0), (0, NEW_ROWS_PAD - T), (0, 0)))
    o = sample_sel_win_attention(q, sel, oc, gates, cache_slc, page_table,
                                 pad_rows(kv_slc), pad_rows(kv_win), cache_win, T)
    return o[:, :T].reshape(Bn * T, D)


def sample_attention_paged(kv_rows, cache_cmp, cache_slc, cache_win, page_table, cmp_pe, cmp_w1, cmp_b1, cmp_w2):
    T = kv_rows[0].shape[1]
    assert T < CMP_STRIDE, "new rows never complete a compression sub-block"
    kcvc = compress_blocks(cmp_lohi(cache_cmp, page_table, cmp_w1), cmp_pe, cmp_w1, cmp_b1, cmp_w2)

    def attend(q2d, gates_pad):
        return sample_nsa(q2d, gates_pad, kv_rows, kcvc, cache_slc, cache_win, page_table)

    return attend


def alibi_slopes(n_heads):
    exps = np.arange(1, n_heads + 1, dtype=np.float32) * np.float32(-8.0 / n_heads)
    return jnp.asarray(np.exp2(exps), dtype=F32).reshape(N_KV, n_heads // N_KV)


def prompt_attention(kv2d, kv_rows, cmp_pe, cmp_w1, cmp_b1, cmp_w2):
    kv_cmp = kv_rows[0]
    L = kv_cmp.shape[1]
    pages = kv_cmp.reshape(L // PAGE_SIZE, PAGE_SIZE, 2, N_KV, HEAD_DIM)
    table = jnp.arange(L // PAGE_SIZE, dtype=jnp.int32)[None]
    kcvc = compress_blocks(cmp_lohi(pages, table, cmp_w1), cmp_pe, cmp_w1, cmp_b1, cmp_w2)

    def attend(q2d, gates_pad):
        return prompt_nsa(q2d[None], gates_pad[None], kv2d, kcvc)

    return attend


def _rows(v, per_tok):
    return v if v.shape[0] == 1 else jnp.repeat(v, per_tok, axis=0)


def trunk(x, mods, kv_mod, h0, make_attend, p):
    Bn, L, D = x.shape
    M = Bn * L
    n_heads = D // HEAD_DIM
    hpg = n_heads // N_KV
    depth = p["mod_w"].shape[0]
    n_a = depth // 2
    xr = x.reshape(M, D)
    new_h = []
    kv_rows = None
    attend = None
    layer_mods = [[_rows(m, L) for m in jnp.split(mods[l], 6, axis=-1)] for l in range(depth)]
    kv_shift, kv_scale = [_rows(m, L) for m in jnp.split(kv_mod, 2, axis=-1)]

    def mixer_dtype(l):
        return F32 if l < n_a else BF16

    h_mix = norm_mod(xr, p["norm_pre"][0, 0], layer_mods[0][1], layer_mods[0][0], mixer_dtype(0))
    hk = norm_mod(xr, p["kv_norm"], kv_scale, kv_shift, BF16) if n_a == 0 else None
    for l in range(depth):
        sh1, sc1, ga1, sh2, sc2, ga2 = layer_mods[l]
        mlp_pre = [(p["norm_pre"][l, 1], sc2, sh2, BF16)]
        after_mlp = []
        if l + 1 < depth:
            nsh1, nsc1 = layer_mods[l + 1][0], layer_mods[l + 1][1]
            after_mlp.append((p["norm_pre"][l + 1, 0], nsc1, nsh1, mixer_dtype(l + 1)))
            if l + 1 == n_a:
                after_mlp.append((p["kv_norm"], kv_scale, kv_shift, BF16))
        if l == n_a:
            kv2d = mm_wide(hk, p["w_kv"], tn=1024)
            kv = kv2d.reshape(Bn, L, 3, 2, N_KV, HEAD_DIM)
            kv_rows = (kv[:, :, 0], kv[:, :, 1], kv[:, :, 2])
            attend = make_attend(kv2d, kv_rows)
        if l < n_a:
            u = h_mix
            gy, h_last = s5_mixer_core(u.reshape(Bn, L, D), h0[l], p["ssm_lam_re"][l], p["ssm_lam_im"][l],
                                       p["ssm_log_dt"][l], p["ssm_b_re"][l], p["ssm_b_im"][l],
                                       p["ssm_c_re"][l], p["ssm_c_im"][l], p["ssm_d"][l])
            new_h.append(h_last)
            xr, h = mm_tall(gy, p["ssm_w_glu"], xr, p["norm_post"][l, 0], ga1, layer=l, glu=True, tk=D,
                            next_norms=mlp_pre)
        else:
            lb = l - n_a
            q = mm_wide(h_mix, p["nsa_w_qg"], layer=lb, n_out=n_heads * HEAD_DIM, tn=1024)
            gates_pad = mm_wide(h_mix, p["w_gate_pad"], layer=lb, epilogue="sigmoid", tn=512)
            o = attend(q, gates_pad)
            xr, h = mm_tall(o, p["nsa_w_o"], xr, p["norm_post"][l, 0], ga1, layer=lb, tk=D, next_norms=mlp_pre)
        f = mm_wide(h, p["mlp_w1"], layer=l, epilogue="sqrelu", out_dtype=BF16, tn=1024)
        outs = mm_tall(f, p["mlp_w2"], xr, p["norm_post"][l, 1], ga2, layer=l, tk=1024, next_norms=after_mlp)
        if after_mlp:
            xr, h_mix = outs[0], outs[1]
            if l + 1 == n_a:
                hk = outs[2]
        else:
            xr = outs
    return xr.reshape(Bn, L, D), jnp.stack(new_h), kv_rows


def kernel(x_prompt, x_sample, c_prompt, c_sample, state_ssm, cache_cmp, cache_slc, cache_win, page_table, mod_w, mod_b, norm_pre, norm_post, mlp_w1, mlp_w2, ssm_lam_re, ssm_lam_im, ssm_log_dt, ssm_b_re, ssm_b_im, ssm_c_re, ssm_c_im, ssm_d, ssm_w_glu, kv_norm, kv_mod_w, kv_mod_b, w_kv, cmp_pe, cmp_w1, cmp_b1, cmp_w2, nsa_w_qg, nsa_w_o):
    D = x_prompt.shape[-1]
    depth = mod_w.shape[0]
    n_heads = D // HEAD_DIM
    bp, bs = c_prompt.shape[0], c_sample.shape[0]
    c_all = jnp.concatenate([c_prompt, c_sample], axis=0)
    n_c = c_all.shape[0]
    c_all = jnp.pad(c_all, ((0, -n_c % 8), (0, 0)))
    mods = [mm_wide(c_all, mod_w, layer=l, bias=mod_b, prologue="silu", exact=True, tn=512) for l in range(depth)]
    kv_mod = mm_wide(c_all, kv_mod_w, bias=kv_mod_b, prologue="silu", exact=True, tn=512)
    hpg = n_heads // N_KV
    w_gate = nsa_w_qg[:, :, n_heads * HEAD_DIM:].reshape(nsa_w_qg.shape[0], D, N_KV, 3 * hpg)
    w_gate_pad = jnp.pad(w_gate, ((0, 0), (0, 0), (0, 0), (0, 128 - 3 * hpg))).reshape(-1, D, N_KV * 128)
    p = dict(mod_w=mod_w, norm_pre=norm_pre, norm_post=norm_post, mlp_w1=mlp_w1, mlp_w2=mlp_w2.astype(BF16),
             ssm_lam_re=ssm_lam_re, ssm_lam_im=ssm_lam_im, ssm_log_dt=ssm_log_dt, ssm_b_re=ssm_b_re,
             ssm_b_im=ssm_b_im, ssm_c_re=ssm_c_re, ssm_c_im=ssm_c_im, ssm_d=ssm_d,
             ssm_w_glu=ssm_w_glu.astype(BF16), kv_norm=kv_norm, w_kv=w_kv, nsa_w_qg=nsa_w_qg,
             nsa_w_o=nsa_w_o.astype(BF16), w_gate_pad=w_gate_pad)

    def make_prompt(kv2d, kv_rows):
        return prompt_attention(kv2d, kv_rows, cmp_pe, cmp_w1, cmp_b1, cmp_w2)

    def make_sample(kv2d, kv_rows):
        return sample_attention_paged(kv_rows, cache_cmp, cache_slc, cache_win, page_table,
                                      cmp_pe, cmp_w1, cmp_b1, cmp_w2)

    n_a = depth // 2
    G = D // SSM_GROUP
    h0_prompt = jnp.zeros((n_a, bp, G, STATE_DIM, 2), F32)
    y_prompt, ssm_prompt, rows_prompt = trunk(
        x_prompt, [m[:bp] for m in mods], kv_mod[:bp], h0_prompt, make_prompt, p)
    y_sample, ssm_sample, rows_sample = trunk(
        x_sample, [m[bp:bp + bs] for m in mods], kv_mod[bp:bp + bs], state_ssm, make_sample, p)
    cmp_prompt, slc_prompt, win_rows_prompt = rows_prompt
    cmp_sample, slc_sample, win_sample = rows_sample
    win_prompt = win_rows_prompt[:, -min(WINDOW, x_prompt.shape[1]):]
    return (y_prompt, y_sample, ssm_prompt, ssm_sample, cmp_prompt, cmp_sample,
            slc_prompt, slc_sample, win_prompt, win_sample)
```

```python
import functools
import math

import jax
import jax.numpy as jnp
import numpy as np
from jax import lax
from jax.experimental import pallas as pl
from jax.experimental.pallas import tpu as pltpu

F32 = jnp.float32
BF16 = jnp.bfloat16

SSM_GROUP = 16
STATE_DIM = 64
HEAD_DIM = 128
N_KV = 4
CMP_STRIDE = 16
CMP_BLOCK = 2 * CMP_STRIDE
SEL_BLOCK = 64
SUBS_PER_SEL = SEL_BLOCK // CMP_STRIDE
TOPK = 16
WINDOW = 512
QBLK = 128
PAGE_SIZE = 128
EPS = 1e-6
NEG = -1e30
FORCE_BONUS = 1e4

V7X_LANES = 128
GATE_LANES = V7X_LANES
V7X_VMEM_LIMIT_BYTES = 56 * 1024 * 1024
HIGHEST = lax.Precision.HIGHEST
LOG2E = math.log2(math.e)


def _cparams(n_axes):
    return pltpu.CompilerParams(dimension_semantics=("arbitrary",) * n_axes,
                                vmem_limit_bytes=V7X_VMEM_LIMIT_BYTES)


def _pick(n, pref):
    if n <= pref:
        return n
    t = pref
    while n % t:
        t //= 2
    return t


def _norm_mod_kernel(x_ref, g_ref, sc_ref, sh_ref, o_ref):
    x = x_ref[...]
    r = lax.rsqrt(jnp.mean(x * x, axis=-1, keepdims=True) + EPS)
    y = (x * r) * g_ref[...]
    o_ref[...] = (y * (1.0 + sc_ref[...]) + sh_ref[...]).astype(o_ref.dtype)


def norm_mod(x, g, scale, shift, out_dtype):
    M, D = x.shape
    tm = _pick(M, 512)
    per_row = scale.shape[0] != 1
    mod_spec = pl.BlockSpec((tm, D), lambda i: (i, 0)) if per_row else pl.BlockSpec((1, D), lambda i: (0, 0))
    return pl.pallas_call(
        _norm_mod_kernel,
        grid=(M // tm,),
        in_specs=[pl.BlockSpec((tm, D), lambda i: (i, 0)),
                  pl.BlockSpec((1, D), lambda i: (0, 0)),
                  mod_spec, mod_spec],
        out_specs=pl.BlockSpec((tm, D), lambda i: (i, 0)),
        out_shape=jax.ShapeDtypeStruct((M, D), out_dtype),
        compiler_params=_cparams(1),
        name="norm_mod",
    )(x, g.reshape(1, D), scale, shift)


def _mm_wide_kernel(*refs, prologue, epilogue, has_bias, exact, n_w):
    a_ref = refs[0]
    w_refs = refs[1:1 + n_w]
    pos = 1 + n_w
    b_ref = refs[pos] if has_bias else None
    pos += int(has_bias)
    o_ref = refs[pos]
    wbf_refs = refs[pos + 1:]

    a = a_ref[...]
    if prologue == "silu":
        a = a * jax.nn.sigmoid(a)
    if exact:
        zs = [jnp.dot(a, w[...], preferred_element_type=F32, precision=HIGHEST) for w in w_refs]
    else:
        @pl.when(pl.program_id(1) == 0)
        def _():
            for w, wbf in zip(w_refs, wbf_refs):
                wbf[...] = w[...].astype(BF16)

        a = a.astype(BF16)
        zs = [jnp.dot(a, wbf[...], preferred_element_type=F32) for wbf in wbf_refs]
    z = zs[0]
    if has_bias:
        z = z + b_ref[...]
    if epilogue == "sqrelu":
        z = jnp.square(jnp.maximum(z, 0.0))
    elif epilogue == "sigmoid":
        z = jax.nn.sigmoid(z)
    elif epilogue == "glu":
        z = z * jax.nn.sigmoid(zs[1])
    o_ref[...] = z.astype(o_ref.dtype)


def mm_wide(a, w, *, layer=None, col0=0, n_out=None, bias=None, prologue=None, epilogue=None,
            exact=False, out_dtype=F32, tm=1024, tn=512):
    M, K = a.shape
    n_out = n_out if n_out is not None else w.shape[-1] - col0
    tm = _pick(M, tm)
    tn = _pick(n_out, tn)
    assert col0 % tn == 0 and n_out % tn == 0
    n_w = 2 if epilogue == "glu" else 1
    jb = col0 // tn

    def w_spec(extra):
        if layer is None:
            return pl.BlockSpec((K, tn), lambda j, i: (0, jb + extra + j))
        return pl.BlockSpec((None, K, tn), lambda j, i: (layer, 0, jb + extra + j))

    in_specs = [pl.BlockSpec((tm, K), lambda j, i: (i, 0))] + [w_spec(e * (n_out // tn)) for e in range(n_w)]
    args = [a] + [w] * n_w
    if bias is not None:
        if layer is None:
            in_specs.append(pl.BlockSpec((1, tn), lambda j, i: (0, jb + j)))
            args.append(bias.reshape(1, -1))
        else:
            in_specs.append(pl.BlockSpec((None, 1, tn), lambda j, i: (layer, 0, jb + j)))
            args.append(bias.reshape(bias.shape[0], 1, -1))
    scratch = [] if exact else [pltpu.VMEM((K, tn), BF16) for _ in range(n_w)]
    return pl.pallas_call(
        functools.partial(_mm_wide_kernel, prologue=prologue, epilogue=epilogue,
                          has_bias=bias is not None, exact=exact, n_w=n_w),
        grid=(n_out // tn, M // tm),
        in_specs=in_specs,
        out_specs=pl.BlockSpec((tm, tn), lambda j, i: (i, j)),
        out_shape=jax.ShapeDtypeStruct((M, n_out), out_dtype),
        scratch_shapes=scratch,
        compiler_params=_cparams(2),
        name="mm_wide",
    )(*args)


def _mm_tall_kernel(a_ref, *refs, n_w, nk, n_next):
    w_refs = refs[:n_w]
    res_ref, g_ref, ga_ref = refs[n_w:n_w + 3]
    nxt_in = refs[n_w + 3:n_w + 3 + 3 * n_next]
    o_ref = refs[n_w + 3 + 3 * n_next]
    nxt_out = refs[n_w + 4 + 3 * n_next:n_w + 4 + 4 * n_next]
    acc_refs = refs[n_w + 4 + 4 * n_next:]
    k = pl.program_id(1)

    @pl.when(k == 0)
    def _():
        for acc in acc_refs:
            acc[...] = jnp.zeros_like(acc)

    a = a_ref[...].astype(BF16)
    for w, acc in zip(w_refs, acc_refs):
        acc[...] += jnp.dot(a, w[...].astype(BF16), preferred_element_type=F32)

    @pl.when(k == nk - 1)
    def _():
        m = acc_refs[0][...]
        if n_w == 2:
            m = m * jax.nn.sigmoid(acc_refs[1][...])
        r = lax.rsqrt(jnp.mean(m * m, axis=-1, keepdims=True) + EPS)
        x = res_ref[...] + ga_ref[...] * ((m * r) * g_ref[...])
        o_ref[...] = x
        if n_next:
            xn = x * lax.rsqrt(jnp.mean(x * x, axis=-1, keepdims=True) + EPS)
            for j in range(n_next):
                gn, sc, sh = nxt_in[3 * j:3 * j + 3]
                nxt_out[j][...] = ((xn * gn[...]) * (1.0 + sc[...]) + sh[...]).astype(nxt_out[j].dtype)


def mm_tall(a, w, res, g, gate, *, layer=None, glu=False, tm=512, tk=512, next_norms=()):
    M, K = a.shape
    N = res.shape[1]
    tm = _pick(M, tm)
    tk = _pick(K, tk)
    nk = K // tk
    n_w = 2 if glu else 1

    w_mode = dict(pipeline_mode=pl.Buffered(1)) if nk == 1 else {}

    def w_spec(e):
        if layer is None:
            return pl.BlockSpec((tk, N), lambda i, k: (k, e), **w_mode)
        return pl.BlockSpec((None, tk, N), lambda i, k: (layer, k, e), **w_mode)

    per_row = gate.shape[0] != 1
    row_spec = pl.BlockSpec((tm, N), lambda i, k: (i, 0))
    vec_spec = pl.BlockSpec((1, N), lambda i, k: (0, 0))
    mod_spec = row_spec if per_row else vec_spec
    n_next = len(next_norms)
    nxt_args, nxt_specs = [], []
    for gn, sc, sh, _ in next_norms:
        nxt_args += [gn.reshape(1, N), sc, sh]
        nxt_specs += [vec_spec, mod_spec, mod_spec]
    outs = pl.pallas_call(
        functools.partial(_mm_tall_kernel, n_w=n_w, nk=nk, n_next=n_next),
        grid=(M // tm, nk),
        in_specs=[pl.BlockSpec((tm, tk), lambda i, k: (i, k))] + [w_spec(e) for e in range(n_w)]
        + [row_spec, vec_spec, mod_spec] + nxt_specs,
        out_specs=[row_spec] * (1 + n_next),
        out_shape=[jax.ShapeDtypeStruct((M, N), F32)]
        + [jax.ShapeDtypeStruct((M, N), dt) for _, _, _, dt in next_norms],
        scratch_shapes=[pltpu.VMEM((tm, N), F32) for _ in range(n_w)],
        compiler_params=_cparams(2),
        name="mm_tall",
    )(a, *([w] * n_w), res, g.reshape(1, N), gate, *nxt_args)
    return outs if n_next else outs[0]


S5_GROUPS_PER_STEP = V7X_LANES // SSM_GROUP
P2 = 2 * STATE_DIM


def _s5_prep_kernel(lam_re_ref, lam_im_ref, ldt_ref, btr_ref, bti_ref, cr_ref, ci_ref,
                    ws_ref, wct_ref, wm_ref, at_ref, *, tc, gb):
    ws_ref[...] = jnp.zeros(ws_ref.shape, ws_ref.dtype)
    wct_ref[...] = jnp.zeros(wct_ref.shape, wct_ref.dtype)
    C = SSM_GROUP
    tcc = tc * C
    wk = max(tcc, V7X_LANES)
    lane = lax.broadcasted_iota(jnp.int32, (C, P2), 1)
    is_re = lane < STATE_DIM
    kk = lax.broadcasted_iota(jnp.int32, (tc + 1, P2), 0).astype(F32)
    lane_k = lax.broadcasted_iota(jnp.int32, (C, wk), 1)
    for gg in range(gb):
        lr = lam_re_ref[gg]
        li = lam_im_ref[gg]
        dt = jnp.exp(ldt_ref[gg])
        mag = jnp.exp(kk * (lr * dt))
        ang = kk * (li * dt)
        pr = mag * jnp.cos(ang)
        pi = mag * jnp.sin(ang)
        x = pr[1:2] - 1.0
        y = pi[1:2]
        den = lr * lr + li * li
        cfr = (x * lr + y * li) / den
        cfi = (y * lr - x * li) / den
        btr = btr_ref[gg]
        bti = bti_ref[gg]
        bbr = cfr * btr - cfi * bti
        bbi = cfr * bti + cfi * btr
        cre = cr_ref[gg]
        cim = ci_ref[gg]

        def bm(k):
            return jnp.where(is_re, pr[k:k + 1] * bbr - pi[k:k + 1] * bbi, pr[k:k + 1] * bbi + pi[k:k + 1] * bbr)

        def cm(k):
            return jnp.where(is_re, cre * pr[k:k + 1] - cim * pi[k:k + 1], -(cre * pi[k:k + 1] + cim * pr[k:k + 1]))

        cms = [cm(k) for k in range(tc + 1)]
        cs0 = jnp.concatenate(cms[:tc] + [jnp.zeros((wk - tcc, P2), F32)] * (wk > tcc), axis=0)
        kst = lax.dot_general(bm(0), cs0, (((1,), (1,)), ((), ())), preferred_element_type=F32,
                              precision=HIGHEST)
        r0 = gg * C
        for s in range(tc):
            ws_ref[s, r0:r0 + C, gg * P2:(gg + 1) * P2] = bm(tc - 1 - s).astype(ws_ref.dtype)
            wct_ref[s, r0:r0 + C, gg * P2:(gg + 1) * P2] = cms[s + 1].astype(wct_ref.dtype)
            shift = (r0 - s * C) % wk
            moved = kst if shift == 0 else pltpu.roll(kst, shift, 1)
            blockdiag = jnp.where((lane_k >= r0) & (lane_k < r0 + C), moved, 0.0)
            wm_ref[s, r0:r0 + C, :] = blockdiag[:, :V7X_LANES].astype(wm_ref.dtype)
        at_ref[gg, 0:1, :] = pr[tc:tc + 1]
        at_ref[gg, 1:2, :] = jnp.where(is_re[0:1], -pi[tc:tc + 1], pi[tc:tc + 1])


def s5_prep(lam_re, lam_im, log_dt, b_re, b_im, c_re, c_im, tc):
    G = lam_re.shape[0]
    gb = _pick(G, S5_GROUPS_PER_STEP)
    C = SSM_GROUP
    tcc = tc * C
    dup = lambda v: jnp.concatenate([v, v], axis=-1)
    lam_re2 = dup(lam_re)[:, None, :]
    lam_im2 = dup(lam_im)[:, None, :]
    ldt2 = jnp.broadcast_to(log_dt[:, None, None], (G, 1, P2))
    btr = dup(jnp.swapaxes(b_re, 1, 2))
    bti = dup(jnp.swapaxes(b_im, 1, 2))
    cr2 = dup(c_re)
    ci2 = dup(c_im)
    vec = pl.BlockSpec((gb, 1, P2), lambda i: (i, 0, 0))
    mat = pl.BlockSpec((gb, C, P2), lambda i: (i, 0, 0))
    return pl.pallas_call(
        functools.partial(_s5_prep_kernel, tc=tc, gb=gb),
        grid=(G // gb,),
        in_specs=[vec, vec, vec, mat, mat, mat, mat],
        out_specs=[pl.BlockSpec((None, tc, V7X_LANES, gb * P2), lambda i: (i, 0, 0, 0)),
                   pl.BlockSpec((None, tc, V7X_LANES, gb * P2), lambda i: (i, 0, 0, 0)),
                   pl.BlockSpec((None, tc, V7X_LANES, V7X_LANES), lambda i: (i, 0, 0, 0)),
                   pl.BlockSpec((gb, 2, P2), lambda i: (i, 0, 0))],
        out_shape=[jax.ShapeDtypeStruct((G // gb, tc, V7X_LANES, gb * P2), BF16),
                   jax.ShapeDtypeStruct((G // gb, tc, V7X_LANES, gb * P2), BF16),
                   jax.ShapeDtypeStruct((G // gb, tc, V7X_LANES, V7X_LANES), BF16),
                   jax.ShapeDtypeStruct((G, 2, P2), F32)],
        compiler_params=_cparams(1),
        name="s5_prep",
    )(lam_re2, lam_im2, ldt2, btr, bti, cr2, ci2)


def _s5_sums_kernel(u_ref, ws_ref, s_ref, *, tc):
    acc = None
    for s in range(tc):
        z = jnp.dot(u_ref[:, s, :].astype(BF16), ws_ref[s], preferred_element_type=F32)
        acc = z if acc is None else acc + z
    s_ref[...] = acc


def s5_chunk_sums(u3, ws):
    R, tc, D = u3.shape
    nb = D // V7X_LANES
    sw = ws.shape[-1]
    return pl.pallas_call(
        functools.partial(_s5_sums_kernel, tc=tc),
        grid=(nb,),
        in_specs=[pl.BlockSpec((R, tc, V7X_LANES), lambda i: (0, 0, i)),
                  pl.BlockSpec((None, tc, V7X_LANES, sw), lambda i: (i, 0, 0, 0))],
        out_specs=pl.BlockSpec((R, sw), lambda i: (0, i)),
        out_shape=jax.ShapeDtypeStruct((R, nb * sw), F32),
        compiler_params=_cparams(1),
        name="s5_sums",
    )(u3, ws)


def _s5_out_kernel(u_ref, hin_ref, wm_ref, wct_ref, d_ref, o_ref, *, tc):
    us = [u_ref[:, s, :] for s in range(tc)]
    ub = [x.astype(BF16) for x in us]
    hb = hin_ref[...].astype(BF16)
    d = d_ref[...]
    for t in range(tc):
        y = lax.dot_general(hb, wct_ref[t], (((1,), (1,)), ((), ())), preferred_element_type=F32)
        for s in range(t + 1):
            y = y + jnp.dot(ub[s], wm_ref[t - s], preferred_element_type=F32)
        o_ref[:, t, :] = jax.nn.gelu(y + d * us[t])


def s5_chunk_out(u3, hin, wm, wct, d_skip):
    R, tc, D = u3.shape
    nb = D // V7X_LANES
    sw = wct.shape[-1]
    return pl.pallas_call(
        functools.partial(_s5_out_kernel, tc=tc),
        grid=(nb,),
        in_specs=[pl.BlockSpec((R, tc, V7X_LANES), lambda i: (0, 0, i)),
                  pl.BlockSpec((R, sw), lambda i: (0, i)),
                  pl.BlockSpec((None, tc, V7X_LANES, V7X_LANES), lambda i: (i, 0, 0, 0)),
                  pl.BlockSpec((None, tc, V7X_LANES, sw), lambda i: (i, 0, 0, 0)),
                  pl.BlockSpec((1, V7X_LANES), lambda i: (0, i))],
        out_specs=pl.BlockSpec((R, tc, V7X_LANES), lambda i: (0, 0, i)),
        out_shape=jax.ShapeDtypeStruct((R, tc, D), F32),
        compiler_params=_cparams(1),
        name="s5_chunk_out",
    )(u3, hin, wm, wct, d_skip.reshape(1, D))


S5_CHUNKS_PER_STEP = 64


def _s5_scan_kernel(s_ref, h0_ref, at_ref, hin_ref, hfin_ref, h_scr, *, cb, n_steps):
    j = pl.program_id(1)

    @pl.when(j == 0)
    def _():
        h_scr[...] = h0_ref[...]

    ar = at_ref[0]
    ai = at_ref[1]

    def step(c, h):
        hin_ref[c] = h
        return ar * h + ai * pltpu.roll(h, STATE_DIM, 1) + s_ref[c]

    h = lax.fori_loop(0, cb, step, h_scr[...])
    h_scr[...] = h

    @pl.when(j == n_steps - 1)
    def _():
        hfin_ref[...] = h


def s5_scan(s, h0, at):
    Bn, n_chunk, G, _ = s.shape
    cb = _pick(n_chunk, S5_CHUNKS_PER_STEP)
    n_steps = n_chunk // cb
    return pl.pallas_call(
        functools.partial(_s5_scan_kernel, cb=cb, n_steps=n_steps),
        grid=(Bn, n_steps),
        in_specs=[pl.BlockSpec((None, cb, G, P2), lambda b, j: (b, j, 0, 0)),
                  pl.BlockSpec((None, G, P2), lambda b, j: (b, 0, 0)),
                  pl.BlockSpec((2, G, P2), lambda b, j: (0, 0, 0))],
        out_specs=[pl.BlockSpec((None, cb, G, P2), lambda b, j: (b, j, 0, 0)),
                   pl.BlockSpec((None, G, P2), lambda b, j: (b, 0, 0))],
        out_shape=[jax.ShapeDtypeStruct((Bn, n_chunk, G, P2), F32),
                   jax.ShapeDtypeStruct((Bn, G, P2), F32)],
        scratch_shapes=[pltpu.VMEM((G, P2), F32)],
        compiler_params=_cparams(2),
        name="s5_scan",
    )(s, h0, at)


def s5_mixer_core(u, h0, lam_re, lam_im, log_dt, b_re, b_im, c_re, c_im, d_skip):
    Bn, L, D = u.shape
    G = D // SSM_GROUP
    tc = _pick(L, 16)
    n_chunk = L // tc
    ws, wct, wm, at = s5_prep(lam_re, lam_im, log_dt, b_re, b_im, c_re, c_im, tc)
    u3 = u.reshape(Bn * n_chunk, tc, D)
    s = s5_chunk_sums(u3, ws).reshape(Bn, n_chunk, G, P2)
    h0v = jnp.concatenate([h0[..., 0], h0[..., 1]], axis=-1)
    hin, hfin = s5_scan(s, h0v, at.transpose(1, 0, 2))
    gy = s5_chunk_out(u3, hin.reshape(Bn * n_chunk, G * P2), wm, wct, d_skip)
    h_last = jnp.stack([hfin[..., :STATE_DIM], hfin[..., STATE_DIM:]], axis=-1)
    return gy.reshape(Bn * L, D), h_last


SUBS_PER_PAGE = PAGE_SIZE // CMP_STRIDE
KV_COLS = 2 * N_KV * HEAD_DIM
CMP_PAGES_PER_STEP = 8


def _cmp_lohi_kernel(pt_ref, *refs, n_pg):
    x_refs = refs[:n_pg]
    w_ref = refs[n_pg]
    o_ref = refs[n_pg + 1]
    rows = SUBS_PER_PAGE * N_KV
    for c in range(2):
        acc = jnp.zeros((n_pg * rows, 2 * HEAD_DIM), F32)
        for rp in range(CMP_STRIDE // 2):
            parts = []
            for p in range(n_pg):
                a0 = x_refs[p][:, 2 * rp, c].reshape(rows, HEAD_DIM)
                a1 = x_refs[p][:, 2 * rp + 1, c].reshape(rows, HEAD_DIM)
                parts.append(jnp.concatenate([a0, a1], axis=1))
            xs = jnp.concatenate(parts, axis=0).astype(BF16)
            acc = acc + jnp.dot(xs, w_ref[c, rp], preferred_element_type=F32)
        o_ref[c] = acc


def cmp_lohi(pages, page_table, cmp_w1):
    Bn, n_pages = page_table.shape
    n_pg = _pick(n_pages, CMP_PAGES_PER_STEP)
    x = pages.reshape(pages.shape[0], SUBS_PER_PAGE, CMP_STRIDE, 2, N_KV, HEAD_DIM)
    half = CMP_STRIDE // 2
    w_lo = cmp_w1[:, :CMP_STRIDE].reshape(2, half, 2 * HEAD_DIM, HEAD_DIM)
    w_hi = cmp_w1[:, CMP_STRIDE:].reshape(2, half, 2 * HEAD_DIM, HEAD_DIM)
    w = jnp.concatenate([w_lo, w_hi], axis=-1).astype(BF16)
    n_sub = n_pages * SUBS_PER_PAGE

    def x_spec(p):
        return pl.BlockSpec((None, SUBS_PER_PAGE, CMP_STRIDE, 2, N_KV, HEAD_DIM),
                            lambda b, i, pt: (pt[b, i * n_pg + p], 0, 0, 0, 0, 0))

    grid_spec = pltpu.PrefetchScalarGridSpec(
        num_scalar_prefetch=1,
        grid=(Bn, n_pages // n_pg),
        in_specs=[x_spec(p) for p in range(n_pg)]
        + [pl.BlockSpec(w.shape, lambda b, i, pt: (0, 0, 0, 0))],
        out_specs=pl.BlockSpec((None, 2, n_pg * SUBS_PER_PAGE * N_KV, 2 * HEAD_DIM),
                               lambda b, i, pt: (b, 0, i, 0)),
    )
    return pl.pallas_call(
        functools.partial(_cmp_lohi_kernel, n_pg=n_pg),
        grid_spec=grid_spec,
        out_shape=jax.ShapeDtypeStruct((Bn, 2, n_sub * N_KV, 2 * HEAD_DIM), F32),
        compiler_params=_cparams(2),
        name="cmp_lohi",
    )(page_table, *([x] * n_pg), w)


def _compress_kernel(x_ref, pe_ref, w1_ref, b1_ref, w2_ref, o_ref, pe_scr):
    n_rows = x_ref.shape[0]
    half = w1_ref.shape[0] // 2
    c = pl.program_id(1)

    @pl.when(pl.program_id(0) == 0)
    def _():
        pe = jnp.broadcast_to(pe_ref[...], (8, 2 * half))
        pe_scr[c, 0] = jnp.dot(pe[:, :half], w1_ref[:half], preferred_element_type=F32, precision=HIGHEST)
        pe_scr[c, 1] = jnp.dot(pe[:, half:], w1_ref[half:], preferred_element_type=F32, precision=HIGHEST)

    x = x_ref[...]
    lo = x[:, :HEAD_DIM] + pe_scr[c, 0][0:1]
    hi = x[:, HEAD_DIM:] + pe_scr[c, 1][0:1]
    hi_next = pltpu.roll(hi, n_rows - N_KV, 0)
    h = jax.nn.gelu(lo + hi_next + b1_ref[...])
    o_ref[...] = jnp.dot(h.astype(BF16), w2_ref[...].astype(BF16), preferred_element_type=F32).astype(o_ref.dtype)


def compress_blocks(lohi, cmp_pe, cmp_w1, cmp_b1, cmp_w2):
    Bn, _, n_rows, _ = lohi.shape
    kdim = CMP_BLOCK * HEAD_DIM
    out = pl.pallas_call(
        _compress_kernel,
        grid=(Bn, 2),
        in_specs=[pl.BlockSpec((None, None, n_rows, 2 * HEAD_DIM), lambda b, c: (b, c, 0, 0)),
                  pl.BlockSpec((None, 1, kdim), lambda b, c: (c, 0, 0)),
                  pl.BlockSpec((None, kdim, HEAD_DIM), lambda b, c: (c, 0, 0)),
                  pl.BlockSpec((None, 1, HEAD_DIM), lambda b, c: (c, 0, 0)),
                  pl.BlockSpec((None, HEAD_DIM, HEAD_DIM), lambda b, c: (c, 0, 0))],
        out_specs=pl.BlockSpec((None, None, n_rows, HEAD_DIM), lambda b, c: (b, c, 0, 0)),
        out_shape=jax.ShapeDtypeStruct((Bn, 2, n_rows, HEAD_DIM), BF16),
        scratch_shapes=[pltpu.VMEM((2, 2, 8, HEAD_DIM), F32)],
        compiler_params=_cparams(2),
        name="compress",
    )(lohi, cmp_pe.reshape(2, 1, kdim), cmp_w1.reshape(2, kdim, HEAD_DIM),
      cmp_b1.reshape(2, 1, HEAD_DIM), cmp_w2)
    return out.reshape(Bn, 2, n_rows // N_KV, N_KV, HEAD_DIM).transpose(0, 1, 3, 2, 4)


N_CAND = V7X_LANES
NEG_TAKEN = -3e38


def _topk_mask(score, axis, k_sel):
    idx = lax.broadcasted_iota(jnp.int32, score.shape, axis)
    sel = jnp.zeros(score.shape, F32)
    for _ in range(k_sel):
        m = jnp.max(score, axis=axis, keepdims=True)
        first = jnp.min(jnp.where(score == m, idx, N_CAND), axis=axis, keepdims=True)
        hit = idx == first
        sel = jnp.where(hit & (m > 0.5 * NEG), 1.0, sel)
        score = jnp.where(hit, NEG_TAKEN, score)
    return sel


def _cmp_attn_kernel(slope_ref, q_ref, kc_ref, vc_ref, gate_ref, wsel_ref, oc_ref, sel_ref, *,
                     tq, hpg, gps, pos0, pos_step, transposed, k_sel, n_cand):
    i = pl.program_id(1)
    n_cmp = kc_ref.shape[1]
    gw = hpg * HEAD_DIM
    scale = HEAD_DIM ** -0.5
    base = pos0 + i * pos_step
    t_idx = lax.broadcasted_iota(jnp.int32, (tq, n_cmp), 0)
    n_idx = lax.broadcasted_iota(jnp.int32, (tq, n_cmp), 1)
    dist_i = base + t_idx - (n_idx * CMP_STRIDE + (CMP_BLOCK - 1))
    valid = dist_i >= 0
    row_seen = base + lax.broadcasted_iota(jnp.int32, (tq, 1), 0) >= CMP_BLOCK - 1
    dist = dist_i.astype(F32)
    gates = gate_ref[...]
    shape, j_ax, t_ax = ((N_CAND, tq), 0, 1) if transposed else ((tq, N_CAND), 1, 0)
    j = lax.broadcasted_iota(jnp.int32, shape, j_ax)
    blk = (base + lax.broadcasted_iota(jnp.int32, shape, t_ax)) // SEL_BLOCK
    forced = (j == 0) | (j == blk) | (j == blk - 1)
    visible = (j <= blk) & (j < n_cand)
    scores = []
    for gi in range(gps):
        g = pl.program_id(2) * gps + gi
        q = q_ref[:, gi * gw:(gi + 1) * gw]
        qs = jnp.concatenate([q[:, h * HEAD_DIM:(h + 1) * HEAD_DIM] for h in range(hpg)], axis=0)
        qs = (qs * (scale * LOG2E)).astype(BF16)
        s = lax.dot_general(qs, kc_ref[gi], (((1,), (1,)), ((), ())), preferred_element_type=F32)
        vc = vc_ref[gi]
        psum = jnp.zeros((tq, n_cmp), F32)
        for h in range(hpg):
            sh = s[h * tq:(h + 1) * tq] - (slope_ref[g * hpg + h] * LOG2E) * dist
            sh = jnp.where(valid, sh, NEG)
            m = jnp.max(sh, axis=-1, keepdims=True)
            e = jnp.exp2(sh - m)
            inv = jnp.where(row_seen, 1.0 / jnp.maximum(jnp.sum(e, axis=-1, keepdims=True), 1e-30), 0.0)
            p = e * inv
            psum = psum + p
            o_h = jnp.dot(p.astype(BF16), vc, preferred_element_type=F32)
            c0 = gi * gw + h * HEAD_DIM
            gc = gi * GATE_LANES + 3 * h
            oc_ref[:, c0:c0 + HEAD_DIM] = o_h * gates[:, gc:gc + 1]
        if transposed:
            imp = lax.dot_general(wsel_ref[...], psum, (((1,), (1,)), ((), ())),
                                  preferred_element_type=F32, precision=HIGHEST)
        else:
            imp = lax.dot_general(psum, wsel_ref[...], (((1,), (1,)), ((), ())),
                                  preferred_element_type=F32, precision=HIGHEST)
        scores.append(jnp.where(visible, imp + jnp.where(forced, FORCE_BONUS, 0.0), NEG))
    if transposed:
        for gi in range(gps):
            sel = _topk_mask(scores[gi], 0, k_sel).T
            sel_ref[:, gi * N_CAND:(gi + 1) * N_CAND] = sel.astype(sel_ref.dtype)
    else:
        sel = _topk_mask(jnp.concatenate(scores, axis=0), 1, k_sel)
        for gi in range(gps):
            sel_ref[:, gi * N_CAND:(gi + 1) * N_CAND] = sel[gi * tq:(gi + 1) * tq].astype(sel_ref.dtype)


def _sel_weights(n_cmp_pad, n_cmp):
    j = np.arange(N_CAND)[:, None]
    n = np.arange(n_cmp_pad)[None, :]
    w = (n >= SUBS_PER_SEL * j - 1) & (n <= SUBS_PER_SEL * j + SUBS_PER_SEL - 1) & (n < n_cmp)
    return jnp.asarray(w.astype(np.float32))


def cmp_attention(q, kcvc, gates, *, tq, pos0, pos_step, transposed, k_sel, n_cand):
    Bn, T, D = q.shape
    n_heads = D // HEAD_DIM
    hpg = n_heads // N_KV
    n_sub = kcvc.shape[3]
    gw = hpg * HEAD_DIM
    slopes = alibi_slopes(n_heads).reshape(-1)
    wsel = _sel_weights(n_sub, n_sub - 1)
    gps = N_KV if tq < QBLK else 1
    kern = functools.partial(_cmp_attn_kernel, tq=tq, hpg=hpg, gps=gps, pos0=pos0, pos_step=pos_step,
                             transposed=transposed, k_sel=k_sel, n_cand=n_cand)
    return pl.pallas_call(
        kern,
        grid=(Bn, T // tq, N_KV // gps),
        in_specs=[pl.BlockSpec(memory_space=pltpu.SMEM),
                  pl.BlockSpec((None, tq, gps * gw), lambda b, i, g: (b, i, g)),
                  pl.BlockSpec((None, None, gps, n_sub, HEAD_DIM), lambda b, i, g: (b, 0, g, 0, 0)),
                  pl.BlockSpec((None, None, gps, n_sub, HEAD_DIM), lambda b, i, g: (b, 1, g, 0, 0)),
                  pl.BlockSpec((None, tq, gps * GATE_LANES), lambda b, i, g: (b, i, g)),
                  pl.BlockSpec((N_CAND, n_sub), lambda b, i, g: (0, 0))],
        out_specs=[pl.BlockSpec((None, tq, gps * gw), lambda b, i, g: (b, i, g)),
                   pl.BlockSpec((None, tq, gps * N_CAND), lambda b, i, g: (b, i, g))],
        out_shape=[jax.ShapeDtypeStruct((Bn, T, D), F32),
                   jax.ShapeDtypeStruct((Bn, T, N_KV * N_CAND), BF16)],
        compiler_params=_cparams(3),
        name="cmp_attn",
    )(slopes, q, kcvc, kcvc, gates, wsel)


SEL_TK = 512
SEL_TQ = 256
MASK_BIG = 1e30


def _sel_win_kernel(slope_ref, q_ref, ks_ref, vs_ref, kw_ref, vw_ref, sel_ref, oc_ref, gate_ref, e_ref,
                    o_ref, ksb, vsb, kwb, vwb, *, tq, hpg):
    g = pl.program_id(0)
    i = pl.program_id(1)
    L = ks_ref.shape[0]
    s0 = i * tq
    scale = HEAD_DIM ** -0.5

    @pl.when(i == 0)
    def _():
        ksb[:, :HEAD_DIM] = ks_ref[...].astype(BF16)
        ksb[:, HEAD_DIM:] = e_ref[...]
        vsb[...] = vs_ref[...].astype(BF16)
        kwb[0:WINDOW] = jnp.zeros((WINDOW, HEAD_DIM), BF16)
        vwb[0:WINDOW] = jnp.zeros((WINDOW, HEAD_DIM), BF16)
        kwb[WINDOW:] = kw_ref[...].astype(BF16)
        vwb[WINDOW:] = vw_ref[...].astype(BF16)

    q = q_ref[...]
    qs = jnp.concatenate([q[:, h * HEAD_DIM:(h + 1) * HEAD_DIM] for h in range(hpg)], axis=0)
    qs = (qs * (scale * LOG2E)).astype(BF16)
    slopes = [slope_ref[g * hpg + h] * LOG2E for h in range(hpg)]
    neg_sel = sel_ref[...] - 1.0
    qaug = jnp.concatenate([qs, jnp.concatenate([neg_sel] * hpg, axis=0)], axis=1)
    n_tiles = s0 // SEL_TK + 1

    def sweep(k0, carry, width, diagonal):
        ms, ls, accs = carry
        k0 = pl.multiple_of(k0, SEL_TK)
        v_t = vsb[pl.ds(k0, width), :]
        if diagonal:
            t_idx = lax.broadcasted_iota(jnp.int32, (tq, width), 0)
            c_idx = lax.broadcasted_iota(jnp.int32, (tq, width), 1)
            causal = s0 + t_idx >= k0 + c_idx
        s = lax.dot_general(qaug, ksb[pl.ds(k0, width), :], (((1,), (1,)), ((), ())),
                            preferred_element_type=F32)
        rel = (k0 - s0 + lax.broadcasted_iota(jnp.int32, (1, width), 1)).astype(F32)
        new_m, new_l, new_acc = [], [], []
        for h in range(hpg):
            sh = s[h * tq:(h + 1) * tq] + slopes[h] * rel
            if diagonal:
                sh = jnp.where(causal, sh, -MASK_BIG)
            m_new = jnp.maximum(ms[h], jnp.max(sh, axis=-1, keepdims=True))
            p = jnp.exp2(sh - m_new)
            alpha = jnp.exp2(ms[h] - m_new)
            new_l.append(alpha * ls[h] + jnp.sum(p, axis=-1, keepdims=True))
            new_acc.append(alpha * accs[h] + jnp.dot(p.astype(BF16), v_t, preferred_element_type=F32))
            new_m.append(m_new)
        return tuple(new_m), tuple(new_l), tuple(new_acc)

    init = ((jnp.full((tq, 1), NEG, F32),) * hpg, (jnp.zeros((tq, 1), F32),) * hpg,
            (jnp.zeros((tq, HEAD_DIM), F32),) * hpg)
    n_pairs = (n_tiles - 1) // 2
    carry = lax.fori_loop(0, n_pairs, lambda i2, c: sweep(i2 * (2 * SEL_TK), c, 2 * SEL_TK, False), init)
    k_last = n_pairs * (2 * SEL_TK)
    ms, ls, accs = lax.cond(n_tiles % 2 == 0,
                            lambda c: sweep(k_last, c, 2 * SEL_TK, True),
                            lambda c: sweep(k_last, c, SEL_TK, True), carry)

    wn = WINDOW + tq
    w0 = pl.multiple_of(s0, tq)
    kw_t = kwb[pl.ds(w0, wn), :]
    vw_t = vwb[pl.ds(w0, wn), :]
    sw = lax.dot_general(qs, kw_t, (((1,), (1,)), ((), ())), preferred_element_type=F32)
    t_idx = lax.broadcasted_iota(jnp.int32, (tq, wn), 0)
    c_idx = lax.broadcasted_iota(jnp.int32, (tq, wn), 1)
    dist_i = t_idx + WINDOW - c_idx
    valid = (dist_i >= 0) & (dist_i < WINDOW) & (c_idx + s0 >= WINDOW)
    dist = jnp.where(valid, dist_i.astype(F32), MASK_BIG)
    gates = gate_ref[...]
    oc = oc_ref[...]
    for h in range(hpg):
        sh = sw[h * tq:(h + 1) * tq] - slopes[h] * dist
        m = jnp.max(sh, axis=-1, keepdims=True)
        e = jnp.exp2(sh - m)
        o_w = (jnp.dot(e.astype(BF16), vw_t, preferred_element_type=F32)
               / jnp.maximum(jnp.sum(e, axis=-1, keepdims=True), 1e-30))
        o_s = accs[h] / jnp.maximum(ls[h], 1e-30)
        out = (oc[:, h * HEAD_DIM:(h + 1) * HEAD_DIM] + gates[:, 3 * h + 1:3 * h + 2] * o_s
               + gates[:, 3 * h + 2:3 * h + 3] * o_w)
        o_ref[:, h * HEAD_DIM:(h + 1) * HEAD_DIM] = out.astype(o_ref.dtype)


def _block_onehot(n_keys):
    key = np.arange(n_keys)[:, None]
    j = np.arange(N_CAND)[None, :]
    return jnp.asarray((key // SEL_BLOCK == j).astype(np.float32) * MASK_BIG, dtype=BF16)


def sel_win_attention(q, kv, sel, oc, gates, *, tq):
    _, L, D = q.shape
    n_heads = D // HEAD_DIM
    hpg = n_heads // N_KV
    gw = hpg * HEAD_DIM
    assert L % SEL_TK == 0 and SEL_TK % tq == 0 and L // SEL_BLOCK <= N_CAND
    slopes = alibi_slopes(n_heads).reshape(-1)
    e = _block_onehot(L)

    def kv_spec(branch, which):
        cb = (branch * 2 + which) * N_KV
        return pl.BlockSpec((L, HEAD_DIM), lambda g, i: (0, cb + g), pipeline_mode=pl.Buffered(1))

    return pl.pallas_call(
        functools.partial(_sel_win_kernel, tq=tq, hpg=hpg),
        grid=(N_KV, L // tq),
        in_specs=[pl.BlockSpec(memory_space=pltpu.SMEM),
                  pl.BlockSpec((None, tq, gw), lambda g, i: (0, i, g)),
                  kv_spec(1, 0), kv_spec(1, 1), kv_spec(2, 0), kv_spec(2, 1),
                  pl.BlockSpec((None, tq, N_CAND), lambda g, i: (0, i, g)),
                  pl.BlockSpec((None, tq, gw), lambda g, i: (0, i, g)),
                  pl.BlockSpec((None, tq, GATE_LANES), lambda g, i: (0, i, g)),
                  pl.BlockSpec(e.shape, lambda g, i: (0, 0))],
        out_specs=pl.BlockSpec((tq, gw), lambda g, i: (i, g)),
        out_shape=jax.ShapeDtypeStruct((L, D), BF16),
        scratch_shapes=[pltpu.VMEM((L, 2 * HEAD_DIM), BF16), pltpu.VMEM((L, HEAD_DIM), BF16),
                        pltpu.VMEM((WINDOW + L, HEAD_DIM), BF16), pltpu.VMEM((WINDOW + L, HEAD_DIM), BF16)],
        compiler_params=_cparams(2),
        name="sel_win_attn",
    )(slopes, q, kv, kv, kv, kv, sel, oc, gates, e)


def prompt_nsa(q, gates, kv, kcvc):
    L = q.shape[1]
    n_sel = L // SEL_BLOCK
    oc, sel = cmp_attention(q, kcvc, gates, tq=QBLK, pos0=0, pos_step=QBLK, transposed=True,
                            k_sel=min(TOPK, n_sel), n_cand=n_sel)
    return sel_win_attention(q, kv, sel, oc, gates, tq=SEL_TQ)


SAMPLE_TQ = 8
SAMPLE_PAGES_PER_STEP = 8
NEW_ROWS_PAD = V7X_LANES


def _sample_sel_win_kernel(pt_ref, slope_ref, q_ref, sel_ref, oc_ref, gate_ref, e_ref, *refs,
                           n_pg, hpg, t_real, n_steps, past_len, win_len):
    pages = refs[:n_pg]
    nslc_ref, nwin_ref, cwk_ref, cwv_ref, o_ref, m_scr, l_scr, acc_scr = refs[n_pg:]
    j = pl.program_id(1)
    tq = SAMPLE_TQ
    tk = n_pg * PAGE_SIZE
    scale = HEAD_DIM ** -0.5

    @pl.when(j == 0)
    def _():
        m_scr[...] = jnp.full(m_scr.shape, NEG, F32)
        l_scr[...] = jnp.zeros(l_scr.shape, F32)
        acc_scr[...] = jnp.zeros(acc_scr.shape, F32)

    q = q_ref[...]
    neg_sel = sel_ref[...] - 1.0

    def heads_of(g):
        gw = hpg * HEAD_DIM
        qg = q[:, g * gw:(g + 1) * gw]
        qs = jnp.concatenate([qg[:, h * HEAD_DIM:(h + 1) * HEAD_DIM] for h in range(hpg)], axis=0)
        return (qs * scale).astype(BF16)

    def online(g, h, sh, v):
        r0 = h * tq
        m_old = m_scr[g, r0:r0 + tq]
        m_new = jnp.maximum(m_old, jnp.max(sh, axis=-1, keepdims=True))
        p = jnp.exp(sh - m_new)
        alpha = jnp.exp(m_old - m_new)
        l_scr[g, r0:r0 + tq] = alpha * l_scr[g, r0:r0 + tq] + jnp.sum(p, axis=-1, keepdims=True)
        acc_scr[g, r0:r0 + tq] = alpha * acc_scr[g, r0:r0 + tq] + jnp.dot(p.astype(BF16), v,
                                                                         preferred_element_type=F32)
        m_scr[g, r0:r0 + tq] = m_new

    rows = hpg * tq
    n_rows = N_KV * rows
    n_lane = tk * N_KV
    kf = jnp.concatenate([pg[:, 0].reshape(PAGE_SIZE * N_KV, HEAD_DIM) for pg in pages], axis=0).astype(BF16)
    vf = jnp.concatenate([pg[:, 1].reshape(PAGE_SIZE * N_KV, HEAD_DIM) for pg in pages], axis=0).astype(BF16)
    kaug = jnp.concatenate([kf, e_ref[...]], axis=1)
    qaug = jnp.concatenate(
        [jnp.concatenate([heads_of(g) for g in range(N_KV)], axis=0),
         jnp.concatenate([neg_sel[:, g * N_CAND:(g + 1) * N_CAND] for g in range(N_KV) for _ in range(hpg)],
                         axis=0)], axis=1)
    s = lax.dot_general(qaug, kaug, (((1,), (1,)), ((), ())), preferred_element_type=F32)
    lane = lax.broadcasted_iota(jnp.int32, (1, n_lane), 1)
    rel = (j * tk - past_len + lane // N_KV).astype(F32)
    slope_col = jnp.concatenate([jnp.full((tq, 1), slope_ref[gh], F32) for gh in range(N_KV * hpg)], axis=0)
    row_g = lax.broadcasted_iota(jnp.int32, (n_rows, 1), 0) // rows
    sh = jnp.where(lane % N_KV == row_g, s + slope_col * rel, -MASK_BIG)
    m_old = m_scr[...].reshape(n_rows, 1)
    m_new = jnp.maximum(m_old, jnp.max(sh, axis=-1, keepdims=True))
    p = jnp.exp(sh - m_new)
    alpha = jnp.exp(m_old - m_new)
    l_new = alpha * l_scr[...].reshape(n_rows, 1) + jnp.sum(p, axis=-1, keepdims=True)
    acc_new = alpha * acc_scr[...].reshape(n_rows, HEAD_DIM) + jnp.dot(p.astype(BF16), vf,
                                                                    preferred_element_type=F32)
    m_scr[...] = m_new.reshape(m_scr.shape)
    l_scr[...] = l_new.reshape(l_scr.shape)
    acc_scr[...] = acc_new.reshape(acc_scr.shape)

    @pl.when(j == n_steps - 1)
    def _():
        gates = gate_ref[...]
        oc = oc_ref[...]
        t_n = lax.broadcasted_iota(jnp.int32, (tq, NEW_ROWS_PAD), 0)
        r_n = lax.broadcasted_iota(jnp.int32, (tq, NEW_ROWS_PAD), 1)
        valid_n = (r_n <= t_n) & (r_n < t_real)
        wn = win_len + NEW_ROWS_PAD
        t_w = lax.broadcasted_iota(jnp.int32, (tq, wn), 0)
        c_w = lax.broadcasted_iota(jnp.int32, (tq, wn), 1)
        dist_w = jnp.where(c_w < win_len, t_w + win_len - c_w, t_w - (c_w - win_len))
        cached = c_w < win_len
        valid_w = ((cached & (dist_w < WINDOW) & (c_w + past_len - win_len >= 0))
                   | ((c_w >= win_len) & (dist_w >= 0) & (c_w - win_len < t_real)))
        dist_wf = dist_w.astype(F32)
        for g in range(N_KV):
            kc0 = g * HEAD_DIM
            vc0 = (N_KV + g) * HEAD_DIM
            qs = heads_of(g)
            k_n = nslc_ref[:, kc0:kc0 + HEAD_DIM].astype(BF16)
            v_n = nslc_ref[:, vc0:vc0 + HEAD_DIM].astype(BF16)
            s_n = lax.dot_general(qs, k_n, (((1,), (1,)), ((), ())), preferred_element_type=F32)
            k_w = jnp.concatenate([cwk_ref[:, g, :], nwin_ref[:, kc0:kc0 + HEAD_DIM]], axis=0).astype(BF16)
            v_w = jnp.concatenate([cwv_ref[:, g, :], nwin_ref[:, vc0:vc0 + HEAD_DIM]], axis=0).astype(BF16)
            s_w = lax.dot_general(qs, k_w, (((1,), (1,)), ((), ())), preferred_element_type=F32)
            for h in range(hpg):
                slope = slope_ref[g * hpg + h]
                r0 = h * tq
                sh = jnp.where(valid_n, s_n[r0:r0 + tq] + slope * r_n.astype(F32), -MASK_BIG)
                online(g, h, sh, v_n)
                o_s = acc_scr[g, r0:r0 + tq] / jnp.maximum(l_scr[g, r0:r0 + tq], 1e-30)
                sw = jnp.where(valid_w, s_w[r0:r0 + tq] - slope * dist_wf, NEG)
                m = jnp.max(sw, axis=-1, keepdims=True)
                e = jnp.where(valid_w, jnp.exp(sw - m), 0.0)
                p = e / jnp.maximum(jnp.sum(e, axis=-1, keepdims=True), 1e-30)
                o_w = jnp.dot(p.astype(BF16), v_w, preferred_element_type=F32)
                c0 = (g * hpg + h) * HEAD_DIM
                gc = g * GATE_LANES + 3 * h
                out = (oc[:, c0:c0 + HEAD_DIM] + gates[:, gc + 1:gc + 2] * o_s + gates[:, gc + 2:gc + 3] * o_w)
                o_ref[:, c0:c0 + HEAD_DIM] = out.astype(o_ref.dtype)


def _block_onehot_rows(n_keys):
    key = np.repeat(np.arange(n_keys), N_KV)[:, None]
    j = np.arange(N_CAND)[None, :]
    return jnp.asarray((key // SEL_BLOCK == j).astype(np.float32) * MASK_BIG, dtype=BF16)


def sample_sel_win_attention(q, sel, oc, gates, cache_slc, page_table, new_slc, new_win, cache_win, t_real):
    Bn, tq, D = q.shape
    n_heads = D // HEAD_DIM
    hpg = n_heads // N_KV
    n_pages = page_table.shape[1]
    n_pg = _pick(n_pages, SAMPLE_PAGES_PER_STEP)
    n_steps = n_pages // n_pg
    past_len = n_pages * PAGE_SIZE
    win_len = cache_win.shape[1]
    assert past_len // SEL_BLOCK <= N_CAND and past_len % SEL_BLOCK == 0 and t_real <= min(tq, SEL_BLOCK)
    slopes = alibi_slopes(n_heads).reshape(-1)
    e = _block_onehot_rows(past_len)
    e_rows = n_pg * PAGE_SIZE * N_KV
    rows = hpg * tq

    def pg_spec(p):
        return pl.BlockSpec((None, PAGE_SIZE, 2, N_KV, HEAD_DIM),
                            lambda b, j, pt: (pt[b, j * n_pg + p], 0, 0, 0, 0))

    def win_spec(c):
        return pl.BlockSpec((None, win_len, None, N_KV, HEAD_DIM), lambda b, j, pt: (b, 0, c, 0, 0))

    per_b = lambda shape: pl.BlockSpec((None,) + shape, lambda b, j, pt: (b, 0, 0))
    grid_spec = pltpu.PrefetchScalarGridSpec(
        num_scalar_prefetch=1,
        grid=(Bn, n_steps),
        in_specs=[pl.BlockSpec(memory_space=pltpu.SMEM),
                  per_b((tq, D)), per_b((tq, N_KV * N_CAND)), per_b((tq, D)), per_b((tq, N_KV * GATE_LANES)),
                  pl.BlockSpec((e_rows, N_CAND), lambda b, j, pt: (j, 0))]
        + [pg_spec(p) for p in range(n_pg)]
        + [per_b((NEW_ROWS_PAD, KV_COLS)), per_b((NEW_ROWS_PAD, KV_COLS)), win_spec(0), win_spec(1)],
        out_specs=per_b((tq, D)),
        scratch_shapes=[pltpu.VMEM((N_KV, rows, 1), F32), pltpu.VMEM((N_KV, rows, 1), F32),
                        pltpu.VMEM((N_KV, rows, HEAD_DIM), F32)],
    )
    return pl.pallas_call(
        functools.partial(_sample_sel_win_kernel, n_pg=n_pg, hpg=hpg, t_real=t_real, n_steps=n_steps,
                          past_len=past_len, win_len=win_len),
        grid_spec=grid_spec,
        out_shape=jax.ShapeDtypeStruct((Bn, tq, D), BF16),
        compiler_params=_cparams(2),
        name="sample_sel_win_attn",
    )(page_table, slopes, q, sel, oc, gates, e, *([cache_slc] * n_pg), new_slc, new_win, cache_win, cache_win)


def sample_nsa(q2d, gates_pad, kv_rows, kcvc, cache_slc, cache_win, page_table):
    _, kv_slc, kv_win = kv_rows
    Bn, T = kv_slc.shape[:2]
    D = q2d.shape[1]
    past_len = page_table.shape[1] * PAGE_SIZE
    n_past_blk = past_len // SEL_BLOCK
    pad_t = lambda a: jnp.pad(a.reshape(Bn, T, -1), ((0, 0), (0, SAMPLE_TQ - T), (0, 0)))
    q = pad_t(q2d)
    gates = pad_t(gates_pad)
    oc, sel = cmp_attention(q, kcvc, gates, tq=SAMPLE_TQ, pos0=past_len, pos_step=0, transposed=False,
                            k_sel=min(TOPK, n_past_blk + 1) - 1, n_cand=n_past_blk)
    pad_rows = lambda a: jnp.pad(a.reshape(Bn, T, KV_COLS), ((0, 0), (0, NEW_ROWS_PAD - T), (0, 0)))
    o = sample_sel_win_attention(q, sel, oc, gates, cache_slc, page_table,
                                 pad_rows(kv_slc), pad_rows(kv_win), cache_win, T)
    return o[:, :T].reshape(Bn * T, D)


def sample_attention_paged(kv_rows, cache_cmp, cache_slc, cache_win, page_table, cmp_pe, cmp_w1, cmp_b1, cmp_w2):
    T = kv_rows[0].shape[1]
    assert T < CMP_STRIDE, "new rows never complete a compression sub-block"
    kcvc = compress_blocks(cmp_lohi(cache_cmp, page_table, cmp_w1), cmp_pe, cmp_w1, cmp_b1, cmp_w2)

    def attend(q2d, gates_pad):
        return sample_nsa(q2d, gates_pad, kv_rows, kcvc, cache_slc, cache_win, page_table)

    return attend


def alibi_slopes(n_heads):
    exps = np.arange(1, n_heads + 1, dtype=np.float32) * np.float32(-8.0 / n_heads)
    return jnp.asarray(np.exp2(exps), dtype=F32).reshape(N_KV, n_heads // N_KV)


def prompt_attention(kv2d, kv_rows, cmp_pe, cmp_w1, cmp_b1, cmp_w2):
    kv_cmp = kv_rows[0]
    L = kv_cmp.shape[1]
    pages = kv_cmp.reshape(L // PAGE_SIZE, PAGE_SIZE, 2, N_KV, HEAD_DIM)
    table = jnp.arange(L // PAGE_SIZE, dtype=jnp.int32)[None]
    kcvc = compress_blocks(cmp_lohi(pages, table, cmp_w1), cmp_pe, cmp_w1, cmp_b1, cmp_w2)

    def attend(q2d, gates_pad):
        return prompt_nsa(q2d[None], gates_pad[None], kv2d, kcvc)

    return attend


def _rows(v, per_tok):
    return v if v.shape[0] == 1 else jnp.repeat(v, per_tok, axis=0)


def trunk(x, mods, kv_mod, h0, make_attend, p):
    Bn, L, D = x.shape
    M = Bn * L
    n_heads = D // HEAD_DIM
    hpg = n_heads // N_KV
    depth = p["mod_w"].shape[0]
    n_a = depth // 2
    xr = x.reshape(M, D)
    new_h = []
    kv_rows = None
    attend = None
    layer_mods = [[_rows(m, L) for m in jnp.split(mods[l], 6, axis=-1)] for l in range(depth)]
    kv_shift, kv_scale = [_rows(m, L) for m in jnp.split(kv_mod, 2, axis=-1)]

    def mixer_dtype(l):
        return F32 if l < n_a else BF16

    h_mix = norm_mod(xr, p["norm_pre"][0, 0], layer_mods[0][1], layer_mods[0][0], mixer_dtype(0))
    hk = norm_mod(xr, p["kv_norm"], kv_scale, kv_shift, BF16) if n_a == 0 else None
    for l in range(depth):
        sh1, sc1, ga1, sh2, sc2, ga2 = layer_mods[l]
        mlp_pre = [(p["norm_pre"][l, 1], sc2, sh2, BF16)]
        after_mlp = []
        if l + 1 < depth:
            nsh1, nsc1 = layer_mods[l + 1][0], layer_mods[l + 1][1]
            after_mlp.append((p["norm_pre"][l + 1, 0], nsc1, nsh1, mixer_dtype(l + 1)))
            if l + 1 == n_a:
                after_mlp.append((p["kv_norm"], kv_scale, kv_shift, BF16))
        if l == n_a:
            kv2d = mm_wide(hk, p["w_kv"], tn=1024)
            kv = kv2d.reshape(Bn, L, 3, 2, N_KV, HEAD_DIM)
            kv_rows = (kv[:, :, 0], kv[:, :, 1], kv[:, :, 2])
            attend = make_attend(kv2d, kv_rows)
        if l < n_a:
            u = h_mix
            gy, h_last = s5_mixer_core(u.reshape(Bn, L, D), h0[l], p["ssm_lam_re"][l], p["ssm_lam_im"][l],
                                       p["ssm_log_dt"][l], p["ssm_b_re"][l], p["ssm_b_im"][l],
                                       p["ssm_c_re"][l], p["ssm_c_im"][l], p["ssm_d"][l])
            new_h.append(h_last)
            xr, h = mm_tall(gy, p["ssm_w_glu"], xr, p["norm_post"][l, 0], ga1, layer=l, glu=True, tk=D,
                            next_norms=mlp_pre)
        else:
            lb = l - n_a
            q = mm_wide(h_mix, p["nsa_w_qg"], layer=lb, n_out=n_heads * HEAD_DIM, tn=1024)
            gates_pad = mm_wide(h_mix, p["w_gate_pad"], layer=lb, epilogue="sigmoid", tn=512)
            o = attend(q, gates_pad)
            xr, h = mm_tall(o, p["nsa_w_o"], xr, p["norm_post"][l, 0], ga1, layer=lb, tk=D, next_norms=mlp_pre)
        f = mm_wide(h, p["mlp_w1"], layer=l, epilogue="sqrelu", out_dtype=BF16, tn=1024)
        outs = mm_tall(f, p["mlp_w2"], xr, p["norm_post"][l, 1], ga2, layer=l, tk=1024, next_norms=after_mlp)
        if after_mlp:
            xr, h_mix = outs[0], outs[1]
            if l + 1 == n_a:
                hk = outs[2]
        else:
            xr = outs
    return xr.reshape(Bn, L, D), jnp.stack(new_h), kv_rows


def kernel(x_prompt, x_sample, c_prompt, c_sample, state_ssm, cache_cmp, cache_slc, cache_win, page_table, mod_w, mod_b, norm_pre, norm_post, mlp_w1, mlp_w2, ssm_lam_re, ssm_lam_im, ssm_log_dt, ssm_b_re, ssm_b_im, ssm_c_re, ssm_c_im, ssm_d, ssm_w_glu, kv_norm, kv_mod_w, kv_mod_b, w_kv, cmp_pe, cmp_w1, cmp_b1, cmp_w2, nsa_w_qg, nsa_w_o):
    D = x_prompt.shape[-1]
    depth = mod_w.shape[0]
    n_heads = D // HEAD_DIM
    bp, bs = c_prompt.shape[0], c_sample.shape[0]
    c_all = jnp.concatenate([c_prompt, c_sample], axis=0)
    n_c = c_all.shape[0]
    c_all = jnp.pad(c_all, ((0, -n_c % 8), (0, 0)))
    mods = [mm_wide(c_all, mod_w, layer=l, bias=mod_b, prologue="silu", exact=True, tn=512) for l in range(depth)]
    kv_mod = mm_wide(c_all, kv_mod_w, bias=kv_mod_b, prologue="silu", exact=True, tn=512)
    hpg = n_heads // N_KV
    w_gate = nsa_w_qg[:, :, n_heads * HEAD_DIM:].reshape(nsa_w_qg.shape[0], D, N_KV, 3 * hpg)
    w_gate_pad = jnp.pad(w_gate, ((0, 0), (0, 0), (0, 0), (0, GATE_LANES - 3 * hpg)))
    w_gate_pad = w_gate_pad.reshape(-1, D, N_KV * GATE_LANES)
    p = dict(mod_w=mod_w, norm_pre=norm_pre, norm_post=norm_post, mlp_w1=mlp_w1, mlp_w2=mlp_w2.astype(BF16),
             ssm_lam_re=ssm_lam_re, ssm_lam_im=ssm_lam_im, ssm_log_dt=ssm_log_dt, ssm_b_re=ssm_b_re,
             ssm_b_im=ssm_b_im, ssm_c_re=ssm_c_re, ssm_c_im=ssm_c_im, ssm_d=ssm_d,
             ssm_w_glu=ssm_w_glu.astype(BF16), kv_norm=kv_norm, w_kv=w_kv, nsa_w_qg=nsa_w_qg,
             nsa_w_o=nsa_w_o.astype(BF16), w_gate_pad=w_gate_pad)

    def make_prompt(kv2d, kv_rows):
        return prompt_attention(kv2d, kv_rows, cmp_pe, cmp_w1, cmp_b1, cmp_w2)

    def make_sample(kv2d, kv_rows):
        return sample_attention_paged(kv_rows, cache_cmp, cache_slc, cache_win, page_table,
                                      cmp_pe, cmp_w1, cmp_b1, cmp_w2)

    n_a = depth // 2
    G = D // SSM_GROUP
    h0_prompt = jnp.zeros((n_a, bp, G, STATE_DIM, 2), F32)
    y_prompt, ssm_prompt, rows_prompt = trunk(
        x_prompt, [m[:bp] for m in mods], kv_mod[:bp], h0_prompt, make_prompt, p)
    y_sample, ssm_sample, rows_sample = trunk(
        x_sample, [m[bp:bp + bs] for m in mods], kv_mod[bp:bp + bs], state_ssm, make_sample, p)
    cmp_prompt, slc_prompt, win_rows_prompt = rows_prompt
    cmp_sample, slc_sample, win_sample = rows_sample
    win_prompt = win_rows_prompt[:, -min(WINDOW, x_prompt.shape[1]):]
    return (y_prompt, y_sample, ssm_prompt, ssm_sample, cmp_prompt, cmp_sample,
            slc_prompt, slc_sample, win_prompt, win_sample)
```

```python
import functools
import math

import jax
import jax.numpy as jnp
import numpy as np
from jax import lax
from jax.experimental import pallas as pl
from jax.experimental.pallas import tpu as pltpu

F32 = jnp.float32
BF16 = jnp.bfloat16

SSM_GROUP = 16
STATE_DIM = 64
HEAD_DIM = 128
N_KV = 4
CMP_STRIDE = 16
CMP_BLOCK = 2 * CMP_STRIDE
SEL_BLOCK = 64
SUBS_PER_SEL = SEL_BLOCK // CMP_STRIDE
TOPK = 16
WINDOW = 512
QBLK = 128
PAGE_SIZE = 128
EPS = 1e-6
NEG = -1e30
FORCE_BONUS = 1e4

V7X_LANES = 128
GATE_LANES = V7X_LANES
V7X_VMEM_LIMIT_BYTES = 56 * 1024 * 1024
HIGHEST = lax.Precision.HIGHEST
LOG2E = math.log2(math.e)


def _cparams(n_axes):
    return pltpu.CompilerParams(dimension_semantics=("arbitrary",) * n_axes,
                                vmem_limit_bytes=V7X_VMEM_LIMIT_BYTES)


def _pick(n, pref):
    if n <= pref:
        return n
    t = pref
    while n % t:
        t //= 2
    return t


def _norm_mod_kernel(x_ref, g_ref, sc_ref, sh_ref, o_ref):
    x = x_ref[...]
    r = lax.rsqrt(jnp.mean(x * x, axis=-1, keepdims=True) + EPS)
    y = (x * r) * g_ref[...]
    o_ref[...] = (y * (1.0 + sc_ref[...]) + sh_ref[...]).astype(o_ref.dtype)


def norm_mod(x, g, scale, shift, out_dtype):
    M, D = x.shape
    tm = _pick(M, 512)
    per_row = scale.shape[0] != 1
    mod_spec = pl.BlockSpec((tm, D), lambda i: (i, 0)) if per_row else pl.BlockSpec((1, D), lambda i: (0, 0))
    return pl.pallas_call(
        _norm_mod_kernel,
        grid=(M // tm,),
        in_specs=[pl.BlockSpec((tm, D), lambda i: (i, 0)),
                  pl.BlockSpec((1, D), lambda i: (0, 0)),
                  mod_spec, mod_spec],
        out_specs=pl.BlockSpec((tm, D), lambda i: (i, 0)),
        out_shape=jax.ShapeDtypeStruct((M, D), out_dtype),
        compiler_params=_cparams(1),
        name="norm_mod",
    )(x, g.reshape(1, D), scale, shift)


def _mm_wide_kernel(*refs, prologue, epilogue, has_bias, exact, n_w):
    a_ref = refs[0]
    w_refs = refs[1:1 + n_w]
    pos = 1 + n_w
    b_ref = refs[pos] if has_bias else None
    pos += int(has_bias)
    o_ref = refs[pos]
    wbf_refs = refs[pos + 1:]

    a = a_ref[...]
    if prologue == "silu":
        a = a * jax.nn.sigmoid(a)
    if exact:
        zs = [jnp.dot(a, w[...], preferred_element_type=F32, precision=HIGHEST) for w in w_refs]
    else:
        @pl.when(pl.program_id(1) == 0)
        def _():
            for w, wbf in zip(w_refs, wbf_refs):
                wbf[...] = w[...].astype(BF16)

        a = a.astype(BF16)
        zs = [jnp.dot(a, wbf[...], preferred_element_type=F32) for wbf in wbf_refs]
    z = zs[0]
    if has_bias:
        z = z + b_ref[...]
    if epilogue == "sqrelu":
        z = jnp.square(jnp.maximum(z, 0.0))
    elif epilogue == "sigmoid":
        z = jax.nn.sigmoid(z)
    elif epilogue == "glu":
        z = z * jax.nn.sigmoid(zs[1])
    o_ref[...] = z.astype(o_ref.dtype)


def mm_wide(a, w, *, layer=None, col0=0, n_out=None, bias=None, prologue=None, epilogue=None,
            exact=False, out_dtype=F32, tm=1024, tn=512):
    M, K = a.shape
    n_out = n_out if n_out is not None else w.shape[-1] - col0
    tm = _pick(M, tm)
    tn = _pick(n_out, tn)
    assert col0 % tn == 0 and n_out % tn == 0
    n_w = 2 if epilogue == "glu" else 1
    jb = col0 // tn

    def w_spec(extra):
        if layer is None:
            return pl.BlockSpec((K, tn), lambda j, i: (0, jb + extra + j))
        return pl.BlockSpec((None, K, tn), lambda j, i: (layer, 0, jb + extra + j))

    in_specs = [pl.BlockSpec((tm, K), lambda j, i: (i, 0))] + [w_spec(e * (n_out // tn)) for e in range(n_w)]
    args = [a] + [w] * n_w
    if bias is not None:
        if layer is None:
            in_specs.append(pl.BlockSpec((1, tn), lambda j, i: (0, jb + j)))
            args.append(bias.reshape(1, -1))
        else:
            in_specs.append(pl.BlockSpec((None, 1, tn), lambda j, i: (layer, 0, jb + j)))
            args.append(bias.reshape(bias.shape[0], 1, -1))
    scratch = [] if exact else [pltpu.VMEM((K, tn), BF16) for _ in range(n_w)]
    return pl.pallas_call(
        functools.partial(_mm_wide_kernel, prologue=prologue, epilogue=epilogue,
                          has_bias=bias is not None, exact=exact, n_w=n_w),
        grid=(n_out // tn, M // tm),
        in_specs=in_specs,
        out_specs=pl.BlockSpec((tm, tn), lambda j, i: (i, j)),
        out_shape=jax.ShapeDtypeStruct((M, n_out), out_dtype),
        scratch_shapes=scratch,
        compiler_params=_cparams(2),
        name="mm_wide",
    )(*args)


def _mm_tall_kernel(a_ref, *refs, n_w, nk, n_next):
    w_refs = refs[:n_w]
    res_ref, g_ref, ga_ref = refs[n_w:n_w + 3]
    nxt_in = refs[n_w + 3:n_w + 3 + 3 * n_next]
    o_ref = refs[n_w + 3 + 3 * n_next]
    nxt_out = refs[n_w + 4 + 3 * n_next:n_w + 4 + 4 * n_next]
    acc_refs = refs[n_w + 4 + 4 * n_next:]
    k = pl.program_id(1)

    @pl.when(k == 0)
    def _():
        for acc in acc_refs:
            acc[...] = jnp.zeros_like(acc)

    a = a_ref[...].astype(BF16)
    for w, acc in zip(w_refs, acc_refs):
        acc[...] += jnp.dot(a, w[...].astype(BF16), preferred_element_type=F32)

    @pl.when(k == nk - 1)
    def _():
        m = acc_refs[0][...]
        if n_w == 2:
            m = m * jax.nn.sigmoid(acc_refs[1][...])
        r = lax.rsqrt(jnp.mean(m * m, axis=-1, keepdims=True) + EPS)
        x = res_ref[...] + ga_ref[...] * ((m * r) * g_ref[...])
        o_ref[...] = x
        if n_next:
            xn = x * lax.rsqrt(jnp.mean(x * x, axis=-1, keepdims=True) + EPS)
            for j in range(n_next):
                gn, sc, sh = nxt_in[3 * j:3 * j + 3]
                nxt_out[j][...] = ((xn * gn[...]) * (1.0 + sc[...]) + sh[...]).astype(nxt_out[j].dtype)


def mm_tall(a, w, res, g, gate, *, layer=None, glu=False, tm=512, tk=512, next_norms=()):
    M, K = a.shape
    N = res.shape[1]
    tm = _pick(M, tm)
    tk = _pick(K, tk)
    nk = K // tk
    n_w = 2 if glu else 1

    w_mode = dict(pipeline_mode=pl.Buffered(1)) if nk == 1 else {}

    def w_spec(e):
        if layer is None:
            return pl.BlockSpec((tk, N), lambda i, k: (k, e), **w_mode)
        return pl.BlockSpec((None, tk, N), lambda i, k: (layer, k, e), **w_mode)

    per_row = gate.shape[0] != 1
    row_spec = pl.BlockSpec((tm, N), lambda i, k: (i, 0))
    vec_spec = pl.BlockSpec((1, N), lambda i, k: (0, 0))
    mod_spec = row_spec if per_row else vec_spec
    n_next = len(next_norms)
    nxt_args, nxt_specs = [], []
    for gn, sc, sh, _ in next_norms:
        nxt_args += [gn.reshape(1, N), sc, sh]
        nxt_specs += [vec_spec, mod_spec, mod_spec]
    outs = pl.pallas_call(
        functools.partial(_mm_tall_kernel, n_w=n_w, nk=nk, n_next=n_next),
        grid=(M // tm, nk),
        in_specs=[pl.BlockSpec((tm, tk), lambda i, k: (i, k))] + [w_spec(e) for e in range(n_w)]
        + [row_spec, vec_spec, mod_spec] + nxt_specs,
        out_specs=[row_spec] * (1 + n_next),
        out_shape=[jax.ShapeDtypeStruct((M, N), F32)]
        + [jax.ShapeDtypeStruct((M, N), dt) for _, _, _, dt in next_norms],
        scratch_shapes=[pltpu.VMEM((tm, N), F32) for _ in range(n_w)],
        compiler_params=_cparams(2),
        name="mm_tall",
    )(a, *([w] * n_w), res, g.reshape(1, N), gate, *nxt_args)
    return outs if n_next else outs[0]


S5_GROUPS_PER_STEP = V7X_LANES // SSM_GROUP
P2 = 2 * STATE_DIM


def _s5_prep_kernel(lam_re_ref, lam_im_ref, ldt_ref, btr_ref, bti_ref, cr_ref, ci_ref,
                    ws_ref, wct_ref, wm_ref, at_ref, *, tc, gb):
    ws_ref[...] = jnp.zeros(ws_ref.shape, ws_ref.dtype)
    wct_ref[...] = jnp.zeros(wct_ref.shape, wct_ref.dtype)
    C = SSM_GROUP
    tcc = tc * C
    wk = max(tcc, V7X_LANES)
    lane = lax.broadcasted_iota(jnp.int32, (C, P2), 1)
    is_re = lane < STATE_DIM
    kk = lax.broadcasted_iota(jnp.int32, (tc + 1, P2), 0).astype(F32)
    lane_k = lax.broadcasted_iota(jnp.int32, (C, wk), 1)
    for gg in range(gb):
        lr = lam_re_ref[gg]
        li = lam_im_ref[gg]
        dt = jnp.exp(ldt_ref[gg])
        mag = jnp.exp(kk * (lr * dt))
        ang = kk * (li * dt)
        pr = mag * jnp.cos(ang)
        pi = mag * jnp.sin(ang)
        x = pr[1:2] - 1.0
        y = pi[1:2]
        den = lr * lr + li * li
        cfr = (x * lr + y * li) / den
        cfi = (y * lr - x * li) / den
        btr = btr_ref[gg]
        bti = bti_ref[gg]
        bbr = cfr * btr - cfi * bti
        bbi = cfr * bti + cfi * btr
        cre = cr_ref[gg]
        cim = ci_ref[gg]

        def bm(k):
            return jnp.where(is_re, pr[k:k + 1] * bbr - pi[k:k + 1] * bbi, pr[k:k + 1] * bbi + pi[k:k + 1] * bbr)

        def cm(k):
            return jnp.where(is_re, cre * pr[k:k + 1] - cim * pi[k:k + 1], -(cre * pi[k:k + 1] + cim * pr[k:k + 1]))

        cms = [cm(k) for k in range(tc + 1)]
        cs0 = jnp.concatenate(cms[:tc] + [jnp.zeros((wk - tcc, P2), F32)] * (wk > tcc), axis=0)
        kst = lax.dot_general(bm(0), cs0, (((1,), (1,)), ((), ())), preferred_element_type=F32,
                              precision=HIGHEST)
        r0 = gg * C
        for s in range(tc):
            ws_ref[s, r0:r0 + C, gg * P2:(gg + 1) * P2] = bm(tc - 1 - s).astype(ws_ref.dtype)
            wct_ref[s, r0:r0 + C, gg * P2:(gg + 1) * P2] = cms[s + 1].astype(wct_ref.dtype)
            shift = (r0 - s * C) % wk
            moved = kst if shift == 0 else pltpu.roll(kst, shift, 1)
            blockdiag = jnp.where((lane_k >= r0) & (lane_k < r0 + C), moved, 0.0)
            wm_ref[s, r0:r0 + C, :] = blockdiag[:, :V7X_LANES].astype(wm_ref.dtype)
        at_ref[gg, 0:1, :] = pr[tc:tc + 1]
        at_ref[gg, 1:2, :] = jnp.where(is_re[0:1], -pi[tc:tc + 1], pi[tc:tc + 1])


def s5_prep(lam_re, lam_im, log_dt, b_re, b_im, c_re, c_im, tc):
    G = lam_re.shape[0]
    gb = _pick(G, S5_GROUPS_PER_STEP)
    C = SSM_GROUP
    tcc = tc * C
    dup = lambda v: jnp.concatenate([v, v], axis=-1)
    lam_re2 = dup(lam_re)[:, None, :]
    lam_im2 = dup(lam_im)[:, None, :]
    ldt2 = jnp.broadcast_to(log_dt[:, None, None], (G, 1, P2))
    btr = dup(jnp.swapaxes(b_re, 1, 2))
    bti = dup(jnp.swapaxes(b_im, 1, 2))
    cr2 = dup(c_re)
    ci2 = dup(c_im)
    vec = pl.BlockSpec((gb, 1, P2), lambda i: (i, 0, 0))
    mat = pl.BlockSpec((gb, C, P2), lambda i: (i, 0, 0))
    return pl.pallas_call(
        functools.partial(_s5_prep_kernel, tc=tc, gb=gb),
        grid=(G // gb,),
        in_specs=[vec, vec, vec, mat, mat, mat, mat],
        out_specs=[pl.BlockSpec((None, tc, V7X_LANES, gb * P2), lambda i: (i, 0, 0, 0)),
                   pl.BlockSpec((None, tc, V7X_LANES, gb * P2), lambda i: (i, 0, 0, 0)),
                   pl.BlockSpec((None, tc, V7X_LANES, V7X_LANES), lambda i: (i, 0, 0, 0)),
                   pl.BlockSpec((gb, 2, P2), lambda i: (i, 0, 0))],
        out_shape=[jax.ShapeDtypeStruct((G // gb, tc, V7X_LANES, gb * P2), BF16),
                   jax.ShapeDtypeStruct((G // gb, tc, V7X_LANES, gb * P2), BF16),
                   jax.ShapeDtypeStruct((G // gb, tc, V7X_LANES, V7X_LANES), BF16),
                   jax.ShapeDtypeStruct((G, 2, P2), F32)],
        compiler_params=_cparams(1),
        name="s5_prep",
    )(lam_re2, lam_im2, ldt2, btr, bti, cr2, ci2)


def _s5_sums_kernel(u_ref, ws_ref, s_ref, *, tc):
    acc = None
    for s in range(tc):
        z = jnp.dot(u_ref[:, s, :].astype(BF16), ws_ref[s], preferred_element_type=F32)
        acc = z if acc is None else acc + z
    s_ref[...] = acc


def s5_chunk_sums(u3, ws):
    R, tc, D = u3.shape
    nb = D // V7X_LANES
    sw = ws.shape[-1]
    return pl.pallas_call(
        functools.partial(_s5_sums_kernel, tc=tc),
        grid=(nb,),
        in_specs=[pl.BlockSpec((R, tc, V7X_LANES), lambda i: (0, 0, i)),
                  pl.BlockSpec((None, tc, V7X_LANES, sw), lambda i: (i, 0, 0, 0))],
        out_specs=pl.BlockSpec((R, sw), lambda i: (0, i)),
        out_shape=jax.ShapeDtypeStruct((R, nb * sw), F32),
        compiler_params=_cparams(1),
        name="s5_sums",
    )(u3, ws)


def _s5_out_kernel(u_ref, hin_ref, wm_ref, wct_ref, d_ref, o_ref, *, tc):
    us = [u_ref[:, s, :] for s in range(tc)]
    ub = [x.astype(BF16) for x in us]
    hb = hin_ref[...].astype(BF16)
    d = d_ref[...]
    for t in range(tc):
        y = lax.dot_general(hb, wct_ref[t], (((1,), (1,)), ((), ())), preferred_element_type=F32)
        for s in range(t + 1):
            y = y + jnp.dot(ub[s], wm_ref[t - s], preferred_element_type=F32)
        o_ref[:, t, :] = jax.nn.gelu(y + d * us[t])


def s5_chunk_out(u3, hin, wm, wct, d_skip):
    R, tc, D = u3.shape
    nb = D // V7X_LANES
    sw = wct.shape[-1]
    return pl.pallas_call(
        functools.partial(_s5_out_kernel, tc=tc),
        grid=(nb,),
        in_specs=[pl.BlockSpec((R, tc, V7X_LANES), lambda i: (0, 0, i)),
                  pl.BlockSpec((R, sw), lambda i: (0, i)),
                  pl.BlockSpec((None, tc, V7X_LANES, V7X_LANES), lambda i: (i, 0, 0, 0)),
                  pl.BlockSpec((None, tc, V7X_LANES, sw), lambda i: (i, 0, 0, 0)),
                  pl.BlockSpec((1, V7X_LANES), lambda i: (0, i))],
        out_specs=pl.BlockSpec((R, tc, V7X_LANES), lambda i: (0, 0, i)),
        out_shape=jax.ShapeDtypeStruct((R, tc, D), F32),
        compiler_params=_cparams(1),
        name="s5_chunk_out",
    )(u3, hin, wm, wct, d_skip.reshape(1, D))


S5_CHUNKS_PER_STEP = 64


def _s5_scan_kernel(s_ref, h0_ref, at_ref, hin_ref, hfin_ref, h_scr, *, cb, n_steps):
    j = pl.program_id(1)

    @pl.when(j == 0)
    def _():
        h_scr[...] = h0_ref[...]

    ar = at_ref[0]
    ai = at_ref[1]

    def step(c, h):
        hin_ref[c] = h
        return ar * h + ai * pltpu.roll(h, STATE_DIM, 1) + s_ref[c]

    h = lax.fori_loop(0, cb, step, h_scr[...])
    h_scr[...] = h

    @pl.when(j == n_steps - 1)
    def _():
        hfin_ref[...] = h


def s5_scan(s, h0, at):
    Bn, n_chunk, G, _ = s.shape
    cb = _pick(n_chunk, S5_CHUNKS_PER_STEP)
    n_steps = n_chunk // cb
    return pl.pallas_call(
        functools.partial(_s5_scan_kernel, cb=cb, n_steps=n_steps),
        grid=(Bn, n_steps),
        in_specs=[pl.BlockSpec((None, cb, G, P2), lambda b, j: (b, j, 0, 0)),
                  pl.BlockSpec((None, G, P2), lambda b, j: (b, 0, 0)),
                  pl.BlockSpec((2, G, P2), lambda b, j: (0, 0, 0))],
        out_specs=[pl.BlockSpec((None, cb, G, P2), lambda b, j: (b, j, 0, 0)),
                   pl.BlockSpec((None, G, P2), lambda b, j: (b, 0, 0))],
        out_shape=[jax.ShapeDtypeStruct((Bn, n_chunk, G, P2), F32),
                   jax.ShapeDtypeStruct((Bn, G, P2), F32)],
        scratch_shapes=[pltpu.VMEM((G, P2), F32)],
        compiler_params=_cparams(2),
        name="s5_scan",
    )(s, h0, at)


def s5_mixer_core(u, h0, lam_re, lam_im, log_dt, b_re, b_im, c_re, c_im, d_skip):
    Bn, L, D = u.shape
    G = D // SSM_GROUP
    tc = _pick(L, 16)
    n_chunk = L // tc
    ws, wct, wm, at = s5_prep(lam_re, lam_im, log_dt, b_re, b_im, c_re, c_im, tc)
    u3 = u.reshape(Bn * n_chunk, tc, D)
    s = s5_chunk_sums(u3, ws).reshape(Bn, n_chunk, G, P2)
    h0v = jnp.concatenate([h0[..., 0], h0[..., 1]], axis=-1)
    hin, hfin = s5_scan(s, h0v, at.transpose(1, 0, 2))
    gy = s5_chunk_out(u3, hin.reshape(Bn * n_chunk, G * P2), wm, wct, d_skip)
    h_last = jnp.stack([hfin[..., :STATE_DIM], hfin[..., STATE_DIM:]], axis=-1)
    return gy.reshape(Bn * L, D), h_last


SUBS_PER_PAGE = PAGE_SIZE // CMP_STRIDE
KV_COLS = 2 * N_KV * HEAD_DIM
CMP_PAGES_PER_STEP = 16


def _cmp_lohi_kernel(pt_ref, *refs, n_pg):
    x_refs = refs[:n_pg]
    w_ref = refs[n_pg]
    o_ref = refs[n_pg + 1]
    rows = SUBS_PER_PAGE * N_KV
    for c in range(2):
        acc = jnp.zeros((n_pg * rows, 2 * HEAD_DIM), F32)
        for rp in range(CMP_STRIDE // 2):
            parts = []
            for p in range(n_pg):
                a0 = x_refs[p][:, 2 * rp, c].reshape(rows, HEAD_DIM)
                a1 = x_refs[p][:, 2 * rp + 1, c].reshape(rows, HEAD_DIM)
                parts.append(jnp.concatenate([a0, a1], axis=1))
            xs = jnp.concatenate(parts, axis=0).astype(BF16)
            acc = acc + jnp.dot(xs, w_ref[c, rp], preferred_element_type=F32)
        o_ref[c] = acc


def cmp_lohi(pages, page_table, cmp_w1):
    Bn, n_pages = page_table.shape
    n_pg = _pick(n_pages, CMP_PAGES_PER_STEP)
    x = pages.reshape(pages.shape[0], SUBS_PER_PAGE, CMP_STRIDE, 2, N_KV, HEAD_DIM)
    half = CMP_STRIDE // 2
    w_lo = cmp_w1[:, :CMP_STRIDE].reshape(2, half, 2 * HEAD_DIM, HEAD_DIM)
    w_hi = cmp_w1[:, CMP_STRIDE:].reshape(2, half, 2 * HEAD_DIM, HEAD_DIM)
    w = jnp.concatenate([w_lo, w_hi], axis=-1).astype(BF16)
    n_sub = n_pages * SUBS_PER_PAGE

    def x_spec(p):
        return pl.BlockSpec((None, SUBS_PER_PAGE, CMP_STRIDE, 2, N_KV, HEAD_DIM),
                            lambda b, i, pt: (pt[b, i * n_pg + p], 0, 0, 0, 0, 0))

    grid_spec = pltpu.PrefetchScalarGridSpec(
        num_scalar_prefetch=1,
        grid=(Bn, n_pages // n_pg),
        in_specs=[x_spec(p) for p in range(n_pg)]
        + [pl.BlockSpec(w.shape, lambda b, i, pt: (0, 0, 0, 0))],
        out_specs=pl.BlockSpec((None, 2, n_pg * SUBS_PER_PAGE * N_KV, 2 * HEAD_DIM),
                               lambda b, i, pt: (b, 0, i, 0)),
    )
    return pl.pallas_call(
        functools.partial(_cmp_lohi_kernel, n_pg=n_pg),
        grid_spec=grid_spec,
        out_shape=jax.ShapeDtypeStruct((Bn, 2, n_sub * N_KV, 2 * HEAD_DIM), F32),
        compiler_params=_cparams(2),
        name="cmp_lohi",
    )(page_table, *([x] * n_pg), w)


def _compress_kernel(x_ref, pe_ref, w1_ref, b1_ref, w2_ref, o_ref, pe_scr):
    n_rows = x_ref.shape[0]
    half = w1_ref.shape[0] // 2
    c = pl.program_id(1)

    @pl.when(pl.program_id(0) == 0)
    def _():
        pe = jnp.broadcast_to(pe_ref[...], (8, 2 * half))
        pe_scr[c, 0] = jnp.dot(pe[:, :half], w1_ref[:half], preferred_element_type=F32, precision=HIGHEST)
        pe_scr[c, 1] = jnp.dot(pe[:, half:], w1_ref[half:], preferred_element_type=F32, precision=HIGHEST)

    x = x_ref[...]
    lo = x[:, :HEAD_DIM] + pe_scr[c, 0][0:1]
    hi = x[:, HEAD_DIM:] + pe_scr[c, 1][0:1]
    hi_next = pltpu.roll(hi, n_rows - N_KV, 0)
    h = jax.nn.gelu(lo + hi_next + b1_ref[...])
    o_ref[...] = jnp.dot(h.astype(BF16), w2_ref[...].astype(BF16), preferred_element_type=F32).astype(o_ref.dtype)


def compress_blocks(lohi, cmp_pe, cmp_w1, cmp_b1, cmp_w2):
    Bn, _, n_rows, _ = lohi.shape
    kdim = CMP_BLOCK * HEAD_DIM
    out = pl.pallas_call(
        _compress_kernel,
        grid=(Bn, 2),
        in_specs=[pl.BlockSpec((None, None, n_rows, 2 * HEAD_DIM), lambda b, c: (b, c, 0, 0)),
                  pl.BlockSpec((None, 1, kdim), lambda b, c: (c, 0, 0)),
                  pl.BlockSpec((None, kdim, HEAD_DIM), lambda b, c: (c, 0, 0)),
                  pl.BlockSpec((None, 1, HEAD_DIM), lambda b, c: (c, 0, 0)),
                  pl.BlockSpec((None, HEAD_DIM, HEAD_DIM), lambda b, c: (c, 0, 0))],
        out_specs=pl.BlockSpec((None, None, n_rows, HEAD_DIM), lambda b, c: (b, c, 0, 0)),
        out_shape=jax.ShapeDtypeStruct((Bn, 2, n_rows, HEAD_DIM), BF16),
        scratch_shapes=[pltpu.VMEM((2, 2, 8, HEAD_DIM), F32)],
        compiler_params=_cparams(2),
        name="compress",
    )(lohi, cmp_pe.reshape(2, 1, kdim), cmp_w1.reshape(2, kdim, HEAD_DIM),
      cmp_b1.reshape(2, 1, HEAD_DIM), cmp_w2)
    return out.reshape(Bn, 2, n_rows // N_KV, N_KV, HEAD_DIM).transpose(0, 1, 3, 2, 4)


N_CAND = V7X_LANES
NEG_TAKEN = -3e38


def _topk_mask(score, axis, k_sel):
    idx = lax.broadcasted_iota(jnp.int32, score.shape, axis)
    sel = jnp.zeros(score.shape, F32)
    for _ in range(k_sel):
        m = jnp.max(score, axis=axis, keepdims=True)
        first = jnp.min(jnp.where(score == m, idx, N_CAND), axis=axis, keepdims=True)
        hit = idx == first
        sel = jnp.where(hit & (m > 0.5 * NEG), 1.0, sel)
        score = jnp.where(hit, NEG_TAKEN, score)
    return sel


def _cmp_attn_kernel(slope_ref, q_ref, kc_ref, vc_ref, gate_ref, wsel_ref, oc_ref, sel_ref, *,
                     tq, hpg, gps, bps, pos0, pos_step, transposed, k_sel, n_cand):
    i = pl.program_id(1)
    n_cmp = kc_ref.shape[2]
    gw = hpg * HEAD_DIM
    scale = HEAD_DIM ** -0.5
    base = pos0 + i * pos_step
    t_idx = lax.broadcasted_iota(jnp.int32, (tq, n_cmp), 0)
    n_idx = lax.broadcasted_iota(jnp.int32, (tq, n_cmp), 1)
    dist_i = base + t_idx - (n_idx * CMP_STRIDE + (CMP_BLOCK - 1))
    valid = dist_i >= 0
    row_seen = base + lax.broadcasted_iota(jnp.int32, (tq, 1), 0) >= CMP_BLOCK - 1
    dist = dist_i.astype(F32)
    shape, j_ax, t_ax = ((N_CAND, tq), 0, 1) if transposed else ((tq, N_CAND), 1, 0)
    j = lax.broadcasted_iota(jnp.int32, shape, j_ax)
    blk = (base + lax.broadcasted_iota(jnp.int32, shape, t_ax)) // SEL_BLOCK
    forced = (j == 0) | (j == blk) | (j == blk - 1)
    visible = (j <= blk) & (j < n_cand)
    scores = []
    for bi, gi in [(b, g) for b in range(bps) for g in range(gps)]:
        g = pl.program_id(2) * gps + gi
        gates = gate_ref[bi]
        q = q_ref[bi, :, gi * gw:(gi + 1) * gw]
        qs = jnp.concatenate([q[:, h * HEAD_DIM:(h + 1) * HEAD_DIM] for h in range(hpg)], axis=0)
        qs = (qs * (scale * LOG2E)).astype(BF16)
        s = lax.dot_general(qs, kc_ref[bi, gi], (((1,), (1,)), ((), ())), preferred_element_type=F32)
        vc = vc_ref[bi, gi]
        psum = jnp.zeros((tq, n_cmp), F32)
        for h in range(hpg):
            sh = s[h * tq:(h + 1) * tq] - (slope_ref[g * hpg + h] * LOG2E) * dist
            sh = jnp.where(valid, sh, NEG)
            m = jnp.max(sh, axis=-1, keepdims=True)
            e = jnp.exp2(sh - m)
            inv = jnp.where(row_seen, 1.0 / jnp.maximum(jnp.sum(e, axis=-1, keepdims=True), 1e-30), 0.0)
            p = e * inv
            psum = psum + p
            o_h = jnp.dot(p.astype(BF16), vc, preferred_element_type=F32)
            c0 = gi * gw + h * HEAD_DIM
            gc = gi * GATE_LANES + 3 * h
            oc_ref[bi, :, c0:c0 + HEAD_DIM] = o_h * gates[:, gc:gc + 1]
        if transposed:
            imp = lax.dot_general(wsel_ref[...], psum, (((1,), (1,)), ((), ())),
                                  preferred_element_type=F32, precision=HIGHEST)
        else:
            imp = lax.dot_general(psum, wsel_ref[...], (((1,), (1,)), ((), ())),
                                  preferred_element_type=F32, precision=HIGHEST)
        scores.append(jnp.where(visible, imp + jnp.where(forced, FORCE_BONUS, 0.0), NEG))
    if transposed:
        for n, (bi, gi) in enumerate([(b, g) for b in range(bps) for g in range(gps)]):
            sel = _topk_mask(scores[n], 0, k_sel).T
            sel_ref[bi, :, gi * N_CAND:(gi + 1) * N_CAND] = sel.astype(sel_ref.dtype)
    else:
        sel = _topk_mask(jnp.concatenate(scores, axis=0), 1, k_sel)
        for n, (bi, gi) in enumerate([(b, g) for b in range(bps) for g in range(gps)]):
            sel_ref[bi, :, gi * N_CAND:(gi + 1) * N_CAND] = sel[n * tq:(n + 1) * tq].astype(sel_ref.dtype)


def _sel_weights(n_cmp_pad, n_cmp):
    j = np.arange(N_CAND)[:, None]
    n = np.arange(n_cmp_pad)[None, :]
    w = (n >= SUBS_PER_SEL * j - 1) & (n <= SUBS_PER_SEL * j + SUBS_PER_SEL - 1) & (n < n_cmp)
    return jnp.asarray(w.astype(np.float32))


def cmp_attention(q, kcvc, gates, *, tq, pos0, pos_step, transposed, k_sel, n_cand):
    Bn, T, D = q.shape
    n_heads = D // HEAD_DIM
    hpg = n_heads // N_KV
    n_sub = kcvc.shape[3]
    gw = hpg * HEAD_DIM
    slopes = alibi_slopes(n_heads).reshape(-1)
    wsel = _sel_weights(n_sub, n_sub - 1)
    gps = N_KV if tq < QBLK else 1
    bps = 2 if (tq < QBLK and Bn % 2 == 0) else 1
    kern = functools.partial(_cmp_attn_kernel, tq=tq, hpg=hpg, gps=gps, bps=bps, pos0=pos0, pos_step=pos_step,
                             transposed=transposed, k_sel=k_sel, n_cand=n_cand)
    return pl.pallas_call(
        kern,
        grid=(Bn // bps, T // tq, N_KV // gps),
        in_specs=[pl.BlockSpec(memory_space=pltpu.SMEM),
                  pl.BlockSpec((bps, tq, gps * gw), lambda b, i, g: (b, i, g)),
                  pl.BlockSpec((bps, None, gps, n_sub, HEAD_DIM), lambda b, i, g: (b, 0, g, 0, 0)),
                  pl.BlockSpec((bps, None, gps, n_sub, HEAD_DIM), lambda b, i, g: (b, 1, g, 0, 0)),
                  pl.BlockSpec((bps, tq, gps * GATE_LANES), lambda b, i, g: (b, i, g)),
                  pl.BlockSpec((N_CAND, n_sub), lambda b, i, g: (0, 0))],
        out_specs=[pl.BlockSpec((bps, tq, gps * gw), lambda b, i, g: (b, i, g)),
                   pl.BlockSpec((bps, tq, gps * N_CAND), lambda b, i, g: (b, i, g))],
        out_shape=[jax.ShapeDtypeStruct((Bn, T, D), F32),
                   jax.ShapeDtypeStruct((Bn, T, N_KV * N_CAND), BF16)],
        compiler_params=_cparams(3),
        name="cmp_attn",
    )(slopes, q, kcvc, kcvc, gates, wsel)


SEL_TK = 512
SEL_TQ = 256
MASK_BIG = 1e30


def _sel_win_kernel(slope_ref, q_ref, ks_ref, vs_ref, kw_ref, vw_ref, sel_ref, oc_ref, gate_ref, e_ref,
                    o_ref, ksb, vsb, kwb, vwb, *, tq, hpg):
    g = pl.program_id(0)
    i = pl.program_id(1)
    L = ks_ref.shape[0]
    s0 = i * tq
    scale = HEAD_DIM ** -0.5

    @pl.when(i == 0)
    def _():
        ksb[:, :HEAD_DIM] = ks_ref[...].astype(BF16)
        ksb[:, HEAD_DIM:] = e_ref[...]
        vsb[...] = vs_ref[...].astype(BF16)
        kwb[0:WINDOW] = jnp.zeros((WINDOW, HEAD_DIM), BF16)
        vwb[0:WINDOW] = jnp.zeros((WINDOW, HEAD_DIM), BF16)
        kwb[WINDOW:] = kw_ref[...].astype(BF16)
        vwb[WINDOW:] = vw_ref[...].astype(BF16)

    q = q_ref[...]
    qs = jnp.concatenate([q[:, h * HEAD_DIM:(h + 1) * HEAD_DIM] for h in range(hpg)], axis=0)
    qs = (qs * (scale * LOG2E)).astype(BF16)
    slopes = [slope_ref[g * hpg + h] * LOG2E for h in range(hpg)]
    neg_sel = sel_ref[...] - 1.0
    qaug = jnp.concatenate([qs, jnp.concatenate([neg_sel] * hpg, axis=0)], axis=1)
    n_tiles = s0 // SEL_TK + 1

    def sweep(k0, carry, width, diagonal):
        ms, ls, accs = carry
        k0 = pl.multiple_of(k0, SEL_TK)
        v_t = vsb[pl.ds(k0, width), :]
        if diagonal:
            t_idx = lax.broadcasted_iota(jnp.int32, (tq, width), 0)
            c_idx = lax.broadcasted_iota(jnp.int32, (tq, width), 1)
            causal = s0 + t_idx >= k0 + c_idx
        s = lax.dot_general(qaug, ksb[pl.ds(k0, width), :], (((1,), (1,)), ((), ())),
                            preferred_element_type=F32)
        rel = (k0 - s0 + lax.broadcasted_iota(jnp.int32, (1, width), 1)).astype(F32)
        new_m, new_l, new_acc = [], [], []
        for h in range(hpg):
            sh = s[h * tq:(h + 1) * tq] + slopes[h] * rel
            if diagonal:
                sh = jnp.where(causal, sh, -MASK_BIG)
            m_new = jnp.maximum(ms[h], jnp.max(sh, axis=-1, keepdims=True))
            p = jnp.exp2(sh - m_new)
            alpha = jnp.exp2(ms[h] - m_new)
            new_l.append(alpha * ls[h] + jnp.sum(p, axis=-1, keepdims=True))
            new_acc.append(alpha * accs[h] + jnp.dot(p.astype(BF16), v_t, preferred_element_type=F32))
            new_m.append(m_new)
        return tuple(new_m), tuple(new_l), tuple(new_acc)

    init = ((jnp.full((tq, 1), NEG, F32),) * hpg, (jnp.zeros((tq, 1), F32),) * hpg,
            (jnp.zeros((tq, HEAD_DIM), F32),) * hpg)
    n_pairs = (n_tiles - 1) // 2
    carry = lax.fori_loop(0, n_pairs, lambda i2, c: sweep(i2 * (2 * SEL_TK), c, 2 * SEL_TK, False), init)
    k_last = n_pairs * (2 * SEL_TK)
    ms, ls, accs = lax.cond(n_tiles % 2 == 0,
                            lambda c: sweep(k_last, c, 2 * SEL_TK, True),
                            lambda c: sweep(k_last, c, SEL_TK, True), carry)

    wn = WINDOW + tq
    w0 = pl.multiple_of(s0, tq)
    kw_t = kwb[pl.ds(w0, wn), :]
    vw_t = vwb[pl.ds(w0, wn), :]
    sw = lax.dot_general(qs, kw_t, (((1,), (1,)), ((), ())), preferred_element_type=F32)
    t_idx = lax.broadcasted_iota(jnp.int32, (tq, wn), 0)
    c_idx = lax.broadcasted_iota(jnp.int32, (tq, wn), 1)
    dist_i = t_idx + WINDOW - c_idx
    valid = (dist_i >= 0) & (dist_i < WINDOW) & (c_idx + s0 >= WINDOW)
    dist = jnp.where(valid, dist_i.astype(F32), MASK_BIG)
    gates = gate_ref[...]
    oc = oc_ref[...]
    for h in range(hpg):
        sh = sw[h * tq:(h + 1) * tq] - slopes[h] * dist
        m = jnp.max(sh, axis=-1, keepdims=True)
        e = jnp.exp2(sh - m)
        o_w = (jnp.dot(e.astype(BF16), vw_t, preferred_element_type=F32)
               / jnp.maximum(jnp.sum(e, axis=-1, keepdims=True), 1e-30))
        o_s = accs[h] / jnp.maximum(ls[h], 1e-30)
        out = (oc[:, h * HEAD_DIM:(h + 1) * HEAD_DIM] + gates[:, 3 * h + 1:3 * h + 2] * o_s
               + gates[:, 3 * h + 2:3 * h + 3] * o_w)
        o_ref[:, h * HEAD_DIM:(h + 1) * HEAD_DIM] = out.astype(o_ref.dtype)


def _block_onehot(n_keys):
    key = np.arange(n_keys)[:, None]
    j = np.arange(N_CAND)[None, :]
    return jnp.asarray((key // SEL_BLOCK == j).astype(np.float32) * MASK_BIG, dtype=BF16)


def sel_win_attention(q, kv, sel, oc, gates, *, tq):
    _, L, D = q.shape
    n_heads = D // HEAD_DIM
    hpg = n_heads // N_KV
    gw = hpg * HEAD_DIM
    assert L % SEL_TK == 0 and SEL_TK % tq == 0 and L // SEL_BLOCK <= N_CAND
    slopes = alibi_slopes(n_heads).reshape(-1)
    e = _block_onehot(L)

    def kv_spec(branch, which):
        cb = (branch * 2 + which) * N_KV
        return pl.BlockSpec((L, HEAD_DIM), lambda g, i: (0, cb + g), pipeline_mode=pl.Buffered(1))

    return pl.pallas_call(
        functools.partial(_sel_win_kernel, tq=tq, hpg=hpg),
        grid=(N_KV, L // tq),
        in_specs=[pl.BlockSpec(memory_space=pltpu.SMEM),
                  pl.BlockSpec((None, tq, gw), lambda g, i: (0, i, g)),
                  kv_spec(1, 0), kv_spec(1, 1), kv_spec(2, 0), kv_spec(2, 1),
                  pl.BlockSpec((None, tq, N_CAND), lambda g, i: (0, i, g)),
                  pl.BlockSpec((None, tq, gw), lambda g, i: (0, i, g)),
                  pl.BlockSpec((None, tq, GATE_LANES), lambda g, i: (0, i, g)),
                  pl.BlockSpec(e.shape, lambda g, i: (0, 0))],
        out_specs=pl.BlockSpec((tq, gw), lambda g, i: (i, g)),
        out_shape=jax.ShapeDtypeStruct((L, D), BF16),
        scratch_shapes=[pltpu.VMEM((L, 2 * HEAD_DIM), BF16), pltpu.VMEM((L, HEAD_DIM), BF16),
                        pltpu.VMEM((WINDOW + L, HEAD_DIM), BF16), pltpu.VMEM((WINDOW + L, HEAD_DIM), BF16)],
        compiler_params=_cparams(2),
        name="sel_win_attn",
    )(slopes, q, kv, kv, kv, kv, sel, oc, gates, e)


def prompt_nsa(q, gates, kv, kcvc):
    L = q.shape[1]
    n_sel = L // SEL_BLOCK
    oc, sel = cmp_attention(q, kcvc, gates, tq=QBLK, pos0=0, pos_step=QBLK, transposed=True,
                            k_sel=min(TOPK, n_sel), n_cand=n_sel)
    return sel_win_attention(q, kv, sel, oc, gates, tq=SEL_TQ)


SAMPLE_TQ = 8
SAMPLE_PAGES_PER_STEP = 8
NEW_ROWS_PAD = V7X_LANES


def _sample_sel_win_kernel(pt_ref, slope_ref, q_ref, sel_ref, oc_ref, gate_ref, e_ref, *refs,
                           n_pg, hpg, t_real, n_steps, past_len, win_len):
    pages = refs[:n_pg]
    nslc_ref, nwin_ref, cwk_ref, cwv_ref, o_ref, m_scr, l_scr, acc_scr = refs[n_pg:]
    j = pl.program_id(1)
    tq = SAMPLE_TQ
    tk = n_pg * PAGE_SIZE
    scale = HEAD_DIM ** -0.5

    @pl.when(j == 0)
    def _():
        m_scr[...] = jnp.full(m_scr.shape, NEG, F32)
        l_scr[...] = jnp.zeros(l_scr.shape, F32)
        acc_scr[...] = jnp.zeros(acc_scr.shape, F32)

    q = q_ref[...]
    neg_sel = sel_ref[...] - 1.0

    def heads_of(g):
        gw = hpg * HEAD_DIM
        qg = q[:, g * gw:(g + 1) * gw]
        qs = jnp.concatenate([qg[:, h * HEAD_DIM:(h + 1) * HEAD_DIM] for h in range(hpg)], axis=0)
        return (qs * scale).astype(BF16)

    def online(g, h, sh, v):
        r0 = h * tq
        m_old = m_scr[g, r0:r0 + tq]
        m_new = jnp.maximum(m_old, jnp.max(sh, axis=-1, keepdims=True))
        p = jnp.exp(sh - m_new)
        alpha = jnp.exp(m_old - m_new)
        l_scr[g, r0:r0 + tq] = alpha * l_scr[g, r0:r0 + tq] + jnp.sum(p, axis=-1, keepdims=True)
        acc_scr[g, r0:r0 + tq] = alpha * acc_scr[g, r0:r0 + tq] + jnp.dot(p.astype(BF16), v,
                                                                         preferred_element_type=F32)
        m_scr[g, r0:r0 + tq] = m_new

    rows = hpg * tq
    n_rows = N_KV * rows
    n_lane = tk * N_KV
    kf = jnp.concatenate([pg[:, 0].reshape(PAGE_SIZE * N_KV, HEAD_DIM) for pg in pages], axis=0).astype(BF16)
    vf = jnp.concatenate([pg[:, 1].reshape(PAGE_SIZE * N_KV, HEAD_DIM) for pg in pages], axis=0).astype(BF16)
    kaug = jnp.concatenate([kf, e_ref[...]], axis=1)
    qaug = jnp.concatenate(
        [jnp.concatenate([heads_of(g) for g in range(N_KV)], axis=0),
         jnp.concatenate([neg_sel[:, g * N_CAND:(g + 1) * N_CAND] for g in range(N_KV) for _ in range(hpg)],
                         axis=0)], axis=1)
    s = lax.dot_general(qaug, kaug, (((1,), (1,)), ((), ())), preferred_element_type=F32)
    lane = lax.broadcasted_iota(jnp.int32, (1, n_lane), 1)
    rel = (j * tk - past_len + lane // N_KV).astype(F32)
    slope_col = jnp.concatenate([jnp.full((tq, 1), slope_ref[gh], F32) for gh in range(N_KV * hpg)], axis=0)
    row_g = lax.broadcasted_iota(jnp.int32, (n_rows, 1), 0) // rows
    sh = jnp.where(lane % N_KV == row_g, s + slope_col * rel, -MASK_BIG)
    m_old = m_scr[...].reshape(n_rows, 1)
    m_new = jnp.maximum(m_old, jnp.max(sh, axis=-1, keepdims=True))
    p = jnp.exp(sh - m_new)
    alpha = jnp.exp(m_old - m_new)
    l_new = alpha * l_scr[...].reshape(n_rows, 1) + jnp.sum(p, axis=-1, keepdims=True)
    acc_new = alpha * acc_scr[...].reshape(n_rows, HEAD_DIM) + jnp.dot(p.astype(BF16), vf,
                                                                    preferred_element_type=F32)
    m_scr[...] = m_new.reshape(m_scr.shape)
    l_scr[...] = l_new.reshape(l_scr.shape)
    acc_scr[...] = acc_new.reshape(acc_scr.shape)

    @pl.when(j == n_steps - 1)
    def _():
        gates = gate_ref[...]
        oc = oc_ref[...]
        t_n = lax.broadcasted_iota(jnp.int32, (tq, NEW_ROWS_PAD), 0)
        r_n = lax.broadcasted_iota(jnp.int32, (tq, NEW_ROWS_PAD), 1)
        valid_n = (r_n <= t_n) & (r_n < t_real)
        wn = win_len + NEW_ROWS_PAD
        t_w = lax.broadcasted_iota(jnp.int32, (tq, wn), 0)
        c_w = lax.broadcasted_iota(jnp.int32, (tq, wn), 1)
        dist_w = jnp.where(c_w < win_len, t_w + win_len - c_w, t_w - (c_w - win_len))
        cached = c_w < win_len
        valid_w = ((cached & (dist_w < WINDOW) & (c_w + past_len - win_len >= 0))
                   | ((c_w >= win_len) & (dist_w >= 0) & (c_w - win_len < t_real)))
        dist_wf = dist_w.astype(F32)
        for g in range(N_KV):
            kc0 = g * HEAD_DIM
            vc0 = (N_KV + g) * HEAD_DIM
            qs = heads_of(g)
            k_n = nslc_ref[:, kc0:kc0 + HEAD_DIM].astype(BF16)
            v_n = nslc_ref[:, vc0:vc0 + HEAD_DIM].astype(BF16)
            s_n = lax.dot_general(qs, k_n, (((1,), (1,)), ((), ())), preferred_element_type=F32)
            k_w = jnp.concatenate([cwk_ref[:, g, :], nwin_ref[:, kc0:kc0 + HEAD_DIM]], axis=0).astype(BF16)
            v_w = jnp.concatenate([cwv_ref[:, g, :], nwin_ref[:, vc0:vc0 + HEAD_DIM]], axis=0).astype(BF16)
            s_w = lax.dot_general(qs, k_w, (((1,), (1,)), ((), ())), preferred_element_type=F32)
            for h in range(hpg):
                slope = slope_ref[g * hpg + h]
                r0 = h * tq
                sh = jnp.where(valid_n, s_n[r0:r0 + tq] + slope * r_n.astype(F32), -MASK_BIG)
                online(g, h, sh, v_n)
                o_s = acc_scr[g, r0:r0 + tq] / jnp.maximum(l_scr[g, r0:r0 + tq], 1e-30)
                sw = jnp.where(valid_w, s_w[r0:r0 + tq] - slope * dist_wf, NEG)
                m = jnp.max(sw, axis=-1, keepdims=True)
                e = jnp.where(valid_w, jnp.exp(sw - m), 0.0)
                p = e / jnp.maximum(jnp.sum(e, axis=-1, keepdims=True), 1e-30)
                o_w = jnp.dot(p.astype(BF16), v_w, preferred_element_type=F32)
                c0 = (g * hpg + h) * HEAD_DIM
                gc = g * GATE_LANES + 3 * h
                out = (oc[:, c0:c0 + HEAD_DIM] + gates[:, gc + 1:gc + 2] * o_s + gates[:, gc + 2:gc + 3] * o_w)
                o_ref[:, c0:c0 + HEAD_DIM] = out.astype(o_ref.dtype)


def _block_onehot_rows(n_keys):
    key = np.repeat(np.arange(n_keys), N_KV)[:, None]
    j = np.arange(N_CAND)[None, :]
    return jnp.asarray((key // SEL_BLOCK == j).astype(np.float32) * MASK_BIG, dtype=BF16)


def sample_sel_win_attention(q, sel, oc, gates, cache_slc, page_table, new_slc, new_win, cache_win, t_real):
    Bn, tq, D = q.shape
    n_heads = D // HEAD_DIM
    hpg = n_heads // N_KV
    n_pages = page_table.shape[1]
    n_pg = _pick(n_pages, SAMPLE_PAGES_PER_STEP)
    n_steps = n_pages // n_pg
    past_len = n_pages * PAGE_SIZE
    win_len = cache_win.shape[1]
    assert past_len // SEL_BLOCK <= N_CAND and past_len % SEL_BLOCK == 0 and t_real <= min(tq, SEL_BLOCK)
    slopes = alibi_slopes(n_heads).reshape(-1)
    e = _block_onehot_rows(past_len)
    e_rows = n_pg * PAGE_SIZE * N_KV
    rows = hpg * tq

    def pg_spec(p):
        return pl.BlockSpec((None, PAGE_SIZE, 2, N_KV, HEAD_DIM),
                            lambda b, j, pt: (pt[b, j * n_pg + p], 0, 0, 0, 0))

    def win_spec(c):
        return pl.BlockSpec((None, win_len, None, N_KV, HEAD_DIM), lambda b, j, pt: (b, 0, c, 0, 0))

    per_b = lambda shape: pl.BlockSpec((None,) + shape, lambda b, j, pt: (b, 0, 0))
    grid_spec = pltpu.PrefetchScalarGridSpec(
        num_scalar_prefetch=1,
        grid=(Bn, n_steps),
        in_specs=[pl.BlockSpec(memory_space=pltpu.SMEM),
                  per_b((tq, D)), per_b((tq, N_KV * N_CAND)), per_b((tq, D)), per_b((tq, N_KV * GATE_LANES)),
                  pl.BlockSpec((e_rows, N_CAND), lambda b, j, pt: (j, 0))]
        + [pg_spec(p) for p in range(n_pg)]
        + [per_b((NEW_ROWS_PAD, KV_COLS)), per_b((NEW_ROWS_PAD, KV_COLS)), win_spec(0), win_spec(1)],
        out_specs=per_b((tq, D)),
        scratch_shapes=[pltpu.VMEM((N_KV, rows, 1), F32), pltpu.VMEM((N_KV, rows, 1), F32),
                        pltpu.VMEM((N_KV, rows, HEAD_DIM), F32)],
    )
    return pl.pallas_call(
        functools.partial(_sample_sel_win_kernel, n_pg=n_pg, hpg=hpg, t_real=t_real, n_steps=n_steps,
                          past_len=past_len, win_len=win_len),
        grid_spec=grid_spec,
        out_shape=jax.ShapeDtypeStruct((Bn, tq, D), BF16),
        compiler_params=_cparams(2),
        name="sample_sel_win_attn",
    )(page_table, slopes, q, sel, oc, gates, e, *([cache_slc] * n_pg), new_slc, new_win, cache_win, cache_win)


def sample_nsa(q2d, gates_pad, kv_rows, kcvc, cache_slc, cache_win, page_table):
    _, kv_slc, kv_win = kv_rows
    Bn, T = kv_slc.shape[:2]
    D = q2d.shape[1]
    past_len = page_table.shape[1] * PAGE_SIZE
    n_past_blk = past_len // SEL_BLOCK
    pad_t = lambda a: jnp.pad(a.reshape(Bn, T, -1), ((0, 0), (0, SAMPLE_TQ - T), (0, 0)))
    q = pad_t(q2d)
    gates = pad_t(gates_pad)
    oc, sel = cmp_attention(q, kcvc, gates, tq=SAMPLE_TQ, pos0=past_len, pos_step=0, transposed=False,
                            k_sel=min(TOPK, n_past_blk + 1) - 1, n_cand=n_past_blk)
    pad_rows = lambda a: jnp.pad(a.reshape(Bn, T, KV_COLS), ((0, 0), (0, NEW_ROWS_PAD - T), (0, 0)))
    o = sample_sel_win_attention(q, sel, oc, gates, cache_slc, page_table,
                                 pad_rows(kv_slc), pad_rows(kv_win), cache_win, T)
    return o[:, :T].reshape(Bn * T, D)


def sample_attention_paged(kv_rows, cache_cmp, cache_slc, cache_win, page_table, cmp_pe, cmp_w1, cmp_b1, cmp_w2):
    T = kv_rows[0].shape[1]
    assert T < CMP_STRIDE, "new rows never complete a compression sub-block"
    kcvc = compress_blocks(cmp_lohi(cache_cmp, page_table, cmp_w1), cmp_pe, cmp_w1, cmp_b1, cmp_w2)

    def attend(q2d, gates_pad):
        return sample_nsa(q2d, gates_pad, kv_rows, kcvc, cache_slc, cache_win, page_table)

    return attend


def alibi_slopes(n_heads):
    exps = np.arange(1, n_heads + 1, dtype=np.float32) * np.float32(-8.0 / n_heads)
    return jnp.asarray(np.exp2(exps), dtype=F32).reshape(N_KV, n_heads // N_KV)


def prompt_attention(kv2d, kv_rows, cmp_pe, cmp_w1, cmp_b1, cmp_w2):
    kv_cmp = kv_rows[0]
    L = kv_cmp.shape[1]
    pages = kv_cmp.reshape(L // PAGE_SIZE, PAGE_SIZE, 2, N_KV, HEAD_DIM)
    table = jnp.arange(L // PAGE_SIZE, dtype=jnp.int32)[None]
    kcvc = compress_blocks(cmp_lohi(pages, table, cmp_w1), cmp_pe, cmp_w1, cmp_b1, cmp_w2)

    def attend(q2d, gates_pad):
        return prompt_nsa(q2d[None], gates_pad[None], kv2d, kcvc)

    return attend


def _rows(v, per_tok):
    return v if v.shape[0] == 1 else jnp.repeat(v, per_tok, axis=0)


def trunk(x, mods, kv_mod, h0, make_attend, p):
    Bn, L, D = x.shape
    M = Bn * L
    n_heads = D // HEAD_DIM
    hpg = n_heads // N_KV
    depth = p["mod_w"].shape[0]
    n_a = depth // 2
    xr = x.reshape(M, D)
    new_h = []
    kv_rows = None
    attend = None
    layer_mods = [[_rows(m, L) for m in jnp.split(mods[l], 6, axis=-1)] for l in range(depth)]
    kv_shift, kv_scale = [_rows(m, L) for m in jnp.split(kv_mod, 2, axis=-1)]

    def mixer_dtype(l):
        return F32 if l < n_a else BF16

    h_mix = norm_mod(xr, p["norm_pre"][0, 0], layer_mods[0][1], layer_mods[0][0], mixer_dtype(0))
    hk = norm_mod(xr, p["kv_norm"], kv_scale, kv_shift, BF16) if n_a == 0 else None
    for l in range(depth):
        sh1, sc1, ga1, sh2, sc2, ga2 = layer_mods[l]
        mlp_pre = [(p["norm_pre"][l, 1], sc2, sh2, BF16)]
        after_mlp = []
        if l + 1 < depth:
            nsh1, nsc1 = layer_mods[l + 1][0], layer_mods[l + 1][1]
            after_mlp.append((p["norm_pre"][l + 1, 0], nsc1, nsh1, mixer_dtype(l + 1)))
            if l + 1 == n_a:
                after_mlp.append((p["kv_norm"], kv_scale, kv_shift, BF16))
        if l == n_a:
            kv2d = mm_wide(hk, p["w_kv"], tn=1024)
            kv = kv2d.reshape(Bn, L, 3, 2, N_KV, HEAD_DIM)
            kv_rows = (kv[:, :, 0], kv[:, :, 1], kv[:, :, 2])
            attend = make_attend(kv2d, kv_rows)
        if l < n_a:
            u = h_mix
            gy, h_last = s5_mixer_core(u.reshape(Bn, L, D), h0[l], p["ssm_lam_re"][l], p["ssm_lam_im"][l],
                                       p["ssm_log_dt"][l], p["ssm_b_re"][l], p["ssm_b_im"][l],
                                       p["ssm_c_re"][l], p["ssm_c_im"][l], p["ssm_d"][l])
            new_h.append(h_last)
            xr, h = mm_tall(gy, p["ssm_w_glu"], xr, p["norm_post"][l, 0], ga1, layer=l, glu=True, tk=D,
                            next_norms=mlp_pre)
        else:
            lb = l - n_a
            q = mm_wide(h_mix, p["nsa_w_qg"], layer=lb, n_out=n_heads * HEAD_DIM, tn=1024)
            gates_pad = mm_wide(h_mix, p["w_gate_pad"], layer=lb, epilogue="sigmoid", tn=512)
            o = attend(q, gates_pad)
            xr, h = mm_tall(o, p["nsa_w_o"], xr, p["norm_post"][l, 0], ga1, layer=lb, tk=D, next_norms=mlp_pre)
        f = mm_wide(h, p["mlp_w1"], layer=l, epilogue="sqrelu", out_dtype=BF16, tn=1024)
        outs = mm_tall(f, p["mlp_w2"], xr, p["norm_post"][l, 1], ga2, layer=l, tk=1024, next_norms=after_mlp)
        if after_mlp:
            xr, h_mix = outs[0], outs[1]
            if l + 1 == n_a:
                hk = outs[2]
        else:
            xr = outs
    return xr.reshape(Bn, L, D), jnp.stack(new_h), kv_rows


def kernel(x_prompt, x_sample, c_prompt, c_sample, state_ssm, cache_cmp, cache_slc, cache_win, page_table, mod_w, mod_b, norm_pre, norm_post, mlp_w1, mlp_w2, ssm_lam_re, ssm_lam_im, ssm_log_dt, ssm_b_re, ssm_b_im, ssm_c_re, ssm_c_im, ssm_d, ssm_w_glu, kv_norm, kv_mod_w, kv_mod_b, w_kv, cmp_pe, cmp_w1, cmp_b1, cmp_w2, nsa_w_qg, nsa_w_o):
    D = x_prompt.shape[-1]
    depth = mod_w.shape[0]
    n_heads = D // HEAD_DIM
    bp, bs = c_prompt.shape[0], c_sample.shape[0]
    c_all = jnp.concatenate([c_prompt, c_sample], axis=0)
    n_c = c_all.shape[0]
    c_all = jnp.pad(c_all, ((0, -n_c % 8), (0, 0)))
    mods = [mm_wide(c_all, mod_w, layer=l, bias=mod_b, prologue="silu", exact=True, tn=512) for l in range(depth)]
    kv_mod = mm_wide(c_all, kv_mod_w, bias=kv_mod_b, prologue="silu", exact=True, tn=512)
    hpg = n_heads // N_KV
    w_gate = nsa_w_qg[:, :, n_heads * HEAD_DIM:].reshape(nsa_w_qg.shape[0], D, N_KV, 3 * hpg)
    w_gate_pad = jnp.pad(w_gate, ((0, 0), (0, 0), (0, 0), (0, GATE_LANES - 3 * hpg)))
    w_gate_pad = w_gate_pad.reshape(-1, D, N_KV * GATE_LANES)
    p = dict(mod_w=mod_w, norm_pre=norm_pre, norm_post=norm_post, mlp_w1=mlp_w1, mlp_w2=mlp_w2.astype(BF16),
             ssm_lam_re=ssm_lam_re, ssm_lam_im=ssm_lam_im, ssm_log_dt=ssm_log_dt, ssm_b_re=ssm_b_re,
             ssm_b_im=ssm_b_im, ssm_c_re=ssm_c_re, ssm_c_im=ssm_c_im, ssm_d=ssm_d,
             ssm_w_glu=ssm_w_glu.astype(BF16), kv_norm=kv_norm, w_kv=w_kv, nsa_w_qg=nsa_w_qg,
             nsa_w_o=nsa_w_o.astype(BF16), w_gate_pad=w_gate_pad)

    def make_prompt(kv2d, kv_rows):
        return prompt_attention(kv2d, kv_rows, cmp_pe, cmp_w1, cmp_b1, cmp_w2)

    def make_sample(kv2d, kv_rows):
        return sample_attention_paged(kv_rows, cache_cmp, cache_slc, cache_win, page_table,
                                      cmp_pe, cmp_w1, cmp_b1, cmp_w2)

    n_a = depth // 2
    G = D // SSM_GROUP
    h0_prompt = jnp.zeros((n_a, bp, G, STATE_DIM, 2), F32)
    y_prompt, ssm_prompt, rows_prompt = trunk(
        x_prompt, [m[:bp] for m in mods], kv_mod[:bp], h0_prompt, make_prompt, p)
    y_sample, ssm_sample, rows_sample = trunk(
        x_sample, [m[bp:bp + bs] for m in mods], kv_mod[bp:bp + bs], state_ssm, make_sample, p)
    cmp_prompt, slc_prompt, win_rows_prompt = rows_prompt
    cmp_sample, slc_sample, win_sample = rows_sample
    win_prompt = win_rows_prompt[:, -min(WINDOW, x_prompt.shape[1]):]
    return (y_prompt, y_sample, ssm_prompt, ssm_sample, cmp_prompt, cmp_sample,
            slc_prompt, slc_sample, win_prompt, win_sample)
```

```python
import functools
import math

import jax
import jax.numpy as jnp
import numpy as np
from jax import lax
from jax.experimental import pallas as pl
from jax.experimental.pallas import tpu as pltpu

F32 = jnp.float32
BF16 = jnp.bfloat16

SSM_GROUP = 16
STATE_DIM = 64
HEAD_DIM = 128
N_KV = 4
CMP_STRIDE = 16
CMP_BLOCK = 2 * CMP_STRIDE
SEL_BLOCK = 64
SUBS_PER_SEL = SEL_BLOCK // CMP_STRIDE
TOPK = 16
WINDOW = 512
QBLK = 128
PAGE_SIZE = 128
EPS = 1e-6
NEG = -1e30
FORCE_BONUS = 1e4

V7X_LANES = 128
GATE_LANES = V7X_LANES
V7X_VMEM_LIMIT_BYTES = 56 * 1024 * 1024
HIGHEST = lax.Precision.HIGHEST
LOG2E = math.log2(math.e)


def _cparams(n_axes):
    return pltpu.CompilerParams(dimension_semantics=("arbitrary",) * n_axes,
                                vmem_limit_bytes=V7X_VMEM_LIMIT_BYTES)


def _pick(n, pref):
    if n <= pref:
        return n
    t = pref
    while n % t:
        t //= 2
    return t


def _norm_mod_kernel(x_ref, g_ref, sc_ref, sh_ref, o_ref):
    x = x_ref[...]
    r = lax.rsqrt(jnp.mean(x * x, axis=-1, keepdims=True) + EPS)
    y = (x * r) * g_ref[...]
    o_ref[...] = (y * (1.0 + sc_ref[...]) + sh_ref[...]).astype(o_ref.dtype)


def norm_mod(x, g, scale, shift, out_dtype):
    M, D = x.shape
    tm = _pick(M, 512)
    per_row = scale.shape[0] != 1
    mod_spec = pl.BlockSpec((tm, D), lambda i: (i, 0)) if per_row else pl.BlockSpec((1, D), lambda i: (0, 0))
    return pl.pallas_call(
        _norm_mod_kernel,
        grid=(M // tm,),
        in_specs=[pl.BlockSpec((tm, D), lambda i: (i, 0)),
                  pl.BlockSpec((1, D), lambda i: (0, 0)),
                  mod_spec, mod_spec],
        out_specs=pl.BlockSpec((tm, D), lambda i: (i, 0)),
        out_shape=jax.ShapeDtypeStruct((M, D), out_dtype),
        compiler_params=_cparams(1),
        name="norm_mod",
    )(x, g.reshape(1, D), scale, shift)


def _mm_wide_kernel(*refs, prologue, epilogue, has_bias, exact, n_w):
    a_ref = refs[0]
    w_refs = refs[1:1 + n_w]
    pos = 1 + n_w
    b_ref = refs[pos] if has_bias else None
    pos += int(has_bias)
    o_ref = refs[pos]
    wbf_refs = refs[pos + 1:]

    a = a_ref[...]
    if prologue == "silu":
        a = a * jax.nn.sigmoid(a)
    if exact:
        zs = [jnp.dot(a, w[...], preferred_element_type=F32, precision=HIGHEST) for w in w_refs]
    else:
        @pl.when(pl.program_id(1) == 0)
        def _():
            for w, wbf in zip(w_refs, wbf_refs):
                wbf[...] = w[...].astype(BF16)

        a = a.astype(BF16)
        zs = [jnp.dot(a, wbf[...], preferred_element_type=F32) for wbf in wbf_refs]
    z = zs[0]
    if has_bias:
        z = z + b_ref[...]
    if epilogue == "sqrelu":
        z = jnp.square(jnp.maximum(z, 0.0))
    elif epilogue == "sigmoid":
        z = jax.nn.sigmoid(z)
    elif epilogue == "glu":
        z = z * jax.nn.sigmoid(zs[1])
    o_ref[...] = z.astype(o_ref.dtype)


def mm_wide(a, w, *, layer=None, col0=0, n_out=None, bias=None, prologue=None, epilogue=None,
            exact=False, out_dtype=F32, tm=1024, tn=512):
    M, K = a.shape
    n_out = n_out if n_out is not None else w.shape[-1] - col0
    tm = _pick(M, tm)
    tn = _pick(n_out, tn)
    assert col0 % tn == 0 and n_out % tn == 0
    n_w = 2 if epilogue == "glu" else 1
    jb = col0 // tn

    def w_spec(extra):
        if layer is None:
            return pl.BlockSpec((K, tn), lambda j, i: (0, jb + extra + j))
        return pl.BlockSpec((None, K, tn), lambda j, i: (layer, 0, jb + extra + j))

    in_specs = [pl.BlockSpec((tm, K), lambda j, i: (i, 0))] + [w_spec(e * (n_out // tn)) for e in range(n_w)]
    args = [a] + [w] * n_w
    if bias is not None:
        if layer is None:
            in_specs.append(pl.BlockSpec((1, tn), lambda j, i: (0, jb + j)))
            args.append(bias.reshape(1, -1))
        else:
            in_specs.append(pl.BlockSpec((None, 1, tn), lambda j, i: (layer, 0, jb + j)))
            args.append(bias.reshape(bias.shape[0], 1, -1))
    scratch = [] if exact else [pltpu.VMEM((K, tn), BF16) for _ in range(n_w)]
    return pl.pallas_call(
        functools.partial(_mm_wide_kernel, prologue=prologue, epilogue=epilogue,
                          has_bias=bias is not None, exact=exact, n_w=n_w),
        grid=(n_out // tn, M // tm),
        in_specs=in_specs,
        out_specs=pl.BlockSpec((tm, tn), lambda j, i: (i, j)),
        out_shape=jax.ShapeDtypeStruct((M, n_out), out_dtype),
        scratch_shapes=scratch,
        compiler_params=_cparams(2),
        name="mm_wide",
    )(*args)


def _mm_tall_kernel(a_ref, *refs, n_w, nk, n_next):
    w_refs = refs[:n_w]
    res_ref, g_ref, ga_ref = refs[n_w:n_w + 3]
    nxt_in = refs[n_w + 3:n_w + 3 + 3 * n_next]
    o_ref = refs[n_w + 3 + 3 * n_next]
    nxt_out = refs[n_w + 4 + 3 * n_next:n_w + 4 + 4 * n_next]
    acc_refs = refs[n_w + 4 + 4 * n_next:]
    k = pl.program_id(1)

    @pl.when(k == 0)
    def _():
        for acc in acc_refs:
            acc[...] = jnp.zeros_like(acc)

    a = a_ref[...].astype(BF16)
    for w, acc in zip(w_refs, acc_refs):
        acc[...] += jnp.dot(a, w[...].astype(BF16), preferred_element_type=F32)

    @pl.when(k == nk - 1)
    def _():
        m = acc_refs[0][...]
        if n_w == 2:
            m = m * jax.nn.sigmoid(acc_refs[1][...])
        r = lax.rsqrt(jnp.mean(m * m, axis=-1, keepdims=True) + EPS)
        x = res_ref[...] + ga_ref[...] * ((m * r) * g_ref[...])
        o_ref[...] = x
        if n_next:
            xn = x * lax.rsqrt(jnp.mean(x * x, axis=-1, keepdims=True) + EPS)
            for j in range(n_next):
                gn, sc, sh = nxt_in[3 * j:3 * j + 3]
                nxt_out[j][...] = ((xn * gn[...]) * (1.0 + sc[...]) + sh[...]).astype(nxt_out[j].dtype)


def mm_tall(a, w, res, g, gate, *, layer=None, glu=False, tm=512, tk=512, next_norms=()):
    M, K = a.shape
    N = res.shape[1]
    tm = _pick(M, tm)
    tk = _pick(K, tk)
    nk = K // tk
    n_w = 2 if glu else 1

    w_mode = dict(pipeline_mode=pl.Buffered(1)) if nk == 1 else {}

    def w_spec(e):
        if layer is None:
            return pl.BlockSpec((tk, N), lambda i, k: (k, e), **w_mode)
        return pl.BlockSpec((None, tk, N), lambda i, k: (layer, k, e), **w_mode)

    per_row = gate.shape[0] != 1
    row_spec = pl.BlockSpec((tm, N), lambda i, k: (i, 0))
    vec_spec = pl.BlockSpec((1, N), lambda i, k: (0, 0))
    mod_spec = row_spec if per_row else vec_spec
    n_next = len(next_norms)
    nxt_args, nxt_specs = [], []
    for gn, sc, sh, _ in next_norms:
        nxt_args += [gn.reshape(1, N), sc, sh]
        nxt_specs += [vec_spec, mod_spec, mod_spec]
    outs = pl.pallas_call(
        functools.partial(_mm_tall_kernel, n_w=n_w, nk=nk, n_next=n_next),
        grid=(M // tm, nk),
        in_specs=[pl.BlockSpec((tm, tk), lambda i, k: (i, k))] + [w_spec(e) for e in range(n_w)]
        + [row_spec, vec_spec, mod_spec] + nxt_specs,
        out_specs=[row_spec] * (1 + n_next),
        out_shape=[jax.ShapeDtypeStruct((M, N), F32)]
        + [jax.ShapeDtypeStruct((M, N), dt) for _, _, _, dt in next_norms],
        scratch_shapes=[pltpu.VMEM((tm, N), F32) for _ in range(n_w)],
        compiler_params=_cparams(2),
        name="mm_tall",
    )(a, *([w] * n_w), res, g.reshape(1, N), gate, *nxt_args)
    return outs if n_next else outs[0]


S5_GROUPS_PER_STEP = V7X_LANES // SSM_GROUP
P2 = 2 * STATE_DIM


def _s5_prep_kernel(lam_re_ref, lam_im_ref, ldt_ref, btr_ref, bti_ref, cr_ref, ci_ref,
                    ws_ref, wct_ref, wm_ref, at_ref, *, tc, gb):
    ws_ref[...] = jnp.zeros(ws_ref.shape, ws_ref.dtype)
    wct_ref[...] = jnp.zeros(wct_ref.shape, wct_ref.dtype)
    C = SSM_GROUP
    tcc = tc * C
    wk = max(tcc, V7X_LANES)
    lane = lax.broadcasted_iota(jnp.int32, (C, P2), 1)
    is_re = lane < STATE_DIM
    kk = lax.broadcasted_iota(jnp.int32, (tc + 1, P2), 0).astype(F32)
    lane_k = lax.broadcasted_iota(jnp.int32, (C, wk), 1)
    for gg in range(gb):
        lr = lam_re_ref[gg]
        li = lam_im_ref[gg]
        dt = jnp.exp(ldt_ref[gg])
        mag = jnp.exp(kk * (lr * dt))
        ang = kk * (li * dt)
        pr = mag * jnp.cos(ang)
        pi = mag * jnp.sin(ang)
        x = pr[1:2] - 1.0
        y = pi[1:2]
        den = lr * lr + li * li
        cfr = (x * lr + y * li) / den
        cfi = (y * lr - x * li) / den
        btr = btr_ref[gg]
        bti = bti_ref[gg]
        bbr = cfr * btr - cfi * bti
        bbi = cfr * bti + cfi * btr
        cre = cr_ref[gg]
        cim = ci_ref[gg]

        def bm(k):
            return jnp.where(is_re, pr[k:k + 1] * bbr - pi[k:k + 1] * bbi, pr[k:k + 1] * bbi + pi[k:k + 1] * bbr)

        def cm(k):
            return jnp.where(is_re, cre * pr[k:k + 1] - cim * pi[k:k + 1], -(cre * pi[k:k + 1] + cim * pr[k:k + 1]))

        cms = [cm(k) for k in range(tc + 1)]
        cs0 = jnp.concatenate(cms[:tc] + [jnp.zeros((wk - tcc, P2), F32)] * (wk > tcc), axis=0)
        kst = lax.dot_general(bm(0), cs0, (((1,), (1,)), ((), ())), preferred_element_type=F32,
                              precision=HIGHEST)
        r0 = gg * C
        for s in range(tc):
            ws_ref[s, r0:r0 + C, gg * P2:(gg + 1) * P2] = bm(tc - 1 - s).astype(ws_ref.dtype)
            wct_ref[s, r0:r0 + C, gg * P2:(gg + 1) * P2] = cms[s + 1].astype(wct_ref.dtype)
            shift = (r0 - s * C) % wk
            moved = kst if shift == 0 else pltpu.roll(kst, shift, 1)
            blockdiag = jnp.where((lane_k >= r0) & (lane_k < r0 + C), moved, 0.0)
            wm_ref[s, r0:r0 + C, :] = blockdiag[:, :V7X_LANES].astype(wm_ref.dtype)
        at_ref[gg, 0:1, :] = pr[tc:tc + 1]
        at_ref[gg, 1:2, :] = jnp.where(is_re[0:1], -pi[tc:tc + 1], pi[tc:tc + 1])


def s5_prep(lam_re, lam_im, log_dt, b_re, b_im, c_re, c_im, tc):
    G = lam_re.shape[0]
    gb = _pick(G, S5_GROUPS_PER_STEP)
    C = SSM_GROUP
    tcc = tc * C
    dup = lambda v: jnp.concatenate([v, v], axis=-1)
    lam_re2 = dup(lam_re)[:, None, :]
    lam_im2 = dup(lam_im)[:, None, :]
    ldt2 = jnp.broadcast_to(log_dt[:, None, None], (G, 1, P2))
    btr = dup(jnp.swapaxes(b_re, 1, 2))
    bti = dup(jnp.swapaxes(b_im, 1, 2))
    cr2 = dup(c_re)
    ci2 = dup(c_im)
    vec = pl.BlockSpec((gb, 1, P2), lambda i: (i, 0, 0))
    mat = pl.BlockSpec((gb, C, P2), lambda i: (i, 0, 0))
    return pl.pallas_call(
        functools.partial(_s5_prep_kernel, tc=tc, gb=gb),
        grid=(G // gb,),
        in_specs=[vec, vec, vec, mat, mat, mat, mat],
        out_specs=[pl.BlockSpec((None, tc, V7X_LANES, gb * P2), lambda i: (i, 0, 0, 0)),
                   pl.BlockSpec((None, tc, V7X_LANES, gb * P2), lambda i: (i, 0, 0, 0)),
                   pl.BlockSpec((None, tc, V7X_LANES, V7X_LANES), lambda i: (i, 0, 0, 0)),
                   pl.BlockSpec((gb, 2, P2), lambda i: (i, 0, 0))],
        out_shape=[jax.ShapeDtypeStruct((G // gb, tc, V7X_LANES, gb * P2), BF16),
                   jax.ShapeDtypeStruct((G // gb, tc, V7X_LANES, gb * P2), BF16),
                   jax.ShapeDtypeStruct((G // gb, tc, V7X_LANES, V7X_LANES), BF16),
                   jax.ShapeDtypeStruct((G, 2, P2), F32)],
        compiler_params=_cparams(1),
        name="s5_prep",
    )(lam_re2, lam_im2, ldt2, btr, bti, cr2, ci2)


def _s5_sums_kernel(u_ref, ws_ref, s_ref, *, tc):
    acc = None
    for s in range(tc):
        z = jnp.dot(u_ref[:, s, :].astype(BF16), ws_ref[s], preferred_element_type=F32)
        acc = z if acc is None else acc + z
    s_ref[...] = acc


def s5_chunk_sums(u3, ws):
    R, tc, D = u3.shape
    nb = D // V7X_LANES
    sw = ws.shape[-1]
    return pl.pallas_call(
        functools.partial(_s5_sums_kernel, tc=tc),
        grid=(nb,),
        in_specs=[pl.BlockSpec((R, tc, V7X_LANES), lambda i: (0, 0, i)),
                  pl.BlockSpec((None, tc, V7X_LANES, sw), lambda i: (i, 0, 0, 0))],
        out_specs=pl.BlockSpec((R, sw), lambda i: (0, i)),
        out_shape=jax.ShapeDtypeStruct((R, nb * sw), F32),
        compiler_params=_cparams(1),
        name="s5_sums",
    )(u3, ws)


def _s5_out_kernel(u_ref, hin_ref, wm_ref, wct_ref, d_ref, o_ref, *, tc):
    us = [u_ref[:, s, :] for s in range(tc)]
    ub = [x.astype(BF16) for x in us]
    hb = hin_ref[...].astype(BF16)
    d = d_ref[...]
    for t in range(tc):
        y = lax.dot_general(hb, wct_ref[t], (((1,), (1,)), ((), ())), preferred_element_type=F32)
        for s in range(t + 1):
            y = y + jnp.dot(ub[s], wm_ref[t - s], preferred_element_type=F32)
        o_ref[:, t, :] = jax.nn.gelu(y + d * us[t])


def s5_chunk_out(u3, hin, wm, wct, d_skip):
    R, tc, D = u3.shape
    nb = D // V7X_LANES
    sw = wct.shape[-1]
    return pl.pallas_call(
        functools.partial(_s5_out_kernel, tc=tc),
        grid=(nb,),
        in_specs=[pl.BlockSpec((R, tc, V7X_LANES), lambda i: (0, 0, i)),
                  pl.BlockSpec((R, sw), lambda i: (0, i)),
                  pl.BlockSpec((None, tc, V7X_LANES, V7X_LANES), lambda i: (i, 0, 0, 0)),
                  pl.BlockSpec((None, tc, V7X_LANES, sw), lambda i: (i, 0, 0, 0)),
                  pl.BlockSpec((1, V7X_LANES), lambda i: (0, i))],
        out_specs=pl.BlockSpec((R, tc, V7X_LANES), lambda i: (0, 0, i)),
        out_shape=jax.ShapeDtypeStruct((R, tc, D), F32),
        compiler_params=_cparams(1),
        name="s5_chunk_out",
    )(u3, hin, wm, wct, d_skip.reshape(1, D))


S5_CHUNKS_PER_STEP = 64


def _s5_scan_kernel(s_ref, h0_ref, at_ref, hin_ref, hfin_ref, h_scr, *, cb, n_steps):
    j = pl.program_id(1)

    @pl.when(j == 0)
    def _():
        h_scr[...] = h0_ref[...]

    ar = at_ref[0]
    ai = at_ref[1]

    def step(c, h):
        hin_ref[c] = h
        return ar * h + ai * pltpu.roll(h, STATE_DIM, 1) + s_ref[c]

    h = lax.fori_loop(0, cb, step, h_scr[...])
    h_scr[...] = h

    @pl.when(j == n_steps - 1)
    def _():
        hfin_ref[...] = h


def s5_scan(s, h0, at):
    Bn, n_chunk, G, _ = s.shape
    cb = _pick(n_chunk, S5_CHUNKS_PER_STEP)
    n_steps = n_chunk // cb
    return pl.pallas_call(
        functools.partial(_s5_scan_kernel, cb=cb, n_steps=n_steps),
        grid=(Bn, n_steps),
        in_specs=[pl.BlockSpec((None, cb, G, P2), lambda b, j: (b, j, 0, 0)),
                  pl.BlockSpec((None, G, P2), lambda b, j: (b, 0, 0)),
                  pl.BlockSpec((2, G, P2), lambda b, j: (0, 0, 0))],
        out_specs=[pl.BlockSpec((None, cb, G, P2), lambda b, j: (b, j, 0, 0)),
                   pl.BlockSpec((None, G, P2), lambda b, j: (b, 0, 0))],
        out_shape=[jax.ShapeDtypeStruct((Bn, n_chunk, G, P2), F32),
                   jax.ShapeDtypeStruct((Bn, G, P2), F32)],
        scratch_shapes=[pltpu.VMEM((G, P2), F32)],
        compiler_params=_cparams(2),
        name="s5_scan",
    )(s, h0, at)


def s5_mixer_core(u, h0, lam_re, lam_im, log_dt, b_re, b_im, c_re, c_im, d_skip):
    Bn, L, D = u.shape
    G = D // SSM_GROUP
    tc = _pick(L, 16)
    n_chunk = L // tc
    ws, wct, wm, at = s5_prep(lam_re, lam_im, log_dt, b_re, b_im, c_re, c_im, tc)
    u3 = u.reshape(Bn * n_chunk, tc, D)
    s = s5_chunk_sums(u3, ws).reshape(Bn, n_chunk, G, P2)
    h0v = jnp.concatenate([h0[..., 0], h0[..., 1]], axis=-1)
    hin, hfin = s5_scan(s, h0v, at.transpose(1, 0, 2))
    gy = s5_chunk_out(u3, hin.reshape(Bn * n_chunk, G * P2), wm, wct, d_skip)
    h_last = jnp.stack([hfin[..., :STATE_DIM], hfin[..., STATE_DIM:]], axis=-1)
    return gy.reshape(Bn * L, D), h_last


SUBS_PER_PAGE = PAGE_SIZE // CMP_STRIDE
KV_COLS = 2 * N_KV * HEAD_DIM
CMP_PAGES_PER_STEP = 16


def _cmp_lohi_kernel(pt_ref, *refs, n_pg):
    x_refs = refs[:n_pg]
    w_ref = refs[n_pg]
    o_ref = refs[n_pg + 1]
    rows = SUBS_PER_PAGE * N_KV
    for c in range(2):
        acc = jnp.zeros((n_pg * rows, 2 * HEAD_DIM), F32)
        for rp in range(CMP_STRIDE // 2):
            parts = []
            for p in range(n_pg):
                a0 = x_refs[p][:, 2 * rp, c].reshape(rows, HEAD_DIM)
                a1 = x_refs[p][:, 2 * rp + 1, c].reshape(rows, HEAD_DIM)
                parts.append(jnp.concatenate([a0, a1], axis=1))
            xs = jnp.concatenate(parts, axis=0).astype(BF16)
            acc = acc + jnp.dot(xs, w_ref[c, rp], preferred_element_type=F32)
        o_ref[c] = acc


def cmp_lohi(pages, page_table, cmp_w1):
    Bn, n_pages = page_table.shape
    n_pg = _pick(n_pages, CMP_PAGES_PER_STEP)
    x = pages.reshape(pages.shape[0], SUBS_PER_PAGE, CMP_STRIDE, 2, N_KV, HEAD_DIM)
    half = CMP_STRIDE // 2
    w_lo = cmp_w1[:, :CMP_STRIDE].reshape(2, half, 2 * HEAD_DIM, HEAD_DIM)
    w_hi = cmp_w1[:, CMP_STRIDE:].reshape(2, half, 2 * HEAD_DIM, HEAD_DIM)
    w = jnp.concatenate([w_lo, w_hi], axis=-1).astype(BF16)
    n_sub = n_pages * SUBS_PER_PAGE

    def x_spec(p):
        return pl.BlockSpec((None, SUBS_PER_PAGE, CMP_STRIDE, 2, N_KV, HEAD_DIM),
                            lambda b, i, pt: (pt[b, i * n_pg + p], 0, 0, 0, 0, 0))

    grid_spec = pltpu.PrefetchScalarGridSpec(
        num_scalar_prefetch=1,
        grid=(Bn, n_pages // n_pg),
        in_specs=[x_spec(p) for p in range(n_pg)]
        + [pl.BlockSpec(w.shape, lambda b, i, pt: (0, 0, 0, 0))],
        out_specs=pl.BlockSpec((None, 2, n_pg * SUBS_PER_PAGE * N_KV, 2 * HEAD_DIM),
                               lambda b, i, pt: (b, 0, i, 0)),
    )
    return pl.pallas_call(
        functools.partial(_cmp_lohi_kernel, n_pg=n_pg),
        grid_spec=grid_spec,
        out_shape=jax.ShapeDtypeStruct((Bn, 2, n_sub * N_KV, 2 * HEAD_DIM), F32),
        compiler_params=_cparams(2),
        name="cmp_lohi",
    )(page_table, *([x] * n_pg), w)


def _compress_kernel(x_ref, pe_ref, w1_ref, b1_ref, w2_ref, o_ref, pe_scr):
    n_rows = x_ref.shape[0]
    half = w1_ref.shape[0] // 2
    c = pl.program_id(1)

    @pl.when(pl.program_id(0) == 0)
    def _():
        pe = jnp.broadcast_to(pe_ref[...], (8, 2 * half))
        pe_scr[c, 0] = jnp.dot(pe[:, :half], w1_ref[:half], preferred_element_type=F32, precision=HIGHEST)
        pe_scr[c, 1] = jnp.dot(pe[:, half:], w1_ref[half:], preferred_element_type=F32, precision=HIGHEST)

    x = x_ref[...]
    lo = x[:, :HEAD_DIM] + pe_scr[c, 0][0:1]
    hi = x[:, HEAD_DIM:] + pe_scr[c, 1][0:1]
    hi_next = pltpu.roll(hi, n_rows - N_KV, 0)
    h = jax.nn.gelu(lo + hi_next + b1_ref[...])
    o_ref[...] = jnp.dot(h.astype(BF16), w2_ref[...].astype(BF16), preferred_element_type=F32).astype(o_ref.dtype)


def compress_blocks(lohi, cmp_pe, cmp_w1, cmp_b1, cmp_w2):
    Bn, _, n_rows, _ = lohi.shape
    kdim = CMP_BLOCK * HEAD_DIM
    out = pl.pallas_call(
        _compress_kernel,
        grid=(Bn, 2),
        in_specs=[pl.BlockSpec((None, None, n_rows, 2 * HEAD_DIM), lambda b, c: (b, c, 0, 0)),
                  pl.BlockSpec((None, 1, kdim), lambda b, c: (c, 0, 0)),
                  pl.BlockSpec((None, kdim, HEAD_DIM), lambda b, c: (c, 0, 0)),
                  pl.BlockSpec((None, 1, HEAD_DIM), lambda b, c: (c, 0, 0)),
                  pl.BlockSpec((None, HEAD_DIM, HEAD_DIM), lambda b, c: (c, 0, 0))],
        out_specs=pl.BlockSpec((None, None, n_rows, HEAD_DIM), lambda b, c: (b, c, 0, 0)),
        out_shape=jax.ShapeDtypeStruct((Bn, 2, n_rows, HEAD_DIM), BF16),
        scratch_shapes=[pltpu.VMEM((2, 2, 8, HEAD_DIM), F32)],
        compiler_params=_cparams(2),
        name="compress",
    )(lohi, cmp_pe.reshape(2, 1, kdim), cmp_w1.reshape(2, kdim, HEAD_DIM),
      cmp_b1.reshape(2, 1, HEAD_DIM), cmp_w2)
    return out.reshape(Bn, 2, n_rows // N_KV, N_KV, HEAD_DIM).transpose(0, 1, 3, 2, 4)


N_CAND = V7X_LANES
NEG_TAKEN = -3e38


def _topk_mask(score, axis, k_sel):
    idx = lax.broadcasted_iota(jnp.int32, score.shape, axis)
    sel = jnp.zeros(score.shape, F32)
    for _ in range(k_sel):
        m = jnp.max(score, axis=axis, keepdims=True)
        first = jnp.min(jnp.where(score == m, idx, N_CAND), axis=axis, keepdims=True)
        hit = idx == first
        sel = jnp.where(hit & (m > 0.5 * NEG), 1.0, sel)
        score = jnp.where(hit, NEG_TAKEN, score)
    return sel


def _cmp_attn_kernel(slope_ref, q_ref, kc_ref, vc_ref, gate_ref, wsel_ref, oc_ref, sel_ref, *,
                     tq, hpg, gps, bps, pos0, pos_step, transposed, k_sel, n_cand):
    i = pl.program_id(1)
    n_cmp = kc_ref.shape[2]
    gw = hpg * HEAD_DIM
    scale = HEAD_DIM ** -0.5
    base = pos0 + i * pos_step
    t_idx = lax.broadcasted_iota(jnp.int32, (tq, n_cmp), 0)
    n_idx = lax.broadcasted_iota(jnp.int32, (tq, n_cmp), 1)
    dist_i = base + t_idx - (n_idx * CMP_STRIDE + (CMP_BLOCK - 1))
    valid = dist_i >= 0
    row_seen = base + lax.broadcasted_iota(jnp.int32, (tq, 1), 0) >= CMP_BLOCK - 1
    dist = dist_i.astype(F32)
    shape, j_ax, t_ax = ((N_CAND, tq), 0, 1) if transposed else ((tq, N_CAND), 1, 0)
    j = lax.broadcasted_iota(jnp.int32, shape, j_ax)
    blk = (base + lax.broadcasted_iota(jnp.int32, shape, t_ax)) // SEL_BLOCK
    forced = (j == 0) | (j == blk) | (j == blk - 1)
    visible = (j <= blk) & (j < n_cand)
    scores = []
    for bi, gi in [(b, g) for b in range(bps) for g in range(gps)]:
        g = pl.program_id(2) * gps + gi
        gates = gate_ref[bi]
        q = q_ref[bi, :, gi * gw:(gi + 1) * gw]
        qs = jnp.concatenate([q[:, h * HEAD_DIM:(h + 1) * HEAD_DIM] for h in range(hpg)], axis=0)
        qs = (qs * (scale * LOG2E)).astype(BF16)
        s = lax.dot_general(qs, kc_ref[bi, gi], (((1,), (1,)), ((), ())), preferred_element_type=F32)
        vc = vc_ref[bi, gi]
        psum = jnp.zeros((tq, n_cmp), F32)
        for h in range(hpg):
            sh = s[h * tq:(h + 1) * tq] - (slope_ref[g * hpg + h] * LOG2E) * dist
            sh = jnp.where(valid, sh, NEG)
            m = jnp.max(sh, axis=-1, keepdims=True)
            e = jnp.exp2(sh - m)
            inv = jnp.where(row_seen, 1.0 / jnp.maximum(jnp.sum(e, axis=-1, keepdims=True), 1e-30), 0.0)
            p = e * inv
            psum = psum + p
            o_h = jnp.dot(p.astype(BF16), vc, preferred_element_type=F32)
            c0 = gi * gw + h * HEAD_DIM
            gc = gi * GATE_LANES + 3 * h
            oc_ref[bi, :, c0:c0 + HEAD_DIM] = o_h * gates[:, gc:gc + 1]
        if transposed:
            imp = lax.dot_general(wsel_ref[...], psum, (((1,), (1,)), ((), ())),
                                  preferred_element_type=F32, precision=HIGHEST)
        else:
            imp = lax.dot_general(psum, wsel_ref[...], (((1,), (1,)), ((), ())),
                                  preferred_element_type=F32, precision=HIGHEST)
        scores.append(jnp.where(visible, imp + jnp.where(forced, FORCE_BONUS, 0.0), NEG))
    if transposed:
        for n, (bi, gi) in enumerate([(b, g) for b in range(bps) for g in range(gps)]):
            sel = _topk_mask(scores[n], 0, k_sel).T
            sel_ref[bi, :, gi * N_CAND:(gi + 1) * N_CAND] = sel.astype(sel_ref.dtype)
    else:
        sel = _topk_mask(jnp.concatenate(scores, axis=0), 1, k_sel)
        for n, (bi, gi) in enumerate([(b, g) for b in range(bps) for g in range(gps)]):
            sel_ref[bi, :, gi * N_CAND:(gi + 1) * N_CAND] = sel[n * tq:(n + 1) * tq].astype(sel_ref.dtype)


def _sel_weights(n_cmp_pad, n_cmp):
    j = np.arange(N_CAND)[:, None]
    n = np.arange(n_cmp_pad)[None, :]
    w = (n >= SUBS_PER_SEL * j - 1) & (n <= SUBS_PER_SEL * j + SUBS_PER_SEL - 1) & (n < n_cmp)
    return jnp.asarray(w.astype(np.float32))


def cmp_attention(q, kcvc, gates, *, tq, pos0, pos_step, transposed, k_sel, n_cand):
    Bn, T, D = q.shape
    n_heads = D // HEAD_DIM
    hpg = n_heads // N_KV
    n_sub = kcvc.shape[3]
    gw = hpg * HEAD_DIM
    slopes = alibi_slopes(n_heads).reshape(-1)
    wsel = _sel_weights(n_sub, n_sub - 1)
    gps = N_KV if tq < QBLK else 1
    bps = _pick(Bn, 4) if tq < QBLK else 1
    kern = functools.partial(_cmp_attn_kernel, tq=tq, hpg=hpg, gps=gps, bps=bps, pos0=pos0, pos_step=pos_step,
                             transposed=transposed, k_sel=k_sel, n_cand=n_cand)
    return pl.pallas_call(
        kern,
        grid=(Bn // bps, T // tq, N_KV // gps),
        in_specs=[pl.BlockSpec(memory_space=pltpu.SMEM),
                  pl.BlockSpec((bps, tq, gps * gw), lambda b, i, g: (b, i, g)),
                  pl.BlockSpec((bps, None, gps, n_sub, HEAD_DIM), lambda b, i, g: (b, 0, g, 0, 0)),
                  pl.BlockSpec((bps, None, gps, n_sub, HEAD_DIM), lambda b, i, g: (b, 1, g, 0, 0)),
                  pl.BlockSpec((bps, tq, gps * GATE_LANES), lambda b, i, g: (b, i, g)),
                  pl.BlockSpec((N_CAND, n_sub), lambda b, i, g: (0, 0))],
        out_specs=[pl.BlockSpec((bps, tq, gps * gw), lambda b, i, g: (b, i, g)),
                   pl.BlockSpec((bps, tq, gps * N_CAND), lambda b, i, g: (b, i, g))],
        out_shape=[jax.ShapeDtypeStruct((Bn, T, D), F32),
                   jax.ShapeDtypeStruct((Bn, T, N_KV * N_CAND), BF16)],
        compiler_params=_cparams(3),
        name="cmp_attn",
    )(slopes, q, kcvc, kcvc, gates, wsel)


SEL_TK = 512
SEL_TQ = 256
MASK_BIG = 1e30


def _sel_win_kernel(slope_ref, q_ref, ks_ref, vs_ref, kw_ref, vw_ref, sel_ref, oc_ref, gate_ref, e_ref,
                    o_ref, ksb, vsb, kwb, vwb, *, tq, hpg):
    g = pl.program_id(0)
    i = pl.program_id(1)
    L = ks_ref.shape[0]
    s0 = i * tq
    scale = HEAD_DIM ** -0.5

    @pl.when(i == 0)
    def _():
        ksb[:, :HEAD_DIM] = ks_ref[...].astype(BF16)
        ksb[:, HEAD_DIM:] = e_ref[...]
        vsb[...] = vs_ref[...].astype(BF16)
        kwb[0:WINDOW] = jnp.zeros((WINDOW, HEAD_DIM), BF16)
        vwb[0:WINDOW] = jnp.zeros((WINDOW, HEAD_DIM), BF16)
        kwb[WINDOW:] = kw_ref[...].astype(BF16)
        vwb[WINDOW:] = vw_ref[...].astype(BF16)

    q = q_ref[...]
    qs = jnp.concatenate([q[:, h * HEAD_DIM:(h + 1) * HEAD_DIM] for h in range(hpg)], axis=0)
    qs = (qs * (scale * LOG2E)).astype(BF16)
    slopes = [slope_ref[g * hpg + h] * LOG2E for h in range(hpg)]
    neg_sel = sel_ref[...] - 1.0
    qaug = jnp.concatenate([qs, jnp.concatenate([neg_sel] * hpg, axis=0)], axis=1)
    n_tiles = s0 // SEL_TK + 1

    def sweep(k0, carry, width, diagonal):
        ms, ls, accs = carry
        k0 = pl.multiple_of(k0, SEL_TK)
        v_t = vsb[pl.ds(k0, width), :]
        if diagonal:
            t_idx = lax.broadcasted_iota(jnp.int32, (tq, width), 0)
            c_idx = lax.broadcasted_iota(jnp.int32, (tq, width), 1)
            causal = s0 + t_idx >= k0 + c_idx
        s = lax.dot_general(qaug, ksb[pl.ds(k0, width), :], (((1,), (1,)), ((), ())),
                            preferred_element_type=F32)
        rel = (k0 - s0 + lax.broadcasted_iota(jnp.int32, (1, width), 1)).astype(F32)
        new_m, new_l, new_acc = [], [], []
        for h in range(hpg):
            sh = s[h * tq:(h + 1) * tq] + slopes[h] * rel
            if diagonal:
                sh = jnp.where(causal, sh, -MASK_BIG)
            m_new = jnp.maximum(ms[h], jnp.max(sh, axis=-1, keepdims=True))
            p = jnp.exp2(sh - m_new)
            alpha = jnp.exp2(ms[h] - m_new)
            new_l.append(alpha * ls[h] + jnp.sum(p, axis=-1, keepdims=True))
            new_acc.append(alpha * accs[h] + jnp.dot(p.astype(BF16), v_t, preferred_element_type=F32))
            new_m.append(m_new)
        return tuple(new_m), tuple(new_l), tuple(new_acc)

    init = ((jnp.full((tq, 1), NEG, F32),) * hpg, (jnp.zeros((tq, 1), F32),) * hpg,
            (jnp.zeros((tq, HEAD_DIM), F32),) * hpg)
    n_pairs = (n_tiles - 1) // 2
    carry = lax.fori_loop(0, n_pairs, lambda i2, c: sweep(i2 * (2 * SEL_TK), c, 2 * SEL_TK, False), init)
    k_last = n_pairs * (2 * SEL_TK)
    ms, ls, accs = lax.cond(n_tiles % 2 == 0,
                            lambda c: sweep(k_last, c, 2 * SEL_TK, True),
                            lambda c: sweep(k_last, c, SEL_TK, True), carry)

    wn = WINDOW + tq
    w0 = pl.multiple_of(s0, tq)
    kw_t = kwb[pl.ds(w0, wn), :]
    vw_t = vwb[pl.ds(w0, wn), :]
    sw = lax.dot_general(qs, kw_t, (((1,), (1,)), ((), ())), preferred_element_type=F32)
    t_idx = lax.broadcasted_iota(jnp.int32, (tq, wn), 0)
    c_idx = lax.broadcasted_iota(jnp.int32, (tq, wn), 1)
    dist_i = t_idx + WINDOW - c_idx
    valid = (dist_i >= 0) & (dist_i < WINDOW) & (c_idx + s0 >= WINDOW)
    dist = jnp.where(valid, dist_i.astype(F32), MASK_BIG)
    gates = gate_ref[...]
    oc = oc_ref[...]
    for h in range(hpg):
        sh = sw[h * tq:(h + 1) * tq] - slopes[h] * dist
        m = jnp.max(sh, axis=-1, keepdims=True)
        e = jnp.exp2(sh - m)
        o_w = (jnp.dot(e.astype(BF16), vw_t, preferred_element_type=F32)
               / jnp.maximum(jnp.sum(e, axis=-1, keepdims=True), 1e-30))
        o_s = accs[h] / jnp.maximum(ls[h], 1e-30)
        out = (oc[:, h * HEAD_DIM:(h + 1) * HEAD_DIM] + gates[:, 3 * h + 1:3 * h + 2] * o_s
               + gates[:, 3 * h + 2:3 * h + 3] * o_w)
        o_ref[:, h * HEAD_DIM:(h + 1) * HEAD_DIM] = out.astype(o_ref.dtype)


def _block_onehot(n_keys):
    key = np.arange(n_keys)[:, None]
    j = np.arange(N_CAND)[None, :]
    return jnp.asarray((key // SEL_BLOCK == j).astype(np.float32) * MASK_BIG, dtype=BF16)


def sel_win_attention(q, kv, sel, oc, gates, *, tq):
    _, L, D = q.shape
    n_heads = D // HEAD_DIM
    hpg = n_heads // N_KV
    gw = hpg * HEAD_DIM
    assert L % SEL_TK == 0 and SEL_TK % tq == 0 and L // SEL_BLOCK <= N_CAND
    slopes = alibi_slopes(n_heads).reshape(-1)
    e = _block_onehot(L)

    def kv_spec(branch, which):
        cb = (branch * 2 + which) * N_KV
        return pl.BlockSpec((L, HEAD_DIM), lambda g, i: (0, cb + g), pipeline_mode=pl.Buffered(1))

    return pl.pallas_call(
        functools.partial(_sel_win_kernel, tq=tq, hpg=hpg),
        grid=(N_KV, L // tq),
        in_specs=[pl.BlockSpec(memory_space=pltpu.SMEM),
                  pl.BlockSpec((None, tq, gw), lambda g, i: (0, i, g)),
                  kv_spec(1, 0), kv_spec(1, 1), kv_spec(2, 0), kv_spec(2, 1),
                  pl.BlockSpec((None, tq, N_CAND), lambda g, i: (0, i, g)),
                  pl.BlockSpec((None, tq, gw), lambda g, i: (0, i, g)),
                  pl.BlockSpec((None, tq, GATE_LANES), lambda g, i: (0, i, g)),
                  pl.BlockSpec(e.shape, lambda g, i: (0, 0))],
        out_specs=pl.BlockSpec((tq, gw), lambda g, i: (i, g)),
        out_shape=jax.ShapeDtypeStruct((L, D), BF16),
        scratch_shapes=[pltpu.VMEM((L, 2 * HEAD_DIM), BF16), pltpu.VMEM((L, HEAD_DIM), BF16),
                        pltpu.VMEM((WINDOW + L, HEAD_DIM), BF16), pltpu.VMEM((WINDOW + L, HEAD_DIM), BF16)],
        compiler_params=_cparams(2),
        name="sel_win_attn",
    )(slopes, q, kv, kv, kv, kv, sel, oc, gates, e)


def prompt_nsa(q, gates, kv, kcvc):
    L = q.shape[1]
    n_sel = L // SEL_BLOCK
    oc, sel = cmp_attention(q, kcvc, gates, tq=QBLK, pos0=0, pos_step=QBLK, transposed=True,
                            k_sel=min(TOPK, n_sel), n_cand=n_sel)
    return sel_win_attention(q, kv, sel, oc, gates, tq=SEL_TQ)


SAMPLE_TQ = 8
SAMPLE_PAGES_PER_STEP = 16
NEW_ROWS_PAD = V7X_LANES


def _sample_sel_win_kernel(pt_ref, slope_ref, q_ref, sel_ref, oc_ref, gate_ref, e_ref, *refs,
                           n_pg, hpg, t_real, n_steps, past_len, win_len):
    pages = refs[:n_pg]
    nslc_ref, nwin_ref, cwk_ref, cwv_ref, o_ref, m_scr, l_scr, acc_scr = refs[n_pg:]
    j = pl.program_id(1)
    tq = SAMPLE_TQ
    tk = n_pg * PAGE_SIZE
    scale = HEAD_DIM ** -0.5

    @pl.when(j == 0)
    def _():
        m_scr[...] = jnp.full(m_scr.shape, NEG, F32)
        l_scr[...] = jnp.zeros(l_scr.shape, F32)
        acc_scr[...] = jnp.zeros(acc_scr.shape, F32)

    q = q_ref[...]
    neg_sel = sel_ref[...] - 1.0

    def heads_of(g):
        gw = hpg * HEAD_DIM
        qg = q[:, g * gw:(g + 1) * gw]
        qs = jnp.concatenate([qg[:, h * HEAD_DIM:(h + 1) * HEAD_DIM] for h in range(hpg)], axis=0)
        return (qs * scale).astype(BF16)

    def online(g, h, sh, v):
        r0 = h * tq
        m_old = m_scr[g, r0:r0 + tq]
        m_new = jnp.maximum(m_old, jnp.max(sh, axis=-1, keepdims=True))
        p = jnp.exp(sh - m_new)
        alpha = jnp.exp(m_old - m_new)
        l_scr[g, r0:r0 + tq] = alpha * l_scr[g, r0:r0 + tq] + jnp.sum(p, axis=-1, keepdims=True)
        acc_scr[g, r0:r0 + tq] = alpha * acc_scr[g, r0:r0 + tq] + jnp.dot(p.astype(BF16), v,
                                                                         preferred_element_type=F32)
        m_scr[g, r0:r0 + tq] = m_new

    rows = hpg * tq
    n_rows = N_KV * rows
    n_lane = tk * N_KV
    kf = jnp.concatenate([pg[:, 0].reshape(PAGE_SIZE * N_KV, HEAD_DIM) for pg in pages], axis=0).astype(BF16)
    vf = jnp.concatenate([pg[:, 1].reshape(PAGE_SIZE * N_KV, HEAD_DIM) for pg in pages], axis=0).astype(BF16)
    kaug = jnp.concatenate([kf, e_ref[...]], axis=1)
    qaug = jnp.concatenate(
        [jnp.concatenate([heads_of(g) for g in range(N_KV)], axis=0),
         jnp.concatenate([neg_sel[:, g * N_CAND:(g + 1) * N_CAND] for g in range(N_KV) for _ in range(hpg)],
                         axis=0)], axis=1)
    s = lax.dot_general(qaug, kaug, (((1,), (1,)), ((), ())), preferred_element_type=F32)
    lane = lax.broadcasted_iota(jnp.int32, (1, n_lane), 1)
    rel = (j * tk - past_len + lane // N_KV).astype(F32)
    slope_col = jnp.concatenate([jnp.full((tq, 1), slope_ref[gh], F32) for gh in range(N_KV * hpg)], axis=0)
    row_g = lax.broadcasted_iota(jnp.int32, (n_rows, 1), 0) // rows
    sh = jnp.where(lane % N_KV == row_g, s + slope_col * rel, -MASK_BIG)
    m_old = m_scr[...].reshape(n_rows, 1)
    m_new = jnp.maximum(m_old, jnp.max(sh, axis=-1, keepdims=True))
    p = jnp.exp(sh - m_new)
    alpha = jnp.exp(m_old - m_new)
    l_new = alpha * l_scr[...].reshape(n_rows, 1) + jnp.sum(p, axis=-1, keepdims=True)
    acc_new = alpha * acc_scr[...].reshape(n_rows, HEAD_DIM) + jnp.dot(p.astype(BF16), vf,
                                                                    preferred_element_type=F32)
    m_scr[...] = m_new.reshape(m_scr.shape)
    l_scr[...] = l_new.reshape(l_scr.shape)
    acc_scr[...] = acc_new.reshape(acc_scr.shape)

    @pl.when(j == n_steps - 1)
    def _():
        gates = gate_ref[...]
        oc = oc_ref[...]
        t_n = lax.broadcasted_iota(jnp.int32, (tq, NEW_ROWS_PAD), 0)
        r_n = lax.broadcasted_iota(jnp.int32, (tq, NEW_ROWS_PAD), 1)
        valid_n = (r_n <= t_n) & (r_n < t_real)
        wn = win_len + NEW_ROWS_PAD
        t_w = lax.broadcasted_iota(jnp.int32, (tq, wn), 0)
        c_w = lax.broadcasted_iota(jnp.int32, (tq, wn), 1)
        dist_w = jnp.where(c_w < win_len, t_w + win_len - c_w, t_w - (c_w - win_len))
        cached = c_w < win_len
        valid_w = ((cached & (dist_w < WINDOW) & (c_w + past_len - win_len >= 0))
                   | ((c_w >= win_len) & (dist_w >= 0) & (c_w - win_len < t_real)))
        dist_wf = dist_w.astype(F32)
        for g in range(N_KV):
            kc0 = g * HEAD_DIM
            vc0 = (N_KV + g) * HEAD_DIM
            qs = heads_of(g)
            k_n = nslc_ref[:, kc0:kc0 + HEAD_DIM].astype(BF16)
            v_n = nslc_ref[:, vc0:vc0 + HEAD_DIM].astype(BF16)
            s_n = lax.dot_general(qs, k_n, (((1,), (1,)), ((), ())), preferred_element_type=F32)
            k_w = jnp.concatenate([cwk_ref[:, g, :], nwin_ref[:, kc0:kc0 + HEAD_DIM]], axis=0).astype(BF16)
            v_w = jnp.concatenate([cwv_ref[:, g, :], nwin_ref[:, vc0:vc0 + HEAD_DIM]], axis=0).astype(BF16)
            s_w = lax.dot_general(qs, k_w, (((1,), (1,)), ((), ())), preferred_element_type=F32)
            for h in range(hpg):
                slope = slope_ref[g * hpg + h]
                r0 = h * tq
                sh = jnp.where(valid_n, s_n[r0:r0 + tq] + slope * r_n.astype(F32), -MASK_BIG)
                online(g, h, sh, v_n)
                o_s = acc_scr[g, r0:r0 + tq] / jnp.maximum(l_scr[g, r0:r0 + tq], 1e-30)
                sw = jnp.where(valid_w, s_w[r0:r0 + tq] - slope * dist_wf, NEG)
                m = jnp.max(sw, axis=-1, keepdims=True)
                e = jnp.where(valid_w, jnp.exp(sw - m), 0.0)
                p = e / jnp.maximum(jnp.sum(e, axis=-1, keepdims=True), 1e-30)
                o_w = jnp.dot(p.astype(BF16), v_w, preferred_element_type=F32)
                c0 = (g * hpg + h) * HEAD_DIM
                gc = g * GATE_LANES + 3 * h
                out = (oc[:, c0:c0 + HEAD_DIM] + gates[:, gc + 1:gc + 2] * o_s + gates[:, gc + 2:gc + 3] * o_w)
                o_ref[:, c0:c0 + HEAD_DIM] = out.astype(o_ref.dtype)


def _block_onehot_rows(n_keys):
    key = np.repeat(np.arange(n_keys), N_KV)[:, None]
    j = np.arange(N_CAND)[None, :]
    return jnp.asarray((key // SEL_BLOCK == j).astype(np.float32) * MASK_BIG, dtype=BF16)


def sample_sel_win_attention(q, sel, oc, gates, cache_slc, page_table, new_slc, new_win, cache_win, t_real):
    Bn, tq, D = q.shape
    n_heads = D // HEAD_DIM
    hpg = n_heads // N_KV
    n_pages = page_table.shape[1]
    n_pg = _pick(n_pages, SAMPLE_PAGES_PER_STEP)
    n_steps = n_pages // n_pg
    past_len = n_pages * PAGE_SIZE
    win_len = cache_win.shape[1]
    assert past_len // SEL_BLOCK <= N_CAND and past_len % SEL_BLOCK == 0 and t_real <= min(tq, SEL_BLOCK)
    slopes = alibi_slopes(n_heads).reshape(-1)
    e = _block_onehot_rows(past_len)
    e_rows = n_pg * PAGE_SIZE * N_KV
    rows = hpg * tq

    def pg_spec(p):
        return pl.BlockSpec((None, PAGE_SIZE, 2, N_KV, HEAD_DIM),
                            lambda b, j, pt: (pt[b, j * n_pg + p], 0, 0, 0, 0))

    def win_spec(c):
        return pl.BlockSpec((None, win_len, None, N_KV, HEAD_DIM), lambda b, j, pt: (b, 0, c, 0, 0))

    per_b = lambda shape: pl.BlockSpec((None,) + shape, lambda b, j, pt: (b, 0, 0))
    grid_spec = pltpu.PrefetchScalarGridSpec(
        num_scalar_prefetch=1,
        grid=(Bn, n_steps),
        in_specs=[pl.BlockSpec(memory_space=pltpu.SMEM),
                  per_b((tq, D)), per_b((tq, N_KV * N_CAND)), per_b((tq, D)), per_b((tq, N_KV * GATE_LANES)),
                  pl.BlockSpec((e_rows, N_CAND), lambda b, j, pt: (j, 0))]
        + [pg_spec(p) for p in range(n_pg)]
        + [per_b((NEW_ROWS_PAD, KV_COLS)), per_b((NEW_ROWS_PAD, KV_COLS)), win_spec(0), win_spec(1)],
        out_specs=per_b((tq, D)),
        scratch_shapes=[pltpu.VMEM((N_KV, rows, 1), F32), pltpu.VMEM((N_KV, rows, 1), F32),
                        pltpu.VMEM((N_KV, rows, HEAD_DIM), F32)],
    )
    return pl.pallas_call(
        functools.partial(_sample_sel_win_kernel, n_pg=n_pg, hpg=hpg, t_real=t_real, n_steps=n_steps,
                          past_len=past_len, win_len=win_len),
        grid_spec=grid_spec,
        out_shape=jax.ShapeDtypeStruct((Bn, tq, D), BF16),
        compiler_params=_cparams(2),
        name="sample_sel_win_attn",
    )(page_table, slopes, q, sel, oc, gates, e, *([cache_slc] * n_pg), new_slc, new_win, cache_win, cache_win)


def sample_nsa(q2d, gates_pad, kv_rows, kcvc, cache_slc, cache_win, page_table):
    _, kv_slc, kv_win = kv_rows
    Bn, T = kv_slc.shape[:2]
    D = q2d.shape[1]
    past_len = page_table.shape[1] * PAGE_SIZE
    n_past_blk = past_len // SEL_BLOCK
    pad_t = lambda a: jnp.pad(a.reshape(Bn, T, -1), ((0, 0), (0, SAMPLE_TQ - T), (0, 0)))
    q = pad_t(q2d)
    gates = pad_t(gates_pad)
    oc, sel = cmp_attention(q, kcvc, gates, tq=SAMPLE_TQ, pos0=past_len, pos_step=0, transposed=False,
                            k_sel=min(TOPK, n_past_blk + 1) - 1, n_cand=n_past_blk)
    pad_rows = lambda a: jnp.pad(a.reshape(Bn, T, KV_COLS), ((0, 0), (0, NEW_ROWS_PAD - T), (0, 0)))
    o = sample_sel_win_attention(q, sel, oc, gates, cache_slc, page_table,
                                 pad_rows(kv_slc), pad_rows(kv_win), cache_win, T)
    return o[:, :T].reshape(Bn * T, D)


def sample_attention_paged(kv_rows, cache_cmp, cache_slc, cache_win, page_table, cmp_pe, cmp_w1, cmp_b1, cmp_w2):
    T = kv_rows[0].shape[1]
    assert T < CMP_STRIDE, "new rows never complete a compression sub-block"
    kcvc = compress_blocks(cmp_lohi(cache_cmp, page_table, cmp_w1), cmp_pe, cmp_w1, cmp_b1, cmp_w2)

    def attend(q2d, gates_pad):
        return sample_nsa(q2d, gates_pad, kv_rows, kcvc, cache_slc, cache_win, page_table)

    return attend


def alibi_slopes(n_heads):
    exps = np.arange(1, n_heads + 1, dtype=np.float32) * np.float32(-8.0 / n_heads)
    return jnp.asarray(np.exp2(exps), dtype=F32).reshape(N_KV, n_heads // N_KV)


def prompt_attention(kv2d, kv_rows, cmp_pe, cmp_w1, cmp_b1, cmp_w2):
    kv_cmp = kv_rows[0]
    L = kv_cmp.shape[1]
    pages = kv_cmp.reshape(L // PAGE_SIZE, PAGE_SIZE, 2, N_KV, HEAD_DIM)
    table = jnp.arange(L // PAGE_SIZE, dtype=jnp.int32)[None]
    kcvc = compress_blocks(cmp_lohi(pages, table, cmp_w1), cmp_pe, cmp_w1, cmp_b1, cmp_w2)

    def attend(q2d, gates_pad):
        return prompt_nsa(q2d[None], gates_pad[None], kv2d, kcvc)

    return attend


def _rows(v, per_tok):
    return v if v.shape[0] == 1 else jnp.repeat(v, per_tok, axis=0)


def trunk(x, mods, kv_mod, h0, make_attend, p):
    Bn, L, D = x.shape
    M = Bn * L
    n_heads = D // HEAD_DIM
    hpg = n_heads // N_KV
    depth = p["mod_w"].shape[0]
    n_a = depth // 2
    xr = x.reshape(M, D)
    new_h = []
    kv_rows = None
    attend = None
    layer_mods = [[_rows(m, L) for m in jnp.split(mods[l], 6, axis=-1)] for l in range(depth)]
    kv_shift, kv_scale = [_rows(m, L) for m in jnp.split(kv_mod, 2, axis=-1)]

    def mixer_dtype(l):
        return F32 if l < n_a else BF16

    h_mix = norm_mod(xr, p["norm_pre"][0, 0], layer_mods[0][1], layer_mods[0][0], mixer_dtype(0))
    hk = norm_mod(xr, p["kv_norm"], kv_scale, kv_shift, BF16) if n_a == 0 else None
    for l in range(depth):
        sh1, sc1, ga1, sh2, sc2, ga2 = layer_mods[l]
        mlp_pre = [(p["norm_pre"][l, 1], sc2, sh2, BF16)]
        after_mlp = []
        if l + 1 < depth:
            nsh1, nsc1 = layer_mods[l + 1][0], layer_mods[l + 1][1]
            after_mlp.append((p["norm_pre"][l + 1, 0], nsc1, nsh1, mixer_dtype(l + 1)))
            if l + 1 == n_a:
                after_mlp.append((p["kv_norm"], kv_scale, kv_shift, BF16))
        if l == n_a:
            kv2d = mm_wide(hk, p["w_kv"], tn=1024)
            kv = kv2d.reshape(Bn, L, 3, 2, N_KV, HEAD_DIM)
            kv_rows = (kv[:, :, 0], kv[:, :, 1], kv[:, :, 2])
            attend = make_attend(kv2d, kv_rows)
        if l < n_a:
            u = h_mix
            gy, h_last = s5_mixer_core(u.reshape(Bn, L, D), h0[l], p["ssm_lam_re"][l], p["ssm_lam_im"][l],
                                       p["ssm_log_dt"][l], p["ssm_b_re"][l], p["ssm_b_im"][l],
                                       p["ssm_c_re"][l], p["ssm_c_im"][l], p["ssm_d"][l])
            new_h.append(h_last)
            xr, h = mm_tall(gy, p["ssm_w_glu"], xr, p["norm_post"][l, 0], ga1, layer=l, glu=True, tk=D,
                            next_norms=mlp_pre)
        else:
            lb = l - n_a
            q = mm_wide(h_mix, p["nsa_w_qg"], layer=lb, n_out=n_heads * HEAD_DIM, tn=1024)
            gates_pad = mm_wide(h_mix, p["w_gate_pad"], layer=lb, epilogue="sigmoid", tn=512)
            o = attend(q, gates_pad)
            xr, h = mm_tall(o, p["nsa_w_o"], xr, p["norm_post"][l, 0], ga1, layer=lb, tk=D, next_norms=mlp_pre)
        f = mm_wide(h, p["mlp_w1"], layer=l, epilogue="sqrelu", out_dtype=BF16, tn=1024)
        outs = mm_tall(f, p["mlp_w2"], xr, p["norm_post"][l, 1], ga2, layer=l, tk=1024, next_norms=after_mlp)
        if after_mlp:
            xr, h_mix = outs[0], outs[1]
            if l + 1 == n_a:
                hk = outs[2]
        else:
            xr = outs
    return xr.reshape(Bn, L, D), jnp.stack(new_h), kv_rows


def kernel(x_prompt, x_sample, c_prompt, c_sample, state_ssm, cache_cmp, cache_slc, cache_win, page_table, mod_w, mod_b, norm_pre, norm_post, mlp_w1, mlp_w2, ssm_lam_re, ssm_lam_im, ssm_log_dt, ssm_b_re, ssm_b_im, ssm_c_re, ssm_c_im, ssm_d, ssm_w_glu, kv_norm, kv_mod_w, kv_mod_b, w_kv, cmp_pe, cmp_w1, cmp_b1, cmp_w2, nsa_w_qg, nsa_w_o):
    D = x_prompt.shape[-1]
    depth = mod_w.shape[0]
    n_heads = D // HEAD_DIM
    bp, bs = c_prompt.shape[0], c_sample.shape[0]
    c_all = jnp.concatenate([c_prompt, c_sample], axis=0)
    n_c = c_all.shape[0]
    c_all = jnp.pad(c_all, ((0, -n_c % 8), (0, 0)))
    mods = [mm_wide(c_all, mod_w, layer=l, bias=mod_b, prologue="silu", exact=True, tn=512) for l in range(depth)]
    kv_mod = mm_wide(c_all, kv_mod_w, bias=kv_mod_b, prologue="silu", exact=True, tn=512)
    hpg = n_heads // N_KV
    w_gate = nsa_w_qg[:, :, n_heads * HEAD_DIM:].reshape(nsa_w_qg.shape[0], D, N_KV, 3 * hpg)
    w_gate_pad = jnp.pad(w_gate, ((0, 0), (0, 0), (0, 0), (0, GATE_LANES - 3 * hpg)))
    w_gate_pad = w_gate_pad.reshape(-1, D, N_KV * GATE_LANES)
    p = dict(mod_w=mod_w, norm_pre=norm_pre, norm_post=norm_post, mlp_w1=mlp_w1, mlp_w2=mlp_w2.astype(BF16),
             ssm_lam_re=ssm_lam_re, ssm_lam_im=ssm_lam_im, ssm_log_dt=ssm_log_dt, ssm_b_re=ssm_b_re,
             ssm_b_im=ssm_b_im, ssm_c_re=ssm_c_re, ssm_c_im=ssm_c_im, ssm_d=ssm_d,
             ssm_w_glu=ssm_w_glu.astype(BF16), kv_norm=kv_norm, w_kv=w_kv, nsa_w_qg=nsa_w_qg,
             nsa_w_o=nsa_w_o.astype(BF16), w_gate_pad=w_gate_pad)

    def make_prompt(kv2d, kv_rows):
        return prompt_attention(kv2d, kv_rows, cmp_pe, cmp_w1, cmp_b1, cmp_w2)

    def make_sample(kv2d, kv_rows):
        return sample_attention_paged(kv_rows, cache_cmp, cache_slc, cache_win, page_table,
                                      cmp_pe, cmp_w1, cmp_b1, cmp_w2)

    n_a = depth // 2
    G = D // SSM_GROUP
    h0_prompt = jnp.zeros((n_a, bp, G, STATE_DIM, 2), F32)
    y_prompt, ssm_prompt, rows_prompt = trunk(
        x_prompt, [m[:bp] for m in mods], kv_mod[:bp], h0_prompt, make_prompt, p)
    y_sample, ssm_sample, rows_sample = trunk(
        x_sample, [m[bp:bp + bs] for m in mods], kv_mod[bp:bp + bs], state_ssm, make_sample, p)
    cmp_prompt, slc_prompt, win_rows_prompt = rows_prompt
    cmp_sample, slc_sample, win_sample = rows_sample
    win_prompt = win_rows_prompt[:, -min(WINDOW, x_prompt.shape[1]):]
    return (y_prompt, y_sample, ssm_prompt, ssm_sample, cmp_prompt, cmp_sample,
            slc_prompt, slc_sample, win_prompt, win_sample)
```
